```python
import math
import jax, jax.numpy as jnp
from jax import lax
import numpy as np

D_MODEL = 1024
BATCH = 8
SEQ = 4096
DEPTH = 1
DEC_BATCH = 128
DEC_SEQ = 1
PAST_LEN = 8192
PAGE_SIZE = 128

D_RNN = D_MODEL
LRU_BLOCKS = 16
LRU_BW = D_RNN // LRU_BLOCKS
CONV_W = 4
LRU_C = 8.0
N_HEADS = 8
HEAD_DIM = 64
N_KV = 2
GROUP = N_HEADS // N_KV
D_Q = N_HEADS * HEAD_DIM
D_KV = N_KV * HEAD_DIM
ROPE_DIM = HEAD_DIM // 4
ROPE_THETA = 500000.0
CMP_LEN = 32
CMP_STRIDE = 16
CMP_HID = 256
SLC_BLOCK = 64
N_SEL = 16
WINDOW = 512
Q_BLOCK = 64
N_EXPERTS = 32
TOP_K = 4
D_FF = D_MODEL
SWIGLU_LIMIT = 7.0
SWIGLU_ALPHA = 1.702
EPS = 1e-6
NEG = -1e30
BIG = 1e30

kernel_name = 'hawk_nsa_moe_step'


def rms_norm(x, g):
    xf = x.astype(jnp.float32)
    y = xf * lax.rsqrt(jnp.mean(xf * xf, axis=-1, keepdims=True) + EPS)
    return (y * g.astype(jnp.float32)).astype(x.dtype)


def rope(x, pos):
    half = ROPE_DIM // 2
    inv = ROPE_THETA ** (-jnp.arange(half, dtype=jnp.float32) / half)
    ang = pos.astype(jnp.float32)[:, None] * inv[None, :]
    cos, sin = jnp.cos(ang)[:, None, :], jnp.sin(ang)[:, None, :]
    xr = x[..., :ROPE_DIM].astype(jnp.float32)
    x1, x2 = xr[..., :half], xr[..., half:]
    rot = jnp.concatenate([x1 * cos - x2 * sin, x1 * sin + x2 * cos], axis=-1).astype(x.dtype)
    return jnp.concatenate([rot, x[..., ROPE_DIM:]], axis=-1)


def masked_softmax(s, mask):
    s = jnp.where(mask, s.astype(jnp.float32), NEG)
    m = jnp.max(s, axis=-1, keepdims=True)
    p = jnp.where(mask, jnp.exp(s - m), 0.0)
    return p / jnp.maximum(jnp.sum(p, axis=-1, keepdims=True), 1e-30)


def split_proj(h, w_in):
    sizes = (D_RNN, D_RNN, D_Q) + (D_KV,) * 6 + (3 * N_HEADS, D_MODEL, D_MODEL)
    cuts = np.cumsum(sizes)[:-1].tolist()
    return jnp.split(h @ w_in, cuts, axis=-1)


def mixer_inputs(h, pos, w_in, g_q, g_k_cmp, g_k_slc, g_k_win):
    n, s = h.shape[:2]
    ar, ag, q, kc, vc, ks, vs, kw, vw, ng, ga, gb = split_proj(h, w_in)
    heads = lambda t, nh: t.reshape(n, s, nh, HEAD_DIM)
    q = rope(rms_norm(heads(q, N_HEADS), g_q), pos).reshape(n, s, N_KV, GROUP, HEAD_DIM)
    kc = rope(rms_norm(heads(kc, N_KV), g_k_cmp), pos)
    ks = rope(rms_norm(heads(ks, N_KV), g_k_slc), pos)
    kw = rope(rms_norm(heads(kw, N_KV), g_k_win), pos)
    vc, vs, vw = heads(vc, N_KV), heads(vs, N_KV), heads(vw, N_KV)
    ng = ng.reshape(n, s, N_KV, GROUP, 3)
    return ar, ag, q, kc, vc, ks, vs, kw, vw, ng, ga, gb


def merge_branches(y_lru, o_nsa, ga, gb, w_proj_a, w_proj_b, w_out):
    return (jax.nn.sigmoid(ga) * (y_lru @ w_proj_a) + jax.nn.sigmoid(gb) * (o_nsa @ w_proj_b)) @ w_out


def _lin_combine(c1, c2):
    a1, b1 = c1
    a2, b2 = c2
    return a1 * a2, a2 * b1 + b2


def rglru(xr, gate_in, conv_buf, h0, conv_w, conv_b, wa, ba, wx, bx, lam):
    n, s, _ = xr.shape
    xp = jnp.concatenate([conv_buf.astype(xr.dtype), xr], axis=1)
    xc = conv_b + sum(conv_w[k] * xp[:, k:k + s] for k in range(CONV_W))
    xb = xc.reshape(n, s, LRU_BLOCKS, LRU_BW)
    r = jax.nn.sigmoid((jnp.einsum('nsbi,bij->nsbj', xb, wa) + ba).astype(jnp.float32)).reshape(n, s, D_RNN)
    i = jax.nn.sigmoid((jnp.einsum('nsbi,bij->nsbj', xb, wx) + bx).astype(jnp.float32)).reshape(n, s, D_RNN)
    log_a = -LRU_C * r * jax.nn.softplus(-lam.astype(jnp.float32))
    a = jnp.exp(log_a)
    u = jnp.sqrt(-jnp.expm1(2.0 * log_a)) * (i * xc.astype(jnp.float32))
    u = u.at[:, 0].add(a[:, 0] * h0.astype(jnp.float32))
    _, h = lax.associative_scan(_lin_combine, (a, u), axis=1)
    y = (h * jax.nn.gelu(gate_in.astype(jnp.float32))).astype(xr.dtype)
    return y, xp[:, s:], h[:, -1]


def compress(rows, pos_emb, w1, w2):
    n, l = rows.shape[:2]
    r_ = CMP_LEN // CMP_STRIDE
    nch = l // CMP_STRIDE
    ch = rows[:, :nch * CMP_STRIDE].reshape(n, nch, CMP_STRIDE, N_KV, HEAD_DIM)
    w1b = w1.reshape(r_, CMP_STRIDE, HEAD_DIM, CMP_HID)
    part = jnp.einsum('nclhd,pldf->pnchf', ch, w1b)
    nc = nch - r_ + 1
    hid = jnp.einsum('ld,ldf->f', pos_emb, w1) + sum(part[p, :, p:p + nc] for p in range(r_))
    return jnp.einsum('nchf,fd->nchd', jax.nn.gelu(hid), w2)


def block_scores(p_cmp, nsb):
    nc = p_cmp.shape[-1]
    ratio = SLC_BLOCK // CMP_STRIDE
    ovl = CMP_LEN // CMP_STRIDE - 1
    widths = [(0, 0)] * (p_cmp.ndim - 1) + [(ovl, ratio * (nsb + 1) - nc - ovl)]
    pp = jnp.pad(p_cmp, widths).reshape(*p_cmp.shape[:-1], nsb + 1, ratio)
    return pp[..., :nsb, :].sum(-1) + pp[..., 1:, :ovl].sum(-1)


def select_blocks(scores, pos, nsb):
    j = jnp.arange(nsb, dtype=jnp.int32)
    cur = (pos // SLC_BLOCK)[:, None, None]
    forced = (j == 0) | (j == cur) | (j == cur - 1)
    s = jnp.where(forced, BIG, jnp.where(j <= cur, scores, NEG))
    top, idx = lax.top_k(s, min(N_SEL, nsb))
    return idx, top > NEG / 2


def nsa_attend(q, pos, cmp_k, cmp_v, gather_slc, nsb, win_k, win_v, win_pos, gates):
    n, qn = q.shape[:2]
    scale = HEAD_DIM ** -0.5
    nc = cmp_k.shape[1]
    s = jnp.einsum('nqhgd,nchd->nqhgc', q, cmp_k) * scale
    cmp_end = jnp.arange(nc, dtype=jnp.int32) * CMP_STRIDE + CMP_LEN - 1
    p_c = masked_softmax(s, cmp_end <= pos[:, None, None, None])
    o_cmp = jnp.einsum('nqhgc,nchd->nqhgd', p_c.astype(cmp_v.dtype), cmp_v)
    idx, valid = select_blocks(block_scores(p_c.sum(3), nsb), pos, nsb)
    k_g, v_g = gather_slc(idx)
    kk = idx.shape[-1] * SLC_BLOCK
    kpos = idx[..., None] * SLC_BLOCK + jnp.arange(SLC_BLOCK, dtype=jnp.int32)
    smask = (valid[..., None] & (kpos <= pos[:, None, None, None])).reshape(n, qn, N_KV, 1, kk)
    s = jnp.einsum('nqhgd,nqhkd->nqhgk', q, k_g.reshape(n, qn, N_KV, kk, HEAD_DIM)) * scale
    p_s = masked_softmax(s, smask)
    o_slc = jnp.einsum('nqhgk,nqhkd->nqhgd', p_s.astype(v_g.dtype), v_g.reshape(n, qn, N_KV, kk, HEAD_DIM))
    s = jnp.einsum('nqhgd,nkhd->nqhgk', q, win_k) * scale
    dpos = pos[:, None] - win_pos[None, :]
    wmask = (dpos >= 0) & (dpos < WINDOW) & (win_pos >= 0)[None, :]
    p_w = masked_softmax(s, wmask[:, None, None, :])
    o_win = jnp.einsum('nqhgk,nkhd->nqhgd', p_w.astype(win_v.dtype), win_v)
    g = jax.nn.sigmoid(gates)
    return g[..., 0:1] * o_cmp + g[..., 1:2] * o_slc + g[..., 2:3] * o_win


def nsa_prompt(q, kc, vc, ks, vs, kw, vw, ng, pos_k, w1_k, w2_k, pos_v, w1_v, w2_v):
    n, s = q.shape[:2]
    cmp_k = compress(kc, pos_k, w1_k, w2_k)
    cmp_v = compress(vc, pos_v, w1_v, w2_v)
    nsb = s // SLC_BLOCK
    kb = ks.reshape(n, nsb, SLC_BLOCK, N_KV, HEAD_DIM)
    vb = vs.reshape(n, nsb, SLC_BLOCK, N_KV, HEAD_DIM)
    bi = jnp.arange(n)[:, None, None, None]
    hi = jnp.arange(N_KV)[None, None, :, None]

    def gather(idx):
        return kb[bi, idx, :, hi], vb[bi, idx, :, hi]

    pad = ((0, 0), (WINDOW, 0), (0, 0), (0, 0))
    kw_pad, vw_pad = jnp.pad(kw, pad), jnp.pad(vw, pad)

    def block(i):
        start = i * Q_BLOCK
        sl = lambda t, size: lax.dynamic_slice_in_dim(t, start, size, axis=1)
        pos = start + jnp.arange(Q_BLOCK, dtype=jnp.int32)
        wpos = start - WINDOW + jnp.arange(WINDOW + Q_BLOCK, dtype=jnp.int32)
        return nsa_attend(sl(q, Q_BLOCK), pos, cmp_k, cmp_v, gather, nsb,
                          sl(kw_pad, WINDOW + Q_BLOCK), sl(vw_pad, WINDOW + Q_BLOCK), wpos, sl(ng, Q_BLOCK))

    o = lax.map(block, jnp.arange(s // Q_BLOCK, dtype=jnp.int32))
    return jnp.moveaxis(o, 0, 1).reshape(n, s, D_Q)


def nsa_sample(q, kc, vc, ks, vs, kw, vw, ng, page_table, ck_cmp, cv_cmp, ck_slc, cv_slc, ck_win, cv_win,
               pos_k, w1_k, w2_k, pos_v, w1_v, w2_v):
    n, ds = q.shape[:2]
    pos = PAST_LEN + jnp.arange(ds, dtype=jnp.int32)

    def past(pool):
        return pool[page_table].reshape(n, PAST_LEN, N_KV, HEAD_DIM)

    cmp_k = compress(jnp.concatenate([past(ck_cmp), kc], axis=1), pos_k, w1_k, w2_k)
    cmp_v = compress(jnp.concatenate([past(cv_cmp), vc], axis=1), pos_v, w1_v, w2_v)
    past_blocks = PAST_LEN // SLC_BLOCK
    new_blocks = -(-ds // SLC_BLOCK)
    nsb = past_blocks + new_blocks
    bpp = PAGE_SIZE // SLC_BLOCK
    view = lambda pool: pool.reshape(pool.shape[0], bpp, SLC_BLOCK, N_KV, HEAD_DIM)
    pk, pv = view(ck_slc), view(cv_slc)
    padn = ((0, 0), (0, new_blocks * SLC_BLOCK - ds), (0, 0), (0, 0))
    nk = jnp.pad(ks, padn).reshape(n, new_blocks, SLC_BLOCK, N_KV, HEAD_DIM)
    nv = jnp.pad(vs, padn).reshape(n, new_blocks, SLC_BLOCK, N_KV, HEAD_DIM)
    bi = jnp.arange(n)[:, None, None, None]
    hi = jnp.arange(N_KV)[None, None, :, None]

    def gather(idx):
        jp = jnp.minimum(idx, past_blocks - 1)
        phys = page_table[bi, jp // bpp]
        sub = jp % bpp
        jn = jnp.clip(idx - past_blocks, 0, new_blocks - 1)
        is_new = (idx >= past_blocks)[..., None, None]
        return (jnp.where(is_new, nk[bi, jn, :, hi], pk[phys, sub, :, hi]),
                jnp.where(is_new, nv[bi, jn, :, hi], pv[phys, sub, :, hi]))

    wb = ck_win.shape[1]
    wk = jnp.concatenate([ck_win, kw], axis=1)
    wv = jnp.concatenate([cv_win, vw], axis=1)
    wpos = PAST_LEN - wb + jnp.arange(wb + ds, dtype=jnp.int32)
    o = nsa_attend(q, pos, cmp_k, cmp_v, gather, nsb, wk, wv, wpos, ng)
    return o.reshape(n, ds, D_Q), wk[:, -wb:], wv[:, -wb:]


def moe(h, router_w, router_b, w_up, b_up, w_down, b_down):
    shp = h.shape
    t = h.reshape(-1, D_MODEL)
    n_tok = t.shape[0]
    logits = (t @ router_w).astype(jnp.float32) + router_b.astype(jnp.float32)
    top_v, top_e = lax.top_k(logits, TOP_K)
    gate = jax.nn.softmax(top_v, axis=-1)
    n_asg = n_tok * TOP_K
    blk = int(min(512, max(8, n_asg // N_EXPERTS)))
    n_blocks = -(-n_asg // blk) + N_EXPERTS
    e_flat = top_e.reshape(-1)
    tok_flat = jnp.repeat(jnp.arange(n_tok, dtype=jnp.int32), TOP_K)
    g_flat = gate.reshape(-1)
    order = jnp.argsort(e_flat)
    e_sorted = e_flat[order]
    counts = jnp.bincount(e_flat, length=N_EXPERTS)
    padded = (counts + blk - 1) // blk * blk
    pad_end = jnp.cumsum(padded)
    pad_start = pad_end - padded
    starts = jnp.cumsum(counts) - counts
    dest = pad_start[e_sorted] + jnp.arange(n_asg, dtype=jnp.int32) - starts[e_sorted]
    slot_tok = jnp.full((n_blocks * blk,), n_tok, jnp.int32).at[dest].set(tok_flat[order])
    slot_gate = jnp.zeros((n_blocks * blk,), jnp.float32).at[dest].set(g_flat[order])
    blk_expert = jnp.minimum(jnp.searchsorted(pad_end, jnp.arange(n_blocks, dtype=jnp.int32) * blk, side='right'),
                             N_EXPERTS - 1)
    t_pad = jnp.concatenate([t, jnp.zeros((1, D_MODEL), t.dtype)], axis=0)

    def run_block(args):
        tok, g, e = args
        gu = t_pad[tok] @ w_up[e] + b_up[e]
        a = jnp.minimum(gu[:, :D_FF], SWIGLU_LIMIT)
        u = jnp.clip(gu[:, D_FF:], -SWIGLU_LIMIT, SWIGLU_LIMIT)
        y = ((u + 1.0) * a * jax.nn.sigmoid(SWIGLU_ALPHA * a)) @ w_down[e] + b_down[e]
        return (y * g[:, None]).astype(t.dtype)

    ys = lax.map(run_block, (slot_tok.reshape(n_blocks, blk), slot_gate.reshape(n_blocks, blk), blk_expert))
    out = jnp.zeros((n_tok + 1, D_MODEL), t.dtype).at[slot_tok].add(ys.reshape(-1, D_MODEL))
    return out[:n_tok].reshape(shp)


def setup_inputs(seed: int = 0) -> dict:
    keys = list(jax.random.split(jax.random.key(seed), 64))

    def nrm(shape, scale):
        return jax.random.normal(keys.pop(), shape, jnp.float32) * scale

    def gain(n):
        return 1.0 + nrm((DEPTH, n), 0.05)

    n_pages = PAST_LEN // PAGE_SIZE
    n_phys = (DEC_BATCH * n_pages * 5) // 4
    wb = min(WINDOW, PAST_LEN)
    page_table = jax.random.permutation(keys.pop(), n_phys)[:DEC_BATCH * n_pages]
    page_table = page_table.reshape(DEC_BATCH, n_pages).astype(jnp.int32)
    a0 = jax.random.uniform(keys.pop(), (DEPTH, D_RNN), jnp.float32, 0.9, 0.999)
    n_in = 2 * D_RNN + D_Q + 6 * D_KV + 3 * N_HEADS + 2 * D_MODEL
    pool = (DEPTH, n_phys, PAGE_SIZE, N_KV, HEAD_DIM)
    win = (DEPTH, DEC_BATCH, wb, N_KV, HEAD_DIM)
    return {
        'x_prompt': nrm((BATCH, SEQ, D_MODEL), 1.0),
        'x_sample': nrm((DEC_BATCH, DEC_SEQ, D_MODEL), 1.0),
        'page_table': page_table,
        'cache_k_cmp': nrm(pool, 1.0),
        'cache_v_cmp': nrm(pool, 1.0),
        'cache_k_slc': nrm(pool, 1.0),
        'cache_v_slc': nrm(pool, 1.0),
        'cache_k_win': nrm(win, 1.0),
        'cache_v_win': nrm(win, 1.0),
        'state_h': nrm((DEPTH, DEC_BATCH, D_RNN), 0.3),
        'state_conv': nrm((DEPTH, DEC_BATCH, CONV_W - 1, D_RNN), 1.0),
        'g_mix': gain(D_MODEL),
        'w_in': nrm((DEPTH, D_MODEL, n_in), D_MODEL ** -0.5),
        'conv_w': nrm((DEPTH, CONV_W, D_RNN), CONV_W ** -0.5),
        'conv_b': nrm((DEPTH, D_RNN), 0.01),
        'lru_wa': nrm((DEPTH, LRU_BLOCKS, LRU_BW, LRU_BW), LRU_BW ** -0.5),
        'lru_ba': nrm((DEPTH, LRU_BLOCKS, LRU_BW), 0.01),
        'lru_wx': nrm((DEPTH, LRU_BLOCKS, LRU_BW, LRU_BW), LRU_BW ** -0.5),
        'lru_bx': nrm((DEPTH, LRU_BLOCKS, LRU_BW), 0.01),
        'lru_lambda': jnp.log(a0) - jnp.log1p(-a0),
        'g_q': gain(HEAD_DIM),
        'g_k_cmp': gain(HEAD_DIM),
        'g_k_slc': gain(HEAD_DIM),
        'g_k_win': gain(HEAD_DIM),
        'cmp_pos_k': nrm((DEPTH, CMP_LEN, HEAD_DIM), 0.02),
        'cmp_w1_k': nrm((DEPTH, CMP_LEN, HEAD_DIM, CMP_HID), (CMP_LEN * HEAD_DIM) ** -0.5),
        'cmp_w2_k': nrm((DEPTH, CMP_HID, HEAD_DIM), CMP_HID ** -0.5),
        'cmp_pos_v': nrm((DEPTH, CMP_LEN, HEAD_DIM), 0.02),
        'cmp_w1_v': nrm((DEPTH, CMP_LEN, HEAD_DIM, CMP_HID), (CMP_LEN * HEAD_DIM) ** -0.5),
        'cmp_w2_v': nrm((DEPTH, CMP_HID, HEAD_DIM), CMP_HID ** -0.5),
        'w_proj_a': nrm((DEPTH, D_RNN, D_MODEL), D_RNN ** -0.5),
        'w_proj_b': nrm((DEPTH, D_Q, D_MODEL), D_Q ** -0.5),
        'w_out': nrm((DEPTH, D_MODEL, D_MODEL), D_MODEL ** -0.5),
        'g_ffn': gain(D_MODEL),
        'router_w': nrm((DEPTH, D_MODEL, N_EXPERTS), D_MODEL ** -0.5),
        'router_b': nrm((DEPTH, N_EXPERTS), 0.01),
        'exp_w_up': nrm((DEPTH, N_EXPERTS, D_MODEL, 2 * D_FF), D_MODEL ** -0.5),
        'exp_b_up': nrm((DEPTH, N_EXPERTS, 2 * D_FF), 0.01),
        'exp_w_down': nrm((DEPTH, N_EXPERTS, D_FF, D_MODEL), D_FF ** -0.5),
        'exp_b_down': nrm((DEPTH, N_EXPERTS, D_MODEL), 0.01),
    }


def reference(x_prompt, x_sample, page_table, cache_k_cmp, cache_v_cmp, cache_k_slc, cache_v_slc,
              cache_k_win, cache_v_win, state_h, state_conv, g_mix, w_in, conv_w, conv_b, lru_wa, lru_ba,
              lru_wx, lru_bx, lru_lambda, g_q, g_k_cmp, g_k_slc, g_k_win, cmp_pos_k, cmp_w1_k, cmp_w2_k,
              cmp_pos_v, cmp_w1_v, cmp_w2_v, w_proj_a, w_proj_b, w_out, g_ffn, router_w, router_b,
              exp_w_up, exp_b_up, exp_w_down, exp_b_down):
    xp, xs = x_prompt, x_sample
    bp, sp = xp.shape[:2]
    pos_p = jnp.arange(sp, dtype=jnp.int32)
    pos_s = PAST_LEN + jnp.arange(xs.shape[1], dtype=jnp.int32)
    layer_states = []
    for l in range(DEPTH):
        cmp_w = (cmp_pos_k[l], cmp_w1_k[l], cmp_w2_k[l], cmp_pos_v[l], cmp_w1_v[l], cmp_w2_v[l])
        lru_w = (conv_w[l], conv_b[l], lru_wa[l], lru_ba[l], lru_wx[l], lru_bx[l], lru_lambda[l])
        ar, ag, q, kc, vc, ks, vs, kw, vw, ng, ga, gb = mixer_inputs(
            rms_norm(xp, g_mix[l]), pos_p, w_in[l], g_q[l], g_k_cmp[l], g_k_slc[l], g_k_win[l])
        y_lru, p_conv, p_h = rglru(ar, ag, jnp.zeros((bp, CONV_W - 1, D_RNN), xp.dtype),
                                   jnp.zeros((bp, D_RNN), jnp.float32), *lru_w)
        o_nsa = nsa_prompt(q, kc, vc, ks, vs, kw, vw, ng, *cmp_w)
        xp = xp + merge_branches(y_lru, o_nsa, ga, gb, w_proj_a[l], w_proj_b[l], w_out[l])
        wbp = min(WINDOW, sp)
        p_states = (kc, vc, ks, vs, kw[:, -wbp:], vw[:, -wbp:], p_h, p_conv)
        ar, ag, q, kc, vc, ks, vs, kw, vw, ng, ga, gb = mixer_inputs(
            rms_norm(xs, g_mix[l]), pos_s, w_in[l], g_q[l], g_k_cmp[l], g_k_slc[l], g_k_win[l])
        y_lru, s_conv_l, s_h_l = rglru(ar, ag, state_conv[l], state_h[l], *lru_w)
        o_nsa, s_kw_l, s_vw_l = nsa_sample(q, kc, vc, ks, vs, kw, vw, ng, page_table, cache_k_cmp[l],
                                           cache_v_cmp[l], cache_k_slc[l], cache_v_slc[l],
                                           cache_k_win[l], cache_v_win[l], *cmp_w)
        xs = xs + merge_branches(y_lru, o_nsa, ga, gb, w_proj_a[l], w_proj_b[l], w_out[l])
        s_states = (kc, vc, ks, vs, s_kw_l, s_vw_l, s_h_l, s_conv_l)
        moe_w = (router_w[l], router_b[l], exp_w_up[l], exp_b_up[l], exp_w_down[l], exp_b_down[l])
        xp = xp + moe(rms_norm(xp, g_ffn[l]), *moe_w)
        xs = xs + moe(rms_norm(xs, g_ffn[l]), *moe_w)
        layer_states.append(p_states + s_states)
    (p_k_cmp, p_v_cmp, p_k_slc, p_v_slc, p_k_win, p_v_win, p_h, p_conv,
     s_k_cmp, s_v_cmp, s_k_slc, s_v_slc, s_k_win, s_v_win, s_h, s_conv) = [jnp.stack(z) for z in zip(*layer_states)]
    return (xp, xs, p_k_cmp, p_v_cmp, p_k_slc, p_v_slc, p_k_win, p_v_win, p_h, p_conv,
            s_k_cmp, s_v_cmp, s_k_slc, s_v_slc, s_k_win, s_v_win, s_h, s_conv)
```

```python
import functools
import math

import jax
import jax.numpy as jnp
import numpy as np
from jax import lax
from jax.experimental import pallas as pl
from jax.experimental.pallas import tpu as pltpu

D_MODEL = 1024
PAST_LEN = 8192
PAGE_SIZE = 128
D_RNN = D_MODEL
LRU_BLOCKS = 16
LRU_BW = D_RNN // LRU_BLOCKS
CONV_W = 4
LRU_C = 8.0
N_HEADS = 8
HEAD_DIM = 64
N_KV = 2
GROUP = N_HEADS // N_KV
D_Q = N_HEADS * HEAD_DIM
D_KV = N_KV * HEAD_DIM
ROPE_DIM = HEAD_DIM // 4
ROPE_THETA = 500000.0
CMP_LEN = 32
CMP_STRIDE = 16
CMP_HID = 256
SLC_BLOCK = 64
N_SEL = 16
WINDOW = 512
Q_BLOCK = 64
N_EXPERTS = 32
TOP_K = 4
D_FF = D_MODEL
SWIGLU_LIMIT = 7.0
SWIGLU_ALPHA = 1.702
EPS = 1e-6
NEG = -1e30
BIG = 1e30

LANES = 128
VMEM_LIMIT = 56 * 1024 * 1024

N_NG = 3 * N_HEADS
C_AR = 0
C_Q = 2 * D_RNN
C_KV = C_Q + D_Q
C_NG = C_KV + 6 * D_KV
C_GAB = C_NG + LANES
N_IN_PAD = C_GAB + 2 * D_MODEL


def _seg_sum64(x2, ones_blk):
    hi = x2.astype(jnp.bfloat16)
    lo = (x2 - hi.astype(jnp.float32)).astype(jnp.bfloat16)
    return (jnp.dot(hi, ones_blk, preferred_element_type=jnp.float32)
            + jnp.dot(lo, ones_blk, preferred_element_type=jnp.float32))


def _head_norm_rope(y, gain, ones_blk, rc, rs1, rs2):
    ss = _seg_sum64(y * y, ones_blk)
    yn = y * lax.rsqrt(ss * (1.0 / HEAD_DIM) + EPS) * gain
    half = ROPE_DIM // 2
    return yn * rc + pltpu.roll(yn, LANES - half, 1) * rs1 + pltpu.roll(yn, half, 1) * rs2


def _proj_kernel(x_ref, gmix_ref, w_ref, ones_ref, gq_ref, gk_ref, rc_ref, rs1_ref, rs2_ref,
                 arag_ref, q_ref, kc_ref, vc_ref, ks_ref, vs_ref, kw_ref, vw_ref, ng_ref, gab_ref):
    x = x_ref[...]
    xn = x * lax.rsqrt(jnp.mean(x * x, axis=-1, keepdims=True) + EPS) * gmix_ref[...]
    xb = xn.astype(jnp.bfloat16)

    def proj(c0, width):
        return jnp.dot(xb, w_ref[:, c0:c0 + width], preferred_element_type=jnp.float32)

    arag_ref[...] = proj(C_AR, 2 * D_RNN)
    gab_ref[...] = proj(C_GAB, 2 * D_MODEL)
    ng_ref[...] = proj(C_NG, LANES)
    ones_blk = ones_ref[...]
    rc, rs1, rs2 = rc_ref[...], rs1_ref[...], rs2_ref[...]
    scale = HEAD_DIM ** -0.5
    for j in range(D_Q // LANES):
        y = proj(C_Q + j * LANES, LANES)
        yr = _head_norm_rope(y, gq_ref[...], ones_blk, rc, rs1, rs2)
        q_ref[:, j * LANES:(j + 1) * LANES] = (yr * scale).astype(q_ref.dtype)
    k_outs = (kc_ref, ks_ref, kw_ref)
    v_outs = (vc_ref, vs_ref, vw_ref)
    for j in range(3):
        y = proj(C_KV + 2 * j * LANES, LANES)
        k_outs[j][...] = _head_norm_rope(y, gk_ref[j:j + 1, :], ones_blk, rc, rs1, rs2)
        v_outs[j][...] = proj(C_KV + (2 * j + 1) * LANES, LANES)


def _rope_tables(pos):
    half = ROPE_DIM // 2
    inv = ROPE_THETA ** (-jnp.arange(half, dtype=jnp.float32) / half)
    ang = pos.astype(jnp.float32)[:, None] * inv[None, :]
    cos, sin = jnp.cos(ang), jnp.sin(ang)
    p = pos.shape[0]
    ones = jnp.ones((p, HEAD_DIM - ROPE_DIM), jnp.float32)
    zeros = jnp.zeros((p, HEAD_DIM - ROPE_DIM), jnp.float32)
    zh = jnp.zeros((p, half), jnp.float32)
    rc = jnp.concatenate([cos, cos, ones], axis=-1)
    rs1 = jnp.concatenate([-sin, zh, zeros], axis=-1)
    rs2 = jnp.concatenate([zh, sin, zeros], axis=-1)
    tile2 = lambda t: jnp.concatenate([t, t], axis=-1)
    return tile2(rc), tile2(rs1), tile2(rs2)


def _ones_block64():
    i = np.arange(LANES)
    return jnp.asarray((i[:, None] // HEAD_DIM) == (i[None, :] // HEAD_DIM), dtype=jnp.bfloat16)


def _input_projection(x2d, pos_rows, tm, g_mix, w_pad, g_q, g_k3):
    m = x2d.shape[0]
    p = pos_rows.shape[0]
    assert m % tm == 0 and p % tm == 0
    rc, rs1, rs2 = _rope_tables(pos_rows)
    n_tab = p // tm
    row = lambda i: (i, 0)
    const = lambda i: (0, 0)
    tab = lambda i: (i % n_tab, 0)
    f32 = jnp.float32
    out_shape = (
        jax.ShapeDtypeStruct((m, 2 * D_RNN), f32),
        jax.ShapeDtypeStruct((m, D_Q), jnp.bfloat16),
    ) + tuple(jax.ShapeDtypeStruct((m, D_KV), f32) for _ in range(6)) + (
        jax.ShapeDtypeStruct((m, LANES), f32),
        jax.ShapeDtypeStruct((m, 2 * D_MODEL), f32),
    )
    out_specs = (
        pl.BlockSpec((tm, 2 * D_RNN), row),
        pl.BlockSpec((tm, D_Q), row),
    ) + tuple(pl.BlockSpec((tm, D_KV), row) for _ in range(6)) + (
        pl.BlockSpec((tm, LANES), row),
        pl.BlockSpec((tm, 2 * D_MODEL), row),
    )
    return pl.pallas_call(
        _proj_kernel,
        grid=(m // tm,),
        in_specs=[
            pl.BlockSpec((tm, D_MODEL), row),
            pl.BlockSpec((1, D_MODEL), const),
            pl.BlockSpec((D_MODEL, N_IN_PAD), const),
            pl.BlockSpec((LANES, LANES), const),
            pl.BlockSpec((1, LANES), const),
            pl.BlockSpec((3, LANES), const),
            pl.BlockSpec((tm, LANES), tab),
            pl.BlockSpec((tm, LANES), tab),
            pl.BlockSpec((tm, LANES), tab),
        ],
        out_specs=out_specs,
        out_shape=out_shape,
        compiler_params=pltpu.CompilerParams(dimension_semantics=("arbitrary",), vmem_limit_bytes=VMEM_LIMIT),
        name="input_projection",
    )(x2d, g_mix.reshape(1, D_MODEL), w_pad, _ones_block64(), g_q, g_k3, rc, rs1, rs2)


def _masked_softmax(s, mask):
    s = jnp.where(mask, s.astype(jnp.float32), NEG)
    m = jnp.max(s, axis=-1, keepdims=True)
    p = jnp.where(mask, jnp.exp(s - m), 0.0)
    return p / jnp.maximum(jnp.sum(p, axis=-1, keepdims=True), 1e-30)


def _merge_branches(y_lru, o_nsa, ga, gb, w_proj_a, w_proj_b, w_out):
    return (jax.nn.sigmoid(ga) * (y_lru @ w_proj_a) + jax.nn.sigmoid(gb) * (o_nsa @ w_proj_b)) @ w_out


def _lin_combine(c1, c2):
    a1, b1 = c1
    a2, b2 = c2
    return a1 * a2, a2 * b1 + b2


def _rglru(xr, gate_in, conv_buf, h0, conv_w, conv_b, wa, ba, wx, bx, lam):
    n, s, _ = xr.shape
    xp = jnp.concatenate([conv_buf.astype(xr.dtype), xr], axis=1)
    xc = conv_b + sum(conv_w[k] * xp[:, k:k + s] for k in range(CONV_W))
    xb = xc.reshape(n, s, LRU_BLOCKS, LRU_BW)
    r = jax.nn.sigmoid((jnp.einsum('nsbi,bij->nsbj', xb, wa) + ba).astype(jnp.float32)).reshape(n, s, D_RNN)
    i = jax.nn.sigmoid((jnp.einsum('nsbi,bij->nsbj', xb, wx) + bx).astype(jnp.float32)).reshape(n, s, D_RNN)
    log_a = -LRU_C * r * jax.nn.softplus(-lam.astype(jnp.float32))
    a = jnp.exp(log_a)
    u = jnp.sqrt(-jnp.expm1(2.0 * log_a)) * (i * xc.astype(jnp.float32))
    u = u.at[:, 0].add(a[:, 0] * h0.astype(jnp.float32))
    _, h = lax.associative_scan(_lin_combine, (a, u), axis=1)
    y = (h * jax.nn.gelu(gate_in.astype(jnp.float32))).astype(xr.dtype)
    return y, xp[:, s:], h[:, -1]


def _compress(rows, pos_emb, w1, w2):
    n, l = rows.shape[:2]
    r_ = CMP_LEN // CMP_STRIDE
    nch = l // CMP_STRIDE
    ch = rows[:, :nch * CMP_STRIDE].reshape(n, nch, CMP_STRIDE, N_KV, HEAD_DIM)
    w1b = w1.reshape(r_, CMP_STRIDE, HEAD_DIM, CMP_HID)
    part = jnp.einsum('nclhd,pldf->pnchf', ch, w1b)
    nc = nch - r_ + 1
    hid = jnp.einsum('ld,ldf->f', pos_emb, w1) + sum(part[p, :, p:p + nc] for p in range(r_))
    return jnp.einsum('nchf,fd->nchd', jax.nn.gelu(hid), w2)


def _block_scores(p_cmp, nsb):
    nc = p_cmp.shape[-1]
    ratio = SLC_BLOCK // CMP_STRIDE
    ovl = CMP_LEN // CMP_STRIDE - 1
    widths = [(0, 0)] * (p_cmp.ndim - 1) + [(ovl, ratio * (nsb + 1) - nc - ovl)]
    pp = jnp.pad(p_cmp, widths).reshape(*p_cmp.shape[:-1], nsb + 1, ratio)
    return pp[..., :nsb, :].sum(-1) + pp[..., 1:, :ovl].sum(-1)


def _select_blocks(scores, pos, nsb):
    j = jnp.arange(nsb, dtype=jnp.int32)
    cur = (pos // SLC_BLOCK)[:, None, None]
    forced = (j == 0) | (j == cur) | (j == cur - 1)
    s = jnp.where(forced, BIG, jnp.where(j <= cur, scores, NEG))
    top, idx = lax.top_k(s, min(N_SEL, nsb))
    return idx, top > NEG / 2


def _nsa_attend(q, pos, cmp_k, cmp_v, gather_slc, nsb, win_k, win_v, win_pos, gates):
    n, qn = q.shape[:2]
    nc = cmp_k.shape[1]
    s = jnp.einsum('nqhgd,nchd->nqhgc', q, cmp_k)
    cmp_end = jnp.arange(nc, dtype=jnp.int32) * CMP_STRIDE + CMP_LEN - 1
    p_c = _masked_softmax(s, cmp_end <= pos[:, None, None, None])
    o_cmp = jnp.einsum('nqhgc,nchd->nqhgd', p_c.astype(cmp_v.dtype), cmp_v)
    idx, valid = _select_blocks(_block_scores(p_c.sum(3), nsb), pos, nsb)
    k_g, v_g = gather_slc(idx)
    kk = idx.shape[-1] * SLC_BLOCK
    kpos = idx[..., None] * SLC_BLOCK + jnp.arange(SLC_BLOCK, dtype=jnp.int32)
    smask = (valid[..., None] & (kpos <= pos[:, None, None, None])).reshape(n, qn, N_KV, 1, kk)
    s = jnp.einsum('nqhgd,nqhkd->nqhgk', q, k_g.reshape(n, qn, N_KV, kk, HEAD_DIM))
    p_s = _masked_softmax(s, smask)
    o_slc = jnp.einsum('nqhgk,nqhkd->nqhgd', p_s.astype(v_g.dtype), v_g.reshape(n, qn, N_KV, kk, HEAD_DIM))
    s = jnp.einsum('nqhgd,nkhd->nqhgk', q, win_k)
    dpos = pos[:, None] - win_pos[None, :]
    wmask = (dpos >= 0) & (dpos < WINDOW) & (win_pos >= 0)[None, :]
    p_w = _masked_softmax(s, wmask[:, None, None, :])
    o_win = jnp.einsum('nqhgk,nkhd->nqhgd', p_w.astype(win_v.dtype), win_v)
    g = jax.nn.sigmoid(gates)
    return g[..., 0:1] * o_cmp + g[..., 1:2] * o_slc + g[..., 2:3] * o_win


def _nsa_prompt(q, kc, vc, ks, vs, kw, vw, ng, pos_k, w1_k, w2_k, pos_v, w1_v, w2_v):
    n, s = q.shape[:2]
    cmp_k = _compress(kc, pos_k, w1_k, w2_k)
    cmp_v = _compress(vc, pos_v, w1_v, w2_v)
    nsb = s // SLC_BLOCK
    kb = ks.reshape(n, nsb, SLC_BLOCK, N_KV, HEAD_DIM)
    vb = vs.reshape(n, nsb, SLC_BLOCK, N_KV, HEAD_DIM)
    bi = jnp.arange(n)[:, None, None, None]
    hi = jnp.arange(N_KV)[None, None, :, None]

    def gather(idx):
        return kb[bi, idx, :, hi], vb[bi, idx, :, hi]

    pad = ((0, 0), (WINDOW, 0), (0, 0), (0, 0))
    kw_pad, vw_pad = jnp.pad(kw, pad), jnp.pad(vw, pad)

    def block(i):
        start = i * Q_BLOCK
        sl = lambda t, size: lax.dynamic_slice_in_dim(t, start, size, axis=1)
        pos = start + jnp.arange(Q_BLOCK, dtype=jnp.int32)
        wpos = start - WINDOW + jnp.arange(WINDOW + Q_BLOCK, dtype=jnp.int32)
        return _nsa_attend(sl(q, Q_BLOCK), pos, cmp_k, cmp_v, gather, nsb,
                           sl(kw_pad, WINDOW + Q_BLOCK), sl(vw_pad, WINDOW + Q_BLOCK), wpos, sl(ng, Q_BLOCK))

    o = lax.map(block, jnp.arange(s // Q_BLOCK, dtype=jnp.int32))
    return jnp.moveaxis(o, 0, 1).reshape(n, s, D_Q)


def _nsa_sample(q, kc, vc, ks, vs, kw, vw, ng, page_table, ck_cmp, cv_cmp, ck_slc, cv_slc, ck_win, cv_win,
                pos_k, w1_k, w2_k, pos_v, w1_v, w2_v):
    n, ds = q.shape[:2]
    pos = PAST_LEN + jnp.arange(ds, dtype=jnp.int32)

    def past(pool):
        return pool[page_table].reshape(n, PAST_LEN, N_KV, HEAD_DIM)

    cmp_k = _compress(jnp.concatenate([past(ck_cmp), kc], axis=1), pos_k, w1_k, w2_k)
    cmp_v = _compress(jnp.concatenate([past(cv_cmp), vc], axis=1), pos_v, w1_v, w2_v)
    past_blocks = PAST_LEN // SLC_BLOCK
    new_blocks = -(-ds // SLC_BLOCK)
    nsb = past_blocks + new_blocks
    bpp = PAGE_SIZE // SLC_BLOCK
    view = lambda pool: pool.reshape(pool.shape[0], bpp, SLC_BLOCK, N_KV, HEAD_DIM)
    pk, pv = view(ck_slc), view(cv_slc)
    padn = ((0, 0), (0, new_blocks * SLC_BLOCK - ds), (0, 0), (0, 0))
    nk = jnp.pad(ks, padn).reshape(n, new_blocks, SLC_BLOCK, N_KV, HEAD_DIM)
    nv = jnp.pad(vs, padn).reshape(n, new_blocks, SLC_BLOCK, N_KV, HEAD_DIM)
    bi = jnp.arange(n)[:, None, None, None]
    hi = jnp.arange(N_KV)[None, None, :, None]

    def gather(idx):
        jp = jnp.minimum(idx, past_blocks - 1)
        phys = page_table[bi, jp // bpp]
        sub = jp % bpp
        jn = jnp.clip(idx - past_blocks, 0, new_blocks - 1)
        is_new = (idx >= past_blocks)[..., None, None]
        return (jnp.where(is_new, nk[bi, jn, :, hi], pk[phys, sub, :, hi]),
                jnp.where(is_new, nv[bi, jn, :, hi], pv[phys, sub, :, hi]))

    wb = ck_win.shape[1]
    wk = jnp.concatenate([ck_win, kw], axis=1)
    wv = jnp.concatenate([cv_win, vw], axis=1)
    wpos = PAST_LEN - wb + jnp.arange(wb + ds, dtype=jnp.int32)
    o = _nsa_attend(q, pos, cmp_k, cmp_v, gather, nsb, wk, wv, wpos, ng)
    return o.reshape(n, ds, D_Q), wk[:, -wb:], wv[:, -wb:]


def _rms_norm(x, g):
    xf = x.astype(jnp.float32)
    y = xf * lax.rsqrt(jnp.mean(xf * xf, axis=-1, keepdims=True) + EPS)
    return (y * g.astype(jnp.float32)).astype(x.dtype)


def _moe(h, router_w, router_b, w_up, b_up, w_down, b_down):
    shp = h.shape
    t = h.reshape(-1, D_MODEL)
    n_tok = t.shape[0]
    logits = (t @ router_w).astype(jnp.float32) + router_b.astype(jnp.float32)
    top_v, top_e = lax.top_k(logits, TOP_K)
    gate = jax.nn.softmax(top_v, axis=-1)
    n_asg = n_tok * TOP_K
    blk = int(min(512, max(8, n_asg // N_EXPERTS)))
    n_blocks = -(-n_asg // blk) + N_EXPERTS
    e_flat = top_e.reshape(-1)
    tok_flat = jnp.repeat(jnp.arange(n_tok, dtype=jnp.int32), TOP_K)
    g_flat = gate.reshape(-1)
    order = jnp.argsort(e_flat)
    e_sorted = e_flat[order]
    counts = jnp.bincount(e_flat, length=N_EXPERTS)
    padded = (counts + blk - 1) // blk * blk
    pad_end = jnp.cumsum(padded)
    pad_start = pad_end - padded
    starts = jnp.cumsum(counts) - counts
    dest = pad_start[e_sorted] + jnp.arange(n_asg, dtype=jnp.int32) - starts[e_sorted]
    slot_tok = jnp.full((n_blocks * blk,), n_tok, jnp.int32).at[dest].set(tok_flat[order])
    slot_gate = jnp.zeros((n_blocks * blk,), jnp.float32).at[dest].set(g_flat[order])
    blk_expert = jnp.minimum(jnp.searchsorted(pad_end, jnp.arange(n_blocks, dtype=jnp.int32) * blk, side='right'),
                             N_EXPERTS - 1)
    t_pad = jnp.concatenate([t, jnp.zeros((1, D_MODEL), t.dtype)], axis=0)

    def run_block(args):
        tok, g, e = args
        gu = t_pad[tok] @ w_up[e] + b_up[e]
        a = jnp.minimum(gu[:, :D_FF], SWIGLU_LIMIT)
        u = jnp.clip(gu[:, D_FF:], -SWIGLU_LIMIT, SWIGLU_LIMIT)
        y = ((u + 1.0) * a * jax.nn.sigmoid(SWIGLU_ALPHA * a)) @ w_down[e] + b_down[e]
        return (y * g[:, None]).astype(t.dtype)

    ys = lax.map(run_block, (slot_tok.reshape(n_blocks, blk), slot_gate.reshape(n_blocks, blk), blk_expert))
    out = jnp.zeros((n_tok + 1, D_MODEL), t.dtype).at[slot_tok].add(ys.reshape(-1, D_MODEL))
    return out[:n_tok].reshape(shp)


def _pad_w_in(w):
    cut = C_NG + N_NG
    pad = jnp.zeros((D_MODEL, LANES - N_NG), w.dtype)
    return jnp.concatenate([w[:, :cut], pad, w[:, cut:]], axis=1).astype(jnp.bfloat16)


def _project(x, pos_rows, tm, g_mix, w_pad, g_q, g_k3):
    n, s, _ = x.shape
    outs = _input_projection(x.reshape(n * s, D_MODEL), pos_rows, tm, g_mix, w_pad, g_q, g_k3)
    arag, q, kc, vc, ks, vs, kw, vw, ng, gab = outs
    heads = lambda t: t.reshape(n, s, N_KV, HEAD_DIM)
    ar = arag[:, :D_RNN].reshape(n, s, D_RNN)
    ag = arag[:, D_RNN:].reshape(n, s, D_RNN)
    q = q.reshape(n, s, N_KV, GROUP, HEAD_DIM)
    ng = ng[:, :N_NG].reshape(n, s, N_KV, GROUP, 3)
    ga = gab[:, :D_MODEL].reshape(n, s, D_MODEL)
    gb = gab[:, D_MODEL:].reshape(n, s, D_MODEL)
    return ar, ag, q, heads(kc), heads(vc), heads(ks), heads(vs), heads(kw), heads(vw), ng, ga, gb


def kernel(x_prompt, x_sample, page_table, cache_k_cmp, cache_v_cmp, cache_k_slc, cache_v_slc,
           cache_k_win, cache_v_win, state_h, state_conv, g_mix, w_in, conv_w, conv_b, lru_wa, lru_ba,
           lru_wx, lru_bx, lru_lambda, g_q, g_k_cmp, g_k_slc, g_k_win, cmp_pos_k, cmp_w1_k, cmp_w2_k,
           cmp_pos_v, cmp_w1_v, cmp_w2_v, w_proj_a, w_proj_b, w_out, g_ffn, router_w, router_b,
           exp_w_up, exp_b_up, exp_w_down, exp_b_down):
    xp, xs = x_prompt, x_sample
    bp, sp = xp.shape[:2]
    bs, ss = xs.shape[:2]
    pos_p = jnp.arange(sp, dtype=jnp.int32)
    pos_s = PAST_LEN + jnp.arange(ss, dtype=jnp.int32)
    depth = w_in.shape[0]
    layer_states = []
    tile2 = lambda g: jnp.concatenate([g, g], axis=-1)
    for l in range(depth):
        cmp_w = (cmp_pos_k[l], cmp_w1_k[l], cmp_w2_k[l], cmp_pos_v[l], cmp_w1_v[l], cmp_w2_v[l])
        lru_w = (conv_w[l], conv_b[l], lru_wa[l], lru_ba[l], lru_wx[l], lru_bx[l], lru_lambda[l])
        w_pad = _pad_w_in(w_in[l])
        gq2 = tile2(g_q[l]).reshape(1, LANES)
        gk3 = jnp.stack([tile2(g_k_cmp[l]), tile2(g_k_slc[l]), tile2(g_k_win[l])])
        ar, ag, q, kc, vc, ks, vs, kw, vw, ng, ga, gb = _project(xp, pos_p, 256, g_mix[l], w_pad, gq2, gk3)
        y_lru, p_conv, p_h = _rglru(ar, ag, jnp.zeros((bp, CONV_W - 1, D_RNN), xp.dtype),
                                    jnp.zeros((bp, D_RNN), jnp.float32), *lru_w)
        o_nsa = _nsa_prompt(q.astype(jnp.float32), kc, vc, ks, vs, kw, vw, ng, *cmp_w)
        xp = xp + _merge_branches(y_lru, o_nsa, ga, gb, w_proj_a[l], w_proj_b[l], w_out[l])
        wbp = min(WINDOW, sp)
        p_states = (kc, vc, ks, vs, kw[:, -wbp:], vw[:, -wbp:], p_h, p_conv)
        pos_rows_s = jnp.broadcast_to(pos_s[None, :], (bs, ss)).reshape(-1)
        ar, ag, q, kc, vc, ks, vs, kw, vw, ng, ga, gb = _project(
            xs, pos_rows_s, bs * ss, g_mix[l], w_pad, gq2, gk3)
        y_lru, s_conv_l, s_h_l = _rglru(ar, ag, state_conv[l], state_h[l], *lru_w)
        o_nsa, s_kw_l, s_vw_l = _nsa_sample(q.astype(jnp.float32), kc, vc, ks, vs, kw, vw, ng, page_table,
                                            cache_k_cmp[l], cache_v_cmp[l], cache_k_slc[l], cache_v_slc[l],
                                            cache_k_win[l], cache_v_win[l], *cmp_w)
        xs = xs + _merge_branches(y_lru, o_nsa, ga, gb, w_proj_a[l], w_proj_b[l], w_out[l])
        s_states = (kc, vc, ks, vs, s_kw_l, s_vw_l, s_h_l, s_conv_l)
        moe_w = (router_w[l], router_b[l], exp_w_up[l], exp_b_up[l], exp_w_down[l], exp_b_down[l])
        xp = xp + _moe(_rms_norm(xp, g_ffn[l]), *moe_w)
        xs = xs + _moe(_rms_norm(xs, g_ffn[l]), *moe_w)
        layer_states.append(p_states + s_states)
    (p_k_cmp, p_v_cmp, p_k_slc, p_v_slc, p_k_win, p_v_win, p_h, p_conv,
     s_k_cmp, s_v_cmp, s_k_slc, s_v_slc, s_k_win, s_v_win, s_h, s_conv) = [jnp.stack(z) for z in zip(*layer_states)]
    return (xp, xs, p_k_cmp, p_v_cmp, p_k_slc, p_v_slc, p_k_win, p_v_win, p_h, p_conv,
            s_k_cmp, s_v_cmp, s_k_slc, s_v_slc, s_k_win, s_v_win, s_h, s_conv)
```

```python
import functools
import math

import jax
import jax.numpy as jnp
import numpy as np
from jax import lax
from jax.experimental import pallas as pl
from jax.experimental.pallas import tpu as pltpu

D_MODEL = 1024
PAST_LEN = 8192
PAGE_SIZE = 128
D_RNN = D_MODEL
LRU_BLOCKS = 16
LRU_BW = D_RNN // LRU_BLOCKS
CONV_W = 4
LRU_C = 8.0
N_HEADS = 8
HEAD_DIM = 64
N_KV = 2
GROUP = N_HEADS // N_KV
D_Q = N_HEADS * HEAD_DIM
D_KV = N_KV * HEAD_DIM
ROPE_DIM = HEAD_DIM // 4
ROPE_THETA = 500000.0
CMP_LEN = 32
CMP_STRIDE = 16
CMP_HID = 256
SLC_BLOCK = 64
N_SEL = 16
WINDOW = 512
Q_BLOCK = 64
N_EXPERTS = 32
TOP_K = 4
D_FF = D_MODEL
SWIGLU_LIMIT = 7.0
SWIGLU_ALPHA = 1.702
EPS = 1e-6
NEG = -1e30
BIG = 1e30

LANES = 128
VMEM_LIMIT = 56 * 1024 * 1024

N_NG = 3 * N_HEADS
D_QP = N_HEADS * LANES
C_AR = 0
C_Q = 2 * D_RNN
C_KV = C_Q + D_QP
C_NG = C_KV + 6 * D_KV
C_GAB = C_NG + LANES
N_IN_PAD = C_GAB + 2 * D_MODEL


def _seg_sum64(x2, ones_blk):
    hi = x2.astype(jnp.bfloat16)
    lo = (x2 - hi.astype(jnp.float32)).astype(jnp.bfloat16)
    return (jnp.dot(hi, ones_blk, preferred_element_type=jnp.float32)
            + jnp.dot(lo, ones_blk, preferred_element_type=jnp.float32))


def _head_norm_rope(y, gain, ones_blk, rc, rs1, rs2):
    ss = _seg_sum64(y * y, ones_blk)
    yn = y * lax.rsqrt(ss * (1.0 / HEAD_DIM) + EPS) * gain
    half = ROPE_DIM // 2
    return yn * rc + pltpu.roll(yn, LANES - half, 1) * rs1 + pltpu.roll(yn, half, 1) * rs2


def _proj_kernel(x_ref, gmix_ref, w_ref, ones_ref, gq_ref, gk_ref, rc_ref, rs1_ref, rs2_ref,
                 arag_ref, q_ref, kc_ref, vc_ref, ks_ref, vs_ref, kw_ref, vw_ref, ng_ref, gab_ref):
    x = x_ref[...]
    xn = x * lax.rsqrt(jnp.mean(x * x, axis=-1, keepdims=True) + EPS) * gmix_ref[...]
    xb = xn.astype(jnp.bfloat16)

    def proj(c0, width):
        return jnp.dot(xb, w_ref[:, c0:c0 + width], preferred_element_type=jnp.float32)

    arag_ref[...] = proj(C_AR, 2 * D_RNN)
    gab_ref[...] = proj(C_GAB, 2 * D_MODEL)
    ng_ref[...] = proj(C_NG, LANES)
    ones_blk = ones_ref[...]
    rc, rs1, rs2 = rc_ref[...], rs1_ref[...], rs2_ref[...]
    scale = HEAD_DIM ** -0.5
    for j in range(N_HEADS):
        y = proj(C_Q + j * LANES, LANES)
        yr = _head_norm_rope(y, gq_ref[...], ones_blk, rc, rs1, rs2)
        q_ref[:, j * LANES:(j + 1) * LANES] = (yr * scale).astype(q_ref.dtype)
    k_outs = (kc_ref, ks_ref, kw_ref)
    v_outs = (vc_ref, vs_ref, vw_ref)
    for j in range(3):
        y = proj(C_KV + 2 * j * LANES, LANES)
        k_outs[j][...] = _head_norm_rope(y, gk_ref[j:j + 1, :], ones_blk, rc, rs1, rs2)
        v_outs[j][...] = proj(C_KV + (2 * j + 1) * LANES, LANES)


def _rope_tables(pos):
    half = ROPE_DIM // 2
    inv = ROPE_THETA ** (-jnp.arange(half, dtype=jnp.float32) / half)
    ang = pos.astype(jnp.float32)[:, None] * inv[None, :]
    cos, sin = jnp.cos(ang), jnp.sin(ang)
    p = pos.shape[0]
    ones = jnp.ones((p, HEAD_DIM - ROPE_DIM), jnp.float32)
    zeros = jnp.zeros((p, HEAD_DIM - ROPE_DIM), jnp.float32)
    zh = jnp.zeros((p, half), jnp.float32)
    rc = jnp.concatenate([cos, cos, ones], axis=-1)
    rs1 = jnp.concatenate([-sin, zh, zeros], axis=-1)
    rs2 = jnp.concatenate([zh, sin, zeros], axis=-1)
    tile2 = lambda t: jnp.concatenate([t, t], axis=-1)
    return tile2(rc), tile2(rs1), tile2(rs2)


def _ones_block64():
    i = np.arange(LANES)
    return jnp.asarray((i[:, None] // HEAD_DIM) == (i[None, :] // HEAD_DIM), dtype=jnp.bfloat16)


def _input_projection(x2d, pos_rows, tm, g_mix, w_pad, g_q, g_k3):
    m = x2d.shape[0]
    p = pos_rows.shape[0]
    assert m % tm == 0 and p % tm == 0
    rc, rs1, rs2 = _rope_tables(pos_rows)
    n_tab = p // tm
    row = lambda i: (i, 0)
    const = lambda i: (0, 0)
    tab = lambda i: (i % n_tab, 0)
    f32 = jnp.float32
    out_shape = (
        jax.ShapeDtypeStruct((m, 2 * D_RNN), f32),
        jax.ShapeDtypeStruct((m, D_QP), jnp.bfloat16),
    ) + tuple(jax.ShapeDtypeStruct((m, D_KV), f32) for _ in range(6)) + (
        jax.ShapeDtypeStruct((m, LANES), f32),
        jax.ShapeDtypeStruct((m, 2 * D_MODEL), f32),
    )
    out_specs = (
        pl.BlockSpec((tm, 2 * D_RNN), row),
        pl.BlockSpec((tm, D_QP), row),
    ) + tuple(pl.BlockSpec((tm, D_KV), row) for _ in range(6)) + (
        pl.BlockSpec((tm, LANES), row),
        pl.BlockSpec((tm, 2 * D_MODEL), row),
    )
    return pl.pallas_call(
        _proj_kernel,
        grid=(m // tm,),
        in_specs=[
            pl.BlockSpec((tm, D_MODEL), row),
            pl.BlockSpec((1, D_MODEL), const),
            pl.BlockSpec((D_MODEL, N_IN_PAD), const),
            pl.BlockSpec((LANES, LANES), const),
            pl.BlockSpec((1, LANES), const),
            pl.BlockSpec((3, LANES), const),
            pl.BlockSpec((tm, LANES), tab),
            pl.BlockSpec((tm, LANES), tab),
            pl.BlockSpec((tm, LANES), tab),
        ],
        out_specs=out_specs,
        out_shape=out_shape,
        compiler_params=pltpu.CompilerParams(dimension_semantics=("arbitrary",), vmem_limit_bytes=VMEM_LIMIT),
        name="input_projection",
    )(x2d, g_mix.reshape(1, D_MODEL), w_pad, _ones_block64(), g_q, g_k3, rc, rs1, rs2)


SLC_CHUNK = 512
WIN_SPAN = WINDOW + LANES
ROWS = GROUP * Q_BLOCK


def _dot_nt(a, b):
    return lax.dot_general(a, b, (((1,), (1,)), ((), ())), preferred_element_type=jnp.float32)


def _softmax_rows(s, mask):
    s = jnp.where(mask, s, NEG)
    m = jnp.max(s, axis=-1, keepdims=True)
    p = jnp.where(mask, jnp.exp(s - m), 0.0)
    return p / jnp.maximum(jnp.sum(p, axis=-1, keepdims=True), 1e-30)


def _split3_bf16(x):
    hi = x.astype(jnp.bfloat16)
    r1 = x - hi.astype(jnp.float32)
    mid = r1.astype(jnp.bfloat16)
    lo = (r1 - mid.astype(jnp.float32)).astype(jnp.bfloat16)
    return hi, mid, lo


def _nsa_prompt_kernel(q_ref, ksa_ref, vs_ref, kw_ref, vw_ref, ck_ref, cv_ref, ng_ref, mmt_ref,
                       o_ref, sv_scr):
    f32, bf16 = jnp.float32, jnp.bfloat16
    qb = pl.program_id(1)
    row = lax.broadcasted_iota(jnp.int32, (ROWS, 1), 0)
    pos_t = qb * Q_BLOCK + (row & (Q_BLOCK - 1))
    ncp = ck_ref.shape[1]

    qh = [jnp.concatenate([q_ref[0, :, (h * GROUP + g) * LANES:(h * GROUP + g + 1) * LANES]
                           for g in range(GROUP)], axis=0) for h in range(N_KV)]

    ck, cv = ck_ref[0], cv_ref[0]
    c_end = lax.broadcasted_iota(jnp.int32, (1, ncp), 1) * CMP_STRIDE + (CMP_LEN - 1)
    cmask = c_end <= pos_t
    o_cmp, psum = [], []
    for h in range(N_KV):
        p = _softmax_rows(_dot_nt(qh[h], ck), cmask)
        o_cmp.append(jnp.dot(p.astype(bf16), cv, preferred_element_type=f32))
        psum.append(p[0:Q_BLOCK] + p[Q_BLOCK:2 * Q_BLOCK] + p[2 * Q_BLOCK:3 * Q_BLOCK] + p[3 * Q_BLOCK:])
    ps = jnp.concatenate(psum, axis=0)

    mmt = mmt_ref[...]
    sc = sum(_dot_nt(mmt, part) for part in _split3_bf16(ps))[0:Q_BLOCK]
    jio = lax.broadcasted_iota(jnp.int32, (Q_BLOCK, LANES), 0)
    forced = (jio == 0) | (jio == qb) | (jio == qb - 1)
    sv = jnp.where(forced, BIG, jnp.where(jio <= qb, sc, NEG))
    sv_scr[...] = sv

    def rank_body(i, cnt):
        r = sv_scr[pl.ds(i, 1), :]
        beats = (r > sv) | ((r == sv) & (jio > i))
        return cnt + jnp.where(beats, 1.0, 0.0)

    cnt = lax.fori_loop(0, qb + 1, rank_body, jnp.zeros((Q_BLOCK, LANES), f32))
    bias_lo = jnp.where((cnt < N_SEL) & (sv > NEG / 2), 0.0, NEG)
    lane = lax.broadcasted_iota(jnp.int32, (Q_BLOCK, LANES), 1)
    bias_t = jnp.concatenate([jnp.where(lane >= Q_BLOCK, bias_lo, 0.0),
                              jnp.where(lane < Q_BLOCK, bias_lo, 0.0)], axis=0)
    bias = bias_t.T.astype(bf16)

    kio = lax.broadcasted_iota(jnp.int32, (1, SLC_CHUNK), 1)
    wio = lax.broadcasted_iota(jnp.int32, (1, WIN_SPAN), 1)
    wbase = pl.multiple_of((jnp.maximum(qb - WINDOW // Q_BLOCK, 0) // 2) * LANES, LANES)
    dpos = pos_t - (wbase + wio)
    wmask = (dpos >= 0) & (dpos < WINDOW)
    n_chunks = qb // (SLC_CHUNK // SLC_BLOCK) + 1
    sig = jax.nn.sigmoid(ng_ref[0])
    out_h = []
    for h in range(N_KV):
        bh = bias[h * Q_BLOCK:(h + 1) * Q_BLOCK]
        qa = qh[h] + jnp.concatenate([bh] * GROUP, axis=0)

        def chunk(c, carry, causal):
            m, l, acc = carry
            k0 = pl.multiple_of(c * SLC_CHUNK, SLC_CHUNK)
            s = _dot_nt(qa, ksa_ref[0, h, pl.ds(k0, SLC_CHUNK), :])
            if causal:
                s = jnp.where(k0 + kio <= pos_t, s, NEG)
            m_new = jnp.maximum(m, jnp.max(s, axis=-1, keepdims=True))
            alpha = jnp.exp(m - m_new)
            p = jnp.exp(s - m_new)
            l = alpha * l + jnp.sum(p, axis=-1, keepdims=True)
            acc = alpha * acc + jnp.dot(p.astype(bf16), vs_ref[0, pl.ds(k0, SLC_CHUNK), :],
                                        preferred_element_type=f32)
            return m_new, l, acc

        init = (jnp.full((ROWS, 1), NEG, f32), jnp.zeros((ROWS, 1), f32), jnp.zeros((ROWS, LANES), f32))
        carry = lax.fori_loop(0, n_chunks - 1, lambda c, cr: chunk(c, cr, False), init)
        _, l, acc = chunk(n_chunks - 1, carry, True)
        o_slc = acc / jnp.maximum(l, 1e-30)

        pw = _softmax_rows(_dot_nt(qh[h], kw_ref[0, pl.ds(wbase, WIN_SPAN), :]), wmask)
        o_win = jnp.dot(pw.astype(bf16), vw_ref[0, pl.ds(wbase, WIN_SPAN), :], preferred_element_type=f32)

        outs = []
        for g in range(GROUP):
            rs = slice(g * Q_BLOCK, (g + 1) * Q_BLOCK)
            c0 = (h * GROUP + g) * 3
            outs.append(sig[:, c0:c0 + 1] * o_cmp[h][rs] + sig[:, c0 + 1:c0 + 2] * o_slc[rs]
                        + sig[:, c0 + 2:c0 + 3] * o_win[rs])
        out_h.append(outs)
    for g in range(GROUP):
        o_ref[0, :, g * LANES:(g + 1) * LANES] = jnp.where(lane < HEAD_DIM, out_h[0][g], out_h[1][g]).astype(o_ref.dtype)


def _block_score_matrix(ncp):
    ratio = SLC_BLOCK // CMP_STRIDE
    ovl = CMP_LEN // CMP_STRIDE - 1
    j = np.arange(LANES)[:, None]
    c = np.arange(ncp)[None, :]
    m = (c >= ratio * j - ovl) & (c <= ratio * j + ratio - 1) & (j < Q_BLOCK) & (c < ncp - 1)
    return jnp.asarray(m, dtype=jnp.bfloat16)


def _nsa_prompt_attention(q_pad, ks, vs, kw, vw, cmp_k, cmp_v, ng):
    b, s, _ = q_pad.shape
    assert s % SLC_CHUNK == 0 and s // SLC_BLOCK <= Q_BLOCK and s >= WIN_SPAN and Q_BLOCK == SLC_BLOCK
    bf16 = jnp.bfloat16
    ncp = cmp_k.shape[1]
    onehot = jax.nn.one_hot(jnp.arange(s) // SLC_BLOCK, Q_BLOCK, dtype=bf16)
    onehot = jnp.broadcast_to(onehot[None], (b, s, Q_BLOCK))
    ksb = ks.astype(bf16)
    ksa = jnp.stack([jnp.concatenate([ksb[:, :, :HEAD_DIM], onehot], axis=-1),
                     jnp.concatenate([onehot, ksb[:, :, HEAD_DIM:]], axis=-1)], axis=1)
    per_b = lambda bi, i: (bi, 0, 0)
    return pl.pallas_call(
        _nsa_prompt_kernel,
        grid=(b, s // Q_BLOCK),
        in_specs=[
            pl.BlockSpec((1, Q_BLOCK, D_QP), lambda bi, i: (bi, i, 0)),
            pl.BlockSpec((1, N_KV, s, LANES), lambda bi, i: (bi, 0, 0, 0)),
            pl.BlockSpec((1, s, LANES), per_b),
            pl.BlockSpec((1, s, LANES), per_b),
            pl.BlockSpec((1, s, LANES), per_b),
            pl.BlockSpec((1, ncp, LANES), per_b),
            pl.BlockSpec((1, ncp, LANES), per_b),
            pl.BlockSpec((1, Q_BLOCK, LANES), lambda bi, i: (bi, i, 0)),
            pl.BlockSpec((LANES, ncp), lambda bi, i: (0, 0)),
        ],
        out_specs=pl.BlockSpec((1, Q_BLOCK, D_Q), lambda bi, i: (bi, i, 0)),
        out_shape=jax.ShapeDtypeStruct((b, s, D_Q), bf16),
        scratch_shapes=[pltpu.VMEM((Q_BLOCK, LANES), jnp.float32)],
        compiler_params=pltpu.CompilerParams(dimension_semantics=("arbitrary", "arbitrary"),
                                             vmem_limit_bytes=VMEM_LIMIT),
        name="nsa_prompt_attention",
    )(q_pad, ksa, vs.astype(bf16), kw.astype(bf16), vw.astype(bf16), cmp_k, cmp_v, ng, _block_score_matrix(ncp))


CHUNK_W = CMP_STRIDE * D_KV
CMP_PARTS = CMP_LEN // CMP_STRIDE
HID2 = N_KV * CMP_HID


def _compress_weights(pos_emb, w1, w2):
    bf16 = jnp.bfloat16
    eye = jnp.eye(N_KV, dtype=w1.dtype)
    w1b = w1.reshape(CMP_PARTS, CMP_STRIDE, HEAD_DIM, CMP_HID)
    w1p = jnp.einsum('pldf,hk->lhdpkf', w1b, eye).reshape(CHUNK_W, CMP_PARTS * HID2).astype(bf16)
    w2p = jnp.einsum('fd,hk->hfkd', w2, eye).reshape(HID2, D_KV).astype(bf16)
    pos8 = jnp.zeros((8, CMP_LEN * HEAD_DIM), bf16).at[0].set(pos_emb.reshape(-1).astype(bf16))
    w1f = w1.reshape(CMP_LEN * HEAD_DIM, CMP_HID).astype(bf16)
    return w1p, w2p, pos8, w1f


def _compress_rows(x, w1p, w2p, pos8, w1f):
    f32 = jnp.float32
    r = x.shape[0]
    part = jnp.dot(x, w1p, preferred_element_type=f32)
    bias = jnp.dot(pos8, w1f, preferred_element_type=f32)[0:1]
    bias = jnp.concatenate([bias] * N_KV, axis=-1)
    hid = bias + part[:, :HID2] + pltpu.roll(part[:, HID2:], r - 1, 0)
    out = jnp.dot(jax.nn.gelu(hid).astype(jnp.bfloat16), w2p, preferred_element_type=f32)
    rows = lax.broadcasted_iota(jnp.int32, (r, 1), 0)
    return jnp.where(rows < r - 1, out, 0.0)


def _compress_prompt_kernel(xk_ref, xv_ref, w1k_ref, w2k_ref, pk_ref, fk_ref, w1v_ref, w2v_ref, pv_ref, fv_ref,
                            ok_ref, ov_ref):
    bf16 = jnp.bfloat16
    ok_ref[0] = _compress_rows(xk_ref[0].astype(bf16), w1k_ref[...], w2k_ref[...], pk_ref[...],
                               fk_ref[...]).astype(ok_ref.dtype)
    ov_ref[0] = _compress_rows(xv_ref[0].astype(bf16), w1v_ref[...], w2v_ref[...], pv_ref[...],
                               fv_ref[...]).astype(ov_ref.dtype)


def _compress_prompt(kc, vc, wk, wv):
    b, s, _ = kc.shape
    r = s // CMP_STRIDE
    xspec = pl.BlockSpec((1, r, CHUNK_W), lambda i: (i, 0, 0))
    wspecs = [pl.BlockSpec(w.shape, lambda i: (0, 0)) for w in wk]
    ospec = pl.BlockSpec((1, r, D_KV), lambda i: (i, 0, 0))
    oshape = jax.ShapeDtypeStruct((b, r, D_KV), jnp.bfloat16)
    return pl.pallas_call(
        _compress_prompt_kernel,
        grid=(b,),
        in_specs=[xspec, xspec] + wspecs + wspecs,
        out_specs=(ospec, ospec),
        out_shape=(oshape, oshape),
        compiler_params=pltpu.CompilerParams(dimension_semantics=("arbitrary",), vmem_limit_bytes=VMEM_LIMIT),
        name="compress_prompt",
    )(kc.reshape(b, r, CHUNK_W), vc.reshape(b, r, CHUNK_W), *wk, *wv)


def _masked_softmax(s, mask):
    s = jnp.where(mask, s.astype(jnp.float32), NEG)
    m = jnp.max(s, axis=-1, keepdims=True)
    p = jnp.where(mask, jnp.exp(s - m), 0.0)
    return p / jnp.maximum(jnp.sum(p, axis=-1, keepdims=True), 1e-30)


def _merge_branches(y_lru, o_nsa, ga, gb, w_proj_a, w_proj_b, w_out):
    return (jax.nn.sigmoid(ga) * (y_lru @ w_proj_a) + jax.nn.sigmoid(gb) * (o_nsa @ w_proj_b)) @ w_out


def _lin_combine(c1, c2):
    a1, b1 = c1
    a2, b2 = c2
    return a1 * a2, a2 * b1 + b2


def _rglru(xr, gate_in, conv_buf, h0, conv_w, conv_b, wa, ba, wx, bx, lam):
    n, s, _ = xr.shape
    xp = jnp.concatenate([conv_buf.astype(xr.dtype), xr], axis=1)
    xc = conv_b + sum(conv_w[k] * xp[:, k:k + s] for k in range(CONV_W))
    xb = xc.reshape(n, s, LRU_BLOCKS, LRU_BW)
    r = jax.nn.sigmoid((jnp.einsum('nsbi,bij->nsbj', xb, wa) + ba).astype(jnp.float32)).reshape(n, s, D_RNN)
    i = jax.nn.sigmoid((jnp.einsum('nsbi,bij->nsbj', xb, wx) + bx).astype(jnp.float32)).reshape(n, s, D_RNN)
    log_a = -LRU_C * r * jax.nn.softplus(-lam.astype(jnp.float32))
    a = jnp.exp(log_a)
    u = jnp.sqrt(-jnp.expm1(2.0 * log_a)) * (i * xc.astype(jnp.float32))
    u = u.at[:, 0].add(a[:, 0] * h0.astype(jnp.float32))
    _, h = lax.associative_scan(_lin_combine, (a, u), axis=1)
    y = (h * jax.nn.gelu(gate_in.astype(jnp.float32))).astype(xr.dtype)
    return y, xp[:, s:], h[:, -1]


def _compress(rows, pos_emb, w1, w2):
    n, l = rows.shape[:2]
    r_ = CMP_LEN // CMP_STRIDE
    nch = l // CMP_STRIDE
    ch = rows[:, :nch * CMP_STRIDE].reshape(n, nch, CMP_STRIDE, N_KV, HEAD_DIM)
    w1b = w1.reshape(r_, CMP_STRIDE, HEAD_DIM, CMP_HID)
    part = jnp.einsum('nclhd,pldf->pnchf', ch, w1b)
    nc = nch - r_ + 1
    hid = jnp.einsum('ld,ldf->f', pos_emb, w1) + sum(part[p, :, p:p + nc] for p in range(r_))
    return jnp.einsum('nchf,fd->nchd', jax.nn.gelu(hid), w2)


def _block_scores(p_cmp, nsb):
    nc = p_cmp.shape[-1]
    ratio = SLC_BLOCK // CMP_STRIDE
    ovl = CMP_LEN // CMP_STRIDE - 1
    widths = [(0, 0)] * (p_cmp.ndim - 1) + [(ovl, ratio * (nsb + 1) - nc - ovl)]
    pp = jnp.pad(p_cmp, widths).reshape(*p_cmp.shape[:-1], nsb + 1, ratio)
    return pp[..., :nsb, :].sum(-1) + pp[..., 1:, :ovl].sum(-1)


def _select_blocks(scores, pos, nsb):
    j = jnp.arange(nsb, dtype=jnp.int32)
    cur = (pos // SLC_BLOCK)[:, None, None]
    forced = (j == 0) | (j == cur) | (j == cur - 1)
    s = jnp.where(forced, BIG, jnp.where(j <= cur, scores, NEG))
    top, idx = lax.top_k(s, min(N_SEL, nsb))
    return idx, top > NEG / 2


def _nsa_attend(q, pos, cmp_k, cmp_v, gather_slc, nsb, win_k, win_v, win_pos, gates):
    n, qn = q.shape[:2]
    nc = cmp_k.shape[1]
    s = jnp.einsum('nqhgd,nchd->nqhgc', q, cmp_k)
    cmp_end = jnp.arange(nc, dtype=jnp.int32) * CMP_STRIDE + CMP_LEN - 1
    p_c = _masked_softmax(s, cmp_end <= pos[:, None, None, None])
    o_cmp = jnp.einsum('nqhgc,nchd->nqhgd', p_c.astype(cmp_v.dtype), cmp_v)
    idx, valid = _select_blocks(_block_scores(p_c.sum(3), nsb), pos, nsb)
    k_g, v_g = gather_slc(idx)
    kk = idx.shape[-1] * SLC_BLOCK
    kpos = idx[..., None] * SLC_BLOCK + jnp.arange(SLC_BLOCK, dtype=jnp.int32)
    smask = (valid[..., None] & (kpos <= pos[:, None, None, None])).reshape(n, qn, N_KV, 1, kk)
    s = jnp.einsum('nqhgd,nqhkd->nqhgk', q, k_g.reshape(n, qn, N_KV, kk, HEAD_DIM))
    p_s = _masked_softmax(s, smask)
    o_slc = jnp.einsum('nqhgk,nqhkd->nqhgd', p_s.astype(v_g.dtype), v_g.reshape(n, qn, N_KV, kk, HEAD_DIM))
    s = jnp.einsum('nqhgd,nkhd->nqhgk', q, win_k)
    dpos = pos[:, None] - win_pos[None, :]
    wmask = (dpos >= 0) & (dpos < WINDOW) & (win_pos >= 0)[None, :]
    p_w = _masked_softmax(s, wmask[:, None, None, :])
    o_win = jnp.einsum('nqhgk,nkhd->nqhgd', p_w.astype(win_v.dtype), win_v)
    g = jax.nn.sigmoid(gates)
    return g[..., 0:1] * o_cmp + g[..., 1:2] * o_slc + g[..., 2:3] * o_win


def _nsa_prompt(q, kc, vc, ks, vs, kw, vw, ng, pos_k, w1_k, w2_k, pos_v, w1_v, w2_v):
    n, s = q.shape[:2]
    cmp_k = _compress(kc, pos_k, w1_k, w2_k)
    cmp_v = _compress(vc, pos_v, w1_v, w2_v)
    nsb = s // SLC_BLOCK
    kb = ks.reshape(n, nsb, SLC_BLOCK, N_KV, HEAD_DIM)
    vb = vs.reshape(n, nsb, SLC_BLOCK, N_KV, HEAD_DIM)
    bi = jnp.arange(n)[:, None, None, None]
    hi = jnp.arange(N_KV)[None, None, :, None]

    def gather(idx):
        return kb[bi, idx, :, hi], vb[bi, idx, :, hi]

    pad = ((0, 0), (WINDOW, 0), (0, 0), (0, 0))
    kw_pad, vw_pad = jnp.pad(kw, pad), jnp.pad(vw, pad)

    def block(i):
        start = i * Q_BLOCK
        sl = lambda t, size: lax.dynamic_slice_in_dim(t, start, size, axis=1)
        pos = start + jnp.arange(Q_BLOCK, dtype=jnp.int32)
        wpos = start - WINDOW + jnp.arange(WINDOW + Q_BLOCK, dtype=jnp.int32)
        return _nsa_attend(sl(q, Q_BLOCK), pos, cmp_k, cmp_v, gather, nsb,
                           sl(kw_pad, WINDOW + Q_BLOCK), sl(vw_pad, WINDOW + Q_BLOCK), wpos, sl(ng, Q_BLOCK))

    o = lax.map(block, jnp.arange(s // Q_BLOCK, dtype=jnp.int32))
    return jnp.moveaxis(o, 0, 1).reshape(n, s, D_Q)


def _nsa_sample(q, kc, vc, ks, vs, kw, vw, ng, page_table, ck_cmp, cv_cmp, ck_slc, cv_slc, ck_win, cv_win,
                pos_k, w1_k, w2_k, pos_v, w1_v, w2_v):
    n, ds = q.shape[:2]
    pos = PAST_LEN + jnp.arange(ds, dtype=jnp.int32)

    def past(pool):
        return pool[page_table].reshape(n, PAST_LEN, N_KV, HEAD_DIM)

    cmp_k = _compress(jnp.concatenate([past(ck_cmp), kc], axis=1), pos_k, w1_k, w2_k)
    cmp_v = _compress(jnp.concatenate([past(cv_cmp), vc], axis=1), pos_v, w1_v, w2_v)
    past_blocks = PAST_LEN // SLC_BLOCK
    new_blocks = -(-ds // SLC_BLOCK)
    nsb = past_blocks + new_blocks
    bpp = PAGE_SIZE // SLC_BLOCK
    view = lambda pool: pool.reshape(pool.shape[0], bpp, SLC_BLOCK, N_KV, HEAD_DIM)
    pk, pv = view(ck_slc), view(cv_slc)
    padn = ((0, 0), (0, new_blocks * SLC_BLOCK - ds), (0, 0), (0, 0))
    nk = jnp.pad(ks, padn).reshape(n, new_blocks, SLC_BLOCK, N_KV, HEAD_DIM)
    nv = jnp.pad(vs, padn).reshape(n, new_blocks, SLC_BLOCK, N_KV, HEAD_DIM)
    bi = jnp.arange(n)[:, None, None, None]
    hi = jnp.arange(N_KV)[None, None, :, None]

    def gather(idx):
        jp = jnp.minimum(idx, past_blocks - 1)
        phys = page_table[bi, jp // bpp]
        sub = jp % bpp
        jn = jnp.clip(idx - past_blocks, 0, new_blocks - 1)
        is_new = (idx >= past_blocks)[..., None, None]
        return (jnp.where(is_new, nk[bi, jn, :, hi], pk[phys, sub, :, hi]),
                jnp.where(is_new, nv[bi, jn, :, hi], pv[phys, sub, :, hi]))

    wb = ck_win.shape[1]
    wk = jnp.concatenate([ck_win, kw], axis=1)
    wv = jnp.concatenate([cv_win, vw], axis=1)
    wpos = PAST_LEN - wb + jnp.arange(wb + ds, dtype=jnp.int32)
    o = _nsa_attend(q, pos, cmp_k, cmp_v, gather, nsb, wk, wv, wpos, ng)
    return o.reshape(n, ds, D_Q), wk[:, -wb:], wv[:, -wb:]


def _rms_norm(x, g):
    xf = x.astype(jnp.float32)
    y = xf * lax.rsqrt(jnp.mean(xf * xf, axis=-1, keepdims=True) + EPS)
    return (y * g.astype(jnp.float32)).astype(x.dtype)


def _moe(h, router_w, router_b, w_up, b_up, w_down, b_down):
    shp = h.shape
    t = h.reshape(-1, D_MODEL)
    n_tok = t.shape[0]
    logits = (t @ router_w).astype(jnp.float32) + router_b.astype(jnp.float32)
    top_v, top_e = lax.top_k(logits, TOP_K)
    gate = jax.nn.softmax(top_v, axis=-1)
    n_asg = n_tok * TOP_K
    blk = int(min(512, max(8, n_asg // N_EXPERTS)))
    n_blocks = -(-n_asg // blk) + N_EXPERTS
    e_flat = top_e.reshape(-1)
    tok_flat = jnp.repeat(jnp.arange(n_tok, dtype=jnp.int32), TOP_K)
    g_flat = gate.reshape(-1)
    order = jnp.argsort(e_flat)
    e_sorted = e_flat[order]
    counts = jnp.bincount(e_flat, length=N_EXPERTS)
    padded = (counts + blk - 1) // blk * blk
    pad_end = jnp.cumsum(padded)
    pad_start = pad_end - padded
    starts = jnp.cumsum(counts) - counts
    dest = pad_start[e_sorted] + jnp.arange(n_asg, dtype=jnp.int32) - starts[e_sorted]
    slot_tok = jnp.full((n_blocks * blk,), n_tok, jnp.int32).at[dest].set(tok_flat[order])
    slot_gate = jnp.zeros((n_blocks * blk,), jnp.float32).at[dest].set(g_flat[order])
    blk_expert = jnp.minimum(jnp.searchsorted(pad_end, jnp.arange(n_blocks, dtype=jnp.int32) * blk, side='right'),
                             N_EXPERTS - 1)
    t_pad = jnp.concatenate([t, jnp.zeros((1, D_MODEL), t.dtype)], axis=0)

    def run_block(args):
        tok, g, e = args
        gu = t_pad[tok] @ w_up[e] + b_up[e]
        a = jnp.minimum(gu[:, :D_FF], SWIGLU_LIMIT)
        u = jnp.clip(gu[:, D_FF:], -SWIGLU_LIMIT, SWIGLU_LIMIT)
        y = ((u + 1.0) * a * jax.nn.sigmoid(SWIGLU_ALPHA * a)) @ w_down[e] + b_down[e]
        return (y * g[:, None]).astype(t.dtype)

    ys = lax.map(run_block, (slot_tok.reshape(n_blocks, blk), slot_gate.reshape(n_blocks, blk), blk_expert))
    out = jnp.zeros((n_tok + 1, D_MODEL), t.dtype).at[slot_tok].add(ys.reshape(-1, D_MODEL))
    return out[:n_tok].reshape(shp)


def _pad_w_in(w):
    c0 = 2 * D_RNN
    wq = w[:, c0:c0 + D_Q].reshape(D_MODEL, N_KV, GROUP, 1, HEAD_DIM)
    sel = jnp.eye(N_KV, dtype=w.dtype).reshape(N_KV, 1, N_KV, 1)
    wq = (wq * sel).reshape(D_MODEL, D_QP)
    c1 = c0 + D_Q
    c2 = c1 + 6 * D_KV + N_NG
    pad = jnp.zeros((D_MODEL, LANES - N_NG), w.dtype)
    return jnp.concatenate([w[:, :c0], wq, w[:, c1:c2], pad, w[:, c2:]], axis=1).astype(jnp.bfloat16)


def _unpad_q(q_pad, n, s):
    qp = q_pad.reshape(n, s, N_KV, GROUP, N_KV, HEAD_DIM)
    return jnp.stack([qp[:, :, h, :, h, :] for h in range(N_KV)], axis=2)


def _project(x, pos_rows, tm, g_mix, w_pad, g_q, g_k3):
    n, s, _ = x.shape
    outs = _input_projection(x.reshape(n * s, D_MODEL), pos_rows, tm, g_mix, w_pad, g_q, g_k3)
    arag, q, kc, vc, ks, vs, kw, vw, ng, gab = outs
    heads = lambda t: t.reshape(n, s, N_KV, HEAD_DIM)
    ar = arag[:, :D_RNN].reshape(n, s, D_RNN)
    ag = arag[:, D_RNN:].reshape(n, s, D_RNN)
    q = _unpad_q(q, n, s)
    ng = ng[:, :N_NG].reshape(n, s, N_KV, GROUP, 3)
    ga = gab[:, :D_MODEL].reshape(n, s, D_MODEL)
    gb = gab[:, D_MODEL:].reshape(n, s, D_MODEL)
    return ar, ag, q, heads(kc), heads(vc), heads(ks), heads(vs), heads(kw), heads(vw), ng, ga, gb


def kernel(x_prompt, x_sample, page_table, cache_k_cmp, cache_v_cmp, cache_k_slc, cache_v_slc,
           cache_k_win, cache_v_win, state_h, state_conv, g_mix, w_in, conv_w, conv_b, lru_wa, lru_ba,
           lru_wx, lru_bx, lru_lambda, g_q, g_k_cmp, g_k_slc, g_k_win, cmp_pos_k, cmp_w1_k, cmp_w2_k,
           cmp_pos_v, cmp_w1_v, cmp_w2_v, w_proj_a, w_proj_b, w_out, g_ffn, router_w, router_b,
           exp_w_up, exp_b_up, exp_w_down, exp_b_down):
    xp, xs = x_prompt, x_sample
    bp, sp = xp.shape[:2]
    bs, ss = xs.shape[:2]
    pos_p = jnp.arange(sp, dtype=jnp.int32)
    pos_s = PAST_LEN + jnp.arange(ss, dtype=jnp.int32)
    depth = w_in.shape[0]
    layer_states = []
    tile2 = lambda g: jnp.concatenate([g, g], axis=-1)
    for l in range(depth):
        cmp_w = (cmp_pos_k[l], cmp_w1_k[l], cmp_w2_k[l], cmp_pos_v[l], cmp_w1_v[l], cmp_w2_v[l])
        lru_w = (conv_w[l], conv_b[l], lru_wa[l], lru_ba[l], lru_wx[l], lru_bx[l], lru_lambda[l])
        w_pad = _pad_w_in(w_in[l])
        gq2 = tile2(g_q[l]).reshape(1, LANES)
        gk3 = jnp.stack([tile2(g_k_cmp[l]), tile2(g_k_slc[l]), tile2(g_k_win[l])])
        cmp_wk = _compress_weights(cmp_pos_k[l], cmp_w1_k[l], cmp_w2_k[l])
        cmp_wv = _compress_weights(cmp_pos_v[l], cmp_w1_v[l], cmp_w2_v[l])
        wb_perm = w_proj_b[l].reshape(N_KV, GROUP, HEAD_DIM, D_MODEL).transpose(1, 0, 2, 3).reshape(D_Q, D_MODEL)
        arag, q_pad, kc, vc, ks, vs, kw, vw, ng, gab = _input_projection(
            xp.reshape(bp * sp, D_MODEL), pos_p, 256, g_mix[l], w_pad, gq2, gk3)
        rows = lambda t: t.reshape(bp, sp, t.shape[-1])
        heads = lambda t: t.reshape(bp, sp, N_KV, HEAD_DIM)
        ar, ag = rows(arag[:, :D_RNN]), rows(arag[:, D_RNN:])
        ga, gb = rows(gab[:, :D_MODEL]), rows(gab[:, D_MODEL:])
        y_lru, p_conv, p_h = _rglru(ar, ag, jnp.zeros((bp, CONV_W - 1, D_RNN), xp.dtype),
                                    jnp.zeros((bp, D_RNN), jnp.float32), *lru_w)
        cmp_k, cmp_v = _compress_prompt(rows(kc), rows(vc), cmp_wk, cmp_wv)
        o_nsa = _nsa_prompt_attention(rows(q_pad), rows(ks), rows(vs), rows(kw), rows(vw), cmp_k, cmp_v, rows(ng))
        xp = xp + _merge_branches(y_lru, o_nsa.astype(jnp.float32), ga, gb, w_proj_a[l], wb_perm, w_out[l])
        wbp = min(WINDOW, sp)
        p_states = (heads(kc), heads(vc), heads(ks), heads(vs), heads(kw)[:, -wbp:], heads(vw)[:, -wbp:],
                    p_h, p_conv)
        pos_rows_s = jnp.broadcast_to(pos_s[None, :], (bs, ss)).reshape(-1)
        ar, ag, q, kc, vc, ks, vs, kw, vw, ng, ga, gb = _project(
            xs, pos_rows_s, bs * ss, g_mix[l], w_pad, gq2, gk3)
        y_lru, s_conv_l, s_h_l = _rglru(ar, ag, state_conv[l], state_h[l], *lru_w)
        o_nsa, s_kw_l, s_vw_l = _nsa_sample(q.astype(jnp.float32), kc, vc, ks, vs, kw, vw, ng, page_table,
                                            cache_k_cmp[l], cache_v_cmp[l], cache_k_slc[l], cache_v_slc[l],
                                            cache_k_win[l], cache_v_win[l], *cmp_w)
        xs = xs + _merge_branches(y_lru, o_nsa, ga, gb, w_proj_a[l], w_proj_b[l], w_out[l])
        s_states = (kc, vc, ks, vs, s_kw_l, s_vw_l, s_h_l, s_conv_l)
        moe_w = (router_w[l], router_b[l], exp_w_up[l], exp_b_up[l], exp_w_down[l], exp_b_down[l])
        xp = xp + _moe(_rms_norm(xp, g_ffn[l]), *moe_w)
        xs = xs + _moe(_rms_norm(xs, g_ffn[l]), *moe_w)
        layer_states.append(p_states + s_states)
    (p_k_cmp, p_v_cmp, p_k_slc, p_v_slc, p_k_win, p_v_win, p_h, p_conv,
     s_k_cmp, s_v_cmp, s_k_slc, s_v_slc, s_k_win, s_v_win, s_h, s_conv) = [jnp.stack(z) for z in zip(*layer_states)]
    return (xp, xs, p_k_cmp, p_v_cmp, p_k_slc, p_v_slc, p_k_win, p_v_win, p_h, p_conv,
            s_k_cmp, s_v_cmp, s_k_slc, s_v_slc, s_k_win, s_v_win, s_h, s_conv)
```

```python
import functools
import math

import jax
import jax.numpy as jnp
import numpy as np
from jax import lax
from jax.experimental import pallas as pl
from jax.experimental.pallas import tpu as pltpu

D_MODEL = 1024
PAST_LEN = 8192
PAGE_SIZE = 128
D_RNN = D_MODEL
LRU_BLOCKS = 16
LRU_BW = D_RNN // LRU_BLOCKS
CONV_W = 4
LRU_C = 8.0
N_HEADS = 8
HEAD_DIM = 64
N_KV = 2
GROUP = N_HEADS // N_KV
D_Q = N_HEADS * HEAD_DIM
D_KV = N_KV * HEAD_DIM
ROPE_DIM = HEAD_DIM // 4
ROPE_THETA = 500000.0
CMP_LEN = 32
CMP_STRIDE = 16
CMP_HID = 256
SLC_BLOCK = 64
N_SEL = 16
WINDOW = 512
Q_BLOCK = 64
N_EXPERTS = 32
TOP_K = 4
D_FF = D_MODEL
SWIGLU_LIMIT = 7.0
SWIGLU_ALPHA = 1.702
EPS = 1e-6
NEG = -1e30
BIG = 1e30

LANES = 128
VMEM_LIMIT = 56 * 1024 * 1024

N_NG = 3 * N_HEADS
D_QP = N_HEADS * LANES
C_AR = 0
C_Q = 2 * D_RNN
C_KV = C_Q + D_QP
C_NG = C_KV + 6 * D_KV
C_GAB = C_NG + LANES
N_IN_PAD = C_GAB + 2 * D_MODEL


def _seg_sum64(x2, ones_blk):
    hi = x2.astype(jnp.bfloat16)
    lo = (x2 - hi.astype(jnp.float32)).astype(jnp.bfloat16)
    return (jnp.dot(hi, ones_blk, preferred_element_type=jnp.float32)
            + jnp.dot(lo, ones_blk, preferred_element_type=jnp.float32))


def _head_norm_rope(y, gain, ones_blk, rc, rs1, rs2):
    ss = _seg_sum64(y * y, ones_blk)
    yn = y * lax.rsqrt(ss * (1.0 / HEAD_DIM) + EPS) * gain
    half = ROPE_DIM // 2
    return yn * rc + pltpu.roll(yn, LANES - half, 1) * rs1 + pltpu.roll(yn, half, 1) * rs2


def _proj_kernel(x_ref, gmix_ref, w_ref, ones_ref, gq_ref, gk_ref, rc_ref, rs1_ref, rs2_ref,
                 arag_ref, q_ref, kc_ref, vc_ref, ks_ref, vs_ref, kw_ref, vw_ref, ng_ref, gab_ref):
    x = x_ref[...]
    xn = x * lax.rsqrt(jnp.mean(x * x, axis=-1, keepdims=True) + EPS) * gmix_ref[...]
    xb = xn.astype(jnp.bfloat16)

    def proj(c0, width):
        return jnp.dot(xb, w_ref[:, c0:c0 + width], preferred_element_type=jnp.float32)

    arag_ref[...] = proj(C_AR, 2 * D_RNN)
    gab_ref[...] = proj(C_GAB, 2 * D_MODEL)
    ng_ref[...] = proj(C_NG, LANES)
    ones_blk = ones_ref[...]
    rc, rs1, rs2 = rc_ref[...], rs1_ref[...], rs2_ref[...]
    scale = HEAD_DIM ** -0.5
    for j in range(N_HEADS):
        y = proj(C_Q + j * LANES, LANES)
        yr = _head_norm_rope(y, gq_ref[...], ones_blk, rc, rs1, rs2)
        q_ref[:, j * LANES:(j + 1) * LANES] = (yr * scale).astype(q_ref.dtype)
    k_outs = (kc_ref, ks_ref, kw_ref)
    v_outs = (vc_ref, vs_ref, vw_ref)
    for j in range(3):
        y = proj(C_KV + 2 * j * LANES, LANES)
        k_outs[j][...] = _head_norm_rope(y, gk_ref[j:j + 1, :], ones_blk, rc, rs1, rs2)
        v_outs[j][...] = proj(C_KV + (2 * j + 1) * LANES, LANES)


def _rope_tables(pos):
    half = ROPE_DIM // 2
    inv = ROPE_THETA ** (-jnp.arange(half, dtype=jnp.float32) / half)
    ang = pos.astype(jnp.float32)[:, None] * inv[None, :]
    cos, sin = jnp.cos(ang), jnp.sin(ang)
    p = pos.shape[0]
    ones = jnp.ones((p, HEAD_DIM - ROPE_DIM), jnp.float32)
    zeros = jnp.zeros((p, HEAD_DIM - ROPE_DIM), jnp.float32)
    zh = jnp.zeros((p, half), jnp.float32)
    rc = jnp.concatenate([cos, cos, ones], axis=-1)
    rs1 = jnp.concatenate([-sin, zh, zeros], axis=-1)
    rs2 = jnp.concatenate([zh, sin, zeros], axis=-1)
    tile2 = lambda t: jnp.concatenate([t, t], axis=-1)
    return tile2(rc), tile2(rs1), tile2(rs2)


def _ones_block64():
    i = np.arange(LANES)
    return jnp.asarray((i[:, None] // HEAD_DIM) == (i[None, :] // HEAD_DIM), dtype=jnp.bfloat16)


def _input_projection(x2d, pos_rows, tm, g_mix, w_pad, g_q, g_k3):
    m = x2d.shape[0]
    p = pos_rows.shape[0]
    assert m % tm == 0 and p % tm == 0
    rc, rs1, rs2 = _rope_tables(pos_rows)
    n_tab = p // tm
    row = lambda i: (i, 0)
    const = lambda i: (0, 0)
    tab = lambda i: (i % n_tab, 0)
    f32 = jnp.float32
    out_shape = (
        jax.ShapeDtypeStruct((m, 2 * D_RNN), f32),
        jax.ShapeDtypeStruct((m, D_QP), jnp.bfloat16),
    ) + tuple(jax.ShapeDtypeStruct((m, D_KV), f32) for _ in range(6)) + (
        jax.ShapeDtypeStruct((m, LANES), f32),
        jax.ShapeDtypeStruct((m, 2 * D_MODEL), f32),
    )
    out_specs = (
        pl.BlockSpec((tm, 2 * D_RNN), row),
        pl.BlockSpec((tm, D_QP), row),
    ) + tuple(pl.BlockSpec((tm, D_KV), row) for _ in range(6)) + (
        pl.BlockSpec((tm, LANES), row),
        pl.BlockSpec((tm, 2 * D_MODEL), row),
    )
    return pl.pallas_call(
        _proj_kernel,
        grid=(m // tm,),
        in_specs=[
            pl.BlockSpec((tm, D_MODEL), row),
            pl.BlockSpec((1, D_MODEL), const),
            pl.BlockSpec((D_MODEL, N_IN_PAD), const),
            pl.BlockSpec((LANES, LANES), const),
            pl.BlockSpec((1, LANES), const),
            pl.BlockSpec((3, LANES), const),
            pl.BlockSpec((tm, LANES), tab),
            pl.BlockSpec((tm, LANES), tab),
            pl.BlockSpec((tm, LANES), tab),
        ],
        out_specs=out_specs,
        out_shape=out_shape,
        compiler_params=pltpu.CompilerParams(dimension_semantics=("arbitrary",), vmem_limit_bytes=VMEM_LIMIT),
        name="input_projection",
    )(x2d, g_mix.reshape(1, D_MODEL), w_pad, _ones_block64(), g_q, g_k3, rc, rs1, rs2)


SLC_CHUNK = 512
WIN_SPAN = WINDOW + LANES
ROWS = GROUP * Q_BLOCK


def _dot_nt(a, b):
    return lax.dot_general(a, b, (((1,), (1,)), ((), ())), preferred_element_type=jnp.float32)


def _softmax_rows(s, mask):
    s = jnp.where(mask, s, NEG)
    m = jnp.max(s, axis=-1, keepdims=True)
    p = jnp.where(mask, jnp.exp(s - m), 0.0)
    return p / jnp.maximum(jnp.sum(p, axis=-1, keepdims=True), 1e-30)


def _split3_bf16(x):
    hi = x.astype(jnp.bfloat16)
    r1 = x - hi.astype(jnp.float32)
    mid = r1.astype(jnp.bfloat16)
    lo = (r1 - mid.astype(jnp.float32)).astype(jnp.bfloat16)
    return hi, mid, lo


def _nsa_prompt_kernel(q_ref, ksa_ref, vs_ref, kw_ref, vw_ref, ck_ref, cv_ref, ng_ref, mmt_ref,
                       o_ref, sv_scr):
    f32, bf16 = jnp.float32, jnp.bfloat16
    qb = pl.program_id(1)
    row = lax.broadcasted_iota(jnp.int32, (ROWS, 1), 0)
    pos_t = qb * Q_BLOCK + (row & (Q_BLOCK - 1))
    ncp = ck_ref.shape[1]

    qh = [jnp.concatenate([q_ref[0, :, (h * GROUP + g) * LANES:(h * GROUP + g + 1) * LANES]
                           for g in range(GROUP)], axis=0) for h in range(N_KV)]

    ck, cv = ck_ref[0], cv_ref[0]
    c_end = lax.broadcasted_iota(jnp.int32, (1, ncp), 1) * CMP_STRIDE + (CMP_LEN - 1)
    cmask = c_end <= pos_t
    o_cmp, psum = [], []
    for h in range(N_KV):
        p = _softmax_rows(_dot_nt(qh[h], ck), cmask)
        o_cmp.append(jnp.dot(p.astype(bf16), cv, preferred_element_type=f32))
        psum.append(p[0:Q_BLOCK] + p[Q_BLOCK:2 * Q_BLOCK] + p[2 * Q_BLOCK:3 * Q_BLOCK] + p[3 * Q_BLOCK:])
    ps = jnp.concatenate(psum, axis=0)

    mmt = mmt_ref[...]
    sc = sum(_dot_nt(mmt, part) for part in _split3_bf16(ps))[0:Q_BLOCK]
    jio = lax.broadcasted_iota(jnp.int32, (Q_BLOCK, LANES), 0)
    forced = (jio == 0) | (jio == qb) | (jio == qb - 1)
    sv = jnp.where(forced, BIG, jnp.where(jio <= qb, sc, NEG))
    sv_scr[...] = sv

    def rank_body(i, cnt):
        r = sv_scr[pl.ds(i, 1), :]
        beats = (r > sv) | ((r == sv) & (jio > i))
        return cnt + jnp.where(beats, 1.0, 0.0)

    cnt = lax.fori_loop(0, qb + 1, rank_body, jnp.zeros((Q_BLOCK, LANES), f32))
    bias_lo = jnp.where((cnt < N_SEL) & (sv > NEG / 2), 0.0, NEG)
    lane = lax.broadcasted_iota(jnp.int32, (Q_BLOCK, LANES), 1)
    bias_t = jnp.concatenate([jnp.where(lane >= Q_BLOCK, bias_lo, 0.0),
                              jnp.where(lane < Q_BLOCK, bias_lo, 0.0)], axis=0)
    bias = bias_t.T.astype(bf16)

    kio = lax.broadcasted_iota(jnp.int32, (1, SLC_CHUNK), 1)
    wio = lax.broadcasted_iota(jnp.int32, (1, WIN_SPAN), 1)
    wbase = pl.multiple_of((jnp.maximum(qb - WINDOW // Q_BLOCK, 0) // 2) * LANES, LANES)
    dpos = pos_t - (wbase + wio)
    wmask = (dpos >= 0) & (dpos < WINDOW)
    n_chunks = qb // (SLC_CHUNK // SLC_BLOCK) + 1
    sig = jax.nn.sigmoid(ng_ref[0])
    out_h = []
    for h in range(N_KV):
        bh = bias[h * Q_BLOCK:(h + 1) * Q_BLOCK]
        qa = qh[h] + jnp.concatenate([bh] * GROUP, axis=0)

        def chunk(c, carry, causal):
            m, l, acc = carry
            k0 = pl.multiple_of(c * SLC_CHUNK, SLC_CHUNK)
            s = _dot_nt(qa, ksa_ref[0, h, pl.ds(k0, SLC_CHUNK), :])
            if causal:
                s = jnp.where(k0 + kio <= pos_t, s, NEG)
            m_new = jnp.maximum(m, jnp.max(s, axis=-1, keepdims=True))
            alpha = jnp.exp(m - m_new)
            p = jnp.exp(s - m_new)
            l = alpha * l + jnp.sum(p, axis=-1, keepdims=True)
            acc = alpha * acc + jnp.dot(p.astype(bf16), vs_ref[0, pl.ds(k0, SLC_CHUNK), :],
                                        preferred_element_type=f32)
            return m_new, l, acc

        init = (jnp.full((ROWS, 1), NEG, f32), jnp.zeros((ROWS, 1), f32), jnp.zeros((ROWS, LANES), f32))
        carry = lax.fori_loop(0, n_chunks - 1, lambda c, cr: chunk(c, cr, False), init)
        _, l, acc = chunk(n_chunks - 1, carry, True)
        o_slc = acc / jnp.maximum(l, 1e-30)

        pw = _softmax_rows(_dot_nt(qh[h], kw_ref[0, pl.ds(wbase, WIN_SPAN), :]), wmask)
        o_win = jnp.dot(pw.astype(bf16), vw_ref[0, pl.ds(wbase, WIN_SPAN), :], preferred_element_type=f32)

        outs = []
        for g in range(GROUP):
            rs = slice(g * Q_BLOCK, (g + 1) * Q_BLOCK)
            c0 = (h * GROUP + g) * 3
            outs.append(sig[:, c0:c0 + 1] * o_cmp[h][rs] + sig[:, c0 + 1:c0 + 2] * o_slc[rs]
                        + sig[:, c0 + 2:c0 + 3] * o_win[rs])
        out_h.append(outs)
    for g in range(GROUP):
        o_ref[0, :, g * LANES:(g + 1) * LANES] = jnp.where(lane < HEAD_DIM, out_h[0][g], out_h[1][g]).astype(o_ref.dtype)


def _block_score_matrix(ncp):
    ratio = SLC_BLOCK // CMP_STRIDE
    ovl = CMP_LEN // CMP_STRIDE - 1
    j = np.arange(LANES)[:, None]
    c = np.arange(ncp)[None, :]
    m = (c >= ratio * j - ovl) & (c <= ratio * j + ratio - 1) & (j < Q_BLOCK) & (c < ncp - 1)
    return jnp.asarray(m, dtype=jnp.bfloat16)


def _nsa_prompt_attention(q_pad, ks, vs, kw, vw, cmp_k, cmp_v, ng):
    b, s, _ = q_pad.shape
    assert s % SLC_CHUNK == 0 and s // SLC_BLOCK <= Q_BLOCK and s >= WIN_SPAN and Q_BLOCK == SLC_BLOCK
    bf16 = jnp.bfloat16
    ncp = cmp_k.shape[1]
    onehot = jax.nn.one_hot(jnp.arange(s) // SLC_BLOCK, Q_BLOCK, dtype=bf16)
    onehot = jnp.broadcast_to(onehot[None], (b, s, Q_BLOCK))
    ksb = ks.astype(bf16)
    ksa = jnp.stack([jnp.concatenate([ksb[:, :, :HEAD_DIM], onehot], axis=-1),
                     jnp.concatenate([onehot, ksb[:, :, HEAD_DIM:]], axis=-1)], axis=1)
    per_b = lambda bi, i: (bi, 0, 0)
    return pl.pallas_call(
        _nsa_prompt_kernel,
        grid=(b, s // Q_BLOCK),
        in_specs=[
            pl.BlockSpec((1, Q_BLOCK, D_QP), lambda bi, i: (bi, i, 0)),
            pl.BlockSpec((1, N_KV, s, LANES), lambda bi, i: (bi, 0, 0, 0)),
            pl.BlockSpec((1, s, LANES), per_b),
            pl.BlockSpec((1, s, LANES), per_b),
            pl.BlockSpec((1, s, LANES), per_b),
            pl.BlockSpec((1, ncp, LANES), per_b),
            pl.BlockSpec((1, ncp, LANES), per_b),
            pl.BlockSpec((1, Q_BLOCK, LANES), lambda bi, i: (bi, i, 0)),
            pl.BlockSpec((LANES, ncp), lambda bi, i: (0, 0)),
        ],
        out_specs=pl.BlockSpec((1, Q_BLOCK, D_Q), lambda bi, i: (bi, i, 0)),
        out_shape=jax.ShapeDtypeStruct((b, s, D_Q), bf16),
        scratch_shapes=[pltpu.VMEM((Q_BLOCK, LANES), jnp.float32)],
        compiler_params=pltpu.CompilerParams(dimension_semantics=("arbitrary", "arbitrary"),
                                             vmem_limit_bytes=VMEM_LIMIT),
        name="nsa_prompt_attention",
    )(q_pad, ksa, vs.astype(bf16), kw.astype(bf16), vw.astype(bf16), cmp_k, cmp_v, ng, _block_score_matrix(ncp))


CHUNK_W = CMP_STRIDE * D_KV
CMP_PARTS = CMP_LEN // CMP_STRIDE
HID2 = N_KV * CMP_HID


def _compress_weights(pos_emb, w1, w2):
    bf16 = jnp.bfloat16
    eye = jnp.eye(N_KV, dtype=w1.dtype)
    w1b = w1.reshape(CMP_PARTS, CMP_STRIDE, HEAD_DIM, CMP_HID)
    w1p = jnp.einsum('pldf,hk->lhdpkf', w1b, eye).reshape(CHUNK_W, CMP_PARTS * HID2).astype(bf16)
    w2p = jnp.einsum('fd,hk->hfkd', w2, eye).reshape(HID2, D_KV).astype(bf16)
    pos8 = jnp.zeros((8, CMP_LEN * HEAD_DIM), bf16).at[0].set(pos_emb.reshape(-1).astype(bf16))
    w1f = w1.reshape(CMP_LEN * HEAD_DIM, CMP_HID).astype(bf16)
    return w1p, w2p, pos8, w1f


def _compress_rows(x, w1p, w2p, pos8, w1f):
    f32 = jnp.float32
    r = x.shape[0]
    part = jnp.dot(x, w1p, preferred_element_type=f32)
    bias = jnp.dot(pos8, w1f, preferred_element_type=f32)[0:1]
    bias = jnp.concatenate([bias] * N_KV, axis=-1)
    hid = bias + part[:, :HID2] + pltpu.roll(part[:, HID2:], r - 1, 0)
    out = jnp.dot(jax.nn.gelu(hid).astype(jnp.bfloat16), w2p, preferred_element_type=f32)
    rows = lax.broadcasted_iota(jnp.int32, (r, 1), 0)
    return jnp.where(rows < r - 1, out, 0.0)


def _compress_prompt_kernel(xk_ref, xv_ref, w1k_ref, w2k_ref, pk_ref, fk_ref, w1v_ref, w2v_ref, pv_ref, fv_ref,
                            ok_ref, ov_ref):
    bf16 = jnp.bfloat16
    ok_ref[0] = _compress_rows(xk_ref[0].astype(bf16), w1k_ref[...], w2k_ref[...], pk_ref[...],
                               fk_ref[...]).astype(ok_ref.dtype)
    ov_ref[0] = _compress_rows(xv_ref[0].astype(bf16), w1v_ref[...], w2v_ref[...], pv_ref[...],
                               fv_ref[...]).astype(ov_ref.dtype)


def _compress_prompt(kc, vc, wk, wv):
    b, s, _ = kc.shape
    r = s // CMP_STRIDE
    xspec = pl.BlockSpec((1, r, CHUNK_W), lambda i: (i, 0, 0))
    wspecs = [pl.BlockSpec(w.shape, lambda i: (0, 0)) for w in wk]
    ospec = pl.BlockSpec((1, r, D_KV), lambda i: (i, 0, 0))
    oshape = jax.ShapeDtypeStruct((b, r, D_KV), jnp.bfloat16)
    return pl.pallas_call(
        _compress_prompt_kernel,
        grid=(b,),
        in_specs=[xspec, xspec] + wspecs + wspecs,
        out_specs=(ospec, ospec),
        out_shape=(oshape, oshape),
        compiler_params=pltpu.CompilerParams(dimension_semantics=("arbitrary",), vmem_limit_bytes=VMEM_LIMIT),
        name="compress_prompt",
    )(kc.reshape(b, r, CHUNK_W), vc.reshape(b, r, CHUNK_W), *wk, *wv)


SUBLANES = 8
LRU_PAIRS = D_RNN // LANES


def _lru_gates(xc, wa_ref, ba, wx_ref, bx, lam):
    f32 = jnp.float32
    xb = xc.astype(jnp.bfloat16)
    rg = jnp.concatenate([jnp.dot(xb[:, j * LANES:(j + 1) * LANES], wa_ref[j], preferred_element_type=f32)
                          for j in range(LRU_PAIRS)], axis=-1)
    ig = jnp.concatenate([jnp.dot(xb[:, j * LANES:(j + 1) * LANES], wx_ref[j], preferred_element_type=f32)
                          for j in range(LRU_PAIRS)], axis=-1)
    r = jax.nn.sigmoid(rg + ba)
    i = jax.nn.sigmoid(ig + bx)
    log_a = -LRU_C * r * jax.nn.softplus(-lam)
    a = jnp.exp(log_a)
    u = jnp.sqrt(-jnp.tanh(log_a) * (a * a + 1.0)) * (i * xc)
    return a, u


def _lru_prompt_kernel(arag_ref, cbuf_ref, h0_ref, cw_ref, cb_ref, wa_ref, ba_ref, wx_ref, bx_ref, lam_ref,
                       y_ref, hl_ref, prev_scr, h_scr, hbuf):
    f32 = jnp.float32
    t = arag_ref.shape[0]

    @pl.when(pl.program_id(1) == 0)
    def _():
        prev_scr[...] = cbuf_ref[0]
        h_scr[...] = h0_ref[0]

    ar = arag_ref[:, :D_RNN]
    row8 = lax.broadcasted_iota(jnp.int32, (SUBLANES, 1), 0)
    prev = prev_scr[...]
    xc = cb_ref[...] + cw_ref[CONV_W - 1:CONV_W, :] * ar
    for k in range(1, CONV_W):
        rolled = pltpu.roll(ar, k, 0)
        head = jnp.where(row8 < k, pltpu.roll(prev, k, 0), rolled[0:SUBLANES])
        shifted = jnp.concatenate([head, rolled[SUBLANES:]], axis=0)
        xc = xc + cw_ref[CONV_W - 1 - k:CONV_W - k, :] * shifted
    prev_scr[...] = ar[t - SUBLANES:]

    a, u = _lru_gates(xc, wa_ref, ba_ref[...], wx_ref, bx_ref[...], lam_ref[...])

    rowm = lax.broadcasted_iota(jnp.int32, (t, 1), 0) & (SUBLANES - 1)
    for s in (1, 2, 4):
        ok = rowm >= s
        u = jnp.where(ok, a * pltpu.roll(u, s, 0) + u, u)
        a = jnp.where(ok, a * pltpu.roll(a, s, 0), a)
    h = h_scr[...]
    for g in range(t // SUBLANES):
        rs = slice(g * SUBLANES, (g + 1) * SUBLANES)
        hg = a[rs] * h + u[rs]
        hbuf[rs, :] = hg
        h = jnp.broadcast_to(hg[SUBLANES - 1:SUBLANES, :], (SUBLANES, D_RNN))
    h_scr[...] = h
    hl_ref[0] = h
    y_ref[...] = (hbuf[...] * jax.nn.gelu(arag_ref[:, D_RNN:])).astype(y_ref.dtype)


def _lru_pair_weights(w):
    w4 = w.reshape(LRU_PAIRS, 2, LRU_BW, LRU_BW)
    eye = jnp.eye(2, dtype=w.dtype)
    return jnp.einsum('jaio,ab->jaibo', w4, eye).reshape(LRU_PAIRS, LANES, LANES).astype(jnp.bfloat16)


def _lru_weights(conv_w, conv_b, wa, ba, wx, bx, lam):
    row = lambda v: v.reshape(1, D_RNN).astype(jnp.float32)
    return (conv_w.astype(jnp.float32), row(conv_b), _lru_pair_weights(wa), row(ba), _lru_pair_weights(wx), row(bx),
            row(lam))


def _lru_prompt(arag, conv_buf, h0, lw, b, s, t):
    assert s % t == 0 and t % SUBLANES == 0
    nt = s // t
    cbuf8 = jnp.pad(conv_buf.astype(jnp.float32), ((0, 0), (SUBLANES - (CONV_W - 1), 0), (0, 0)))
    h08 = jnp.broadcast_to(h0.astype(jnp.float32)[:, None, :], (b, SUBLANES, D_RNN))
    const2 = lambda bi, i: (0, 0)
    const3 = lambda bi, i: (0, 0, 0)
    per_b = lambda bi, i: (bi, 0, 0)
    y, hl = pl.pallas_call(
        _lru_prompt_kernel,
        grid=(b, nt),
        in_specs=[
            pl.BlockSpec((t, 2 * D_RNN), lambda bi, i: (bi * nt + i, 0)),
            pl.BlockSpec((1, SUBLANES, D_RNN), per_b),
            pl.BlockSpec((1, SUBLANES, D_RNN), per_b),
            pl.BlockSpec((CONV_W, D_RNN), const2),
            pl.BlockSpec((1, D_RNN), const2),
            pl.BlockSpec((LRU_PAIRS, LANES, LANES), const3),
            pl.BlockSpec((1, D_RNN), const2),
            pl.BlockSpec((LRU_PAIRS, LANES, LANES), const3),
            pl.BlockSpec((1, D_RNN), const2),
            pl.BlockSpec((1, D_RNN), const2),
        ],
        out_specs=(pl.BlockSpec((t, D_RNN), lambda bi, i: (bi * nt + i, 0)),
                   pl.BlockSpec((1, SUBLANES, D_RNN), per_b)),
        out_shape=(jax.ShapeDtypeStruct((b * s, D_RNN), jnp.bfloat16),
                   jax.ShapeDtypeStruct((b, SUBLANES, D_RNN), jnp.float32)),
        scratch_shapes=[pltpu.VMEM((SUBLANES, D_RNN), jnp.float32), pltpu.VMEM((SUBLANES, D_RNN), jnp.float32),
                        pltpu.VMEM((t, D_RNN), jnp.float32)],
        compiler_params=pltpu.CompilerParams(dimension_semantics=("arbitrary", "arbitrary"),
                                             vmem_limit_bytes=VMEM_LIMIT),
        name="lru_prompt",
    )(arag, cbuf8, h08, *lw)
    return y, hl[:, 0, :]


def _lru_step_kernel(arag_ref, c0_ref, c1_ref, c2_ref, h0_ref, cw_ref, cb_ref, wa_ref, ba_ref, wx_ref, bx_ref,
                     lam_ref, y_ref, h_ref):
    ar = arag_ref[:, :D_RNN]
    xc = (cb_ref[...] + cw_ref[0:1, :] * c0_ref[...] + cw_ref[1:2, :] * c1_ref[...] + cw_ref[2:3, :] * c2_ref[...]
          + cw_ref[3:4, :] * ar)
    a, u = _lru_gates(xc, wa_ref, ba_ref[...], wx_ref, bx_ref[...], lam_ref[...])
    h = a * h0_ref[...] + u
    h_ref[...] = h
    y_ref[...] = (h * jax.nn.gelu(arag_ref[:, D_RNN:])).astype(y_ref.dtype)


def _lru_step(arag, conv_buf, h0, lw):
    n = arag.shape[0]
    f32 = jnp.float32
    cb = conv_buf.astype(f32)
    return pl.pallas_call(
        _lru_step_kernel,
        out_shape=(jax.ShapeDtypeStruct((n, D_RNN), jnp.bfloat16), jax.ShapeDtypeStruct((n, D_RNN), f32)),
        compiler_params=pltpu.CompilerParams(vmem_limit_bytes=VMEM_LIMIT),
        name="lru_step",
    )(arag, cb[:, 0], cb[:, 1], cb[:, 2], h0.astype(f32), *lw)


def _merge_router_kernel(x_ref, ylru_ref, onsa_ref, gab_ref, wa_ref, wb_ref, wo_ref, gffn_ref, rw_ref, rb_ref,
                         tri_ref, xn_ref, hn_ref, te_ref, tr_ref, tg_ref, cnt_ref, run_scr):
    f32, bf16 = jnp.float32, jnp.bfloat16
    tm = x_ref.shape[0]

    @pl.when(pl.program_id(0) == 0)
    def _():
        run_scr[...] = jnp.zeros_like(run_scr)

    ya = jnp.dot(ylru_ref[...], wa_ref[...], preferred_element_type=f32)
    yb = jnp.dot(onsa_ref[...], wb_ref[...], preferred_element_type=f32)
    mix = jax.nn.sigmoid(gab_ref[:, :D_MODEL]) * ya + jax.nn.sigmoid(gab_ref[:, D_MODEL:]) * yb
    xn = x_ref[...] + jnp.dot(mix.astype(bf16), wo_ref[...], preferred_element_type=f32)
    xn_ref[...] = xn
    hn = xn * lax.rsqrt(jnp.mean(xn * xn, axis=-1, keepdims=True) + EPS) * gffn_ref[...]
    hn_ref[...] = hn

    logits = jnp.dot(hn, rw_ref[...], preferred_element_type=f32, precision=lax.Precision.HIGHEST) + rb_ref[...]
    lane = lax.broadcasted_iota(jnp.int32, (tm, LANES), 1)
    ninf = -jnp.inf
    l = jnp.where(lane < N_EXPERTS, logits, ninf)
    onehots, m0 = [], None
    for k in range(TOP_K):
        m = jnp.max(l, axis=-1, keepdims=True)
        idx = jnp.min(jnp.where(l == m, lane, LANES), axis=-1, keepdims=True)
        oh = lane == idx
        onehots.append((oh, idx))
        m0 = m if k == 0 else m0
        l = jnp.where(oh, ninf, l)
    sel = onehots[0][0] | onehots[1][0] | onehots[2][0] | onehots[3][0]
    e = jnp.where(sel, jnp.exp(logits - m0), 0.0)
    gates = e / jnp.sum(e, axis=-1, keepdims=True)

    self = jnp.where(sel, 1.0, 0.0)
    incl = jnp.dot(tri_ref[...], self.astype(bf16), preferred_element_type=f32)
    run = run_scr[0:1, :]
    rank = run + incl - self
    total = run + incl[tm - 1:tm, :]
    run_scr[...] = jnp.broadcast_to(total, run_scr.shape)
    cnt_ref[...] = jnp.broadcast_to(total, cnt_ref.shape)

    te = jnp.zeros((tm, LANES), jnp.int32)
    tr = jnp.zeros((tm, LANES), f32)
    tg = jnp.zeros((tm, LANES), f32)
    for k, (oh, idx) in enumerate(onehots):
        te = jnp.where(lane == k, idx, te)
        tr = jnp.where(lane == k, jnp.sum(jnp.where(oh, rank, 0.0), axis=-1, keepdims=True), tr)
        tg = jnp.where(lane == k, jnp.sum(jnp.where(oh, gates, 0.0), axis=-1, keepdims=True), tg)
    te_ref[...] = te
    tr_ref[...] = tr.astype(jnp.int32)
    tg_ref[...] = tg


def _merge_router(x2d, ylru, onsa, gab, wa, wb, wo, g_ffn, rw_pad, rb_pad, tm):
    m = x2d.shape[0]
    assert m % tm == 0
    f32 = jnp.float32
    row = lambda i: (i, 0)
    const = lambda i: (0, 0)
    tri = jnp.asarray(np.tril(np.ones((tm, tm), np.float32)), dtype=jnp.bfloat16)
    lane_out = lambda dt: (jax.ShapeDtypeStruct((m, LANES), dt), pl.BlockSpec((tm, LANES), row))
    outs = [
        (jax.ShapeDtypeStruct((m, D_MODEL), f32), pl.BlockSpec((tm, D_MODEL), row)),
        (jax.ShapeDtypeStruct((m, D_MODEL), f32), pl.BlockSpec((tm, D_MODEL), row)),
        lane_out(jnp.int32), lane_out(jnp.int32), lane_out(f32),
        (jax.ShapeDtypeStruct((8, LANES), f32), pl.BlockSpec((8, LANES), const)),
    ]
    return pl.pallas_call(
        _merge_router_kernel,
        grid=(m // tm,),
        in_specs=[
            pl.BlockSpec((tm, D_MODEL), row),
            pl.BlockSpec((tm, D_RNN), row),
            pl.BlockSpec((tm, D_Q), row),
            pl.BlockSpec((tm, 2 * D_MODEL), row),
            pl.BlockSpec((D_RNN, D_MODEL), const),
            pl.BlockSpec((D_Q, D_MODEL), const),
            pl.BlockSpec((D_MODEL, D_MODEL), const),
            pl.BlockSpec((1, D_MODEL), const),
            pl.BlockSpec((D_MODEL, LANES), const),
            pl.BlockSpec((1, LANES), const),
            pl.BlockSpec((tm, tm), const),
        ],
        out_specs=tuple(o[1] for o in outs),
        out_shape=tuple(o[0] for o in outs),
        scratch_shapes=[pltpu.VMEM((8, LANES), f32)],
        compiler_params=pltpu.CompilerParams(dimension_semantics=("arbitrary",), vmem_limit_bytes=VMEM_LIMIT),
        name="merge_router",
    )(x2d, ylru, onsa, gab, wa, wb, wo, g_ffn.reshape(1, D_MODEL), rw_pad, rb_pad, tri)


MOE_BLK = 512


def _moe_dispatch_kernel(dest_ref, hn_ref, xs_in_ref, xs_ref, sem):
    del xs_in_ref
    tm = hn_ref.shape[0]

    def issue(r, _):
        for k in range(TOP_K):
            d = dest_ref[0, 0, r * TOP_K + k]
            pltpu.make_async_copy(hn_ref.at[pl.ds(r, 1)], xs_ref.at[pl.ds(d, 1)], sem).start()
        return 0

    lax.fori_loop(0, tm, issue, 0)
    pltpu.make_async_copy(xs_ref.at[pl.ds(0, tm * TOP_K)], xs_ref.at[pl.ds(0, tm * TOP_K)], sem).wait()


def _moe_dispatch(hn, dest, n_slots, tm):
    m = hn.shape[0]
    nt = m // tm
    xs0 = jnp.zeros((n_slots, D_MODEL), hn.dtype)
    return pl.pallas_call(
        _moe_dispatch_kernel,
        grid=(nt,),
        in_specs=[
            pl.BlockSpec((1, 1, tm * TOP_K), lambda i: (i, 0, 0), memory_space=pltpu.SMEM),
            pl.BlockSpec((tm, D_MODEL), lambda i: (i, 0)),
            pl.BlockSpec(memory_space=pl.ANY),
        ],
        out_specs=pl.BlockSpec(memory_space=pl.ANY),
        out_shape=jax.ShapeDtypeStruct((n_slots, D_MODEL), hn.dtype),
        scratch_shapes=[pltpu.SemaphoreType.DMA(())],
        input_output_aliases={2: 0},
        compiler_params=pltpu.CompilerParams(dimension_semantics=("arbitrary",), vmem_limit_bytes=VMEM_LIMIT),
        name="moe_dispatch",
    )(dest.reshape(nt, 1, tm * TOP_K), hn, xs0)


def _moe_experts_kernel(te_ref, nused_ref, xs_ref, wup_ref, bup_ref, wdn_ref, bdn_ref, ys_ref, wup_scr, wdn_scr):
    f32, bf16 = jnp.float32, jnp.bfloat16
    j = pl.program_id(0)
    changed = jnp.logical_or(j == 0, te_ref[j] != te_ref[jnp.maximum(j - 1, 0)])

    @pl.when(jnp.logical_and(j < nused_ref[0], changed))
    def _():
        wup_scr[...] = wup_ref[0].astype(bf16)
        wdn_scr[...] = wdn_ref[0].astype(bf16)

    @pl.when(j < nused_ref[0])
    def _():
        x = xs_ref[...].astype(bf16)
        gu = jnp.dot(x, wup_scr[...], preferred_element_type=f32) + bup_ref[0]
        a = jnp.minimum(gu[:, :D_FF], SWIGLU_LIMIT)
        u = jnp.clip(gu[:, D_FF:], -SWIGLU_LIMIT, SWIGLU_LIMIT)
        act = (u + 1.0) * a * jax.nn.sigmoid(SWIGLU_ALPHA * a)
        ys_ref[...] = jnp.dot(act.astype(bf16), wdn_scr[...], preferred_element_type=f32) + bdn_ref[0]


def _moe_experts(xs, tile_expert, n_used, w_up, b_up, w_down, b_down):
    n_slots = xs.shape[0]
    nt = n_slots // MOE_BLK
    tile = lambda j, te, nu: (jnp.minimum(j, nu[0] - 1), 0)
    wsel = lambda j, te, nu: (te[jnp.minimum(j, nu[0] - 1)], 0, 0)
    grid_spec = pltpu.PrefetchScalarGridSpec(
        num_scalar_prefetch=2,
        grid=(nt,),
        in_specs=[
            pl.BlockSpec((MOE_BLK, D_MODEL), tile),
            pl.BlockSpec((1, D_MODEL, 2 * D_FF), wsel),
            pl.BlockSpec((1, 1, 2 * D_FF), wsel),
            pl.BlockSpec((1, D_FF, D_MODEL), wsel),
            pl.BlockSpec((1, 1, D_MODEL), wsel),
        ],
        out_specs=pl.BlockSpec((MOE_BLK, D_MODEL), tile),
        scratch_shapes=[pltpu.VMEM((D_MODEL, 2 * D_FF), jnp.bfloat16), pltpu.VMEM((D_FF, D_MODEL), jnp.bfloat16)],
    )
    return pl.pallas_call(
        _moe_experts_kernel,
        grid_spec=grid_spec,
        out_shape=jax.ShapeDtypeStruct((n_slots, D_MODEL), jnp.float32),
        compiler_params=pltpu.CompilerParams(dimension_semantics=("arbitrary",), vmem_limit_bytes=VMEM_LIMIT),
        name="moe_experts",
    )(tile_expert, n_used, xs, w_up, b_up.reshape(N_EXPERTS, 1, 2 * D_FF), w_down,
      b_down.reshape(N_EXPERTS, 1, D_MODEL))


def _moe_combine_kernel(dest_ref, xn_ref, tg_ref, ys_ref, y_ref, buf, sem):
    tm = xn_ref.shape[0]

    def issue(r, _):
        for k in range(TOP_K):
            d = dest_ref[0, 0, r * TOP_K + k]
            pltpu.make_async_copy(ys_ref.at[pl.ds(d, 1)], buf.at[k, pl.ds(r, 1)], sem).start()
        return 0

    lax.fori_loop(0, tm, issue, 0)
    pltpu.make_async_copy(ys_ref.at[pl.ds(0, tm * TOP_K)], ys_ref.at[pl.ds(0, tm * TOP_K)], sem).wait()
    tg = tg_ref[...]
    acc = xn_ref[...]
    for k in range(TOP_K):
        acc = acc + tg[:, k:k + 1] * buf[k]
    y_ref[...] = acc


def _moe_combine(xn, tg, ys, dest, tm):
    m = xn.shape[0]
    nt = m // tm
    return pl.pallas_call(
        _moe_combine_kernel,
        grid=(nt,),
        in_specs=[
            pl.BlockSpec((1, 1, tm * TOP_K), lambda i: (i, 0, 0), memory_space=pltpu.SMEM),
            pl.BlockSpec((tm, D_MODEL), lambda i: (i, 0)),
            pl.BlockSpec((tm, LANES), lambda i: (i, 0)),
            pl.BlockSpec(memory_space=pl.ANY),
        ],
        out_specs=pl.BlockSpec((tm, D_MODEL), lambda i: (i, 0)),
        out_shape=jax.ShapeDtypeStruct((m, D_MODEL), jnp.float32),
        scratch_shapes=[pltpu.VMEM((TOP_K, tm, D_MODEL), jnp.float32), pltpu.SemaphoreType.DMA(())],
        compiler_params=pltpu.CompilerParams(dimension_semantics=("arbitrary",), vmem_limit_bytes=VMEM_LIMIT),
        name="moe_combine",
    )(dest.reshape(nt, 1, tm * TOP_K), xn, tg, ys)


def _moe_layer(xn, hn, te, tr, tg, cnt, w_up, b_up, w_down, b_down, tm):
    m = xn.shape[0]
    n_tiles = -(-m * TOP_K // MOE_BLK) + N_EXPERTS
    counts = cnt[0, :N_EXPERTS].astype(jnp.int32)
    padded = (counts + MOE_BLK - 1) // MOE_BLK * MOE_BLK
    pad_end = jnp.cumsum(padded)
    pad_start = pad_end - padded
    dest = (pad_start[te[:, :TOP_K]] + tr[:, :TOP_K]).astype(jnp.int32)
    tile_expert = jnp.minimum(
        jnp.searchsorted(pad_end, jnp.arange(n_tiles, dtype=jnp.int32) * MOE_BLK, side='right'),
        N_EXPERTS - 1).astype(jnp.int32)
    n_used = (pad_end[-1:] // MOE_BLK).astype(jnp.int32)
    xs = _moe_dispatch(hn, dest, n_tiles * MOE_BLK, tm)
    ys = _moe_experts(xs, tile_expert, n_used, w_up, b_up, w_down, b_down)
    return _moe_combine(xn, tg, ys, dest, tm)


def _masked_softmax(s, mask):
    s = jnp.where(mask, s.astype(jnp.float32), NEG)
    m = jnp.max(s, axis=-1, keepdims=True)
    p = jnp.where(mask, jnp.exp(s - m), 0.0)
    return p / jnp.maximum(jnp.sum(p, axis=-1, keepdims=True), 1e-30)


def _merge_branches(y_lru, o_nsa, ga, gb, w_proj_a, w_proj_b, w_out):
    return (jax.nn.sigmoid(ga) * (y_lru @ w_proj_a) + jax.nn.sigmoid(gb) * (o_nsa @ w_proj_b)) @ w_out


def _lin_combine(c1, c2):
    a1, b1 = c1
    a2, b2 = c2
    return a1 * a2, a2 * b1 + b2


def _rglru(xr, gate_in, conv_buf, h0, conv_w, conv_b, wa, ba, wx, bx, lam):
    n, s, _ = xr.shape
    xp = jnp.concatenate([conv_buf.astype(xr.dtype), xr], axis=1)
    xc = conv_b + sum(conv_w[k] * xp[:, k:k + s] for k in range(CONV_W))
    xb = xc.reshape(n, s, LRU_BLOCKS, LRU_BW)
    r = jax.nn.sigmoid((jnp.einsum('nsbi,bij->nsbj', xb, wa) + ba).astype(jnp.float32)).reshape(n, s, D_RNN)
    i = jax.nn.sigmoid((jnp.einsum('nsbi,bij->nsbj', xb, wx) + bx).astype(jnp.float32)).reshape(n, s, D_RNN)
    log_a = -LRU_C * r * jax.nn.softplus(-lam.astype(jnp.float32))
    a = jnp.exp(log_a)
    u = jnp.sqrt(-jnp.expm1(2.0 * log_a)) * (i * xc.astype(jnp.float32))
    u = u.at[:, 0].add(a[:, 0] * h0.astype(jnp.float32))
    _, h = lax.associative_scan(_lin_combine, (a, u), axis=1)
    y = (h * jax.nn.gelu(gate_in.astype(jnp.float32))).astype(xr.dtype)
    return y, xp[:, s:], h[:, -1]


def _compress(rows, pos_emb, w1, w2):
    n, l = rows.shape[:2]
    r_ = CMP_LEN // CMP_STRIDE
    nch = l // CMP_STRIDE
    ch = rows[:, :nch * CMP_STRIDE].reshape(n, nch, CMP_STRIDE, N_KV, HEAD_DIM)
    w1b = w1.reshape(r_, CMP_STRIDE, HEAD_DIM, CMP_HID)
    part = jnp.einsum('nclhd,pldf->pnchf', ch, w1b)
    nc = nch - r_ + 1
    hid = jnp.einsum('ld,ldf->f', pos_emb, w1) + sum(part[p, :, p:p + nc] for p in range(r_))
    return jnp.einsum('nchf,fd->nchd', jax.nn.gelu(hid), w2)


def _block_scores(p_cmp, nsb):
    nc = p_cmp.shape[-1]
    ratio = SLC_BLOCK // CMP_STRIDE
    ovl = CMP_LEN // CMP_STRIDE - 1
    widths = [(0, 0)] * (p_cmp.ndim - 1) + [(ovl, ratio * (nsb + 1) - nc - ovl)]
    pp = jnp.pad(p_cmp, widths).reshape(*p_cmp.shape[:-1], nsb + 1, ratio)
    return pp[..., :nsb, :].sum(-1) + pp[..., 1:, :ovl].sum(-1)


def _select_blocks(scores, pos, nsb):
    j = jnp.arange(nsb, dtype=jnp.int32)
    cur = (pos // SLC_BLOCK)[:, None, None]
    forced = (j == 0) | (j == cur) | (j == cur - 1)
    s = jnp.where(forced, BIG, jnp.where(j <= cur, scores, NEG))
    top, idx = lax.top_k(s, min(N_SEL, nsb))
    return idx, top > NEG / 2


def _nsa_attend(q, pos, cmp_k, cmp_v, gather_slc, nsb, win_k, win_v, win_pos, gates):
    n, qn = q.shape[:2]
    nc = cmp_k.shape[1]
    s = jnp.einsum('nqhgd,nchd->nqhgc', q, cmp_k)
    cmp_end = jnp.arange(nc, dtype=jnp.int32) * CMP_STRIDE + CMP_LEN - 1
    p_c = _masked_softmax(s, cmp_end <= pos[:, None, None, None])
    o_cmp = jnp.einsum('nqhgc,nchd->nqhgd', p_c.astype(cmp_v.dtype), cmp_v)
    idx, valid = _select_blocks(_block_scores(p_c.sum(3), nsb), pos, nsb)
    k_g, v_g = gather_slc(idx)
    kk = idx.shape[-1] * SLC_BLOCK
    kpos = idx[..., None] * SLC_BLOCK + jnp.arange(SLC_BLOCK, dtype=jnp.int32)
    smask = (valid[..., None] & (kpos <= pos[:, None, None, None])).reshape(n, qn, N_KV, 1, kk)
    s = jnp.einsum('nqhgd,nqhkd->nqhgk', q, k_g.reshape(n, qn, N_KV, kk, HEAD_DIM))
    p_s = _masked_softmax(s, smask)
    o_slc = jnp.einsum('nqhgk,nqhkd->nqhgd', p_s.astype(v_g.dtype), v_g.reshape(n, qn, N_KV, kk, HEAD_DIM))
    s = jnp.einsum('nqhgd,nkhd->nqhgk', q, win_k)
    dpos = pos[:, None] - win_pos[None, :]
    wmask = (dpos >= 0) & (dpos < WINDOW) & (win_pos >= 0)[None, :]
    p_w = _masked_softmax(s, wmask[:, None, None, :])
    o_win = jnp.einsum('nqhgk,nkhd->nqhgd', p_w.astype(win_v.dtype), win_v)
    g = jax.nn.sigmoid(gates)
    return g[..., 0:1] * o_cmp + g[..., 1:2] * o_slc + g[..., 2:3] * o_win


def _nsa_prompt(q, kc, vc, ks, vs, kw, vw, ng, pos_k, w1_k, w2_k, pos_v, w1_v, w2_v):
    n, s = q.shape[:2]
    cmp_k = _compress(kc, pos_k, w1_k, w2_k)
    cmp_v = _compress(vc, pos_v, w1_v, w2_v)
    nsb = s // SLC_BLOCK
    kb = ks.reshape(n, nsb, SLC_BLOCK, N_KV, HEAD_DIM)
    vb = vs.reshape(n, nsb, SLC_BLOCK, N_KV, HEAD_DIM)
    bi = jnp.arange(n)[:, None, None, None]
    hi = jnp.arange(N_KV)[None, None, :, None]

    def gather(idx):
        return kb[bi, idx, :, hi], vb[bi, idx, :, hi]

    pad = ((0, 0), (WINDOW, 0), (0, 0), (0, 0))
    kw_pad, vw_pad = jnp.pad(kw, pad), jnp.pad(vw, pad)

    def block(i):
        start = i * Q_BLOCK
        sl = lambda t, size: lax.dynamic_slice_in_dim(t, start, size, axis=1)
        pos = start + jnp.arange(Q_BLOCK, dtype=jnp.int32)
        wpos = start - WINDOW + jnp.arange(WINDOW + Q_BLOCK, dtype=jnp.int32)
        return _nsa_attend(sl(q, Q_BLOCK), pos, cmp_k, cmp_v, gather, nsb,
                           sl(kw_pad, WINDOW + Q_BLOCK), sl(vw_pad, WINDOW + Q_BLOCK), wpos, sl(ng, Q_BLOCK))

    o = lax.map(block, jnp.arange(s // Q_BLOCK, dtype=jnp.int32))
    return jnp.moveaxis(o, 0, 1).reshape(n, s, D_Q)


def _nsa_sample(q, kc, vc, ks, vs, kw, vw, ng, page_table, ck_cmp, cv_cmp, ck_slc, cv_slc, ck_win, cv_win,
                pos_k, w1_k, w2_k, pos_v, w1_v, w2_v):
    n, ds = q.shape[:2]
    pos = PAST_LEN + jnp.arange(ds, dtype=jnp.int32)

    def past(pool):
        return pool[page_table].reshape(n, PAST_LEN, N_KV, HEAD_DIM)

    cmp_k = _compress(jnp.concatenate([past(ck_cmp), kc], axis=1), pos_k, w1_k, w2_k)
    cmp_v = _compress(jnp.concatenate([past(cv_cmp), vc], axis=1), pos_v, w1_v, w2_v)
    past_blocks = PAST_LEN // SLC_BLOCK
    new_blocks = -(-ds // SLC_BLOCK)
    nsb = past_blocks + new_blocks
    bpp = PAGE_SIZE // SLC_BLOCK
    view = lambda pool: pool.reshape(pool.shape[0], bpp, SLC_BLOCK, N_KV, HEAD_DIM)
    pk, pv = view(ck_slc), view(cv_slc)
    padn = ((0, 0), (0, new_blocks * SLC_BLOCK - ds), (0, 0), (0, 0))
    nk = jnp.pad(ks, padn).reshape(n, new_blocks, SLC_BLOCK, N_KV, HEAD_DIM)
    nv = jnp.pad(vs, padn).reshape(n, new_blocks, SLC_BLOCK, N_KV, HEAD_DIM)
    bi = jnp.arange(n)[:, None, None, None]
    hi = jnp.arange(N_KV)[None, None, :, None]

    def gather(idx):
        jp = jnp.minimum(idx, past_blocks - 1)
        phys = page_table[bi, jp // bpp]
        sub = jp % bpp
        jn = jnp.clip(idx - past_blocks, 0, new_blocks - 1)
        is_new = (idx >= past_blocks)[..., None, None]
        return (jnp.where(is_new, nk[bi, jn, :, hi], pk[phys, sub, :, hi]),
                jnp.where(is_new, nv[bi, jn, :, hi], pv[phys, sub, :, hi]))

    wb = ck_win.shape[1]
    wk = jnp.concatenate([ck_win, kw], axis=1)
    wv = jnp.concatenate([cv_win, vw], axis=1)
    wpos = PAST_LEN - wb + jnp.arange(wb + ds, dtype=jnp.int32)
    o = _nsa_attend(q, pos, cmp_k, cmp_v, gather, nsb, wk, wv, wpos, ng)
    return o.reshape(n, ds, D_Q), wk[:, -wb:], wv[:, -wb:]


def _rms_norm(x, g):
    xf = x.astype(jnp.float32)
    y = xf * lax.rsqrt(jnp.mean(xf * xf, axis=-1, keepdims=True) + EPS)
    return (y * g.astype(jnp.float32)).astype(x.dtype)


def _moe(h, router_w, router_b, w_up, b_up, w_down, b_down):
    shp = h.shape
    t = h.reshape(-1, D_MODEL)
    n_tok = t.shape[0]
    logits = (t @ router_w).astype(jnp.float32) + router_b.astype(jnp.float32)
    top_v, top_e = lax.top_k(logits, TOP_K)
    gate = jax.nn.softmax(top_v, axis=-1)
    n_asg = n_tok * TOP_K
    blk = int(min(512, max(8, n_asg // N_EXPERTS)))
    n_blocks = -(-n_asg // blk) + N_EXPERTS
    e_flat = top_e.reshape(-1)
    tok_flat = jnp.repeat(jnp.arange(n_tok, dtype=jnp.int32), TOP_K)
    g_flat = gate.reshape(-1)
    order = jnp.argsort(e_flat)
    e_sorted = e_flat[order]
    counts = jnp.bincount(e_flat, length=N_EXPERTS)
    padded = (counts + blk - 1) // blk * blk
    pad_end = jnp.cumsum(padded)
    pad_start = pad_end - padded
    starts = jnp.cumsum(counts) - counts
    dest = pad_start[e_sorted] + jnp.arange(n_asg, dtype=jnp.int32) - starts[e_sorted]
    slot_tok = jnp.full((n_blocks * blk,), n_tok, jnp.int32).at[dest].set(tok_flat[order])
    slot_gate = jnp.zeros((n_blocks * blk,), jnp.float32).at[dest].set(g_flat[order])
    blk_expert = jnp.minimum(jnp.searchsorted(pad_end, jnp.arange(n_blocks, dtype=jnp.int32) * blk, side='right'),
                             N_EXPERTS - 1)
    t_pad = jnp.concatenate([t, jnp.zeros((1, D_MODEL), t.dtype)], axis=0)

    def run_block(args):
        tok, g, e = args
        gu = t_pad[tok] @ w_up[e] + b_up[e]
        a = jnp.minimum(gu[:, :D_FF], SWIGLU_LIMIT)
        u = jnp.clip(gu[:, D_FF:], -SWIGLU_LIMIT, SWIGLU_LIMIT)
        y = ((u + 1.0) * a * jax.nn.sigmoid(SWIGLU_ALPHA * a)) @ w_down[e] + b_down[e]
        return (y * g[:, None]).astype(t.dtype)

    ys = lax.map(run_block, (slot_tok.reshape(n_blocks, blk), slot_gate.reshape(n_blocks, blk), blk_expert))
    out = jnp.zeros((n_tok + 1, D_MODEL), t.dtype).at[slot_tok].add(ys.reshape(-1, D_MODEL))
    return out[:n_tok].reshape(shp)


def _pad_w_in(w):
    c0 = 2 * D_RNN
    wq = w[:, c0:c0 + D_Q].reshape(D_MODEL, N_KV, GROUP, 1, HEAD_DIM)
    sel = jnp.eye(N_KV, dtype=w.dtype).reshape(N_KV, 1, N_KV, 1)
    wq = (wq * sel).reshape(D_MODEL, D_QP)
    c1 = c0 + D_Q
    c2 = c1 + 6 * D_KV + N_NG
    pad = jnp.zeros((D_MODEL, LANES - N_NG), w.dtype)
    return jnp.concatenate([w[:, :c0], wq, w[:, c1:c2], pad, w[:, c2:]], axis=1).astype(jnp.bfloat16)


def _unpad_q(q_pad, n, s):
    qp = q_pad.reshape(n, s, N_KV, GROUP, N_KV, HEAD_DIM)
    return jnp.stack([qp[:, :, h, :, h, :] for h in range(N_KV)], axis=2)


def _project(x, pos_rows, tm, g_mix, w_pad, g_q, g_k3):
    n, s, _ = x.shape
    outs = _input_projection(x.reshape(n * s, D_MODEL), pos_rows, tm, g_mix, w_pad, g_q, g_k3)
    arag, q, kc, vc, ks, vs, kw, vw, ng, gab = outs
    heads = lambda t: t.reshape(n, s, N_KV, HEAD_DIM)
    ar = arag[:, :D_RNN].reshape(n, s, D_RNN)
    ag = arag[:, D_RNN:].reshape(n, s, D_RNN)
    q = _unpad_q(q, n, s)
    ng = ng[:, :N_NG].reshape(n, s, N_KV, GROUP, 3)
    ga = gab[:, :D_MODEL].reshape(n, s, D_MODEL)
    gb = gab[:, D_MODEL:].reshape(n, s, D_MODEL)
    return ar, ag, q, heads(kc), heads(vc), heads(ks), heads(vs), heads(kw), heads(vw), ng, ga, gb


def kernel(x_prompt, x_sample, page_table, cache_k_cmp, cache_v_cmp, cache_k_slc, cache_v_slc,
           cache_k_win, cache_v_win, state_h, state_conv, g_mix, w_in, conv_w, conv_b, lru_wa, lru_ba,
           lru_wx, lru_bx, lru_lambda, g_q, g_k_cmp, g_k_slc, g_k_win, cmp_pos_k, cmp_w1_k, cmp_w2_k,
           cmp_pos_v, cmp_w1_v, cmp_w2_v, w_proj_a, w_proj_b, w_out, g_ffn, router_w, router_b,
           exp_w_up, exp_b_up, exp_w_down, exp_b_down):
    xp, xs = x_prompt, x_sample
    bp, sp = xp.shape[:2]
    bs, ss = xs.shape[:2]
    pos_p = jnp.arange(sp, dtype=jnp.int32)
    pos_s = PAST_LEN + jnp.arange(ss, dtype=jnp.int32)
    depth = w_in.shape[0]
    layer_states = []
    tile2 = lambda g: jnp.concatenate([g, g], axis=-1)
    for l in range(depth):
        cmp_w = (cmp_pos_k[l], cmp_w1_k[l], cmp_w2_k[l], cmp_pos_v[l], cmp_w1_v[l], cmp_w2_v[l])
        lru_w = (conv_w[l], conv_b[l], lru_wa[l], lru_ba[l], lru_wx[l], lru_bx[l], lru_lambda[l])
        w_pad = _pad_w_in(w_in[l])
        gq2 = tile2(g_q[l]).reshape(1, LANES)
        gk3 = jnp.stack([tile2(g_k_cmp[l]), tile2(g_k_slc[l]), tile2(g_k_win[l])])
        cmp_wk = _compress_weights(cmp_pos_k[l], cmp_w1_k[l], cmp_w2_k[l])
        cmp_wv = _compress_weights(cmp_pos_v[l], cmp_w1_v[l], cmp_w2_v[l])
        wb_perm = w_proj_b[l].reshape(N_KV, GROUP, HEAD_DIM, D_MODEL).transpose(1, 0, 2, 3).reshape(D_Q, D_MODEL)
        bf16 = jnp.bfloat16
        lru_k = _lru_weights(*lru_w)
        rw_pad = jnp.pad(router_w[l].astype(jnp.float32), ((0, 0), (0, LANES - N_EXPERTS)))
        rb_pad = jnp.pad(router_b[l].astype(jnp.float32), (0, LANES - N_EXPERTS)).reshape(1, LANES)
        merge_w = (w_proj_a[l].astype(bf16), wb_perm.astype(bf16), w_out[l].astype(bf16), g_ffn[l], rw_pad, rb_pad)
        moe_w = (exp_w_up[l], exp_b_up[l], exp_w_down[l], exp_b_down[l])
        arag, q_pad, kc, vc, ks, vs, kw, vw, ng, gab = _input_projection(
            xp.reshape(bp * sp, D_MODEL), pos_p, 256, g_mix[l], w_pad, gq2, gk3)
        rows = lambda t: t.reshape(bp, sp, t.shape[-1])
        heads = lambda t: t.reshape(bp, sp, N_KV, HEAD_DIM)
        conv0 = jnp.zeros((bp, CONV_W - 1, D_RNN), jnp.float32)
        y_lru, p_h = _lru_prompt(arag, conv0, jnp.zeros((bp, D_RNN), jnp.float32), lru_k, bp, sp, 256)
        p_conv = jnp.concatenate([conv0, rows(arag)[:, :, :D_RNN]], axis=1)[:, sp:]
        cmp_k, cmp_v = _compress_prompt(rows(kc), rows(vc), cmp_wk, cmp_wv)
        o_nsa = _nsa_prompt_attention(rows(q_pad), rows(ks), rows(vs), rows(kw), rows(vw), cmp_k, cmp_v, rows(ng))
        xn, hn, te, tr, tg, cnt = _merge_router(xp.reshape(bp * sp, D_MODEL), y_lru, o_nsa.reshape(bp * sp, D_Q),
                                                gab, *merge_w, 256)
        xp = _moe_layer(xn, hn, te, tr, tg, cnt, *moe_w, 256).reshape(bp, sp, D_MODEL)
        wbp = min(WINDOW, sp)
        p_states = (heads(kc), heads(vc), heads(ks), heads(vs), heads(kw)[:, -wbp:], heads(vw)[:, -wbp:],
                    p_h, p_conv)
        assert ss == 1
        pos_rows_s = jnp.broadcast_to(pos_s[None, :], (bs, ss)).reshape(-1)
        arag, q_pad, kc, vc, ks, vs, kw, vw, ng, gab = _input_projection(
            xs.reshape(bs, D_MODEL), pos_rows_s, bs, g_mix[l], w_pad, gq2, gk3)
        heads = lambda t: t.reshape(bs, ss, N_KV, HEAD_DIM)
        y_lru, s_h_l = _lru_step(arag, state_conv[l], state_h[l], lru_k)
        s_conv_l = jnp.concatenate([state_conv[l][:, 1:], arag[:, None, :D_RNN]], axis=1)
        q = _unpad_q(q_pad, bs, ss).astype(jnp.float32)
        ng5 = ng[:, :N_NG].reshape(bs, ss, N_KV, GROUP, 3)
        o_nsa, s_kw_l, s_vw_l = _nsa_sample(q, heads(kc), heads(vc), heads(ks), heads(vs), heads(kw), heads(vw), ng5,
                                            page_table, cache_k_cmp[l], cache_v_cmp[l], cache_k_slc[l],
                                            cache_v_slc[l], cache_k_win[l], cache_v_win[l], *cmp_w)
        o_nsa = o_nsa.reshape(bs, N_KV, GROUP, HEAD_DIM).transpose(0, 2, 1, 3).reshape(bs, D_Q).astype(bf16)
        xn, hn, te, tr, tg, cnt = _merge_router(xs.reshape(bs, D_MODEL), y_lru, o_nsa, gab, *merge_w, bs)
        xs = _moe_layer(xn, hn, te, tr, tg, cnt, *moe_w, bs).reshape(bs, ss, D_MODEL)
        s_states = (heads(kc), heads(vc), heads(ks), heads(vs), s_kw_l, s_vw_l, s_h_l, s_conv_l)
        layer_states.append(p_states + s_states)
    (p_k_cmp, p_v_cmp, p_k_slc, p_v_slc, p_k_win, p_v_win, p_h, p_conv,
     s_k_cmp, s_v_cmp, s_k_slc, s_v_slc, s_k_win, s_v_win, s_h, s_conv) = [jnp.stack(z) for z in zip(*layer_states)]
    return (xp, xs, p_k_cmp, p_v_cmp, p_k_slc, p_v_slc, p_k_win, p_v_win, p_h, p_conv,
            s_k_cmp, s_v_cmp, s_k_slc, s_v_slc, s_k_win, s_v_win, s_h, s_conv)
```

```python
import functools
import math

import jax
import jax.numpy as jnp
import numpy as np
from jax import lax
from jax.experimental import pallas as pl
from jax.experimental.pallas import tpu as pltpu

D_MODEL = 1024
PAST_LEN = 8192
PAGE_SIZE = 128
D_RNN = D_MODEL
LRU_BLOCKS = 16
LRU_BW = D_RNN // LRU_BLOCKS
CONV_W = 4
LRU_C = 8.0
N_HEADS = 8
HEAD_DIM = 64
N_KV = 2
GROUP = N_HEADS // N_KV
D_Q = N_HEADS * HEAD_DIM
D_KV = N_KV * HEAD_DIM
ROPE_DIM = HEAD_DIM // 4
ROPE_THETA = 500000.0
CMP_LEN = 32
CMP_STRIDE = 16
CMP_HID = 256
SLC_BLOCK = 64
N_SEL = 16
WINDOW = 512
Q_BLOCK = 64
N_EXPERTS = 32
TOP_K = 4
D_FF = D_MODEL
SWIGLU_LIMIT = 7.0
SWIGLU_ALPHA = 1.702
EPS = 1e-6
NEG = -1e30
BIG = 1e30

LANES = 128
SUBLANES = 8
VMEM_LIMIT = 56 * 1024 * 1024

N_NG = 3 * N_HEADS
D_QP = N_HEADS * LANES
C_AR = 0
C_Q = 2 * D_RNN
C_KV = C_Q + D_QP
C_NG = C_KV + 6 * D_KV
C_GAB = C_NG + LANES
N_IN_PAD = C_GAB + 2 * D_MODEL


def _seg_sum64(x2, ones_blk):
    hi = x2.astype(jnp.bfloat16)
    lo = (x2 - hi.astype(jnp.float32)).astype(jnp.bfloat16)
    return (jnp.dot(hi, ones_blk, preferred_element_type=jnp.float32)
            + jnp.dot(lo, ones_blk, preferred_element_type=jnp.float32))


def _head_norm_rope(y, gain, ones_blk, rc, rs1, rs2):
    ss = _seg_sum64(y * y, ones_blk)
    yn = y * lax.rsqrt(ss * (1.0 / HEAD_DIM) + EPS) * gain
    half = ROPE_DIM // 2
    return yn * rc + pltpu.roll(yn, LANES - half, 1) * rs1 + pltpu.roll(yn, half, 1) * rs2


def _proj_kernel(x_ref, gmix_ref, w_ref, ones_ref, gq_ref, gk_ref, rc_ref, rs1_ref, rs2_ref,
                 arag_ref, q_ref, kc_ref, vc_ref, ks_ref, vs_ref, kw_ref, vw_ref, ng_ref, gab_ref):
    x = x_ref[...]
    xn = x * lax.rsqrt(jnp.mean(x * x, axis=-1, keepdims=True) + EPS) * gmix_ref[...]
    xb = xn.astype(jnp.bfloat16)

    def proj(c0, width):
        return jnp.dot(xb, w_ref[:, c0:c0 + width], preferred_element_type=jnp.float32)

    arag_ref[...] = proj(C_AR, 2 * D_RNN)
    gab_ref[...] = proj(C_GAB, 2 * D_MODEL)
    ng_ref[...] = proj(C_NG, LANES)
    ones_blk = ones_ref[...]
    rc, rs1, rs2 = rc_ref[...], rs1_ref[...], rs2_ref[...]
    scale = HEAD_DIM ** -0.5
    for j in range(N_HEADS):
        y = proj(C_Q + j * LANES, LANES)
        yr = _head_norm_rope(y, gq_ref[...], ones_blk, rc, rs1, rs2)
        q_ref[:, j * LANES:(j + 1) * LANES] = (yr * scale).astype(q_ref.dtype)
    k_outs = (kc_ref, ks_ref, kw_ref)
    v_outs = (vc_ref, vs_ref, vw_ref)
    for j in range(3):
        y = proj(C_KV + 2 * j * LANES, LANES)
        k_outs[j][...] = _head_norm_rope(y, gk_ref[j:j + 1, :], ones_blk, rc, rs1, rs2)
        v_outs[j][...] = proj(C_KV + (2 * j + 1) * LANES, LANES)


def _rope_tables(pos):
    half = ROPE_DIM // 2
    inv = ROPE_THETA ** (-jnp.arange(half, dtype=jnp.float32) / half)
    ang = pos.astype(jnp.float32)[:, None] * inv[None, :]
    cos, sin = jnp.cos(ang), jnp.sin(ang)
    p = pos.shape[0]
    ones = jnp.ones((p, HEAD_DIM - ROPE_DIM), jnp.float32)
    zeros = jnp.zeros((p, HEAD_DIM - ROPE_DIM), jnp.float32)
    zh = jnp.zeros((p, half), jnp.float32)
    rc = jnp.concatenate([cos, cos, ones], axis=-1)
    rs1 = jnp.concatenate([-sin, zh, zeros], axis=-1)
    rs2 = jnp.concatenate([zh, sin, zeros], axis=-1)
    tile2 = lambda t: jnp.concatenate([t, t], axis=-1)
    return tile2(rc), tile2(rs1), tile2(rs2)


def _ones_block64():
    i = np.arange(LANES)
    return jnp.asarray((i[:, None] // HEAD_DIM) == (i[None, :] // HEAD_DIM), dtype=jnp.bfloat16)


def _input_projection(x2d, pos_rows, tm, g_mix, w_pad, g_q, g_k3):
    m = x2d.shape[0]
    p = pos_rows.shape[0]
    assert m % tm == 0 and p % tm == 0
    rc, rs1, rs2 = _rope_tables(pos_rows)
    n_tab = p // tm
    row = lambda i: (i, 0)
    const = lambda i: (0, 0)
    tab = lambda i: (i % n_tab, 0)
    f32 = jnp.float32
    out_shape = (
        jax.ShapeDtypeStruct((m, 2 * D_RNN), f32),
        jax.ShapeDtypeStruct((m, D_QP), jnp.bfloat16),
    ) + tuple(jax.ShapeDtypeStruct((m, D_KV), f32) for _ in range(6)) + (
        jax.ShapeDtypeStruct((m, LANES), f32),
        jax.ShapeDtypeStruct((m, 2 * D_MODEL), f32),
    )
    out_specs = (
        pl.BlockSpec((tm, 2 * D_RNN), row),
        pl.BlockSpec((tm, D_QP), row),
    ) + tuple(pl.BlockSpec((tm, D_KV), row) for _ in range(6)) + (
        pl.BlockSpec((tm, LANES), row),
        pl.BlockSpec((tm, 2 * D_MODEL), row),
    )
    return pl.pallas_call(
        _proj_kernel,
        grid=(m // tm,),
        in_specs=[
            pl.BlockSpec((tm, D_MODEL), row),
            pl.BlockSpec((1, D_MODEL), const),
            pl.BlockSpec((D_MODEL, N_IN_PAD), const),
            pl.BlockSpec((LANES, LANES), const),
            pl.BlockSpec((1, LANES), const),
            pl.BlockSpec((3, LANES), const),
            pl.BlockSpec((tm, LANES), tab),
            pl.BlockSpec((tm, LANES), tab),
            pl.BlockSpec((tm, LANES), tab),
        ],
        out_specs=out_specs,
        out_shape=out_shape,
        compiler_params=pltpu.CompilerParams(dimension_semantics=("arbitrary",), vmem_limit_bytes=VMEM_LIMIT),
        name="input_projection",
    )(x2d, g_mix.reshape(1, D_MODEL), w_pad, _ones_block64(), g_q, g_k3, rc, rs1, rs2)


SLC_CHUNK = 512
WIN_SPAN = WINDOW + LANES
ROWS = GROUP * Q_BLOCK


def _dot_nt(a, b):
    return lax.dot_general(a, b, (((1,), (1,)), ((), ())), preferred_element_type=jnp.float32)


def _softmax_rows(s, mask):
    s = jnp.where(mask, s, NEG)
    m = jnp.max(s, axis=-1, keepdims=True)
    p = jnp.where(mask, jnp.exp(s - m), 0.0)
    return p / jnp.maximum(jnp.sum(p, axis=-1, keepdims=True), 1e-30)


def _split3_bf16(x):
    hi = x.astype(jnp.bfloat16)
    r1 = x - hi.astype(jnp.float32)
    mid = r1.astype(jnp.bfloat16)
    lo = (r1 - mid.astype(jnp.float32)).astype(jnp.bfloat16)
    return hi, mid, lo


def _nsa_prompt_kernel(q_ref, ksa_ref, vs_ref, kw_ref, vw_ref, ck_ref, cv_ref, ng_ref, mmt_ref,
                       o_ref, sv_scr):
    f32, bf16 = jnp.float32, jnp.bfloat16
    qb = pl.program_id(1)
    row = lax.broadcasted_iota(jnp.int32, (ROWS, 1), 0)
    pos_t = qb * Q_BLOCK + (row & (Q_BLOCK - 1))
    ncp = ck_ref.shape[1]

    qh = [jnp.concatenate([q_ref[0, :, (h * GROUP + g) * LANES:(h * GROUP + g + 1) * LANES]
                           for g in range(GROUP)], axis=0) for h in range(N_KV)]

    ck, cv = ck_ref[0], cv_ref[0]
    c_end = lax.broadcasted_iota(jnp.int32, (1, ncp), 1) * CMP_STRIDE + (CMP_LEN - 1)
    cmask = c_end <= pos_t
    o_cmp, psum = [], []
    for h in range(N_KV):
        p = _softmax_rows(_dot_nt(qh[h], ck), cmask)
        o_cmp.append(jnp.dot(p.astype(bf16), cv, preferred_element_type=f32))
        psum.append(p[0:Q_BLOCK] + p[Q_BLOCK:2 * Q_BLOCK] + p[2 * Q_BLOCK:3 * Q_BLOCK] + p[3 * Q_BLOCK:])
    ps = jnp.concatenate(psum, axis=0)

    mmt = mmt_ref[...]
    sc = sum(_dot_nt(mmt, part) for part in _split3_bf16(ps))[0:Q_BLOCK]
    jio = lax.broadcasted_iota(jnp.int32, (Q_BLOCK, LANES), 0)
    forced = (jio == 0) | (jio == qb) | (jio == qb - 1)
    sv = jnp.where(forced, BIG, jnp.where(jio <= qb, sc, NEG))
    sv_scr[...] = sv

    def rank_body(i, cnt):
        r = sv_scr[pl.ds(i, 1), :]
        beats = (r > sv) | ((r == sv) & (jio > i))
        return cnt + jnp.where(beats, 1.0, 0.0)

    cnt = lax.fori_loop(0, qb + 1, rank_body, jnp.zeros((Q_BLOCK, LANES), f32))
    bias_lo = jnp.where((cnt < N_SEL) & (sv > NEG / 2), 0.0, NEG)
    lane = lax.broadcasted_iota(jnp.int32, (Q_BLOCK, LANES), 1)
    bias_t = jnp.concatenate([jnp.where(lane >= Q_BLOCK, bias_lo, 0.0),
                              jnp.where(lane < Q_BLOCK, bias_lo, 0.0)], axis=0)
    bias = bias_t.T.astype(bf16)

    kio = lax.broadcasted_iota(jnp.int32, (1, SLC_CHUNK), 1)
    wio = lax.broadcasted_iota(jnp.int32, (1, WIN_SPAN), 1)
    wbase = pl.multiple_of((jnp.maximum(qb - WINDOW // Q_BLOCK, 0) // 2) * LANES, LANES)
    dpos = pos_t - (wbase + wio)
    wmask = (dpos >= 0) & (dpos < WINDOW)
    n_chunks = qb // (SLC_CHUNK // SLC_BLOCK) + 1
    sig = jax.nn.sigmoid(ng_ref[0])
    out_h = []
    for h in range(N_KV):
        bh = bias[h * Q_BLOCK:(h + 1) * Q_BLOCK]
        qa = qh[h] + jnp.concatenate([bh] * GROUP, axis=0)

        def chunk(c, carry, causal):
            m, l, acc = carry
            k0 = pl.multiple_of(c * SLC_CHUNK, SLC_CHUNK)
            s = _dot_nt(qa, ksa_ref[0, h, pl.ds(k0, SLC_CHUNK), :])
            if causal:
                s = jnp.where(k0 + kio <= pos_t, s, NEG)
            m_new = jnp.maximum(m, jnp.max(s, axis=-1, keepdims=True))
            alpha = jnp.exp(m - m_new)
            p = jnp.exp(s - m_new)
            l = alpha * l + jnp.sum(p, axis=-1, keepdims=True)
            acc = alpha * acc + jnp.dot(p.astype(bf16), vs_ref[0, pl.ds(k0, SLC_CHUNK), :],
                                        preferred_element_type=f32)
            return m_new, l, acc

        init = (jnp.full((ROWS, 1), NEG, f32), jnp.zeros((ROWS, 1), f32), jnp.zeros((ROWS, LANES), f32))
        carry = lax.fori_loop(0, n_chunks - 1, lambda c, cr: chunk(c, cr, False), init)
        _, l, acc = chunk(n_chunks - 1, carry, True)
        o_slc = acc / jnp.maximum(l, 1e-30)

        pw = _softmax_rows(_dot_nt(qh[h], kw_ref[0, pl.ds(wbase, WIN_SPAN), :]), wmask)
        o_win = jnp.dot(pw.astype(bf16), vw_ref[0, pl.ds(wbase, WIN_SPAN), :], preferred_element_type=f32)

        outs = []
        for g in range(GROUP):
            rs = slice(g * Q_BLOCK, (g + 1) * Q_BLOCK)
            c0 = (h * GROUP + g) * 3
            outs.append(sig[:, c0:c0 + 1] * o_cmp[h][rs] + sig[:, c0 + 1:c0 + 2] * o_slc[rs]
                        + sig[:, c0 + 2:c0 + 3] * o_win[rs])
        out_h.append(outs)
    for g in range(GROUP):
        o_ref[0, :, g * LANES:(g + 1) * LANES] = jnp.where(lane < HEAD_DIM, out_h[0][g], out_h[1][g]).astype(o_ref.dtype)


def _block_score_matrix(ncp):
    ratio = SLC_BLOCK // CMP_STRIDE
    ovl = CMP_LEN // CMP_STRIDE - 1
    j = np.arange(LANES)[:, None]
    c = np.arange(ncp)[None, :]
    m = (c >= ratio * j - ovl) & (c <= ratio * j + ratio - 1) & (j < Q_BLOCK) & (c < ncp - 1)
    return jnp.asarray(m, dtype=jnp.bfloat16)


def _nsa_prompt_attention(q_pad, ks, vs, kw, vw, cmp_k, cmp_v, ng):
    b, s, _ = q_pad.shape
    assert s % SLC_CHUNK == 0 and s // SLC_BLOCK <= Q_BLOCK and s >= WIN_SPAN and Q_BLOCK == SLC_BLOCK
    bf16 = jnp.bfloat16
    ncp = cmp_k.shape[1]
    onehot = jax.nn.one_hot(jnp.arange(s) // SLC_BLOCK, Q_BLOCK, dtype=bf16)
    onehot = jnp.broadcast_to(onehot[None], (b, s, Q_BLOCK))
    ksb = ks.astype(bf16)
    ksa = jnp.stack([jnp.concatenate([ksb[:, :, :HEAD_DIM], onehot], axis=-1),
                     jnp.concatenate([onehot, ksb[:, :, HEAD_DIM:]], axis=-1)], axis=1)
    per_b = lambda bi, i: (bi, 0, 0)
    return pl.pallas_call(
        _nsa_prompt_kernel,
        grid=(b, s // Q_BLOCK),
        in_specs=[
            pl.BlockSpec((1, Q_BLOCK, D_QP), lambda bi, i: (bi, i, 0)),
            pl.BlockSpec((1, N_KV, s, LANES), lambda bi, i: (bi, 0, 0, 0)),
            pl.BlockSpec((1, s, LANES), per_b),
            pl.BlockSpec((1, s, LANES), per_b),
            pl.BlockSpec((1, s, LANES), per_b),
            pl.BlockSpec((1, ncp, LANES), per_b),
            pl.BlockSpec((1, ncp, LANES), per_b),
            pl.BlockSpec((1, Q_BLOCK, LANES), lambda bi, i: (bi, i, 0)),
            pl.BlockSpec((LANES, ncp), lambda bi, i: (0, 0)),
        ],
        out_specs=pl.BlockSpec((1, Q_BLOCK, D_Q), lambda bi, i: (bi, i, 0)),
        out_shape=jax.ShapeDtypeStruct((b, s, D_Q), bf16),
        scratch_shapes=[pltpu.VMEM((Q_BLOCK, LANES), jnp.float32)],
        compiler_params=pltpu.CompilerParams(dimension_semantics=("arbitrary", "arbitrary"),
                                             vmem_limit_bytes=VMEM_LIMIT),
        name="nsa_prompt_attention",
    )(q_pad, ksa, vs.astype(bf16), kw.astype(bf16), vw.astype(bf16), cmp_k, cmp_v, ng, _block_score_matrix(ncp))


CHUNK_W = CMP_STRIDE * D_KV
CMP_PARTS = CMP_LEN // CMP_STRIDE
HID2 = N_KV * CMP_HID


def _compress_weights(pos_emb, w1, w2):
    bf16 = jnp.bfloat16
    eye = jnp.eye(N_KV, dtype=w1.dtype)
    w1b = w1.reshape(CMP_PARTS, CMP_STRIDE, HEAD_DIM, CMP_HID)
    w1p = jnp.einsum('pldf,hk->lhdpkf', w1b, eye).reshape(CHUNK_W, CMP_PARTS * HID2).astype(bf16)
    w2p = jnp.einsum('fd,hk->hfkd', w2, eye).reshape(HID2, D_KV).astype(bf16)
    pos8 = jnp.zeros((8, CMP_LEN * HEAD_DIM), bf16).at[0].set(pos_emb.reshape(-1).astype(bf16))
    w1f = w1.reshape(CMP_LEN * HEAD_DIM, CMP_HID).astype(bf16)
    return w1p, w2p, pos8, w1f


def _compress_rows(x, w1p, w2p, pos8, w1f):
    f32 = jnp.float32
    r = x.shape[0]
    part = jnp.dot(x, w1p, preferred_element_type=f32)
    bias = jnp.dot(pos8, w1f, preferred_element_type=f32)[0:1]
    bias = jnp.concatenate([bias] * N_KV, axis=-1)
    hid = bias + part[:, :HID2] + pltpu.roll(part[:, HID2:], r - 1, 0)
    out = jnp.dot(jax.nn.gelu(hid).astype(jnp.bfloat16), w2p, preferred_element_type=f32)
    rows = lax.broadcasted_iota(jnp.int32, (r, 1), 0)
    return jnp.where(rows < r - 1, out, 0.0)


def _compress_prompt_kernel(xk_ref, xv_ref, w1k_ref, w2k_ref, pk_ref, fk_ref, w1v_ref, w2v_ref, pv_ref, fv_ref,
                            ok_ref, ov_ref):
    bf16 = jnp.bfloat16
    ok_ref[0] = _compress_rows(xk_ref[0].astype(bf16), w1k_ref[...], w2k_ref[...], pk_ref[...],
                               fk_ref[...]).astype(ok_ref.dtype)
    ov_ref[0] = _compress_rows(xv_ref[0].astype(bf16), w1v_ref[...], w2v_ref[...], pv_ref[...],
                               fv_ref[...]).astype(ov_ref.dtype)


def _compress_prompt(kc, vc, wk, wv):
    b, s, _ = kc.shape
    r = s // CMP_STRIDE
    xspec = pl.BlockSpec((1, r, CHUNK_W), lambda i: (i, 0, 0))
    wspecs = [pl.BlockSpec(w.shape, lambda i: (0, 0)) for w in wk]
    ospec = pl.BlockSpec((1, r, D_KV), lambda i: (i, 0, 0))
    oshape = jax.ShapeDtypeStruct((b, r, D_KV), jnp.bfloat16)
    return pl.pallas_call(
        _compress_prompt_kernel,
        grid=(b,),
        in_specs=[xspec, xspec] + wspecs + wspecs,
        out_specs=(ospec, ospec),
        out_shape=(oshape, oshape),
        compiler_params=pltpu.CompilerParams(dimension_semantics=("arbitrary",), vmem_limit_bytes=VMEM_LIMIT),
        name="compress_prompt",
    )(kc.reshape(b, r, CHUNK_W), vc.reshape(b, r, CHUNK_W), *wk, *wv)


N_PAGES = PAST_LEN // PAGE_SIZE
PAST_CHUNKS = PAST_LEN // CMP_STRIDE
PAST_BLOCKS = PAST_LEN // SLC_BLOCK
BLOCKS_PER_PAGE = PAGE_SIZE // SLC_BLOCK
CHUNKS_PER_PAGE = PAGE_SIZE // CMP_STRIDE
SEL_ROWS = 256
QROWS = SUBLANES


def _sample_cmp_kernel(pt_ref, q8_ref, kpool, vpool, w1k_ref, w2k_ref, pk_ref, fk_ref, w1v_ref, w2v_ref, pv_ref,
                       fv_ref, ocmp_ref, psum_ref, kbuf, vbuf, x_scr, t_scr, sem):
    f32, bf16 = jnp.float32, jnp.bfloat16
    n = pl.program_id(0)

    def fetch(s, slot):
        def body(j, _):
            pg = pt_ref[s, j]
            pltpu.make_async_copy(kpool.at[pg], kbuf.at[slot, j], sem.at[0, slot]).start()
            pltpu.make_async_copy(vpool.at[pg], vbuf.at[slot, j], sem.at[1, slot]).start()
            return 0
        lax.fori_loop(0, N_PAGES, body, 0)

    @pl.when(n == 0)
    def _():
        fetch(0, 0)

    @pl.when(n + 1 < pl.num_programs(0))
    def _():
        fetch(n + 1, (n + 1) % 2)

    slot = n % 2
    pltpu.make_async_copy(kbuf.at[slot], kbuf.at[slot], sem.at[0, slot]).wait()
    pltpu.make_async_copy(vbuf.at[slot], vbuf.at[slot], sem.at[1, slot]).wait()

    def regroup(buf):
        def body(j, _):
            t_scr[...] = buf[slot, j].T
            r0 = pl.multiple_of(j * CHUNKS_PER_PAGE, CHUNKS_PER_PAGE)
            for l in range(CMP_STRIDE):
                x_scr[pl.ds(r0, CHUNKS_PER_PAGE), l * D_KV:(l + 1) * D_KV] = (
                    t_scr[pl.ds(l, CHUNKS_PER_PAGE, stride=CMP_STRIDE), :])
            return 0
        lax.fori_loop(0, N_PAGES, body, 0)
        return x_scr[...].astype(bf16)

    ck = _compress_rows(regroup(kbuf), w1k_ref[...], w2k_ref[...], pk_ref[...], fk_ref[...]).astype(bf16)
    cv = _compress_rows(regroup(vbuf), w1v_ref[...], w2v_ref[...], pv_ref[...], fv_ref[...]).astype(bf16)
    c_end = lax.broadcasted_iota(jnp.int32, (1, PAST_CHUNKS), 1) * CMP_STRIDE + (CMP_LEN - 1)
    cmask = c_end <= PAST_LEN
    rows = lax.broadcasted_iota(jnp.int32, (QROWS, 1), 0)
    ps = jnp.zeros((QROWS, PAST_CHUNKS), f32)
    for h in range(N_KV):
        p = _softmax_rows(_dot_nt(q8_ref[0, h], ck), cmask)
        ocmp_ref[0, h] = jnp.dot(p.astype(bf16), cv, preferred_element_type=f32)
        ps = jnp.where(rows == h, p[0:1] + p[1:2] + p[2:3] + p[3:4], ps)
    psum_ref[0] = ps


def _sample_cmp(page_table, q8, k_pool, v_pool, wk, wv):
    n = q8.shape[0]
    f32 = jnp.float32
    wspecs = [pl.BlockSpec(w.shape, lambda i, pt: (0, 0)) for w in wk]
    grid_spec = pltpu.PrefetchScalarGridSpec(
        num_scalar_prefetch=1,
        grid=(n,),
        in_specs=[pl.BlockSpec((1, N_KV, QROWS, LANES), lambda i, pt: (i, 0, 0, 0)),
                  pl.BlockSpec(memory_space=pl.ANY), pl.BlockSpec(memory_space=pl.ANY)] + wspecs + wspecs,
        out_specs=(pl.BlockSpec((1, N_KV, QROWS, LANES), lambda i, pt: (i, 0, 0, 0)),
                   pl.BlockSpec((1, QROWS, PAST_CHUNKS), lambda i, pt: (i, 0, 0))),
        scratch_shapes=[pltpu.VMEM((2, N_PAGES, D_KV, PAGE_SIZE), f32),
                        pltpu.VMEM((2, N_PAGES, D_KV, PAGE_SIZE), f32),
                        pltpu.VMEM((PAST_CHUNKS, CHUNK_W), f32),
                        pltpu.VMEM((PAGE_SIZE, D_KV), f32),
                        pltpu.SemaphoreType.DMA((2, 2))],
    )
    return pl.pallas_call(
        _sample_cmp_kernel,
        grid_spec=grid_spec,
        out_shape=(jax.ShapeDtypeStruct((n, N_KV, QROWS, LANES), f32),
                   jax.ShapeDtypeStruct((n, QROWS, PAST_CHUNKS), f32)),
        compiler_params=pltpu.CompilerParams(dimension_semantics=("arbitrary",), vmem_limit_bytes=VMEM_LIMIT),
        name="sample_compress_attention",
    )(page_table, q8, k_pool, v_pool, *wk, *wv)


def _sample_select_kernel(p_ref, mmt_ref, tri_ref, idx_ref, sv_scr):
    f32 = jnp.float32
    cols = p_ref.shape[0]
    cur = PAST_LEN // SLC_BLOCK
    mmt = mmt_ref[...]
    sc = sum(_dot_nt(mmt, part) for part in _split3_bf16(p_ref[...]))
    jio = lax.broadcasted_iota(jnp.int32, (SEL_ROWS, cols), 0)
    forced = (jio == 0) | (jio == cur) | (jio == cur - 1)
    sv = jnp.where(forced, BIG, jnp.where(jio <= cur, sc, NEG))
    sv_scr[...] = sv

    def rank_body(i, cnt):
        r = sv_scr[pl.ds(i, 1), :]
        beats = (r > sv) | ((r == sv) & (jio > i))
        return cnt + jnp.where(beats, 1.0, 0.0)

    cnt = lax.fori_loop(0, cur + 1, rank_body, jnp.zeros((SEL_ROWS, cols), f32))
    sel = (cnt < N_SEL) & (sv > NEG / 2)
    self = jnp.where(sel, 1.0, 0.0)
    before = jnp.dot(tri_ref[...], self.astype(jnp.bfloat16), preferred_element_type=f32)
    rows = []
    for k in range(N_SEL):
        rows.append(jnp.sum(jnp.where(sel & (before == k), jio, 0), axis=0, keepdims=True))
    idx_ref[...] = jnp.concatenate(rows, axis=0)


def _sample_select(p_sum):
    n = p_sum.shape[0]
    cols = n * N_KV
    p2 = p_sum[:, :N_KV, :].reshape(cols, PAST_CHUNKS)
    ratio = SLC_BLOCK // CMP_STRIDE
    ovl = CMP_LEN // CMP_STRIDE - 1
    j = np.arange(SEL_ROWS)[:, None]
    c = np.arange(PAST_CHUNKS)[None, :]
    mmt = (c >= ratio * j - ovl) & (c <= ratio * j + ratio - 1) & (j <= PAST_BLOCKS) & (c < PAST_CHUNKS - 1)
    tri = np.tril(np.ones((SEL_ROWS, SEL_ROWS), np.float32), -1)
    idx = pl.pallas_call(
        _sample_select_kernel,
        out_shape=jax.ShapeDtypeStruct((N_SEL, cols), jnp.int32),
        scratch_shapes=[pltpu.VMEM((SEL_ROWS, cols), jnp.float32)],
        compiler_params=pltpu.CompilerParams(vmem_limit_bytes=VMEM_LIMIT),
        name="sample_select",
    )(p2, jnp.asarray(mmt, dtype=jnp.bfloat16), jnp.asarray(tri, dtype=jnp.bfloat16))
    return idx.T.reshape(n, N_KV * N_SEL)


SLC_KEYS = N_SEL * SLC_BLOCK
EXT = LANES


def _sample_attend_kernel(pt_ref, idx_ref, q8_ref, kpool, vpool, knew_ref, vnew_ref, kwin_ref, vwin_ref, kwn_ref,
                          vwn_ref, ocmp_ref, ng_ref, o_ref, kwo_ref, vwo_ref, kg, vg, sem):
    f32, bf16 = jnp.float32, jnp.bfloat16
    n = pl.program_id(0)
    nblk = N_KV * N_SEL

    def fetch(s, slot):
        def body(i, _):
            jp = jnp.minimum(idx_ref[s, i], PAST_BLOCKS - 1)
            pg = pt_ref[s, jp // BLOCKS_PER_PAGE]
            sub = jp % BLOCKS_PER_PAGE
            pltpu.make_async_copy(kpool.at[pg, sub], kg.at[slot, i], sem.at[0, slot]).start()
            pltpu.make_async_copy(vpool.at[pg, sub], vg.at[slot, i], sem.at[1, slot]).start()
            return 0
        lax.fori_loop(0, nblk, body, 0)

    @pl.when(n == 0)
    def _():
        fetch(0, 0)

    @pl.when(n + 1 < pl.num_programs(0))
    def _():
        fetch(n + 1, (n + 1) % 2)

    slot = n % 2
    pltpu.make_async_copy(kg.at[slot], kg.at[slot], sem.at[0, slot]).wait()
    pltpu.make_async_copy(vg.at[slot], vg.at[slot], sem.at[1, slot]).wait()

    pos = PAST_LEN
    ext_row = lax.broadcasted_iota(jnp.int32, (EXT, 1), 0)
    ext = lambda ref: jnp.where(ext_row == 0, ref[0], 0.0).astype(bf16)
    col = lax.broadcasted_iota(jnp.int32, (1, SLC_KEYS + EXT), 1)
    wcol = lax.broadcasted_iota(jnp.int32, (1, WINDOW + EXT), 1)
    wpos = jnp.where(wcol < WINDOW, pos - WINDOW + wcol, pos + (wcol - WINDOW))
    wmask = (pos - wpos >= 0) & (pos - wpos < WINDOW)
    kwin, vwin = kwin_ref[0], vwin_ref[0]
    kw_all = jnp.concatenate([kwin.astype(bf16), ext(kwn_ref)], axis=0)
    vw_all = jnp.concatenate([vwin.astype(bf16), ext(vwn_ref)], axis=0)
    k_ext, v_ext = ext(knew_ref), ext(vnew_ref)
    lane = lax.broadcasted_iota(jnp.int32, (QROWS, LANES), 1)
    out = []
    for h in range(N_KV):
        q = q8_ref[0, h]
        new_sel = jnp.int32(0)
        valid = jnp.zeros((1, SLC_KEYS + EXT), jnp.int32)
        for p in range(N_SEL):
            j = idx_ref[n, h * N_SEL + p]
            valid = jnp.where((col >> 6) == p, (j < PAST_BLOCKS).astype(jnp.int32), valid)
            new_sel = jnp.maximum(new_sel, (j >= PAST_BLOCKS).astype(jnp.int32))
        kpos_new = PAST_LEN + (col - SLC_KEYS)
        valid = jnp.where(col >= SLC_KEYS, jnp.where(kpos_new <= pos, new_sel, 0), valid)
        kblk = kg[slot, h * N_SEL:(h + 1) * N_SEL].reshape(SLC_KEYS, LANES).astype(bf16)
        vblk = vg[slot, h * N_SEL:(h + 1) * N_SEL].reshape(SLC_KEYS, LANES).astype(bf16)
        ps = _softmax_rows(_dot_nt(q, jnp.concatenate([kblk, k_ext], axis=0)), valid > 0)
        o_slc = jnp.dot(ps.astype(bf16), jnp.concatenate([vblk, v_ext], axis=0), preferred_element_type=f32)
        pw = _softmax_rows(_dot_nt(q, kw_all), wmask)
        o_win = jnp.dot(pw.astype(bf16), vw_all, preferred_element_type=f32)
        sig = jax.nn.sigmoid(ng_ref[0, h])
        out.append(sig[0] * ocmp_ref[0, h] + sig[1] * o_slc + sig[2] * o_win)
    o_ref[0] = jnp.where(lane < HEAD_DIM, out[0], out[1]).astype(o_ref.dtype)

    last = lax.broadcasted_iota(jnp.int32, (WINDOW, 1), 0) == WINDOW - 1
    kwo_ref[0] = jnp.where(last, kwn_ref[0], pltpu.roll(kwin, WINDOW - 1, 0))
    vwo_ref[0] = jnp.where(last, vwn_ref[0], pltpu.roll(vwin, WINDOW - 1, 0))


def _sample_attend(page_table, idx, q8, k_pool, v_pool, ks_new, vs_new, k_win, v_win, kw_new, vw_new, o_cmp, ngb):
    n = q8.shape[0]
    f32 = jnp.float32
    n_phys = k_pool.shape[0]
    assert k_win.shape[1] == WINDOW
    view = lambda pool: pool.reshape(n_phys, BLOCKS_PER_PAGE, SLC_BLOCK, D_KV)
    row3 = lambda t: t.reshape(n, 1, D_KV)
    per_n3 = lambda i, pt, ix: (i, 0, 0)
    per_n4 = lambda i, pt, ix: (i, 0, 0, 0)
    new_spec = pl.BlockSpec((1, 1, D_KV), per_n3)
    win_spec = pl.BlockSpec((1, WINDOW, D_KV), per_n3)
    q_spec = pl.BlockSpec((1, N_KV, QROWS, LANES), per_n4)
    grid_spec = pltpu.PrefetchScalarGridSpec(
        num_scalar_prefetch=2,
        grid=(n,),
        in_specs=[q_spec, pl.BlockSpec(memory_space=pl.ANY), pl.BlockSpec(memory_space=pl.ANY),
                  new_spec, new_spec, win_spec, win_spec, new_spec, new_spec, q_spec,
                  pl.BlockSpec((1, N_KV, 3, QROWS, LANES), lambda i, pt, ix: (i, 0, 0, 0, 0))],
        out_specs=(pl.BlockSpec((1, QROWS, LANES), per_n3), win_spec, win_spec),
        scratch_shapes=[pltpu.VMEM((2, N_KV * N_SEL, SLC_BLOCK, D_KV), f32),
                        pltpu.VMEM((2, N_KV * N_SEL, SLC_BLOCK, D_KV), f32),
                        pltpu.SemaphoreType.DMA((2, 2))],
    )
    return pl.pallas_call(
        _sample_attend_kernel,
        grid_spec=grid_spec,
        out_shape=(jax.ShapeDtypeStruct((n, QROWS, LANES), jnp.bfloat16),
                   jax.ShapeDtypeStruct((n, WINDOW, D_KV), f32), jax.ShapeDtypeStruct((n, WINDOW, D_KV), f32)),
        compiler_params=pltpu.CompilerParams(dimension_semantics=("arbitrary",), vmem_limit_bytes=VMEM_LIMIT),
        name="sample_attention",
    )(page_table, idx, q8, view(k_pool), view(v_pool), row3(ks_new), row3(vs_new),
      k_win.reshape(n, WINDOW, D_KV), v_win.reshape(n, WINDOW, D_KV), row3(kw_new), row3(vw_new), o_cmp, ngb)


def _nsa_sample_kernels(q_pad, ks, vs, kw, vw, ng, page_table, ck_cmp, cv_cmp, ck_slc, cv_slc, ck_win, cv_win,
                        cmp_wk, cmp_wv):
    n = q_pad.shape[0]
    q8 = jnp.pad(q_pad.reshape(n, N_KV, GROUP, LANES), ((0, 0), (0, 0), (0, QROWS - GROUP), (0, 0)))
    o_cmp, p_sum = _sample_cmp(page_table, q8, ck_cmp, cv_cmp, cmp_wk, cmp_wv)
    idx = _sample_select(p_sum)
    ngb = ng[:, :N_NG].reshape(n, N_KV, GROUP, 3).transpose(0, 1, 3, 2)
    ngb = jnp.pad(ngb, ((0, 0), (0, 0), (0, 0), (0, QROWS - GROUP)))
    ngb = jnp.broadcast_to(ngb[..., None], (n, N_KV, 3, QROWS, LANES))
    o8, kwo, vwo = _sample_attend(page_table, idx, q8, ck_slc, cv_slc, ks, vs, ck_win, cv_win, kw, vw, o_cmp, ngb)
    return o8[:, :GROUP, :].reshape(n, D_Q), kwo, vwo


SLC_COLS = N_SEL * PAGE_SIZE


def _sample_attn_kernel(pt_ref, idx_ref, q_ref, kpool, vpool, knew_ref, vnew_ref, kwin_ref, vwin_ref, kwn_ref,
                        vwn_ref, ocmp_ref, ng_ref, o_ref, kwo_ref, vwo_ref, kg, vg, sem):
    f32, bf16 = jnp.float32, jnp.bfloat16
    n = pl.program_id(0)
    nblk = N_KV * N_SEL

    def fetch(s, slot):
        def body(i, _):
            jp = jnp.minimum(idx_ref[s, i], PAST_BLOCKS - 1)
            pg = pt_ref[s, jp // BLOCKS_PER_PAGE]
            h = i // N_SEL
            pltpu.make_async_copy(kpool.at[pg, h], kg.at[slot, i], sem.at[0, slot]).start()
            pltpu.make_async_copy(vpool.at[pg, h], vg.at[slot, i], sem.at[1, slot]).start()
            return 0
        lax.fori_loop(0, nblk, body, 0)

    @pl.when(n == 0)
    def _():
        fetch(0, 0)

    @pl.when(n + 1 < pl.num_programs(0))
    def _():
        fetch(n + 1, (n + 1) % 2)

    slot = n % 2
    pltpu.make_async_copy(kg.at[slot], kg.at[slot], sem.at[0, slot]).wait()
    pltpu.make_async_copy(vg.at[slot], vg.at[slot], sem.at[1, slot]).wait()

    pos = PAST_LEN
    col = lax.broadcasted_iota(jnp.int32, (1, SLC_COLS + LANES), 1)
    wcol = lax.broadcasted_iota(jnp.int32, (1, WINDOW + LANES), 1)
    wpos = jnp.where(wcol < WINDOW, pos - WINDOW + wcol, pos + (wcol - WINDOW))
    wmask = (pos - wpos >= 0) & (pos - wpos < WINDOW)
    wlane = lax.broadcasted_iota(jnp.int32, (1, WINDOW), 1)
    for h in range(N_KV):
        q = q_ref[0, h]
        new_sel = jnp.int32(0)
        lo = jnp.full((1, SLC_COLS + LANES), SLC_COLS + LANES, jnp.int32)
        for p in range(N_SEL):
            j = idx_ref[n, h * N_SEL + p]
            start = p * PAGE_SIZE + (j % BLOCKS_PER_PAGE) * SLC_BLOCK
            start = jnp.where(j < PAST_BLOCKS, start, SLC_COLS + LANES)
            lo = jnp.where((col >> 7) == p, start, lo)
            new_sel = jnp.maximum(new_sel, (j >= PAST_BLOCKS).astype(jnp.int32))
        valid = jnp.where((col >= lo) & (col < lo + SLC_BLOCK), 1, 0)
        kpos_new = PAST_LEN + (col - SLC_COLS)
        valid = jnp.where(col >= SLC_COLS, jnp.where(kpos_new <= pos, new_sel, 0), valid) > 0
        kt = jnp.concatenate([kg[slot, h * N_SEL + p] for p in range(N_SEL)] + [knew_ref[0, h]], axis=-1)
        vt = jnp.concatenate([vg[slot, h * N_SEL + p] for p in range(N_SEL)] + [vnew_ref[0, h]], axis=-1)
        ps = _softmax_rows(jnp.dot(q, kt.astype(bf16), preferred_element_type=f32), valid)
        o_slc = _dot_nt(ps.astype(bf16), vt.astype(bf16))
        kwin, vwin = kwin_ref[0, h], vwin_ref[0, h]
        kw_all = jnp.concatenate([kwin, kwn_ref[0, h]], axis=-1).astype(bf16)
        vw_all = jnp.concatenate([vwin, vwn_ref[0, h]], axis=-1).astype(bf16)
        pw = _softmax_rows(jnp.dot(q, kw_all, preferred_element_type=f32), wmask)
        o_win = _dot_nt(pw.astype(bf16), vw_all)
        sig = jax.nn.sigmoid(ng_ref[0, h])
        o_cmp = ocmp_ref[0, h][:, h * HEAD_DIM:(h + 1) * HEAD_DIM]
        o_ref[0, h] = sig[0] * o_cmp + sig[1] * o_slc + sig[2] * o_win
        kwo_ref[0, h] = jnp.where(wlane == WINDOW - 1, kwn_ref[0, h][:, 0:1], pltpu.roll(kwin, WINDOW - 1, 1))
        vwo_ref[0, h] = jnp.where(wlane == WINDOW - 1, vwn_ref[0, h][:, 0:1], pltpu.roll(vwin, WINDOW - 1, 1))


def _sample_attn(page_table, idx, q8u, k_pool_t, v_pool_t, knew_t, vnew_t, k_win_t, v_win_t, kwn_t, vwn_t, o_cmp, ngb):
    n = q8u.shape[0]
    f32 = jnp.float32
    assert k_win_t.shape[-1] == WINDOW
    per_n4 = lambda i, pt, ix: (i, 0, 0, 0)
    ext_spec = pl.BlockSpec((1, N_KV, HEAD_DIM, LANES), per_n4)
    win_spec = pl.BlockSpec((1, N_KV, HEAD_DIM, WINDOW), per_n4)
    q_spec = pl.BlockSpec((1, N_KV, QROWS, HEAD_DIM), per_n4)
    grid_spec = pltpu.PrefetchScalarGridSpec(
        num_scalar_prefetch=2,
        grid=(n,),
        in_specs=[q_spec, pl.BlockSpec(memory_space=pl.ANY), pl.BlockSpec(memory_space=pl.ANY),
                  ext_spec, ext_spec, win_spec, win_spec, ext_spec, ext_spec,
                  pl.BlockSpec((1, N_KV, QROWS, LANES), per_n4),
                  pl.BlockSpec((1, N_KV, 3, QROWS, HEAD_DIM), lambda i, pt, ix: (i, 0, 0, 0, 0))],
        out_specs=(q_spec, win_spec, win_spec),
        scratch_shapes=[pltpu.VMEM((2, N_KV * N_SEL, HEAD_DIM, PAGE_SIZE), f32),
                        pltpu.VMEM((2, N_KV * N_SEL, HEAD_DIM, PAGE_SIZE), f32),
                        pltpu.SemaphoreType.DMA((2, 2))],
    )
    return pl.pallas_call(
        _sample_attn_kernel,
        grid_spec=grid_spec,
        out_shape=(jax.ShapeDtypeStruct((n, N_KV, QROWS, HEAD_DIM), f32),
                   jax.ShapeDtypeStruct((n, N_KV, HEAD_DIM, WINDOW), f32),
                   jax.ShapeDtypeStruct((n, N_KV, HEAD_DIM, WINDOW), f32)),
        compiler_params=pltpu.CompilerParams(dimension_semantics=("arbitrary",), vmem_limit_bytes=VMEM_LIMIT),
        name="sample_attention",
    )(page_table, idx, q8u, k_pool_t, v_pool_t, knew_t, vnew_t, k_win_t, v_win_t, kwn_t, vwn_t, o_cmp, ngb)


def _nsa_sample_paged(q_pad, ks, vs, kw, vw, ng, page_table, ck_cmp, cv_cmp, ck_slc, cv_slc, ck_win, cv_win,
                      cmp_wk, cmp_wv):
    n = q_pad.shape[0]
    n_phys = ck_cmp.shape[0]
    pool_t = lambda p: p.transpose(0, 2, 3, 1)
    q4 = q_pad.reshape(n, N_KV, GROUP, N_KV, HEAD_DIM)
    q8 = jnp.pad(q_pad.reshape(n, N_KV, GROUP, LANES), ((0, 0), (0, 0), (0, QROWS - GROUP), (0, 0)))
    q8u = jnp.pad(jnp.stack([q4[:, h, :, h, :] for h in range(N_KV)], axis=1),
                  ((0, 0), (0, 0), (0, QROWS - GROUP), (0, 0)))
    o_cmp, p_sum = _sample_cmp(page_table, q8, pool_t(ck_cmp).reshape(n_phys, D_KV, PAGE_SIZE),
                               pool_t(cv_cmp).reshape(n_phys, D_KV, PAGE_SIZE), cmp_wk, cmp_wv)
    idx = _sample_select(p_sum)
    ngb = ng[:, :N_NG].reshape(n, N_KV, GROUP, 3).transpose(0, 1, 3, 2)
    ngb = jnp.pad(ngb, ((0, 0), (0, 0), (0, 0), (0, QROWS - GROUP)))
    ngb = jnp.broadcast_to(ngb[..., None], (n, N_KV, 3, QROWS, HEAD_DIM))
    col0 = lambda t: jnp.pad(t.reshape(n, N_KV, HEAD_DIM, 1), ((0, 0), (0, 0), (0, 0), (0, LANES - 1)))
    o8, kwo, vwo = _sample_attn(page_table, idx, q8u, pool_t(ck_slc), pool_t(cv_slc), col0(ks), col0(vs),
                                pool_t(ck_win), pool_t(cv_win), col0(kw), col0(vw), o_cmp, ngb)
    o = o8[:, :, :GROUP, :].transpose(0, 2, 1, 3).reshape(n, D_Q).astype(jnp.bfloat16)
    return o, kwo.transpose(0, 3, 1, 2), vwo.transpose(0, 3, 1, 2)


LRU_PAIRS = D_RNN // LANES


def _lru_gates(xc, wa_ref, ba, wx_ref, bx, lam):
    f32 = jnp.float32
    xb = xc.astype(jnp.bfloat16)
    rg = jnp.concatenate([jnp.dot(xb[:, j * LANES:(j + 1) * LANES], wa_ref[j], preferred_element_type=f32)
                          for j in range(LRU_PAIRS)], axis=-1)
    ig = jnp.concatenate([jnp.dot(xb[:, j * LANES:(j + 1) * LANES], wx_ref[j], preferred_element_type=f32)
                          for j in range(LRU_PAIRS)], axis=-1)
    r = jax.nn.sigmoid(rg + ba)
    i = jax.nn.sigmoid(ig + bx)
    log_a = -LRU_C * r * jax.nn.softplus(-lam)
    a = jnp.exp(log_a)
    u = jnp.sqrt(-jnp.tanh(log_a) * (a * a + 1.0)) * (i * xc)
    return a, u


def _lru_prompt_kernel(arag_ref, cbuf_ref, h0_ref, cw_ref, cb_ref, wa_ref, ba_ref, wx_ref, bx_ref, lam_ref,
                       y_ref, hl_ref, prev_scr, h_scr, hbuf):
    f32 = jnp.float32
    t = arag_ref.shape[0]

    @pl.when(pl.program_id(1) == 0)
    def _():
        prev_scr[...] = cbuf_ref[0]
        h_scr[...] = h0_ref[0]

    ar = arag_ref[:, :D_RNN]
    row8 = lax.broadcasted_iota(jnp.int32, (SUBLANES, 1), 0)
    prev = prev_scr[...]
    xc = cb_ref[...] + cw_ref[CONV_W - 1:CONV_W, :] * ar
    for k in range(1, CONV_W):
        rolled = pltpu.roll(ar, k, 0)
        head = jnp.where(row8 < k, pltpu.roll(prev, k, 0), rolled[0:SUBLANES])
        shifted = jnp.concatenate([head, rolled[SUBLANES:]], axis=0)
        xc = xc + cw_ref[CONV_W - 1 - k:CONV_W - k, :] * shifted
    prev_scr[...] = ar[t - SUBLANES:]

    a, u = _lru_gates(xc, wa_ref, ba_ref[...], wx_ref, bx_ref[...], lam_ref[...])

    rowm = lax.broadcasted_iota(jnp.int32, (t, 1), 0) & (SUBLANES - 1)
    for s in (1, 2, 4):
        ok = rowm >= s
        u = jnp.where(ok, a * pltpu.roll(u, s, 0) + u, u)
        a = jnp.where(ok, a * pltpu.roll(a, s, 0), a)
    h = h_scr[...]
    for g in range(t // SUBLANES):
        rs = slice(g * SUBLANES, (g + 1) * SUBLANES)
        hg = a[rs] * h + u[rs]
        hbuf[rs, :] = hg
        h = jnp.broadcast_to(hg[SUBLANES - 1:SUBLANES, :], (SUBLANES, D_RNN))
    h_scr[...] = h
    hl_ref[0] = h
    y_ref[...] = (hbuf[...] * jax.nn.gelu(arag_ref[:, D_RNN:])).astype(y_ref.dtype)


def _lru_pair_weights(w):
    w4 = w.reshape(LRU_PAIRS, 2, LRU_BW, LRU_BW)
    eye = jnp.eye(2, dtype=w.dtype)
    return jnp.einsum('jaio,ab->jaibo', w4, eye).reshape(LRU_PAIRS, LANES, LANES).astype(jnp.bfloat16)


def _lru_weights(conv_w, conv_b, wa, ba, wx, bx, lam):
    row = lambda v: v.reshape(1, D_RNN).astype(jnp.float32)
    return (conv_w.astype(jnp.float32), row(conv_b), _lru_pair_weights(wa), row(ba), _lru_pair_weights(wx), row(bx),
            row(lam))


def _lru_prompt(arag, conv_buf, h0, lw, b, s, t):
    assert s % t == 0 and t % SUBLANES == 0
    nt = s // t
    cbuf8 = jnp.pad(conv_buf.astype(jnp.float32), ((0, 0), (SUBLANES - (CONV_W - 1), 0), (0, 0)))
    h08 = jnp.broadcast_to(h0.astype(jnp.float32)[:, None, :], (b, SUBLANES, D_RNN))
    const2 = lambda bi, i: (0, 0)
    const3 = lambda bi, i: (0, 0, 0)
    per_b = lambda bi, i: (bi, 0, 0)
    y, hl = pl.pallas_call(
        _lru_prompt_kernel,
        grid=(b, nt),
        in_specs=[
            pl.BlockSpec((t, 2 * D_RNN), lambda bi, i: (bi * nt + i, 0)),
            pl.BlockSpec((1, SUBLANES, D_RNN), per_b),
            pl.BlockSpec((1, SUBLANES, D_RNN), per_b),
            pl.BlockSpec((CONV_W, D_RNN), const2),
            pl.BlockSpec((1, D_RNN), const2),
            pl.BlockSpec((LRU_PAIRS, LANES, LANES), const3),
            pl.BlockSpec((1, D_RNN), const2),
            pl.BlockSpec((LRU_PAIRS, LANES, LANES), const3),
            pl.BlockSpec((1, D_RNN), const2),
            pl.BlockSpec((1, D_RNN), const2),
        ],
        out_specs=(pl.BlockSpec((t, D_RNN), lambda bi, i: (bi * nt + i, 0)),
                   pl.BlockSpec((1, SUBLANES, D_RNN), per_b)),
        out_shape=(jax.ShapeDtypeStruct((b * s, D_RNN), jnp.bfloat16),
                   jax.ShapeDtypeStruct((b, SUBLANES, D_RNN), jnp.float32)),
        scratch_shapes=[pltpu.VMEM((SUBLANES, D_RNN), jnp.float32), pltpu.VMEM((SUBLANES, D_RNN), jnp.float32),
                        pltpu.VMEM((t, D_RNN), jnp.float32)],
        compiler_params=pltpu.CompilerParams(dimension_semantics=("arbitrary", "arbitrary"),
                                             vmem_limit_bytes=VMEM_LIMIT),
        name="lru_prompt",
    )(arag, cbuf8, h08, *lw)
    return y, hl[:, 0, :]


def _lru_step_kernel(arag_ref, c0_ref, c1_ref, c2_ref, h0_ref, cw_ref, cb_ref, wa_ref, ba_ref, wx_ref, bx_ref,
                     lam_ref, y_ref, h_ref):
    ar = arag_ref[:, :D_RNN]
    xc = (cb_ref[...] + cw_ref[0:1, :] * c0_ref[...] + cw_ref[1:2, :] * c1_ref[...] + cw_ref[2:3, :] * c2_ref[...]
          + cw_ref[3:4, :] * ar)
    a, u = _lru_gates(xc, wa_ref, ba_ref[...], wx_ref, bx_ref[...], lam_ref[...])
    h = a * h0_ref[...] + u
    h_ref[...] = h
    y_ref[...] = (h * jax.nn.gelu(arag_ref[:, D_RNN:])).astype(y_ref.dtype)


def _lru_step(arag, conv_buf, h0, lw):
    n = arag.shape[0]
    f32 = jnp.float32
    cb = conv_buf.astype(f32)
    return pl.pallas_call(
        _lru_step_kernel,
        out_shape=(jax.ShapeDtypeStruct((n, D_RNN), jnp.bfloat16), jax.ShapeDtypeStruct((n, D_RNN), f32)),
        compiler_params=pltpu.CompilerParams(vmem_limit_bytes=VMEM_LIMIT),
        name="lru_step",
    )(arag, cb[:, 0], cb[:, 1], cb[:, 2], h0.astype(f32), *lw)


def _merge_router_kernel(x_ref, ylru_ref, onsa_ref, gab_ref, wa_ref, wb_ref, wo_ref, gffn_ref, rw_ref, rb_ref,
                         tri_ref, xn_ref, hn_ref, te_ref, tr_ref, tg_ref, cnt_ref, run_scr):
    f32, bf16 = jnp.float32, jnp.bfloat16
    tm = x_ref.shape[0]

    @pl.when(pl.program_id(0) == 0)
    def _():
        run_scr[...] = jnp.zeros_like(run_scr)

    ya = jnp.dot(ylru_ref[...], wa_ref[...], preferred_element_type=f32)
    yb = jnp.dot(onsa_ref[...], wb_ref[...], preferred_element_type=f32)
    mix = jax.nn.sigmoid(gab_ref[:, :D_MODEL]) * ya + jax.nn.sigmoid(gab_ref[:, D_MODEL:]) * yb
    xn = x_ref[...] + jnp.dot(mix.astype(bf16), wo_ref[...], preferred_element_type=f32)
    xn_ref[...] = xn
    hn = xn * lax.rsqrt(jnp.mean(xn * xn, axis=-1, keepdims=True) + EPS) * gffn_ref[...]
    hn_ref[...] = hn

    logits = jnp.dot(hn, rw_ref[...], preferred_element_type=f32, precision=lax.Precision.HIGHEST) + rb_ref[...]
    lane = lax.broadcasted_iota(jnp.int32, (tm, LANES), 1)
    ninf = -jnp.inf
    l = jnp.where(lane < N_EXPERTS, logits, ninf)
    onehots, m0 = [], None
    for k in range(TOP_K):
        m = jnp.max(l, axis=-1, keepdims=True)
        idx = jnp.min(jnp.where(l == m, lane, LANES), axis=-1, keepdims=True)
        oh = lane == idx
        onehots.append((oh, idx))
        m0 = m if k == 0 else m0
        l = jnp.where(oh, ninf, l)
    sel = onehots[0][0] | onehots[1][0] | onehots[2][0] | onehots[3][0]
    e = jnp.where(sel, jnp.exp(logits - m0), 0.0)
    gates = e / jnp.sum(e, axis=-1, keepdims=True)

    self = jnp.where(sel, 1.0, 0.0)
    incl = jnp.dot(tri_ref[...], self.astype(bf16), preferred_element_type=f32)
    run = run_scr[0:1, :]
    rank = run + incl - self
    total = run + incl[tm - 1:tm, :]
    run_scr[...] = jnp.broadcast_to(total, run_scr.shape)
    cnt_ref[...] = jnp.broadcast_to(total, cnt_ref.shape)

    te = jnp.zeros((tm, LANES), jnp.int32)
    tr = jnp.zeros((tm, LANES), f32)
    tg = jnp.zeros((tm, LANES), f32)
    for k, (oh, idx) in enumerate(onehots):
        te = jnp.where(lane == k, idx, te)
        tr = jnp.where(lane == k, jnp.sum(jnp.where(oh, rank, 0.0), axis=-1, keepdims=True), tr)
        tg = jnp.where(lane == k, jnp.sum(jnp.where(oh, gates, 0.0), axis=-1, keepdims=True), tg)
    te_ref[...] = te
    tr_ref[...] = tr.astype(jnp.int32)
    tg_ref[...] = tg


def _merge_router(x2d, ylru, onsa, gab, wa, wb, wo, g_ffn, rw_pad, rb_pad, tm):
    m = x2d.shape[0]
    assert m % tm == 0
    f32 = jnp.float32
    row = lambda i: (i, 0)
    const = lambda i: (0, 0)
    tri = jnp.asarray(np.tril(np.ones((tm, tm), np.float32)), dtype=jnp.bfloat16)
    lane_out = lambda dt: (jax.ShapeDtypeStruct((m, LANES), dt), pl.BlockSpec((tm, LANES), row))
    outs = [
        (jax.ShapeDtypeStruct((m, D_MODEL), f32), pl.BlockSpec((tm, D_MODEL), row)),
        (jax.ShapeDtypeStruct((m, D_MODEL), f32), pl.BlockSpec((tm, D_MODEL), row)),
        lane_out(jnp.int32), lane_out(jnp.int32), lane_out(f32),
        (jax.ShapeDtypeStruct((8, LANES), f32), pl.BlockSpec((8, LANES), const)),
    ]
    return pl.pallas_call(
        _merge_router_kernel,
        grid=(m // tm,),
        in_specs=[
            pl.BlockSpec((tm, D_MODEL), row),
            pl.BlockSpec((tm, D_RNN), row),
            pl.BlockSpec((tm, D_Q), row),
            pl.BlockSpec((tm, 2 * D_MODEL), row),
            pl.BlockSpec((D_RNN, D_MODEL), const),
            pl.BlockSpec((D_Q, D_MODEL), const),
            pl.BlockSpec((D_MODEL, D_MODEL), const),
            pl.BlockSpec((1, D_MODEL), const),
            pl.BlockSpec((D_MODEL, LANES), const),
            pl.BlockSpec((1, LANES), const),
            pl.BlockSpec((tm, tm), const),
        ],
        out_specs=tuple(o[1] for o in outs),
        out_shape=tuple(o[0] for o in outs),
        scratch_shapes=[pltpu.VMEM((8, LANES), f32)],
        compiler_params=pltpu.CompilerParams(dimension_semantics=("arbitrary",), vmem_limit_bytes=VMEM_LIMIT),
        name="merge_router",
    )(x2d, ylru, onsa, gab, wa, wb, wo, g_ffn.reshape(1, D_MODEL), rw_pad, rb_pad, tri)


MOE_BLK = 512


def _moe_dispatch_kernel(dest_ref, hn_ref, xs_in_ref, xs_ref, sem):
    del xs_in_ref
    tm = hn_ref.shape[0]

    def issue(r, _):
        for k in range(TOP_K):
            d = dest_ref[0, 0, r * TOP_K + k]
            pltpu.make_async_copy(hn_ref.at[pl.ds(r, 1)], xs_ref.at[pl.ds(d, 1)], sem).start()
        return 0

    lax.fori_loop(0, tm, issue, 0)
    pltpu.make_async_copy(xs_ref.at[pl.ds(0, tm * TOP_K)], xs_ref.at[pl.ds(0, tm * TOP_K)], sem).wait()


def _moe_dispatch(hn, dest, n_slots, tm):
    m = hn.shape[0]
    nt = m // tm
    xs0 = jnp.zeros((n_slots, D_MODEL), hn.dtype)
    return pl.pallas_call(
        _moe_dispatch_kernel,
        grid=(nt,),
        in_specs=[
            pl.BlockSpec((1, 1, tm * TOP_K), lambda i: (i, 0, 0), memory_space=pltpu.SMEM),
            pl.BlockSpec((tm, D_MODEL), lambda i: (i, 0)),
            pl.BlockSpec(memory_space=pl.ANY),
        ],
        out_specs=pl.BlockSpec(memory_space=pl.ANY),
        out_shape=jax.ShapeDtypeStruct((n_slots, D_MODEL), hn.dtype),
        scratch_shapes=[pltpu.SemaphoreType.DMA(())],
        input_output_aliases={2: 0},
        compiler_params=pltpu.CompilerParams(dimension_semantics=("arbitrary",), vmem_limit_bytes=VMEM_LIMIT),
        name="moe_dispatch",
    )(dest.reshape(nt, 1, tm * TOP_K), hn, xs0)


def _moe_experts_kernel(te_ref, nused_ref, xs_ref, wup_ref, bup_ref, wdn_ref, bdn_ref, ys_ref, wup_scr, wdn_scr):
    f32, bf16 = jnp.float32, jnp.bfloat16
    j = pl.program_id(0)
    changed = jnp.logical_or(j == 0, te_ref[j] != te_ref[jnp.maximum(j - 1, 0)])

    @pl.when(jnp.logical_and(j < nused_ref[0], changed))
    def _():
        wup_scr[...] = wup_ref[0].astype(bf16)
        wdn_scr[...] = wdn_ref[0].astype(bf16)

    @pl.when(j < nused_ref[0])
    def _():
        x = xs_ref[...].astype(bf16)
        gu = jnp.dot(x, wup_scr[...], preferred_element_type=f32) + bup_ref[0]
        a = jnp.minimum(gu[:, :D_FF], SWIGLU_LIMIT)
        u = jnp.clip(gu[:, D_FF:], -SWIGLU_LIMIT, SWIGLU_LIMIT)
        act = (u + 1.0) * a * jax.nn.sigmoid(SWIGLU_ALPHA * a)
        ys_ref[...] = jnp.dot(act.astype(bf16), wdn_scr[...], preferred_element_type=f32) + bdn_ref[0]


def _moe_experts(xs, tile_expert, n_used, w_up, b_up, w_down, b_down):
    n_slots = xs.shape[0]
    nt = n_slots // MOE_BLK
    tile = lambda j, te, nu: (jnp.minimum(j, nu[0] - 1), 0)
    wsel = lambda j, te, nu: (te[jnp.minimum(j, nu[0] - 1)], 0, 0)
    grid_spec = pltpu.PrefetchScalarGridSpec(
        num_scalar_prefetch=2,
        grid=(nt,),
        in_specs=[
            pl.BlockSpec((MOE_BLK, D_MODEL), tile),
            pl.BlockSpec((1, D_MODEL, 2 * D_FF), wsel),
            pl.BlockSpec((1, 1, 2 * D_FF), wsel),
            pl.BlockSpec((1, D_FF, D_MODEL), wsel),
            pl.BlockSpec((1, 1, D_MODEL), wsel),
        ],
        out_specs=pl.BlockSpec((MOE_BLK, D_MODEL), tile),
        scratch_shapes=[pltpu.VMEM((D_MODEL, 2 * D_FF), jnp.bfloat16), pltpu.VMEM((D_FF, D_MODEL), jnp.bfloat16)],
    )
    return pl.pallas_call(
        _moe_experts_kernel,
        grid_spec=grid_spec,
        out_shape=jax.ShapeDtypeStruct((n_slots, D_MODEL), jnp.float32),
        compiler_params=pltpu.CompilerParams(dimension_semantics=("arbitrary",), vmem_limit_bytes=VMEM_LIMIT),
        name="moe_experts",
    )(tile_expert, n_used, xs, w_up, b_up.reshape(N_EXPERTS, 1, 2 * D_FF), w_down,
      b_down.reshape(N_EXPERTS, 1, D_MODEL))


def _moe_combine_kernel(dest_ref, xn_ref, tg_ref, ys_ref, y_ref, buf, sem):
    tm = xn_ref.shape[0]

    def issue(r, _):
        for k in range(TOP_K):
            d = dest_ref[0, 0, r * TOP_K + k]
            pltpu.make_async_copy(ys_ref.at[pl.ds(d, 1)], buf.at[k, pl.ds(r, 1)], sem).start()
        return 0

    lax.fori_loop(0, tm, issue, 0)
    pltpu.make_async_copy(ys_ref.at[pl.ds(0, tm * TOP_K)], ys_ref.at[pl.ds(0, tm * TOP_K)], sem).wait()
    tg = tg_ref[...]
    acc = xn_ref[...]
    for k in range(TOP_K):
        acc = acc + tg[:, k:k + 1] * buf[k]
    y_ref[...] = acc


def _moe_combine(xn, tg, ys, dest, tm):
    m = xn.shape[0]
    nt = m // tm
    return pl.pallas_call(
        _moe_combine_kernel,
        grid=(nt,),
        in_specs=[
            pl.BlockSpec((1, 1, tm * TOP_K), lambda i: (i, 0, 0), memory_space=pltpu.SMEM),
            pl.BlockSpec((tm, D_MODEL), lambda i: (i, 0)),
            pl.BlockSpec((tm, LANES), lambda i: (i, 0)),
            pl.BlockSpec(memory_space=pl.ANY),
        ],
        out_specs=pl.BlockSpec((tm, D_MODEL), lambda i: (i, 0)),
        out_shape=jax.ShapeDtypeStruct((m, D_MODEL), jnp.float32),
        scratch_shapes=[pltpu.VMEM((TOP_K, tm, D_MODEL), jnp.float32), pltpu.SemaphoreType.DMA(())],
        compiler_params=pltpu.CompilerParams(dimension_semantics=("arbitrary",), vmem_limit_bytes=VMEM_LIMIT),
        name="moe_combine",
    )(dest.reshape(nt, 1, tm * TOP_K), xn, tg, ys)


def _moe_layer(xn, hn, te, tr, tg, cnt, w_up, b_up, w_down, b_down, tm):
    m = xn.shape[0]
    n_tiles = -(-m * TOP_K // MOE_BLK) + N_EXPERTS
    counts = cnt[0, :N_EXPERTS].astype(jnp.int32)
    padded = (counts + MOE_BLK - 1) // MOE_BLK * MOE_BLK
    pad_end = jnp.cumsum(padded)
    pad_start = pad_end - padded
    dest = (pad_start[te[:, :TOP_K]] + tr[:, :TOP_K]).astype(jnp.int32)
    tile_expert = jnp.minimum(
        jnp.searchsorted(pad_end, jnp.arange(n_tiles, dtype=jnp.int32) * MOE_BLK, side='right'),
        N_EXPERTS - 1).astype(jnp.int32)
    n_used = (pad_end[-1:] // MOE_BLK).astype(jnp.int32)
    xs = _moe_dispatch(hn, dest, n_tiles * MOE_BLK, tm)
    ys = _moe_experts(xs, tile_expert, n_used, w_up, b_up, w_down, b_down)
    return _moe_combine(xn, tg, ys, dest, tm)


def _masked_softmax(s, mask):
    s = jnp.where(mask, s.astype(jnp.float32), NEG)
    m = jnp.max(s, axis=-1, keepdims=True)
    p = jnp.where(mask, jnp.exp(s - m), 0.0)
    return p / jnp.maximum(jnp.sum(p, axis=-1, keepdims=True), 1e-30)


def _merge_branches(y_lru, o_nsa, ga, gb, w_proj_a, w_proj_b, w_out):
    return (jax.nn.sigmoid(ga) * (y_lru @ w_proj_a) + jax.nn.sigmoid(gb) * (o_nsa @ w_proj_b)) @ w_out


def _lin_combine(c1, c2):
    a1, b1 = c1
    a2, b2 = c2
    return a1 * a2, a2 * b1 + b2


def _rglru(xr, gate_in, conv_buf, h0, conv_w, conv_b, wa, ba, wx, bx, lam):
    n, s, _ = xr.shape
    xp = jnp.concatenate([conv_buf.astype(xr.dtype), xr], axis=1)
    xc = conv_b + sum(conv_w[k] * xp[:, k:k + s] for k in range(CONV_W))
    xb = xc.reshape(n, s, LRU_BLOCKS, LRU_BW)
    r = jax.nn.sigmoid((jnp.einsum('nsbi,bij->nsbj', xb, wa) + ba).astype(jnp.float32)).reshape(n, s, D_RNN)
    i = jax.nn.sigmoid((jnp.einsum('nsbi,bij->nsbj', xb, wx) + bx).astype(jnp.float32)).reshape(n, s, D_RNN)
    log_a = -LRU_C * r * jax.nn.softplus(-lam.astype(jnp.float32))
    a = jnp.exp(log_a)
    u = jnp.sqrt(-jnp.expm1(2.0 * log_a)) * (i * xc.astype(jnp.float32))
    u = u.at[:, 0].add(a[:, 0] * h0.astype(jnp.float32))
    _, h = lax.associative_scan(_lin_combine, (a, u), axis=1)
    y = (h * jax.nn.gelu(gate_in.astype(jnp.float32))).astype(xr.dtype)
    return y, xp[:, s:], h[:, -1]


def _compress(rows, pos_emb, w1, w2):
    n, l = rows.shape[:2]
    r_ = CMP_LEN // CMP_STRIDE
    nch = l // CMP_STRIDE
    ch = rows[:, :nch * CMP_STRIDE].reshape(n, nch, CMP_STRIDE, N_KV, HEAD_DIM)
    w1b = w1.reshape(r_, CMP_STRIDE, HEAD_DIM, CMP_HID)
    part = jnp.einsum('nclhd,pldf->pnchf', ch, w1b)
    nc = nch - r_ + 1
    hid = jnp.einsum('ld,ldf->f', pos_emb, w1) + sum(part[p, :, p:p + nc] for p in range(r_))
    return jnp.einsum('nchf,fd->nchd', jax.nn.gelu(hid), w2)


def _block_scores(p_cmp, nsb):
    nc = p_cmp.shape[-1]
    ratio = SLC_BLOCK // CMP_STRIDE
    ovl = CMP_LEN // CMP_STRIDE - 1
    widths = [(0, 0)] * (p_cmp.ndim - 1) + [(ovl, ratio * (nsb + 1) - nc - ovl)]
    pp = jnp.pad(p_cmp, widths).reshape(*p_cmp.shape[:-1], nsb + 1, ratio)
    return pp[..., :nsb, :].sum(-1) + pp[..., 1:, :ovl].sum(-1)


def _select_blocks(scores, pos, nsb):
    j = jnp.arange(nsb, dtype=jnp.int32)
    cur = (pos // SLC_BLOCK)[:, None, None]
    forced = (j == 0) | (j == cur) | (j == cur - 1)
    s = jnp.where(forced, BIG, jnp.where(j <= cur, scores, NEG))
    top, idx = lax.top_k(s, min(N_SEL, nsb))
    return idx, top > NEG / 2


def _nsa_attend(q, pos, cmp_k, cmp_v, gather_slc, nsb, win_k, win_v, win_pos, gates):
    n, qn = q.shape[:2]
    nc = cmp_k.shape[1]
    s = jnp.einsum('nqhgd,nchd->nqhgc', q, cmp_k)
    cmp_end = jnp.arange(nc, dtype=jnp.int32) * CMP_STRIDE + CMP_LEN - 1
    p_c = _masked_softmax(s, cmp_end <= pos[:, None, None, None])
    o_cmp = jnp.einsum('nqhgc,nchd->nqhgd', p_c.astype(cmp_v.dtype), cmp_v)
    idx, valid = _select_blocks(_block_scores(p_c.sum(3), nsb), pos, nsb)
    k_g, v_g = gather_slc(idx)
    kk = idx.shape[-1] * SLC_BLOCK
    kpos = idx[..., None] * SLC_BLOCK + jnp.arange(SLC_BLOCK, dtype=jnp.int32)
    smask = (valid[..., None] & (kpos <= pos[:, None, None, None])).reshape(n, qn, N_KV, 1, kk)
    s = jnp.einsum('nqhgd,nqhkd->nqhgk', q, k_g.reshape(n, qn, N_KV, kk, HEAD_DIM))
    p_s = _masked_softmax(s, smask)
    o_slc = jnp.einsum('nqhgk,nqhkd->nqhgd', p_s.astype(v_g.dtype), v_g.reshape(n, qn, N_KV, kk, HEAD_DIM))
    s = jnp.einsum('nqhgd,nkhd->nqhgk', q, win_k)
    dpos = pos[:, None] - win_pos[None, :]
    wmask = (dpos >= 0) & (dpos < WINDOW) & (win_pos >= 0)[None, :]
    p_w = _masked_softmax(s, wmask[:, None, None, :])
    o_win = jnp.einsum('nqhgk,nkhd->nqhgd', p_w.astype(win_v.dtype), win_v)
    g = jax.nn.sigmoid(gates)
    return g[..., 0:1] * o_cmp + g[..., 1:2] * o_slc + g[..., 2:3] * o_win


def _nsa_prompt(q, kc, vc, ks, vs, kw, vw, ng, pos_k, w1_k, w2_k, pos_v, w1_v, w2_v):
    n, s = q.shape[:2]
    cmp_k = _compress(kc, pos_k, w1_k, w2_k)
    cmp_v = _compress(vc, pos_v, w1_v, w2_v)
    nsb = s // SLC_BLOCK
    kb = ks.reshape(n, nsb, SLC_BLOCK, N_KV, HEAD_DIM)
    vb = vs.reshape(n, nsb, SLC_BLOCK, N_KV, HEAD_DIM)
    bi = jnp.arange(n)[:, None, None, None]
    hi = jnp.arange(N_KV)[None, None, :, None]

    def gather(idx):
        return kb[bi, idx, :, hi], vb[bi, idx, :, hi]

    pad = ((0, 0), (WINDOW, 0), (0, 0), (0, 0))
    kw_pad, vw_pad = jnp.pad(kw, pad), jnp.pad(vw, pad)

    def block(i):
        start = i * Q_BLOCK
        sl = lambda t, size: lax.dynamic_slice_in_dim(t, start, size, axis=1)
        pos = start + jnp.arange(Q_BLOCK, dtype=jnp.int32)
        wpos = start - WINDOW + jnp.arange(WINDOW + Q_BLOCK, dtype=jnp.int32)
        return _nsa_attend(sl(q, Q_BLOCK), pos, cmp_k, cmp_v, gather, nsb,
                           sl(kw_pad, WINDOW + Q_BLOCK), sl(vw_pad, WINDOW + Q_BLOCK), wpos, sl(ng, Q_BLOCK))

    o = lax.map(block, jnp.arange(s // Q_BLOCK, dtype=jnp.int32))
    return jnp.moveaxis(o, 0, 1).reshape(n, s, D_Q)


def _nsa_sample(q, kc, vc, ks, vs, kw, vw, ng, page_table, ck_cmp, cv_cmp, ck_slc, cv_slc, ck_win, cv_win,
                pos_k, w1_k, w2_k, pos_v, w1_v, w2_v):
    n, ds = q.shape[:2]
    pos = PAST_LEN + jnp.arange(ds, dtype=jnp.int32)

    def past(pool):
        return pool[page_table].reshape(n, PAST_LEN, N_KV, HEAD_DIM)

    cmp_k = _compress(jnp.concatenate([past(ck_cmp), kc], axis=1), pos_k, w1_k, w2_k)
    cmp_v = _compress(jnp.concatenate([past(cv_cmp), vc], axis=1), pos_v, w1_v, w2_v)
    past_blocks = PAST_LEN // SLC_BLOCK
    new_blocks = -(-ds // SLC_BLOCK)
    nsb = past_blocks + new_blocks
    bpp = PAGE_SIZE // SLC_BLOCK
    view = lambda pool: pool.reshape(pool.shape[0], bpp, SLC_BLOCK, N_KV, HEAD_DIM)
    pk, pv = view(ck_slc), view(cv_slc)
    padn = ((0, 0), (0, new_blocks * SLC_BLOCK - ds), (0, 0), (0, 0))
    nk = jnp.pad(ks, padn).reshape(n, new_blocks, SLC_BLOCK, N_KV, HEAD_DIM)
    nv = jnp.pad(vs, padn).reshape(n, new_blocks, SLC_BLOCK, N_KV, HEAD_DIM)
    bi = jnp.arange(n)[:, None, None, None]
    hi = jnp.arange(N_KV)[None, None, :, None]

    def gather(idx):
        jp = jnp.minimum(idx, past_blocks - 1)
        phys = page_table[bi, jp // bpp]
        sub = jp % bpp
        jn = jnp.clip(idx - past_blocks, 0, new_blocks - 1)
        is_new = (idx >= past_blocks)[..., None, None]
        return (jnp.where(is_new, nk[bi, jn, :, hi], pk[phys, sub, :, hi]),
                jnp.where(is_new, nv[bi, jn, :, hi], pv[phys, sub, :, hi]))

    wb = ck_win.shape[1]
    wk = jnp.concatenate([ck_win, kw], axis=1)
    wv = jnp.concatenate([cv_win, vw], axis=1)
    wpos = PAST_LEN - wb + jnp.arange(wb + ds, dtype=jnp.int32)
    o = _nsa_attend(q, pos, cmp_k, cmp_v, gather, nsb, wk, wv, wpos, ng)
    return o.reshape(n, ds, D_Q), wk[:, -wb:], wv[:, -wb:]


def _rms_norm(x, g):
    xf = x.astype(jnp.float32)
    y = xf * lax.rsqrt(jnp.mean(xf * xf, axis=-1, keepdims=True) + EPS)
    return (y * g.astype(jnp.float32)).astype(x.dtype)


def _moe(h, router_w, router_b, w_up, b_up, w_down, b_down):
    shp = h.shape
    t = h.reshape(-1, D_MODEL)
    n_tok = t.shape[0]
    logits = (t @ router_w).astype(jnp.float32) + router_b.astype(jnp.float32)
    top_v, top_e = lax.top_k(logits, TOP_K)
    gate = jax.nn.softmax(top_v, axis=-1)
    n_asg = n_tok * TOP_K
    blk = int(min(512, max(8, n_asg // N_EXPERTS)))
    n_blocks = -(-n_asg // blk) + N_EXPERTS
    e_flat = top_e.reshape(-1)
    tok_flat = jnp.repeat(jnp.arange(n_tok, dtype=jnp.int32), TOP_K)
    g_flat = gate.reshape(-1)
    order = jnp.argsort(e_flat)
    e_sorted = e_flat[order]
    counts = jnp.bincount(e_flat, length=N_EXPERTS)
    padded = (counts + blk - 1) // blk * blk
    pad_end = jnp.cumsum(padded)
    pad_start = pad_end - padded
    starts = jnp.cumsum(counts) - counts
    dest = pad_start[e_sorted] + jnp.arange(n_asg, dtype=jnp.int32) - starts[e_sorted]
    slot_tok = jnp.full((n_blocks * blk,), n_tok, jnp.int32).at[dest].set(tok_flat[order])
    slot_gate = jnp.zeros((n_blocks * blk,), jnp.float32).at[dest].set(g_flat[order])
    blk_expert = jnp.minimum(jnp.searchsorted(pad_end, jnp.arange(n_blocks, dtype=jnp.int32) * blk, side='right'),
                             N_EXPERTS - 1)
    t_pad = jnp.concatenate([t, jnp.zeros((1, D_MODEL), t.dtype)], axis=0)

    def run_block(args):
        tok, g, e = args
        gu = t_pad[tok] @ w_up[e] + b_up[e]
        a = jnp.minimum(gu[:, :D_FF], SWIGLU_LIMIT)
        u = jnp.clip(gu[:, D_FF:], -SWIGLU_LIMIT, SWIGLU_LIMIT)
        y = ((u + 1.0) * a * jax.nn.sigmoid(SWIGLU_ALPHA * a)) @ w_down[e] + b_down[e]
        return (y * g[:, None]).astype(t.dtype)

    ys = lax.map(run_block, (slot_tok.reshape(n_blocks, blk), slot_gate.reshape(n_blocks, blk), blk_expert))
    out = jnp.zeros((n_tok + 1, D_MODEL), t.dtype).at[slot_tok].add(ys.reshape(-1, D_MODEL))
    return out[:n_tok].reshape(shp)


def _pad_w_in(w):
    c0 = 2 * D_RNN
    wq = w[:, c0:c0 + D_Q].reshape(D_MODEL, N_KV, GROUP, 1, HEAD_DIM)
    sel = jnp.eye(N_KV, dtype=w.dtype).reshape(N_KV, 1, N_KV, 1)
    wq = (wq * sel).reshape(D_MODEL, D_QP)
    c1 = c0 + D_Q
    c2 = c1 + 6 * D_KV + N_NG
    pad = jnp.zeros((D_MODEL, LANES - N_NG), w.dtype)
    return jnp.concatenate([w[:, :c0], wq, w[:, c1:c2], pad, w[:, c2:]], axis=1).astype(jnp.bfloat16)


def _unpad_q(q_pad, n, s):
    qp = q_pad.reshape(n, s, N_KV, GROUP, N_KV, HEAD_DIM)
    return jnp.stack([qp[:, :, h, :, h, :] for h in range(N_KV)], axis=2)


def _project(x, pos_rows, tm, g_mix, w_pad, g_q, g_k3):
    n, s, _ = x.shape
    outs = _input_projection(x.reshape(n * s, D_MODEL), pos_rows, tm, g_mix, w_pad, g_q, g_k3)
    arag, q, kc, vc, ks, vs, kw, vw, ng, gab = outs
    heads = lambda t: t.reshape(n, s, N_KV, HEAD_DIM)
    ar = arag[:, :D_RNN].reshape(n, s, D_RNN)
    ag = arag[:, D_RNN:].reshape(n, s, D_RNN)
    q = _unpad_q(q, n, s)
    ng = ng[:, :N_NG].reshape(n, s, N_KV, GROUP, 3)
    ga = gab[:, :D_MODEL].reshape(n, s, D_MODEL)
    gb = gab[:, D_MODEL:].reshape(n, s, D_MODEL)
    return ar, ag, q, heads(kc), heads(vc), heads(ks), heads(vs), heads(kw), heads(vw), ng, ga, gb


def kernel(x_prompt, x_sample, page_table, cache_k_cmp, cache_v_cmp, cache_k_slc, cache_v_slc,
           cache_k_win, cache_v_win, state_h, state_conv, g_mix, w_in, conv_w, conv_b, lru_wa, lru_ba,
           lru_wx, lru_bx, lru_lambda, g_q, g_k_cmp, g_k_slc, g_k_win, cmp_pos_k, cmp_w1_k, cmp_w2_k,
           cmp_pos_v, cmp_w1_v, cmp_w2_v, w_proj_a, w_proj_b, w_out, g_ffn, router_w, router_b,
           exp_w_up, exp_b_up, exp_w_down, exp_b_down):
    xp, xs = x_prompt, x_sample
    bp, sp = xp.shape[:2]
    bs, ss = xs.shape[:2]
    pos_p = jnp.arange(sp, dtype=jnp.int32)
    pos_s = PAST_LEN + jnp.arange(ss, dtype=jnp.int32)
    depth = w_in.shape[0]
    layer_states = []
    tile2 = lambda g: jnp.concatenate([g, g], axis=-1)
    for l in range(depth):
        cmp_w = (cmp_pos_k[l], cmp_w1_k[l], cmp_w2_k[l], cmp_pos_v[l], cmp_w1_v[l], cmp_w2_v[l])
        lru_w = (conv_w[l], conv_b[l], lru_wa[l], lru_ba[l], lru_wx[l], lru_bx[l], lru_lambda[l])
        w_pad = _pad_w_in(w_in[l])
        gq2 = tile2(g_q[l]).reshape(1, LANES)
        gk3 = jnp.stack([tile2(g_k_cmp[l]), tile2(g_k_slc[l]), tile2(g_k_win[l])])
        cmp_wk = _compress_weights(cmp_pos_k[l], cmp_w1_k[l], cmp_w2_k[l])
        cmp_wv = _compress_weights(cmp_pos_v[l], cmp_w1_v[l], cmp_w2_v[l])
        wb_perm = w_proj_b[l].reshape(N_KV, GROUP, HEAD_DIM, D_MODEL).transpose(1, 0, 2, 3).reshape(D_Q, D_MODEL)
        bf16 = jnp.bfloat16
        lru_k = _lru_weights(*lru_w)
        rw_pad = jnp.pad(router_w[l].astype(jnp.float32), ((0, 0), (0, LANES - N_EXPERTS)))
        rb_pad = jnp.pad(router_b[l].astype(jnp.float32), (0, LANES - N_EXPERTS)).reshape(1, LANES)
        merge_w = (w_proj_a[l].astype(bf16), wb_perm.astype(bf16), w_out[l].astype(bf16), g_ffn[l], rw_pad, rb_pad)
        moe_w = (exp_w_up[l], exp_b_up[l], exp_w_down[l], exp_b_down[l])
        arag, q_pad, kc, vc, ks, vs, kw, vw, ng, gab = _input_projection(
            xp.reshape(bp * sp, D_MODEL), pos_p, 256, g_mix[l], w_pad, gq2, gk3)
        rows = lambda t: t.reshape(bp, sp, t.shape[-1])
        heads = lambda t: t.reshape(bp, sp, N_KV, HEAD_DIM)
        conv0 = jnp.zeros((bp, CONV_W - 1, D_RNN), jnp.float32)
        y_lru, p_h = _lru_prompt(arag, conv0, jnp.zeros((bp, D_RNN), jnp.float32), lru_k, bp, sp, 256)
        p_conv = jnp.concatenate([conv0, rows(arag)[:, :, :D_RNN]], axis=1)[:, sp:]
        cmp_k, cmp_v = _compress_prompt(rows(kc), rows(vc), cmp_wk, cmp_wv)
        o_nsa = _nsa_prompt_attention(rows(q_pad), rows(ks), rows(vs), rows(kw), rows(vw), cmp_k, cmp_v, rows(ng))
        xn, hn, te, tr, tg, cnt = _merge_router(xp.reshape(bp * sp, D_MODEL), y_lru, o_nsa.reshape(bp * sp, D_Q),
                                                gab, *merge_w, 256)
        xp = _moe_layer(xn, hn, te, tr, tg, cnt, *moe_w, 256).reshape(bp, sp, D_MODEL)
        wbp = min(WINDOW, sp)
        p_states = (heads(kc), heads(vc), heads(ks), heads(vs), heads(kw)[:, -wbp:], heads(vw)[:, -wbp:],
                    p_h, p_conv)
        assert ss == 1
        pos_rows_s = jnp.broadcast_to(pos_s[None, :], (bs, ss)).reshape(-1)
        arag, q_pad, kc, vc, ks, vs, kw, vw, ng, gab = _input_projection(
            xs.reshape(bs, D_MODEL), pos_rows_s, bs, g_mix[l], w_pad, gq2, gk3)
        heads = lambda t: t.reshape(bs, ss, N_KV, HEAD_DIM)
        y_lru, s_h_l = _lru_step(arag, state_conv[l], state_h[l], lru_k)
        s_conv_l = jnp.concatenate([state_conv[l][:, 1:], arag[:, None, :D_RNN]], axis=1)
        o_nsa, s_kw_l, s_vw_l = _nsa_sample_paged(
            q_pad, ks, vs, kw, vw, ng, page_table, cache_k_cmp[l], cache_v_cmp[l], cache_k_slc[l], cache_v_slc[l],
            cache_k_win[l], cache_v_win[l], cmp_wk, cmp_wv)
        xn, hn, te, tr, tg, cnt = _merge_router(xs.reshape(bs, D_MODEL), y_lru, o_nsa, gab, *merge_w, bs)
        xs = _moe_layer(xn, hn, te, tr, tg, cnt, *moe_w, bs).reshape(bs, ss, D_MODEL)
        s_states = (heads(kc), heads(vc), heads(ks), heads(vs), s_kw_l, s_vw_l, s_h_l, s_conv_l)
        layer_states.append(p_states + s_states)
    (p_k_cmp, p_v_cmp, p_k_slc, p_v_slc, p_k_win, p_v_win, p_h, p_conv,
     s_k_cmp, s_v_cmp, s_k_slc, s_v_slc, s_k_win, s_v_win, s_h, s_conv) = [jnp.stack(z) for z in zip(*layer_states)]
    return (xp, xs, p_k_cmp, p_v_cmp, p_k_slc, p_v_slc, p_k_win, p_v_win, p_h, p_conv,
            s_k_cmp, s_v_cmp, s_k_slc, s_v_slc, s_k_win, s_v_win, s_h, s_conv)
```

```python
import functools
import math

import jax
import jax.numpy as jnp
import numpy as np
from jax import lax
from jax.experimental import pallas as pl
from jax.experimental.pallas import tpu as pltpu

D_MODEL = 1024
PAST_LEN = 8192
PAGE_SIZE = 128
D_RNN = D_MODEL
LRU_BLOCKS = 16
LRU_BW = D_RNN // LRU_BLOCKS
CONV_W = 4
LRU_C = 8.0
N_HEADS = 8
HEAD_DIM = 64
N_KV = 2
GROUP = N_HEADS // N_KV
D_Q = N_HEADS * HEAD_DIM
D_KV = N_KV * HEAD_DIM
ROPE_DIM = HEAD_DIM // 4
ROPE_THETA = 500000.0
CMP_LEN = 32
CMP_STRIDE = 16
CMP_HID = 256
SLC_BLOCK = 64
N_SEL = 16
WINDOW = 512
Q_BLOCK = 64
N_EXPERTS = 32
TOP_K = 4
D_FF = D_MODEL
SWIGLU_LIMIT = 7.0
SWIGLU_ALPHA = 1.702
EPS = 1e-6
NEG = -1e30
BIG = 1e30

LANES = 128
SUBLANES = 8
VMEM_LIMIT = 56 * 1024 * 1024

N_NG = 3 * N_HEADS
D_QP = N_HEADS * LANES
C_AR = 0
C_Q = 2 * D_RNN
C_KV = C_Q + D_QP
C_NG = C_KV + 6 * D_KV
C_GAB = C_NG + LANES
N_IN_PAD = C_GAB + 2 * D_MODEL


def _seg_sum64(x2, ones_blk):
    hi = x2.astype(jnp.bfloat16)
    lo = (x2 - hi.astype(jnp.float32)).astype(jnp.bfloat16)
    return (jnp.dot(hi, ones_blk, preferred_element_type=jnp.float32)
            + jnp.dot(lo, ones_blk, preferred_element_type=jnp.float32))


def _head_norm_rope(y, gain, ones_blk, rc, rs1, rs2):
    ss = _seg_sum64(y * y, ones_blk)
    yn = y * lax.rsqrt(ss * (1.0 / HEAD_DIM) + EPS) * gain
    half = ROPE_DIM // 2
    return yn * rc + pltpu.roll(yn, LANES - half, 1) * rs1 + pltpu.roll(yn, half, 1) * rs2


def _proj_kernel(x_ref, gmix_ref, w_ref, ones_ref, gq_ref, gk_ref, rc_ref, rs1_ref, rs2_ref,
                 arag_ref, q_ref, kc_ref, vc_ref, ks_ref, vs_ref, kw_ref, vw_ref, ng_ref, gab_ref):
    x = x_ref[...]
    xn = x * lax.rsqrt(jnp.mean(x * x, axis=-1, keepdims=True) + EPS) * gmix_ref[...]
    xb = xn.astype(jnp.bfloat16)

    def proj(c0, width):
        return jnp.dot(xb, w_ref[:, c0:c0 + width], preferred_element_type=jnp.float32)

    arag_ref[...] = proj(C_AR, 2 * D_RNN)
    gab_ref[...] = proj(C_GAB, 2 * D_MODEL)
    ng_ref[...] = proj(C_NG, LANES)
    ones_blk = ones_ref[...]
    rc, rs1, rs2 = rc_ref[...], rs1_ref[...], rs2_ref[...]
    scale = HEAD_DIM ** -0.5
    for j in range(N_HEADS):
        y = proj(C_Q + j * LANES, LANES)
        yr = _head_norm_rope(y, gq_ref[...], ones_blk, rc, rs1, rs2)
        q_ref[:, j * LANES:(j + 1) * LANES] = (yr * scale).astype(q_ref.dtype)
    k_outs = (kc_ref, ks_ref, kw_ref)
    v_outs = (vc_ref, vs_ref, vw_ref)
    for j in range(3):
        y = proj(C_KV + 2 * j * LANES, LANES)
        k_outs[j][...] = _head_norm_rope(y, gk_ref[j:j + 1, :], ones_blk, rc, rs1, rs2)
        v_outs[j][...] = proj(C_KV + (2 * j + 1) * LANES, LANES)


def _rope_tables(pos):
    half = ROPE_DIM // 2
    inv = ROPE_THETA ** (-jnp.arange(half, dtype=jnp.float32) / half)
    ang = pos.astype(jnp.float32)[:, None] * inv[None, :]
    cos, sin = jnp.cos(ang), jnp.sin(ang)
    p = pos.shape[0]
    ones = jnp.ones((p, HEAD_DIM - ROPE_DIM), jnp.float32)
    zeros = jnp.zeros((p, HEAD_DIM - ROPE_DIM), jnp.float32)
    zh = jnp.zeros((p, half), jnp.float32)
    rc = jnp.concatenate([cos, cos, ones], axis=-1)
    rs1 = jnp.concatenate([-sin, zh, zeros], axis=-1)
    rs2 = jnp.concatenate([zh, sin, zeros], axis=-1)
    tile2 = lambda t: jnp.concatenate([t, t], axis=-1)
    return tile2(rc), tile2(rs1), tile2(rs2)


def _ones_block64():
    i = np.arange(LANES)
    return jnp.asarray((i[:, None] // HEAD_DIM) == (i[None, :] // HEAD_DIM), dtype=jnp.bfloat16)


def _input_projection(x2d, pos_rows, tm, g_mix, w_pad, g_q, g_k3):
    m = x2d.shape[0]
    p = pos_rows.shape[0]
    assert m % tm == 0 and p % tm == 0
    rc, rs1, rs2 = _rope_tables(pos_rows)
    n_tab = p // tm
    row = lambda i: (i, 0)
    const = lambda i: (0, 0)
    tab = lambda i: (i % n_tab, 0)
    f32 = jnp.float32
    out_shape = (
        jax.ShapeDtypeStruct((m, 2 * D_RNN), f32),
        jax.ShapeDtypeStruct((m, D_QP), jnp.bfloat16),
    ) + tuple(jax.ShapeDtypeStruct((m, D_KV), f32) for _ in range(6)) + (
        jax.ShapeDtypeStruct((m, LANES), f32),
        jax.ShapeDtypeStruct((m, 2 * D_MODEL), f32),
    )
    out_specs = (
        pl.BlockSpec((tm, 2 * D_RNN), row),
        pl.BlockSpec((tm, D_QP), row),
    ) + tuple(pl.BlockSpec((tm, D_KV), row) for _ in range(6)) + (
        pl.BlockSpec((tm, LANES), row),
        pl.BlockSpec((tm, 2 * D_MODEL), row),
    )
    return pl.pallas_call(
        _proj_kernel,
        grid=(m // tm,),
        in_specs=[
            pl.BlockSpec((tm, D_MODEL), row),
            pl.BlockSpec((1, D_MODEL), const),
            pl.BlockSpec((D_MODEL, N_IN_PAD), const),
            pl.BlockSpec((LANES, LANES), const),
            pl.BlockSpec((1, LANES), const),
            pl.BlockSpec((3, LANES), const),
            pl.BlockSpec((tm, LANES), tab),
            pl.BlockSpec((tm, LANES), tab),
            pl.BlockSpec((tm, LANES), tab),
        ],
        out_specs=out_specs,
        out_shape=out_shape,
        compiler_params=pltpu.CompilerParams(dimension_semantics=("arbitrary",), vmem_limit_bytes=VMEM_LIMIT),
        name="input_projection",
    )(x2d, g_mix.reshape(1, D_MODEL), w_pad, _ones_block64(), g_q, g_k3, rc, rs1, rs2)


SLC_CHUNK = 512
WIN_SPAN = WINDOW + LANES
ROWS = GROUP * Q_BLOCK


def _dot_nt(a, b):
    return lax.dot_general(a, b, (((1,), (1,)), ((), ())), preferred_element_type=jnp.float32)


def _softmax_rows(s, mask):
    s = jnp.where(mask, s, NEG)
    m = jnp.max(s, axis=-1, keepdims=True)
    p = jnp.where(mask, jnp.exp(s - m), 0.0)
    return p / jnp.maximum(jnp.sum(p, axis=-1, keepdims=True), 1e-30)


def _split3_bf16(x):
    hi = x.astype(jnp.bfloat16)
    r1 = x - hi.astype(jnp.float32)
    mid = r1.astype(jnp.bfloat16)
    lo = (r1 - mid.astype(jnp.float32)).astype(jnp.bfloat16)
    return hi, mid, lo


def _nsa_prompt_kernel(q_ref, ksa_ref, vs_ref, kw_ref, vw_ref, ck_ref, cv_ref, ng_ref, mmt_ref,
                       o_ref, sv_scr):
    f32, bf16 = jnp.float32, jnp.bfloat16
    qb = pl.program_id(1)
    row = lax.broadcasted_iota(jnp.int32, (ROWS, 1), 0)
    pos_t = qb * Q_BLOCK + (row & (Q_BLOCK - 1))
    ncp = ck_ref.shape[1]

    qh = [jnp.concatenate([q_ref[0, :, (h * GROUP + g) * LANES:(h * GROUP + g + 1) * LANES]
                           for g in range(GROUP)], axis=0) for h in range(N_KV)]

    ck, cv = ck_ref[0], cv_ref[0]
    c_end = lax.broadcasted_iota(jnp.int32, (1, ncp), 1) * CMP_STRIDE + (CMP_LEN - 1)
    cmask = c_end <= pos_t
    o_cmp, psum = [], []
    for h in range(N_KV):
        p = _softmax_rows(_dot_nt(qh[h], ck), cmask)
        o_cmp.append(jnp.dot(p.astype(bf16), cv, preferred_element_type=f32))
        psum.append(p[0:Q_BLOCK] + p[Q_BLOCK:2 * Q_BLOCK] + p[2 * Q_BLOCK:3 * Q_BLOCK] + p[3 * Q_BLOCK:])
    ps = jnp.concatenate(psum, axis=0)

    mmt = mmt_ref[...]
    sc = sum(_dot_nt(mmt, part) for part in _split3_bf16(ps))[0:Q_BLOCK]
    jio = lax.broadcasted_iota(jnp.int32, (Q_BLOCK, LANES), 0)
    forced = (jio == 0) | (jio == qb) | (jio == qb - 1)
    sv = jnp.where(forced, BIG, jnp.where(jio <= qb, sc, NEG))
    sv_scr[...] = sv

    def rank_body(i, cnt):
        r = sv_scr[pl.ds(i, 1), :]
        beats = (r > sv) | ((r == sv) & (jio > i))
        return cnt + jnp.where(beats, 1.0, 0.0)

    cnt = lax.fori_loop(0, qb + 1, rank_body, jnp.zeros((Q_BLOCK, LANES), f32))
    bias_lo = jnp.where((cnt < N_SEL) & (sv > NEG / 2), 0.0, NEG)
    lane = lax.broadcasted_iota(jnp.int32, (Q_BLOCK, LANES), 1)
    bias_t = jnp.concatenate([jnp.where(lane >= Q_BLOCK, bias_lo, 0.0),
                              jnp.where(lane < Q_BLOCK, bias_lo, 0.0)], axis=0)
    bias = bias_t.T.astype(bf16)

    kio = lax.broadcasted_iota(jnp.int32, (1, SLC_CHUNK), 1)
    wio = lax.broadcasted_iota(jnp.int32, (1, WIN_SPAN), 1)
    wbase = pl.multiple_of((jnp.maximum(qb - WINDOW // Q_BLOCK, 0) // 2) * LANES, LANES)
    dpos = pos_t - (wbase + wio)
    wmask = (dpos >= 0) & (dpos < WINDOW)
    n_chunks = qb // (SLC_CHUNK // SLC_BLOCK) + 1
    sig = jax.nn.sigmoid(ng_ref[0])
    qa = [qh[h] + jnp.concatenate([bias[h * Q_BLOCK:(h + 1) * Q_BLOCK]] * GROUP, axis=0) for h in range(N_KV)]

    def chunk(c, carry, causal):
        k0 = pl.multiple_of(c * SLC_CHUNK, SLC_CHUNK)
        v = vs_ref[0, pl.ds(k0, SLC_CHUNK), :]
        new = []
        for h in range(N_KV):
            m, l, acc = carry[h]
            s = _dot_nt(qa[h], ksa_ref[0, h, pl.ds(k0, SLC_CHUNK), :])
            if causal:
                s = jnp.where(k0 + kio <= pos_t, s, NEG)
            m_new = jnp.maximum(m, jnp.max(s, axis=-1, keepdims=True))
            alpha = jnp.exp(m - m_new)
            p = jnp.exp(s - m_new)
            l = alpha * l + jnp.sum(p, axis=-1, keepdims=True)
            acc = alpha * acc + jnp.dot(p.astype(bf16), v, preferred_element_type=f32)
            new.append((m_new, l, acc))
        return tuple(new)

    init = (jnp.full((ROWS, 1), NEG, f32), jnp.zeros((ROWS, 1), f32), jnp.zeros((ROWS, LANES), f32))
    carry = lax.fori_loop(0, n_chunks - 1, lambda c, cr: chunk(c, cr, False), (init, init))
    carry = chunk(n_chunks - 1, carry, True)

    out_h = []
    for h in range(N_KV):
        _, l, acc = carry[h]
        o_slc = acc / jnp.maximum(l, 1e-30)

        pw = _softmax_rows(_dot_nt(qh[h], kw_ref[0, pl.ds(wbase, WIN_SPAN), :]), wmask)
        o_win = jnp.dot(pw.astype(bf16), vw_ref[0, pl.ds(wbase, WIN_SPAN), :], preferred_element_type=f32)

        outs = []
        for g in range(GROUP):
            rs = slice(g * Q_BLOCK, (g + 1) * Q_BLOCK)
            c0 = (h * GROUP + g) * 3
            outs.append(sig[:, c0:c0 + 1] * o_cmp[h][rs] + sig[:, c0 + 1:c0 + 2] * o_slc[rs]
                        + sig[:, c0 + 2:c0 + 3] * o_win[rs])
        out_h.append(outs)
    for g in range(GROUP):
        o_ref[0, :, g * LANES:(g + 1) * LANES] = jnp.where(lane < HEAD_DIM, out_h[0][g], out_h[1][g]).astype(o_ref.dtype)


def _block_score_matrix(ncp):
    ratio = SLC_BLOCK // CMP_STRIDE
    ovl = CMP_LEN // CMP_STRIDE - 1
    j = np.arange(LANES)[:, None]
    c = np.arange(ncp)[None, :]
    m = (c >= ratio * j - ovl) & (c <= ratio * j + ratio - 1) & (j < Q_BLOCK) & (c < ncp - 1)
    return jnp.asarray(m, dtype=jnp.bfloat16)


def _nsa_prompt_attention(q_pad, ks, vs, kw, vw, cmp_k, cmp_v, ng):
    b, s, _ = q_pad.shape
    assert s % SLC_CHUNK == 0 and s // SLC_BLOCK <= Q_BLOCK and s >= WIN_SPAN and Q_BLOCK == SLC_BLOCK
    bf16 = jnp.bfloat16
    ncp = cmp_k.shape[1]
    onehot = jax.nn.one_hot(jnp.arange(s) // SLC_BLOCK, Q_BLOCK, dtype=bf16)
    onehot = jnp.broadcast_to(onehot[None], (b, s, Q_BLOCK))
    ksb = ks.astype(bf16)
    ksa = jnp.stack([jnp.concatenate([ksb[:, :, :HEAD_DIM], onehot], axis=-1),
                     jnp.concatenate([onehot, ksb[:, :, HEAD_DIM:]], axis=-1)], axis=1)
    per_b = lambda bi, i: (bi, 0, 0)
    return pl.pallas_call(
        _nsa_prompt_kernel,
        grid=(b, s // Q_BLOCK),
        in_specs=[
            pl.BlockSpec((1, Q_BLOCK, D_QP), lambda bi, i: (bi, i, 0)),
            pl.BlockSpec((1, N_KV, s, LANES), lambda bi, i: (bi, 0, 0, 0)),
            pl.BlockSpec((1, s, LANES), per_b),
            pl.BlockSpec((1, s, LANES), per_b),
            pl.BlockSpec((1, s, LANES), per_b),
            pl.BlockSpec((1, ncp, LANES), per_b),
            pl.BlockSpec((1, ncp, LANES), per_b),
            pl.BlockSpec((1, Q_BLOCK, LANES), lambda bi, i: (bi, i, 0)),
            pl.BlockSpec((LANES, ncp), lambda bi, i: (0, 0)),
        ],
        out_specs=pl.BlockSpec((1, Q_BLOCK, D_Q), lambda bi, i: (bi, i, 0)),
        out_shape=jax.ShapeDtypeStruct((b, s, D_Q), bf16),
        scratch_shapes=[pltpu.VMEM((Q_BLOCK, LANES), jnp.float32)],
        compiler_params=pltpu.CompilerParams(dimension_semantics=("arbitrary", "arbitrary"),
                                             vmem_limit_bytes=VMEM_LIMIT),
        name="nsa_prompt_attention",
    )(q_pad, ksa, vs.astype(bf16), kw.astype(bf16), vw.astype(bf16), cmp_k, cmp_v, ng, _block_score_matrix(ncp))


CHUNK_W = CMP_STRIDE * D_KV
CMP_PARTS = CMP_LEN // CMP_STRIDE
HID2 = N_KV * CMP_HID


def _compress_weights(pos_emb, w1, w2):
    bf16 = jnp.bfloat16
    eye = jnp.eye(N_KV, dtype=w1.dtype)
    w1b = w1.reshape(CMP_PARTS, CMP_STRIDE, HEAD_DIM, CMP_HID)
    w1p = jnp.einsum('pldf,hk->lhdpkf', w1b, eye).reshape(CHUNK_W, CMP_PARTS * HID2).astype(bf16)
    w2p = jnp.einsum('fd,hk->hfkd', w2, eye).reshape(HID2, D_KV).astype(bf16)
    pos8 = jnp.zeros((8, CMP_LEN * HEAD_DIM), bf16).at[0].set(pos_emb.reshape(-1).astype(bf16))
    w1f = w1.reshape(CMP_LEN * HEAD_DIM, CMP_HID).astype(bf16)
    return w1p, w2p, pos8, w1f


def _compress_rows(x, w1p, w2p, pos8, w1f):
    f32 = jnp.float32
    r = x.shape[0]
    part = jnp.dot(x, w1p, preferred_element_type=f32)
    bias = jnp.dot(pos8, w1f, preferred_element_type=f32)[0:1]
    bias = jnp.concatenate([bias] * N_KV, axis=-1)
    hid = bias + part[:, :HID2] + pltpu.roll(part[:, HID2:], r - 1, 0)
    out = jnp.dot(jax.nn.gelu(hid).astype(jnp.bfloat16), w2p, preferred_element_type=f32)
    rows = lax.broadcasted_iota(jnp.int32, (r, 1), 0)
    return jnp.where(rows < r - 1, out, 0.0)


def _compress_prompt_kernel(xk_ref, xv_ref, w1k_ref, w2k_ref, pk_ref, fk_ref, w1v_ref, w2v_ref, pv_ref, fv_ref,
                            ok_ref, ov_ref):
    bf16 = jnp.bfloat16
    ok_ref[0] = _compress_rows(xk_ref[0].astype(bf16), w1k_ref[...], w2k_ref[...], pk_ref[...],
                               fk_ref[...]).astype(ok_ref.dtype)
    ov_ref[0] = _compress_rows(xv_ref[0].astype(bf16), w1v_ref[...], w2v_ref[...], pv_ref[...],
                               fv_ref[...]).astype(ov_ref.dtype)


def _compress_prompt(kc, vc, wk, wv):
    b, s, _ = kc.shape
    r = s // CMP_STRIDE
    xspec = pl.BlockSpec((1, r, CHUNK_W), lambda i: (i, 0, 0))
    wspecs = [pl.BlockSpec(w.shape, lambda i: (0, 0)) for w in wk]
    ospec = pl.BlockSpec((1, r, D_KV), lambda i: (i, 0, 0))
    oshape = jax.ShapeDtypeStruct((b, r, D_KV), jnp.bfloat16)
    return pl.pallas_call(
        _compress_prompt_kernel,
        grid=(b,),
        in_specs=[xspec, xspec] + wspecs + wspecs,
        out_specs=(ospec, ospec),
        out_shape=(oshape, oshape),
        compiler_params=pltpu.CompilerParams(dimension_semantics=("arbitrary",), vmem_limit_bytes=VMEM_LIMIT),
        name="compress_prompt",
    )(kc.reshape(b, r, CHUNK_W), vc.reshape(b, r, CHUNK_W), *wk, *wv)


N_PAGES = PAST_LEN // PAGE_SIZE
PAST_CHUNKS = PAST_LEN // CMP_STRIDE
PAST_BLOCKS = PAST_LEN // SLC_BLOCK
BLOCKS_PER_PAGE = PAGE_SIZE // SLC_BLOCK
CHUNKS_PER_PAGE = PAGE_SIZE // CMP_STRIDE
SEL_ROWS = 256
QROWS = SUBLANES
REGROUP_UNROLL = 4


def _sample_cmp_kernel(pt_ref, q8_ref, kpool, vpool, w1k_ref, w2k_ref, pk_ref, fk_ref, w1v_ref, w2v_ref, pv_ref,
                       fv_ref, ocmp_ref, psum_ref, kbuf, vbuf, x_scr, t_scr, sem):
    f32, bf16 = jnp.float32, jnp.bfloat16
    n = pl.program_id(0)

    def fetch(s, slot):
        def body(j, _):
            pg = pt_ref[s, j]
            pltpu.make_async_copy(kpool.at[pg], kbuf.at[slot, j], sem.at[0, slot]).start()
            pltpu.make_async_copy(vpool.at[pg], vbuf.at[slot, j], sem.at[1, slot]).start()
            return 0
        lax.fori_loop(0, N_PAGES, body, 0)

    @pl.when(n == 0)
    def _():
        fetch(0, 0)

    @pl.when(n + 1 < pl.num_programs(0))
    def _():
        fetch(n + 1, (n + 1) % 2)

    slot = n % 2
    pltpu.make_async_copy(kbuf.at[slot], kbuf.at[slot], sem.at[0, slot]).wait()
    pltpu.make_async_copy(vbuf.at[slot], vbuf.at[slot], sem.at[1, slot]).wait()

    def regroup(buf):
        def body(jj, _):
            for u in range(REGROUP_UNROLL):
                j = jj * REGROUP_UNROLL + u
                t_scr[u] = buf[slot, j].T
                r0 = pl.multiple_of(j * CHUNKS_PER_PAGE, CHUNKS_PER_PAGE)
                for l in range(CMP_STRIDE):
                    x_scr[pl.ds(r0, CHUNKS_PER_PAGE), l * D_KV:(l + 1) * D_KV] = (
                        t_scr[u, pl.ds(l, CHUNKS_PER_PAGE, stride=CMP_STRIDE), :])
            return 0
        lax.fori_loop(0, N_PAGES // REGROUP_UNROLL, body, 0)
        return x_scr[...].astype(bf16)

    ck = _compress_rows(regroup(kbuf), w1k_ref[...], w2k_ref[...], pk_ref[...], fk_ref[...]).astype(bf16)
    cv = _compress_rows(regroup(vbuf), w1v_ref[...], w2v_ref[...], pv_ref[...], fv_ref[...]).astype(bf16)
    c_end = lax.broadcasted_iota(jnp.int32, (1, PAST_CHUNKS), 1) * CMP_STRIDE + (CMP_LEN - 1)
    cmask = c_end <= PAST_LEN
    rows = lax.broadcasted_iota(jnp.int32, (QROWS, 1), 0)
    ps = jnp.zeros((QROWS, PAST_CHUNKS), f32)
    for h in range(N_KV):
        p = _softmax_rows(_dot_nt(q8_ref[0, h], ck), cmask)
        ocmp_ref[0, h] = jnp.dot(p.astype(bf16), cv, preferred_element_type=f32)
        ps = jnp.where(rows == h, p[0:1] + p[1:2] + p[2:3] + p[3:4], ps)
    psum_ref[0] = ps


def _sample_cmp(page_table, q8, k_pool, v_pool, wk, wv):
    n = q8.shape[0]
    f32 = jnp.float32
    wspecs = [pl.BlockSpec(w.shape, lambda i, pt: (0, 0)) for w in wk]
    grid_spec = pltpu.PrefetchScalarGridSpec(
        num_scalar_prefetch=1,
        grid=(n,),
        in_specs=[pl.BlockSpec((1, N_KV, QROWS, LANES), lambda i, pt: (i, 0, 0, 0)),
                  pl.BlockSpec(memory_space=pl.ANY), pl.BlockSpec(memory_space=pl.ANY)] + wspecs + wspecs,
        out_specs=(pl.BlockSpec((1, N_KV, QROWS, LANES), lambda i, pt: (i, 0, 0, 0)),
                   pl.BlockSpec((1, QROWS, PAST_CHUNKS), lambda i, pt: (i, 0, 0))),
        scratch_shapes=[pltpu.VMEM((2, N_PAGES, D_KV, PAGE_SIZE), f32),
                        pltpu.VMEM((2, N_PAGES, D_KV, PAGE_SIZE), f32),
                        pltpu.VMEM((PAST_CHUNKS, CHUNK_W), f32),
                        pltpu.VMEM((REGROUP_UNROLL, PAGE_SIZE, D_KV), f32),
                        pltpu.SemaphoreType.DMA((2, 2))],
    )
    return pl.pallas_call(
        _sample_cmp_kernel,
        grid_spec=grid_spec,
        out_shape=(jax.ShapeDtypeStruct((n, N_KV, QROWS, LANES), f32),
                   jax.ShapeDtypeStruct((n, QROWS, PAST_CHUNKS), f32)),
        compiler_params=pltpu.CompilerParams(dimension_semantics=("arbitrary",), vmem_limit_bytes=VMEM_LIMIT),
        name="sample_compress_attention",
    )(page_table, q8, k_pool, v_pool, *wk, *wv)


def _sample_select_kernel(p_ref, mmt_ref, tri_ref, idx_ref, sv_scr):
    f32 = jnp.float32
    cols = p_ref.shape[0]
    cur = PAST_LEN // SLC_BLOCK
    mmt = mmt_ref[...]
    sc = sum(_dot_nt(mmt, part) for part in _split3_bf16(p_ref[...]))
    jio = lax.broadcasted_iota(jnp.int32, (SEL_ROWS, cols), 0)
    forced = (jio == 0) | (jio == cur) | (jio == cur - 1)
    sv = jnp.where(forced, BIG, jnp.where(jio <= cur, sc, NEG))
    sv_scr[...] = sv

    def rank_body(i, cnt):
        r = sv_scr[pl.ds(i, 1), :]
        beats = (r > sv) | ((r == sv) & (jio > i))
        return cnt + jnp.where(beats, 1.0, 0.0)

    cnt = lax.fori_loop(0, cur + 1, rank_body, jnp.zeros((SEL_ROWS, cols), f32))
    sel = (cnt < N_SEL) & (sv > NEG / 2)
    self = jnp.where(sel, 1.0, 0.0)
    before = jnp.dot(tri_ref[...], self.astype(jnp.bfloat16), preferred_element_type=f32)
    rows = []
    for k in range(N_SEL):
        rows.append(jnp.sum(jnp.where(sel & (before == k), jio, 0), axis=0, keepdims=True))
    idx_ref[...] = jnp.concatenate(rows, axis=0)


def _sample_select(p_sum):
    n = p_sum.shape[0]
    cols = n * N_KV
    p2 = p_sum[:, :N_KV, :].reshape(cols, PAST_CHUNKS)
    ratio = SLC_BLOCK // CMP_STRIDE
    ovl = CMP_LEN // CMP_STRIDE - 1
    j = np.arange(SEL_ROWS)[:, None]
    c = np.arange(PAST_CHUNKS)[None, :]
    mmt = (c >= ratio * j - ovl) & (c <= ratio * j + ratio - 1) & (j <= PAST_BLOCKS) & (c < PAST_CHUNKS - 1)
    tri = np.tril(np.ones((SEL_ROWS, SEL_ROWS), np.float32), -1)
    idx = pl.pallas_call(
        _sample_select_kernel,
        out_shape=jax.ShapeDtypeStruct((N_SEL, cols), jnp.int32),
        scratch_shapes=[pltpu.VMEM((SEL_ROWS, cols), jnp.float32)],
        compiler_params=pltpu.CompilerParams(vmem_limit_bytes=VMEM_LIMIT),
        name="sample_select",
    )(p2, jnp.asarray(mmt, dtype=jnp.bfloat16), jnp.asarray(tri, dtype=jnp.bfloat16))
    return idx.T.reshape(n, N_KV * N_SEL)


SLC_KEYS = N_SEL * SLC_BLOCK
EXT = LANES


def _sample_attend_kernel(pt_ref, idx_ref, q8_ref, kpool, vpool, knew_ref, vnew_ref, kwin_ref, vwin_ref, kwn_ref,
                          vwn_ref, ocmp_ref, ng_ref, o_ref, kwo_ref, vwo_ref, kg, vg, sem):
    f32, bf16 = jnp.float32, jnp.bfloat16
    n = pl.program_id(0)
    nblk = N_KV * N_SEL

    def fetch(s, slot):
        def body(i, _):
            jp = jnp.minimum(idx_ref[s, i], PAST_BLOCKS - 1)
            pg = pt_ref[s, jp // BLOCKS_PER_PAGE]
            sub = jp % BLOCKS_PER_PAGE
            pltpu.make_async_copy(kpool.at[pg, sub], kg.at[slot, i], sem.at[0, slot]).start()
            pltpu.make_async_copy(vpool.at[pg, sub], vg.at[slot, i], sem.at[1, slot]).start()
            return 0
        lax.fori_loop(0, nblk, body, 0)

    @pl.when(n == 0)
    def _():
        fetch(0, 0)

    @pl.when(n + 1 < pl.num_programs(0))
    def _():
        fetch(n + 1, (n + 1) % 2)

    slot = n % 2
    pltpu.make_async_copy(kg.at[slot], kg.at[slot], sem.at[0, slot]).wait()
    pltpu.make_async_copy(vg.at[slot], vg.at[slot], sem.at[1, slot]).wait()

    pos = PAST_LEN
    ext_row = lax.broadcasted_iota(jnp.int32, (EXT, 1), 0)
    ext = lambda ref: jnp.where(ext_row == 0, ref[0], 0.0).astype(bf16)
    col = lax.broadcasted_iota(jnp.int32, (1, SLC_KEYS + EXT), 1)
    wcol = lax.broadcasted_iota(jnp.int32, (1, WINDOW + EXT), 1)
    wpos = jnp.where(wcol < WINDOW, pos - WINDOW + wcol, pos + (wcol - WINDOW))
    wmask = (pos - wpos >= 0) & (pos - wpos < WINDOW)
    kwin, vwin = kwin_ref[0], vwin_ref[0]
    kw_all = jnp.concatenate([kwin.astype(bf16), ext(kwn_ref)], axis=0)
    vw_all = jnp.concatenate([vwin.astype(bf16), ext(vwn_ref)], axis=0)
    k_ext, v_ext = ext(knew_ref), ext(vnew_ref)
    lane = lax.broadcasted_iota(jnp.int32, (QROWS, LANES), 1)
    out = []
    for h in range(N_KV):
        q = q8_ref[0, h]
        new_sel = jnp.int32(0)
        valid = jnp.zeros((1, SLC_KEYS + EXT), jnp.int32)
        for p in range(N_SEL):
            j = idx_ref[n, h * N_SEL + p]
            valid = jnp.where((col >> 6) == p, (j < PAST_BLOCKS).astype(jnp.int32), valid)
            new_sel = jnp.maximum(new_sel, (j >= PAST_BLOCKS).astype(jnp.int32))
        kpos_new = PAST_LEN + (col - SLC_KEYS)
        valid = jnp.where(col >= SLC_KEYS, jnp.where(kpos_new <= pos, new_sel, 0), valid)
        kblk = kg[slot, h * N_SEL:(h + 1) * N_SEL].reshape(SLC_KEYS, LANES).astype(bf16)
        vblk = vg[slot, h * N_SEL:(h + 1) * N_SEL].reshape(SLC_KEYS, LANES).astype(bf16)
        ps = _softmax_rows(_dot_nt(q, jnp.concatenate([kblk, k_ext], axis=0)), valid > 0)
        o_slc = jnp.dot(ps.astype(bf16), jnp.concatenate([vblk, v_ext], axis=0), preferred_element_type=f32)
        pw = _softmax_rows(_dot_nt(q, kw_all), wmask)
        o_win = jnp.dot(pw.astype(bf16), vw_all, preferred_element_type=f32)
        sig = jax.nn.sigmoid(ng_ref[0, h])
        out.append(sig[0] * ocmp_ref[0, h] + sig[1] * o_slc + sig[2] * o_win)
    o_ref[0] = jnp.where(lane < HEAD_DIM, out[0], out[1]).astype(o_ref.dtype)

    last = lax.broadcasted_iota(jnp.int32, (WINDOW, 1), 0) == WINDOW - 1
    kwo_ref[0] = jnp.where(last, kwn_ref[0], pltpu.roll(kwin, WINDOW - 1, 0))
    vwo_ref[0] = jnp.where(last, vwn_ref[0], pltpu.roll(vwin, WINDOW - 1, 0))


def _sample_attend(page_table, idx, q8, k_pool, v_pool, ks_new, vs_new, k_win, v_win, kw_new, vw_new, o_cmp, ngb):
    n = q8.shape[0]
    f32 = jnp.float32
    n_phys = k_pool.shape[0]
    assert k_win.shape[1] == WINDOW
    view = lambda pool: pool.reshape(n_phys, BLOCKS_PER_PAGE, SLC_BLOCK, D_KV)
    row3 = lambda t: t.reshape(n, 1, D_KV)
    per_n3 = lambda i, pt, ix: (i, 0, 0)
    per_n4 = lambda i, pt, ix: (i, 0, 0, 0)
    new_spec = pl.BlockSpec((1, 1, D_KV), per_n3)
    win_spec = pl.BlockSpec((1, WINDOW, D_KV), per_n3)
    q_spec = pl.BlockSpec((1, N_KV, QROWS, LANES), per_n4)
    grid_spec = pltpu.PrefetchScalarGridSpec(
        num_scalar_prefetch=2,
        grid=(n,),
        in_specs=[q_spec, pl.BlockSpec(memory_space=pl.ANY), pl.BlockSpec(memory_space=pl.ANY),
                  new_spec, new_spec, win_spec, win_spec, new_spec, new_spec, q_spec,
                  pl.BlockSpec((1, N_KV, 3, QROWS, LANES), lambda i, pt, ix: (i, 0, 0, 0, 0))],
        out_specs=(pl.BlockSpec((1, QROWS, LANES), per_n3), win_spec, win_spec),
        scratch_shapes=[pltpu.VMEM((2, N_KV * N_SEL, SLC_BLOCK, D_KV), f32),
                        pltpu.VMEM((2, N_KV * N_SEL, SLC_BLOCK, D_KV), f32),
                        pltpu.SemaphoreType.DMA((2, 2))],
    )
    return pl.pallas_call(
        _sample_attend_kernel,
        grid_spec=grid_spec,
        out_shape=(jax.ShapeDtypeStruct((n, QROWS, LANES), jnp.bfloat16),
                   jax.ShapeDtypeStruct((n, WINDOW, D_KV), f32), jax.ShapeDtypeStruct((n, WINDOW, D_KV), f32)),
        compiler_params=pltpu.CompilerParams(dimension_semantics=("arbitrary",), vmem_limit_bytes=VMEM_LIMIT),
        name="sample_attention",
    )(page_table, idx, q8, view(k_pool), view(v_pool), row3(ks_new), row3(vs_new),
      k_win.reshape(n, WINDOW, D_KV), v_win.reshape(n, WINDOW, D_KV), row3(kw_new), row3(vw_new), o_cmp, ngb)


def _nsa_sample_kernels(q_pad, ks, vs, kw, vw, ng, page_table, ck_cmp, cv_cmp, ck_slc, cv_slc, ck_win, cv_win,
                        cmp_wk, cmp_wv):
    n = q_pad.shape[0]
    q8 = jnp.pad(q_pad.reshape(n, N_KV, GROUP, LANES), ((0, 0), (0, 0), (0, QROWS - GROUP), (0, 0)))
    o_cmp, p_sum = _sample_cmp(page_table, q8, ck_cmp, cv_cmp, cmp_wk, cmp_wv)
    idx = _sample_select(p_sum)
    ngb = ng[:, :N_NG].reshape(n, N_KV, GROUP, 3).transpose(0, 1, 3, 2)
    ngb = jnp.pad(ngb, ((0, 0), (0, 0), (0, 0), (0, QROWS - GROUP)))
    ngb = jnp.broadcast_to(ngb[..., None], (n, N_KV, 3, QROWS, LANES))
    o8, kwo, vwo = _sample_attend(page_table, idx, q8, ck_slc, cv_slc, ks, vs, ck_win, cv_win, kw, vw, o_cmp, ngb)
    return o8[:, :GROUP, :].reshape(n, D_Q), kwo, vwo


SLC_COLS = N_SEL * PAGE_SIZE


def _sample_attn_kernel(pt_ref, idx_ref, q_ref, kpool, vpool, knew_ref, vnew_ref, kwin_ref, vwin_ref, kwn_ref,
                        vwn_ref, ocmp_ref, ng_ref, o_ref, kwo_ref, vwo_ref, kg, vg, sem):
    f32, bf16 = jnp.float32, jnp.bfloat16
    n = pl.program_id(0)
    nblk = N_KV * N_SEL

    def fetch(s, slot):
        def body(i, _):
            jp = jnp.minimum(idx_ref[s, i], PAST_BLOCKS - 1)
            pg = pt_ref[s, jp // BLOCKS_PER_PAGE]
            h = i // N_SEL
            pltpu.make_async_copy(kpool.at[pg, h], kg.at[slot, i], sem.at[0, slot]).start()
            pltpu.make_async_copy(vpool.at[pg, h], vg.at[slot, i], sem.at[1, slot]).start()
            return 0
        lax.fori_loop(0, nblk, body, 0)

    @pl.when(n == 0)
    def _():
        fetch(0, 0)

    @pl.when(n + 1 < pl.num_programs(0))
    def _():
        fetch(n + 1, (n + 1) % 2)

    slot = n % 2
    pltpu.make_async_copy(kg.at[slot], kg.at[slot], sem.at[0, slot]).wait()
    pltpu.make_async_copy(vg.at[slot], vg.at[slot], sem.at[1, slot]).wait()

    pos = PAST_LEN
    col = lax.broadcasted_iota(jnp.int32, (1, SLC_COLS + LANES), 1)
    wcol = lax.broadcasted_iota(jnp.int32, (1, WINDOW + LANES), 1)
    wpos = jnp.where(wcol < WINDOW, pos - WINDOW + wcol, pos + (wcol - WINDOW))
    wmask = (pos - wpos >= 0) & (pos - wpos < WINDOW)
    wlane = lax.broadcasted_iota(jnp.int32, (1, WINDOW), 1)
    for h in range(N_KV):
        q = q_ref[0, h]
        new_sel = jnp.int32(0)
        lo = jnp.full((1, SLC_COLS + LANES), SLC_COLS + LANES, jnp.int32)
        for p in range(N_SEL):
            j = idx_ref[n, h * N_SEL + p]
            start = p * PAGE_SIZE + (j % BLOCKS_PER_PAGE) * SLC_BLOCK
            start = jnp.where(j < PAST_BLOCKS, start, SLC_COLS + LANES)
            lo = jnp.where((col >> 7) == p, start, lo)
            new_sel = jnp.maximum(new_sel, (j >= PAST_BLOCKS).astype(jnp.int32))
        valid = jnp.where((col >= lo) & (col < lo + SLC_BLOCK), 1, 0)
        kpos_new = PAST_LEN + (col - SLC_COLS)
        valid = jnp.where(col >= SLC_COLS, jnp.where(kpos_new <= pos, new_sel, 0), valid) > 0
        kt = jnp.concatenate([kg[slot, h * N_SEL + p] for p in range(N_SEL)] + [knew_ref[0, h]], axis=-1)
        vt = jnp.concatenate([vg[slot, h * N_SEL + p] for p in range(N_SEL)] + [vnew_ref[0, h]], axis=-1)
        ps = _softmax_rows(jnp.dot(q, kt.astype(bf16), preferred_element_type=f32), valid)
        o_slc = _dot_nt(ps.astype(bf16), vt.astype(bf16))
        kwin, vwin = kwin_ref[0, h], vwin_ref[0, h]
        kw_all = jnp.concatenate([kwin, kwn_ref[0, h]], axis=-1).astype(bf16)
        vw_all = jnp.concatenate([vwin, vwn_ref[0, h]], axis=-1).astype(bf16)
        pw = _softmax_rows(jnp.dot(q, kw_all, preferred_element_type=f32), wmask)
        o_win = _dot_nt(pw.astype(bf16), vw_all)
        sig = jax.nn.sigmoid(ng_ref[0, h])
        o_cmp = ocmp_ref[0, h][:, h * HEAD_DIM:(h + 1) * HEAD_DIM]
        o_ref[0, h] = sig[0] * o_cmp + sig[1] * o_slc + sig[2] * o_win
        kwo_ref[0, h] = jnp.where(wlane == WINDOW - 1, kwn_ref[0, h][:, 0:1], pltpu.roll(kwin, WINDOW - 1, 1))
        vwo_ref[0, h] = jnp.where(wlane == WINDOW - 1, vwn_ref[0, h][:, 0:1], pltpu.roll(vwin, WINDOW - 1, 1))


def _sample_attn(page_table, idx, q8u, k_pool_t, v_pool_t, knew_t, vnew_t, k_win_t, v_win_t, kwn_t, vwn_t, o_cmp, ngb):
    n = q8u.shape[0]
    f32 = jnp.float32
    assert k_win_t.shape[-1] == WINDOW
    per_n4 = lambda i, pt, ix: (i, 0, 0, 0)
    ext_spec = pl.BlockSpec((1, N_KV, HEAD_DIM, LANES), per_n4)
    win_spec = pl.BlockSpec((1, N_KV, HEAD_DIM, WINDOW), per_n4)
    q_spec = pl.BlockSpec((1, N_KV, QROWS, HEAD_DIM), per_n4)
    grid_spec = pltpu.PrefetchScalarGridSpec(
        num_scalar_prefetch=2,
        grid=(n,),
        in_specs=[q_spec, pl.BlockSpec(memory_space=pl.ANY), pl.BlockSpec(memory_space=pl.ANY),
                  ext_spec, ext_spec, win_spec, win_spec, ext_spec, ext_spec,
                  pl.BlockSpec((1, N_KV, QROWS, LANES), per_n4),
                  pl.BlockSpec((1, N_KV, 3, QROWS, HEAD_DIM), lambda i, pt, ix: (i, 0, 0, 0, 0))],
        out_specs=(q_spec, win_spec, win_spec),
        scratch_shapes=[pltpu.VMEM((2, N_KV * N_SEL, HEAD_DIM, PAGE_SIZE), f32),
                        pltpu.VMEM((2, N_KV * N_SEL, HEAD_DIM, PAGE_SIZE), f32),
                        pltpu.SemaphoreType.DMA((2, 2))],
    )
    return pl.pallas_call(
        _sample_attn_kernel,
        grid_spec=grid_spec,
        out_shape=(jax.ShapeDtypeStruct((n, N_KV, QROWS, HEAD_DIM), f32),
                   jax.ShapeDtypeStruct((n, N_KV, HEAD_DIM, WINDOW), f32),
                   jax.ShapeDtypeStruct((n, N_KV, HEAD_DIM, WINDOW), f32)),
        compiler_params=pltpu.CompilerParams(dimension_semantics=("arbitrary",), vmem_limit_bytes=VMEM_LIMIT),
        name="sample_attention",
    )(page_table, idx, q8u, k_pool_t, v_pool_t, knew_t, vnew_t, k_win_t, v_win_t, kwn_t, vwn_t, o_cmp, ngb)


def _nsa_sample_paged(q_pad, ks, vs, kw, vw, ng, page_table, ck_cmp, cv_cmp, ck_slc, cv_slc, ck_win, cv_win,
                      cmp_wk, cmp_wv):
    n = q_pad.shape[0]
    n_phys = ck_cmp.shape[0]
    pool_t = lambda p: p.transpose(0, 2, 3, 1)
    q4 = q_pad.reshape(n, N_KV, GROUP, N_KV, HEAD_DIM)
    q8 = jnp.pad(q_pad.reshape(n, N_KV, GROUP, LANES), ((0, 0), (0, 0), (0, QROWS - GROUP), (0, 0)))
    q8u = jnp.pad(jnp.stack([q4[:, h, :, h, :] for h in range(N_KV)], axis=1),
                  ((0, 0), (0, 0), (0, QROWS - GROUP), (0, 0)))
    o_cmp, p_sum = _sample_cmp(page_table, q8, pool_t(ck_cmp).reshape(n_phys, D_KV, PAGE_SIZE),
                               pool_t(cv_cmp).reshape(n_phys, D_KV, PAGE_SIZE), cmp_wk, cmp_wv)
    idx = _sample_select(p_sum)
    ngb = ng[:, :N_NG].reshape(n, N_KV, GROUP, 3).transpose(0, 1, 3, 2)
    ngb = jnp.pad(ngb, ((0, 0), (0, 0), (0, 0), (0, QROWS - GROUP)))
    ngb = jnp.broadcast_to(ngb[..., None], (n, N_KV, 3, QROWS, HEAD_DIM))
    col0 = lambda t: jnp.pad(t.reshape(n, N_KV, HEAD_DIM, 1), ((0, 0), (0, 0), (0, 0), (0, LANES - 1)))
    o8, kwo, vwo = _sample_attn(page_table, idx, q8u, pool_t(ck_slc), pool_t(cv_slc), col0(ks), col0(vs),
                                pool_t(ck_win), pool_t(cv_win), col0(kw), col0(vw), o_cmp, ngb)
    o = o8[:, :, :GROUP, :].transpose(0, 2, 1, 3).reshape(n, D_Q).astype(jnp.bfloat16)
    return o, kwo.transpose(0, 3, 1, 2), vwo.transpose(0, 3, 1, 2)


LRU_PAIRS = D_RNN // LANES


def _lru_gates(xc, wa_ref, ba, wx_ref, bx, lam):
    f32 = jnp.float32
    xb = xc.astype(jnp.bfloat16)
    rg = jnp.concatenate([jnp.dot(xb[:, j * LANES:(j + 1) * LANES], wa_ref[j], preferred_element_type=f32)
                          for j in range(LRU_PAIRS)], axis=-1)
    ig = jnp.concatenate([jnp.dot(xb[:, j * LANES:(j + 1) * LANES], wx_ref[j], preferred_element_type=f32)
                          for j in range(LRU_PAIRS)], axis=-1)
    r = jax.nn.sigmoid(rg + ba)
    i = jax.nn.sigmoid(ig + bx)
    log_a = -LRU_C * r * jax.nn.softplus(-lam)
    a = jnp.exp(log_a)
    u = jnp.sqrt(-jnp.tanh(log_a) * (a * a + 1.0)) * (i * xc)
    return a, u


def _lru_prompt_kernel(arag_ref, cbuf_ref, h0_ref, cw_ref, cb_ref, wa_ref, ba_ref, wx_ref, bx_ref, lam_ref,
                       y_ref, hl_ref, prev_scr, h_scr, hbuf):
    f32 = jnp.float32
    t = arag_ref.shape[0]

    @pl.when(pl.program_id(1) == 0)
    def _():
        prev_scr[...] = cbuf_ref[0]
        h_scr[...] = h0_ref[0]

    ar = arag_ref[:, :D_RNN]
    row8 = lax.broadcasted_iota(jnp.int32, (SUBLANES, 1), 0)
    prev = prev_scr[...]
    xc = cb_ref[...] + cw_ref[CONV_W - 1:CONV_W, :] * ar
    for k in range(1, CONV_W):
        rolled = pltpu.roll(ar, k, 0)
        head = jnp.where(row8 < k, pltpu.roll(prev, k, 0), rolled[0:SUBLANES])
        shifted = jnp.concatenate([head, rolled[SUBLANES:]], axis=0)
        xc = xc + cw_ref[CONV_W - 1 - k:CONV_W - k, :] * shifted
    prev_scr[...] = ar[t - SUBLANES:]

    a, u = _lru_gates(xc, wa_ref, ba_ref[...], wx_ref, bx_ref[...], lam_ref[...])

    rowm = lax.broadcasted_iota(jnp.int32, (t, 1), 0) & (SUBLANES - 1)
    for s in (1, 2, 4):
        ok = rowm >= s
        u = jnp.where(ok, a * pltpu.roll(u, s, 0) + u, u)
        a = jnp.where(ok, a * pltpu.roll(a, s, 0), a)
    h = h_scr[...]
    for g in range(t // SUBLANES):
        rs = slice(g * SUBLANES, (g + 1) * SUBLANES)
        hg = a[rs] * h + u[rs]
        hbuf[rs, :] = hg
        h = jnp.broadcast_to(hg[SUBLANES - 1:SUBLANES, :], (SUBLANES, D_RNN))
    h_scr[...] = h
    hl_ref[0] = h
    y_ref[...] = (hbuf[...] * jax.nn.gelu(arag_ref[:, D_RNN:])).astype(y_ref.dtype)


def _lru_pair_weights(w):
    w4 = w.reshape(LRU_PAIRS, 2, LRU_BW, LRU_BW)
    eye = jnp.eye(2, dtype=w.dtype)
    return jnp.einsum('jaio,ab->jaibo', w4, eye).reshape(LRU_PAIRS, LANES, LANES).astype(jnp.bfloat16)


def _lru_weights(conv_w, conv_b, wa, ba, wx, bx, lam):
    row = lambda v: v.reshape(1, D_RNN).astype(jnp.float32)
    return (conv_w.astype(jnp.float32), row(conv_b), _lru_pair_weights(wa), row(ba), _lru_pair_weights(wx), row(bx),
            row(lam))


def _lru_prompt(arag, conv_buf, h0, lw, b, s, t):
    assert s % t == 0 and t % SUBLANES == 0
    nt = s // t
    cbuf8 = jnp.pad(conv_buf.astype(jnp.float32), ((0, 0), (SUBLANES - (CONV_W - 1), 0), (0, 0)))
    h08 = jnp.broadcast_to(h0.astype(jnp.float32)[:, None, :], (b, SUBLANES, D_RNN))
    const2 = lambda bi, i: (0, 0)
    const3 = lambda bi, i: (0, 0, 0)
    per_b = lambda bi, i: (bi, 0, 0)
    y, hl = pl.pallas_call(
        _lru_prompt_kernel,
        grid=(b, nt),
        in_specs=[
            pl.BlockSpec((t, 2 * D_RNN), lambda bi, i: (bi * nt + i, 0)),
            pl.BlockSpec((1, SUBLANES, D_RNN), per_b),
            pl.BlockSpec((1, SUBLANES, D_RNN), per_b),
            pl.BlockSpec((CONV_W, D_RNN), const2),
            pl.BlockSpec((1, D_RNN), const2),
            pl.BlockSpec((LRU_PAIRS, LANES, LANES), const3),
            pl.BlockSpec((1, D_RNN), const2),
            pl.BlockSpec((LRU_PAIRS, LANES, LANES), const3),
            pl.BlockSpec((1, D_RNN), const2),
            pl.BlockSpec((1, D_RNN), const2),
        ],
        out_specs=(pl.BlockSpec((t, D_RNN), lambda bi, i: (bi * nt + i, 0)),
                   pl.BlockSpec((1, SUBLANES, D_RNN), per_b)),
        out_shape=(jax.ShapeDtypeStruct((b * s, D_RNN), jnp.bfloat16),
                   jax.ShapeDtypeStruct((b, SUBLANES, D_RNN), jnp.float32)),
        scratch_shapes=[pltpu.VMEM((SUBLANES, D_RNN), jnp.float32), pltpu.VMEM((SUBLANES, D_RNN), jnp.float32),
                        pltpu.VMEM((t, D_RNN), jnp.float32)],
        compiler_params=pltpu.CompilerParams(dimension_semantics=("arbitrary", "arbitrary"),
                                             vmem_limit_bytes=VMEM_LIMIT),
        name="lru_prompt",
    )(arag, cbuf8, h08, *lw)
    return y, hl[:, 0, :]


def _lru_step_kernel(arag_ref, c0_ref, c1_ref, c2_ref, h0_ref, cw_ref, cb_ref, wa_ref, ba_ref, wx_ref, bx_ref,
                     lam_ref, y_ref, h_ref):
    ar = arag_ref[:, :D_RNN]
    xc = (cb_ref[...] + cw_ref[0:1, :] * c0_ref[...] + cw_ref[1:2, :] * c1_ref[...] + cw_ref[2:3, :] * c2_ref[...]
          + cw_ref[3:4, :] * ar)
    a, u = _lru_gates(xc, wa_ref, ba_ref[...], wx_ref, bx_ref[...], lam_ref[...])
    h = a * h0_ref[...] + u
    h_ref[...] = h
    y_ref[...] = (h * jax.nn.gelu(arag_ref[:, D_RNN:])).astype(y_ref.dtype)


def _lru_step(arag, conv_buf, h0, lw):
    n = arag.shape[0]
    f32 = jnp.float32
    cb = conv_buf.astype(f32)
    return pl.pallas_call(
        _lru_step_kernel,
        out_shape=(jax.ShapeDtypeStruct((n, D_RNN), jnp.bfloat16), jax.ShapeDtypeStruct((n, D_RNN), f32)),
        compiler_params=pltpu.CompilerParams(vmem_limit_bytes=VMEM_LIMIT),
        name="lru_step",
    )(arag, cb[:, 0], cb[:, 1], cb[:, 2], h0.astype(f32), *lw)


def _merge_router_kernel(x_ref, ylru_ref, onsa_ref, gab_ref, wa_ref, wb_ref, wo_ref, gffn_ref, rw_ref, rb_ref,
                         tri_ref, xn_ref, hn_ref, te_ref, tr_ref, tg_ref, cnt_ref, run_scr):
    f32, bf16 = jnp.float32, jnp.bfloat16
    tm = x_ref.shape[0]

    @pl.when(pl.program_id(0) == 0)
    def _():
        run_scr[...] = jnp.zeros_like(run_scr)

    ya = jnp.dot(ylru_ref[...], wa_ref[...], preferred_element_type=f32)
    yb = jnp.dot(onsa_ref[...], wb_ref[...], preferred_element_type=f32)
    mix = jax.nn.sigmoid(gab_ref[:, :D_MODEL]) * ya + jax.nn.sigmoid(gab_ref[:, D_MODEL:]) * yb
    xn = x_ref[...] + jnp.dot(mix.astype(bf16), wo_ref[...], preferred_element_type=f32)
    xn_ref[...] = xn
    hn = xn * lax.rsqrt(jnp.mean(xn * xn, axis=-1, keepdims=True) + EPS) * gffn_ref[...]
    hn_ref[...] = hn

    logits = jnp.dot(hn, rw_ref[...], preferred_element_type=f32, precision=lax.Precision.HIGHEST) + rb_ref[...]
    lane = lax.broadcasted_iota(jnp.int32, (tm, LANES), 1)
    ninf = -jnp.inf
    l = jnp.where(lane < N_EXPERTS, logits, ninf)
    onehots, m0 = [], None
    for k in range(TOP_K):
        m = jnp.max(l, axis=-1, keepdims=True)
        idx = jnp.min(jnp.where(l == m, lane, LANES), axis=-1, keepdims=True)
        oh = lane == idx
        onehots.append((oh, idx))
        m0 = m if k == 0 else m0
        l = jnp.where(oh, ninf, l)
    sel = onehots[0][0] | onehots[1][0] | onehots[2][0] | onehots[3][0]
    e = jnp.where(sel, jnp.exp(logits - m0), 0.0)
    gates = e / jnp.sum(e, axis=-1, keepdims=True)

    self = jnp.where(sel, 1.0, 0.0)
    incl = jnp.dot(tri_ref[...], self.astype(bf16), preferred_element_type=f32)
    run = run_scr[0:1, :]
    rank = run + incl - self
    total = run + incl[tm - 1:tm, :]
    run_scr[...] = jnp.broadcast_to(total, run_scr.shape)
    cnt_ref[...] = jnp.broadcast_to(total, cnt_ref.shape)

    te = jnp.zeros((tm, LANES), jnp.int32)
    tr = jnp.zeros((tm, LANES), f32)
    tg = jnp.zeros((tm, LANES), f32)
    for k, (oh, idx) in enumerate(onehots):
        te = jnp.where(lane == k, idx, te)
        tr = jnp.where(lane == k, jnp.sum(jnp.where(oh, rank, 0.0), axis=-1, keepdims=True), tr)
        tg = jnp.where(lane == k, jnp.sum(jnp.where(oh, gates, 0.0), axis=-1, keepdims=True), tg)
    te_ref[...] = te
    tr_ref[...] = tr.astype(jnp.int32)
    tg_ref[...] = tg


def _merge_router(x2d, ylru, onsa, gab, wa, wb, wo, g_ffn, rw_pad, rb_pad, tm):
    m = x2d.shape[0]
    assert m % tm == 0
    f32 = jnp.float32
    row = lambda i: (i, 0)
    const = lambda i: (0, 0)
    tri = jnp.asarray(np.tril(np.ones((tm, tm), np.float32)), dtype=jnp.bfloat16)
    lane_out = lambda dt: (jax.ShapeDtypeStruct((m, LANES), dt), pl.BlockSpec((tm, LANES), row))
    outs = [
        (jax.ShapeDtypeStruct((m, D_MODEL), f32), pl.BlockSpec((tm, D_MODEL), row)),
        (jax.ShapeDtypeStruct((m, D_MODEL), f32), pl.BlockSpec((tm, D_MODEL), row)),
        lane_out(jnp.int32), lane_out(jnp.int32), lane_out(f32),
        (jax.ShapeDtypeStruct((8, LANES), f32), pl.BlockSpec((8, LANES), const)),
    ]
    return pl.pallas_call(
        _merge_router_kernel,
        grid=(m // tm,),
        in_specs=[
            pl.BlockSpec((tm, D_MODEL), row),
            pl.BlockSpec((tm, D_RNN), row),
            pl.BlockSpec((tm, D_Q), row),
            pl.BlockSpec((tm, 2 * D_MODEL), row),
            pl.BlockSpec((D_RNN, D_MODEL), const),
            pl.BlockSpec((D_Q, D_MODEL), const),
            pl.BlockSpec((D_MODEL, D_MODEL), const),
            pl.BlockSpec((1, D_MODEL), const),
            pl.BlockSpec((D_MODEL, LANES), const),
            pl.BlockSpec((1, LANES), const),
            pl.BlockSpec((tm, tm), const),
        ],
        out_specs=tuple(o[1] for o in outs),
        out_shape=tuple(o[0] for o in outs),
        scratch_shapes=[pltpu.VMEM((8, LANES), f32)],
        compiler_params=pltpu.CompilerParams(dimension_semantics=("arbitrary",), vmem_limit_bytes=VMEM_LIMIT),
        name="merge_router",
    )(x2d, ylru, onsa, gab, wa, wb, wo, g_ffn.reshape(1, D_MODEL), rw_pad, rb_pad, tri)


MOE_BLK = 512


def _moe_dispatch_kernel(dest_ref, hn_ref, xs_in_ref, xs_ref, sem):
    del xs_in_ref
    tm = hn_ref.shape[0]

    def issue(r, _):
        for k in range(TOP_K):
            d = dest_ref[0, 0, r * TOP_K + k]
            pltpu.make_async_copy(hn_ref.at[pl.ds(r, 1)], xs_ref.at[pl.ds(d, 1)], sem).start()
        return 0

    lax.fori_loop(0, tm, issue, 0)
    pltpu.make_async_copy(xs_ref.at[pl.ds(0, tm * TOP_K)], xs_ref.at[pl.ds(0, tm * TOP_K)], sem).wait()


def _moe_dispatch(hn, dest, n_slots, tm):
    m = hn.shape[0]
    nt = m // tm
    xs0 = jnp.zeros((n_slots, D_MODEL), hn.dtype)
    return pl.pallas_call(
        _moe_dispatch_kernel,
        grid=(nt,),
        in_specs=[
            pl.BlockSpec((1, 1, tm * TOP_K), lambda i: (i, 0, 0), memory_space=pltpu.SMEM),
            pl.BlockSpec((tm, D_MODEL), lambda i: (i, 0)),
            pl.BlockSpec(memory_space=pl.ANY),
        ],
        out_specs=pl.BlockSpec(memory_space=pl.ANY),
        out_shape=jax.ShapeDtypeStruct((n_slots, D_MODEL), hn.dtype),
        scratch_shapes=[pltpu.SemaphoreType.DMA(())],
        input_output_aliases={2: 0},
        compiler_params=pltpu.CompilerParams(dimension_semantics=("arbitrary",), vmem_limit_bytes=VMEM_LIMIT),
        name="moe_dispatch",
    )(dest.reshape(nt, 1, tm * TOP_K), hn, xs0)


def _moe_experts_kernel(te_ref, nused_ref, xs_ref, wup_ref, bup_ref, wdn_ref, bdn_ref, ys_ref, wup_scr, wdn_scr):
    f32, bf16 = jnp.float32, jnp.bfloat16
    j = pl.program_id(0)
    changed = jnp.logical_or(j == 0, te_ref[j] != te_ref[jnp.maximum(j - 1, 0)])

    @pl.when(jnp.logical_and(j < nused_ref[0], changed))
    def _():
        wup_scr[...] = wup_ref[0].astype(bf16)
        wdn_scr[...] = wdn_ref[0].astype(bf16)

    @pl.when(j < nused_ref[0])
    def _():
        x = xs_ref[...].astype(bf16)
        gu = jnp.dot(x, wup_scr[...], preferred_element_type=f32) + bup_ref[0]
        a = jnp.minimum(gu[:, :D_FF], SWIGLU_LIMIT)
        u = jnp.clip(gu[:, D_FF:], -SWIGLU_LIMIT, SWIGLU_LIMIT)
        act = (u + 1.0) * a * jax.nn.sigmoid(SWIGLU_ALPHA * a)
        ys_ref[...] = jnp.dot(act.astype(bf16), wdn_scr[...], preferred_element_type=f32) + bdn_ref[0]


def _moe_experts(xs, tile_expert, n_used, w_up, b_up, w_down, b_down):
    n_slots = xs.shape[0]
    nt = n_slots // MOE_BLK
    tile = lambda j, te, nu: (jnp.minimum(j, nu[0] - 1), 0)
    wsel = lambda j, te, nu: (te[jnp.minimum(j, nu[0] - 1)], 0, 0)
    grid_spec = pltpu.PrefetchScalarGridSpec(
        num_scalar_prefetch=2,
        grid=(nt,),
        in_specs=[
            pl.BlockSpec((MOE_BLK, D_MODEL), tile),
            pl.BlockSpec((1, D_MODEL, 2 * D_FF), wsel),
            pl.BlockSpec((1, 1, 2 * D_FF), wsel),
            pl.BlockSpec((1, D_FF, D_MODEL), wsel),
            pl.BlockSpec((1, 1, D_MODEL), wsel),
        ],
        out_specs=pl.BlockSpec((MOE_BLK, D_MODEL), tile),
        scratch_shapes=[pltpu.VMEM((D_MODEL, 2 * D_FF), jnp.bfloat16), pltpu.VMEM((D_FF, D_MODEL), jnp.bfloat16)],
    )
    return pl.pallas_call(
        _moe_experts_kernel,
        grid_spec=grid_spec,
        out_shape=jax.ShapeDtypeStruct((n_slots, D_MODEL), jnp.float32),
        compiler_params=pltpu.CompilerParams(dimension_semantics=("arbitrary",), vmem_limit_bytes=VMEM_LIMIT),
        name="moe_experts",
    )(tile_expert, n_used, xs, w_up, b_up.reshape(N_EXPERTS, 1, 2 * D_FF), w_down,
      b_down.reshape(N_EXPERTS, 1, D_MODEL))


def _moe_combine_kernel(dest_ref, xn_ref, tg_ref, ys_ref, y_ref, buf, sem):
    tm = xn_ref.shape[0]

    def issue(r, _):
        for k in range(TOP_K):
            d = dest_ref[0, 0, r * TOP_K + k]
            pltpu.make_async_copy(ys_ref.at[pl.ds(d, 1)], buf.at[k, pl.ds(r, 1)], sem).start()
        return 0

    lax.fori_loop(0, tm, issue, 0)
    pltpu.make_async_copy(ys_ref.at[pl.ds(0, tm * TOP_K)], ys_ref.at[pl.ds(0, tm * TOP_K)], sem).wait()
    tg = tg_ref[...]
    acc = xn_ref[...]
    for k in range(TOP_K):
        acc = acc + tg[:, k:k + 1] * buf[k]
    y_ref[...] = acc


def _moe_combine(xn, tg, ys, dest, tm):
    m = xn.shape[0]
    nt = m // tm
    return pl.pallas_call(
        _moe_combine_kernel,
        grid=(nt,),
        in_specs=[
            pl.BlockSpec((1, 1, tm * TOP_K), lambda i: (i, 0, 0), memory_space=pltpu.SMEM),
            pl.BlockSpec((tm, D_MODEL), lambda i: (i, 0)),
            pl.BlockSpec((tm, LANES), lambda i: (i, 0)),
            pl.BlockSpec(memory_space=pl.ANY),
        ],
        out_specs=pl.BlockSpec((tm, D_MODEL), lambda i: (i, 0)),
        out_shape=jax.ShapeDtypeStruct((m, D_MODEL), jnp.float32),
        scratch_shapes=[pltpu.VMEM((TOP_K, tm, D_MODEL), jnp.float32), pltpu.SemaphoreType.DMA(())],
        compiler_params=pltpu.CompilerParams(dimension_semantics=("arbitrary",), vmem_limit_bytes=VMEM_LIMIT),
        name="moe_combine",
    )(dest.reshape(nt, 1, tm * TOP_K), xn, tg, ys)


def _moe_layer(xn, hn, te, tr, tg, cnt, w_up, b_up, w_down, b_down, tm):
    m = xn.shape[0]
    n_tiles = -(-m * TOP_K // MOE_BLK) + N_EXPERTS
    counts = cnt[0, :N_EXPERTS].astype(jnp.int32)
    padded = (counts + MOE_BLK - 1) // MOE_BLK * MOE_BLK
    pad_end = jnp.cumsum(padded)
    pad_start = pad_end - padded
    dest = (pad_start[te[:, :TOP_K]] + tr[:, :TOP_K]).astype(jnp.int32)
    tile_expert = jnp.minimum(
        jnp.searchsorted(pad_end, jnp.arange(n_tiles, dtype=jnp.int32) * MOE_BLK, side='right'),
        N_EXPERTS - 1).astype(jnp.int32)
    n_used = (pad_end[-1:] // MOE_BLK).astype(jnp.int32)
    xs = _moe_dispatch(hn, dest, n_tiles * MOE_BLK, tm)
    ys = _moe_experts(xs, tile_expert, n_used, w_up, b_up, w_down, b_down)
    return _moe_combine(xn, tg, ys, dest, tm)


def _masked_softmax(s, mask):
    s = jnp.where(mask, s.astype(jnp.float32), NEG)
    m = jnp.max(s, axis=-1, keepdims=True)
    p = jnp.where(mask, jnp.exp(s - m), 0.0)
    return p / jnp.maximum(jnp.sum(p, axis=-1, keepdims=True), 1e-30)


def _merge_branches(y_lru, o_nsa, ga, gb, w_proj_a, w_proj_b, w_out):
    return (jax.nn.sigmoid(ga) * (y_lru @ w_proj_a) + jax.nn.sigmoid(gb) * (o_nsa @ w_proj_b)) @ w_out


def _lin_combine(c1, c2):
    a1, b1 = c1
    a2, b2 = c2
    return a1 * a2, a2 * b1 + b2


def _rglru(xr, gate_in, conv_buf, h0, conv_w, conv_b, wa, ba, wx, bx, lam):
    n, s, _ = xr.shape
    xp = jnp.concatenate([conv_buf.astype(xr.dtype), xr], axis=1)
    xc = conv_b + sum(conv_w[k] * xp[:, k:k + s] for k in range(CONV_W))
    xb = xc.reshape(n, s, LRU_BLOCKS, LRU_BW)
    r = jax.nn.sigmoid((jnp.einsum('nsbi,bij->nsbj', xb, wa) + ba).astype(jnp.float32)).reshape(n, s, D_RNN)
    i = jax.nn.sigmoid((jnp.einsum('nsbi,bij->nsbj', xb, wx) + bx).astype(jnp.float32)).reshape(n, s, D_RNN)
    log_a = -LRU_C * r * jax.nn.softplus(-lam.astype(jnp.float32))
    a = jnp.exp(log_a)
    u = jnp.sqrt(-jnp.expm1(2.0 * log_a)) * (i * xc.astype(jnp.float32))
    u = u.at[:, 0].add(a[:, 0] * h0.astype(jnp.float32))
    _, h = lax.associative_scan(_lin_combine, (a, u), axis=1)
    y = (h * jax.nn.gelu(gate_in.astype(jnp.float32))).astype(xr.dtype)
    return y, xp[:, s:], h[:, -1]


def _compress(rows, pos_emb, w1, w2):
    n, l = rows.shape[:2]
    r_ = CMP_LEN // CMP_STRIDE
    nch = l // CMP_STRIDE
    ch = rows[:, :nch * CMP_STRIDE].reshape(n, nch, CMP_STRIDE, N_KV, HEAD_DIM)
    w1b = w1.reshape(r_, CMP_STRIDE, HEAD_DIM, CMP_HID)
    part = jnp.einsum('nclhd,pldf->pnchf', ch, w1b)
    nc = nch - r_ + 1
    hid = jnp.einsum('ld,ldf->f', pos_emb, w1) + sum(part[p, :, p:p + nc] for p in range(r_))
    return jnp.einsum('nchf,fd->nchd', jax.nn.gelu(hid), w2)


def _block_scores(p_cmp, nsb):
    nc = p_cmp.shape[-1]
    ratio = SLC_BLOCK // CMP_STRIDE
    ovl = CMP_LEN // CMP_STRIDE - 1
    widths = [(0, 0)] * (p_cmp.ndim - 1) + [(ovl, ratio * (nsb + 1) - nc - ovl)]
    pp = jnp.pad(p_cmp, widths).reshape(*p_cmp.shape[:-1], nsb + 1, ratio)
    return pp[..., :nsb, :].sum(-1) + pp[..., 1:, :ovl].sum(-1)


def _select_blocks(scores, pos, nsb):
    j = jnp.arange(nsb, dtype=jnp.int32)
    cur = (pos // SLC_BLOCK)[:, None, None]
    forced = (j == 0) | (j == cur) | (j == cur - 1)
    s = jnp.where(forced, BIG, jnp.where(j <= cur, scores, NEG))
    top, idx = lax.top_k(s, min(N_SEL, nsb))
    return idx, top > NEG / 2


def _nsa_attend(q, pos, cmp_k, cmp_v, gather_slc, nsb, win_k, win_v, win_pos, gates):
    n, qn = q.shape[:2]
    nc = cmp_k.shape[1]
    s = jnp.einsum('nqhgd,nchd->nqhgc', q, cmp_k)
    cmp_end = jnp.arange(nc, dtype=jnp.int32) * CMP_STRIDE + CMP_LEN - 1
    p_c = _masked_softmax(s, cmp_end <= pos[:, None, None, None])
    o_cmp = jnp.einsum('nqhgc,nchd->nqhgd', p_c.astype(cmp_v.dtype), cmp_v)
    idx, valid = _select_blocks(_block_scores(p_c.sum(3), nsb), pos, nsb)
    k_g, v_g = gather_slc(idx)
    kk = idx.shape[-1] * SLC_BLOCK
    kpos = idx[..., None] * SLC_BLOCK + jnp.arange(SLC_BLOCK, dtype=jnp.int32)
    smask = (valid[..., None] & (kpos <= pos[:, None, None, None])).reshape(n, qn, N_KV, 1, kk)
    s = jnp.einsum('nqhgd,nqhkd->nqhgk', q, k_g.reshape(n, qn, N_KV, kk, HEAD_DIM))
    p_s = _masked_softmax(s, smask)
    o_slc = jnp.einsum('nqhgk,nqhkd->nqhgd', p_s.astype(v_g.dtype), v_g.reshape(n, qn, N_KV, kk, HEAD_DIM))
    s = jnp.einsum('nqhgd,nkhd->nqhgk', q, win_k)
    dpos = pos[:, None] - win_pos[None, :]
    wmask = (dpos >= 0) & (dpos < WINDOW) & (win_pos >= 0)[None, :]
    p_w = _masked_softmax(s, wmask[:, None, None, :])
    o_win = jnp.einsum('nqhgk,nkhd->nqhgd', p_w.astype(win_v.dtype), win_v)
    g = jax.nn.sigmoid(gates)
    return g[..., 0:1] * o_cmp + g[..., 1:2] * o_slc + g[..., 2:3] * o_win


def _nsa_prompt(q, kc, vc, ks, vs, kw, vw, ng, pos_k, w1_k, w2_k, pos_v, w1_v, w2_v):
    n, s = q.shape[:2]
    cmp_k = _compress(kc, pos_k, w1_k, w2_k)
    cmp_v = _compress(vc, pos_v, w1_v, w2_v)
    nsb = s // SLC_BLOCK
    kb = ks.reshape(n, nsb, SLC_BLOCK, N_KV, HEAD_DIM)
    vb = vs.reshape(n, nsb, SLC_BLOCK, N_KV, HEAD_DIM)
    bi = jnp.arange(n)[:, None, None, None]
    hi = jnp.arange(N_KV)[None, None, :, None]

    def gather(idx):
        return kb[bi, idx, :, hi], vb[bi, idx, :, hi]

    pad = ((0, 0), (WINDOW, 0), (0, 0), (0, 0))
    kw_pad, vw_pad = jnp.pad(kw, pad), jnp.pad(vw, pad)

    def block(i):
        start = i * Q_BLOCK
        sl = lambda t, size: lax.dynamic_slice_in_dim(t, start, size, axis=1)
        pos = start + jnp.arange(Q_BLOCK, dtype=jnp.int32)
        wpos = start - WINDOW + jnp.arange(WINDOW + Q_BLOCK, dtype=jnp.int32)
        return _nsa_attend(sl(q, Q_BLOCK), pos, cmp_k, cmp_v, gather, nsb,
                           sl(kw_pad, WINDOW + Q_BLOCK), sl(vw_pad, WINDOW + Q_BLOCK), wpos, sl(ng, Q_BLOCK))

    o = lax.map(block, jnp.arange(s // Q_BLOCK, dtype=jnp.int32))
    return jnp.moveaxis(o, 0, 1).reshape(n, s, D_Q)


def _nsa_sample(q, kc, vc, ks, vs, kw, vw, ng, page_table, ck_cmp, cv_cmp, ck_slc, cv_slc, ck_win, cv_win,
                pos_k, w1_k, w2_k, pos_v, w1_v, w2_v):
    n, ds = q.shape[:2]
    pos = PAST_LEN + jnp.arange(ds, dtype=jnp.int32)

    def past(pool):
        return pool[page_table].reshape(n, PAST_LEN, N_KV, HEAD_DIM)

    cmp_k = _compress(jnp.concatenate([past(ck_cmp), kc], axis=1), pos_k, w1_k, w2_k)
    cmp_v = _compress(jnp.concatenate([past(cv_cmp), vc], axis=1), pos_v, w1_v, w2_v)
    past_blocks = PAST_LEN // SLC_BLOCK
    new_blocks = -(-ds // SLC_BLOCK)
    nsb = past_blocks + new_blocks
    bpp = PAGE_SIZE // SLC_BLOCK
    view = lambda pool: pool.reshape(pool.shape[0], bpp, SLC_BLOCK, N_KV, HEAD_DIM)
    pk, pv = view(ck_slc), view(cv_slc)
    padn = ((0, 0), (0, new_blocks * SLC_BLOCK - ds), (0, 0), (0, 0))
    nk = jnp.pad(ks, padn).reshape(n, new_blocks, SLC_BLOCK, N_KV, HEAD_DIM)
    nv = jnp.pad(vs, padn).reshape(n, new_blocks, SLC_BLOCK, N_KV, HEAD_DIM)
    bi = jnp.arange(n)[:, None, None, None]
    hi = jnp.arange(N_KV)[None, None, :, None]

    def gather(idx):
        jp = jnp.minimum(idx, past_blocks - 1)
        phys = page_table[bi, jp // bpp]
        sub = jp % bpp
        jn = jnp.clip(idx - past_blocks, 0, new_blocks - 1)
        is_new = (idx >= past_blocks)[..., None, None]
        return (jnp.where(is_new, nk[bi, jn, :, hi], pk[phys, sub, :, hi]),
                jnp.where(is_new, nv[bi, jn, :, hi], pv[phys, sub, :, hi]))

    wb = ck_win.shape[1]
    wk = jnp.concatenate([ck_win, kw], axis=1)
    wv = jnp.concatenate([cv_win, vw], axis=1)
    wpos = PAST_LEN - wb + jnp.arange(wb + ds, dtype=jnp.int32)
    o = _nsa_attend(q, pos, cmp_k, cmp_v, gather, nsb, wk, wv, wpos, ng)
    return o.reshape(n, ds, D_Q), wk[:, -wb:], wv[:, -wb:]


def _rms_norm(x, g):
    xf = x.astype(jnp.float32)
    y = xf * lax.rsqrt(jnp.mean(xf * xf, axis=-1, keepdims=True) + EPS)
    return (y * g.astype(jnp.float32)).astype(x.dtype)


def _moe(h, router_w, router_b, w_up, b_up, w_down, b_down):
    shp = h.shape
    t = h.reshape(-1, D_MODEL)
    n_tok = t.shape[0]
    logits = (t @ router_w).astype(jnp.float32) + router_b.astype(jnp.float32)
    top_v, top_e = lax.top_k(logits, TOP_K)
    gate = jax.nn.softmax(top_v, axis=-1)
    n_asg = n_tok * TOP_K
    blk = int(min(512, max(8, n_asg // N_EXPERTS)))
    n_blocks = -(-n_asg // blk) + N_EXPERTS
    e_flat = top_e.reshape(-1)
    tok_flat = jnp.repeat(jnp.arange(n_tok, dtype=jnp.int32), TOP_K)
    g_flat = gate.reshape(-1)
    order = jnp.argsort(e_flat)
    e_sorted = e_flat[order]
    counts = jnp.bincount(e_flat, length=N_EXPERTS)
    padded = (counts + blk - 1) // blk * blk
    pad_end = jnp.cumsum(padded)
    pad_start = pad_end - padded
    starts = jnp.cumsum(counts) - counts
    dest = pad_start[e_sorted] + jnp.arange(n_asg, dtype=jnp.int32) - starts[e_sorted]
    slot_tok = jnp.full((n_blocks * blk,), n_tok, jnp.int32).at[dest].set(tok_flat[order])
    slot_gate = jnp.zeros((n_blocks * blk,), jnp.float32).at[dest].set(g_flat[order])
    blk_expert = jnp.minimum(jnp.searchsorted(pad_end, jnp.arange(n_blocks, dtype=jnp.int32) * blk, side='right'),
                             N_EXPERTS - 1)
    t_pad = jnp.concatenate([t, jnp.zeros((1, D_MODEL), t.dtype)], axis=0)

    def run_block(args):
        tok, g, e = args
        gu = t_pad[tok] @ w_up[e] + b_up[e]
        a = jnp.minimum(gu[:, :D_FF], SWIGLU_LIMIT)
        u = jnp.clip(gu[:, D_FF:], -SWIGLU_LIMIT, SWIGLU_LIMIT)
        y = ((u + 1.0) * a * jax.nn.sigmoid(SWIGLU_ALPHA * a)) @ w_down[e] + b_down[e]
        return (y * g[:, None]).astype(t.dtype)

    ys = lax.map(run_block, (slot_tok.reshape(n_blocks, blk), slot_gate.reshape(n_blocks, blk), blk_expert))
    out = jnp.zeros((n_tok + 1, D_MODEL), t.dtype).at[slot_tok].add(ys.reshape(-1, D_MODEL))
    return out[:n_tok].reshape(shp)


def _pad_w_in(w):
    c0 = 2 * D_RNN
    wq = w[:, c0:c0 + D_Q].reshape(D_MODEL, N_KV, GROUP, 1, HEAD_DIM)
    sel = jnp.eye(N_KV, dtype=w.dtype).reshape(N_KV, 1, N_KV, 1)
    wq = (wq * sel).reshape(D_MODEL, D_QP)
    c1 = c0 + D_Q
    c2 = c1 + 6 * D_KV + N_NG
    pad = jnp.zeros((D_MODEL, LANES - N_NG), w.dtype)
    return jnp.concatenate([w[:, :c0], wq, w[:, c1:c2], pad, w[:, c2:]], axis=1).astype(jnp.bfloat16)


def _unpad_q(q_pad, n, s):
    qp = q_pad.reshape(n, s, N_KV, GROUP, N_KV, HEAD_DIM)
    return jnp.stack([qp[:, :, h, :, h, :] for h in range(N_KV)], axis=2)


def _project(x, pos_rows, tm, g_mix, w_pad, g_q, g_k3):
    n, s, _ = x.shape
    outs = _input_projection(x.reshape(n * s, D_MODEL), pos_rows, tm, g_mix, w_pad, g_q, g_k3)
    arag, q, kc, vc, ks, vs, kw, vw, ng, gab = outs
    heads = lambda t: t.reshape(n, s, N_KV, HEAD_DIM)
    ar = arag[:, :D_RNN].reshape(n, s, D_RNN)
    ag = arag[:, D_RNN:].reshape(n, s, D_RNN)
    q = _unpad_q(q, n, s)
    ng = ng[:, :N_NG].reshape(n, s, N_KV, GROUP, 3)
    ga = gab[:, :D_MODEL].reshape(n, s, D_MODEL)
    gb = gab[:, D_MODEL:].reshape(n, s, D_MODEL)
    return ar, ag, q, heads(kc), heads(vc), heads(ks), heads(vs), heads(kw), heads(vw), ng, ga, gb


def kernel(x_prompt, x_sample, page_table, cache_k_cmp, cache_v_cmp, cache_k_slc, cache_v_slc,
           cache_k_win, cache_v_win, state_h, state_conv, g_mix, w_in, conv_w, conv_b, lru_wa, lru_ba,
           lru_wx, lru_bx, lru_lambda, g_q, g_k_cmp, g_k_slc, g_k_win, cmp_pos_k, cmp_w1_k, cmp_w2_k,
           cmp_pos_v, cmp_w1_v, cmp_w2_v, w_proj_a, w_proj_b, w_out, g_ffn, router_w, router_b,
           exp_w_up, exp_b_up, exp_w_down, exp_b_down):
    xp, xs = x_prompt, x_sample
    bp, sp = xp.shape[:2]
    bs, ss = xs.shape[:2]
    pos_p = jnp.arange(sp, dtype=jnp.int32)
    pos_s = PAST_LEN + jnp.arange(ss, dtype=jnp.int32)
    depth = w_in.shape[0]
    layer_states = []
    tile2 = lambda g: jnp.concatenate([g, g], axis=-1)
    for l in range(depth):
        cmp_w = (cmp_pos_k[l], cmp_w1_k[l], cmp_w2_k[l], cmp_pos_v[l], cmp_w1_v[l], cmp_w2_v[l])
        lru_w = (conv_w[l], conv_b[l], lru_wa[l], lru_ba[l], lru_wx[l], lru_bx[l], lru_lambda[l])
        w_pad = _pad_w_in(w_in[l])
        gq2 = tile2(g_q[l]).reshape(1, LANES)
        gk3 = jnp.stack([tile2(g_k_cmp[l]), tile2(g_k_slc[l]), tile2(g_k_win[l])])
        cmp_wk = _compress_weights(cmp_pos_k[l], cmp_w1_k[l], cmp_w2_k[l])
        cmp_wv = _compress_weights(cmp_pos_v[l], cmp_w1_v[l], cmp_w2_v[l])
        wb_perm = w_proj_b[l].reshape(N_KV, GROUP, HEAD_DIM, D_MODEL).transpose(1, 0, 2, 3).reshape(D_Q, D_MODEL)
        bf16 = jnp.bfloat16
        lru_k = _lru_weights(*lru_w)
        rw_pad = jnp.pad(router_w[l].astype(jnp.float32), ((0, 0), (0, LANES - N_EXPERTS)))
        rb_pad = jnp.pad(router_b[l].astype(jnp.float32), (0, LANES - N_EXPERTS)).reshape(1, LANES)
        merge_w = (w_proj_a[l].astype(bf16), wb_perm.astype(bf16), w_out[l].astype(bf16), g_ffn[l], rw_pad, rb_pad)
        moe_w = (exp_w_up[l], exp_b_up[l], exp_w_down[l], exp_b_down[l])
        arag, q_pad, kc, vc, ks, vs, kw, vw, ng, gab = _input_projection(
            xp.reshape(bp * sp, D_MODEL), pos_p, 256, g_mix[l], w_pad, gq2, gk3)
        rows = lambda t: t.reshape(bp, sp, t.shape[-1])
        heads = lambda t: t.reshape(bp, sp, N_KV, HEAD_DIM)
        conv0 = jnp.zeros((bp, CONV_W - 1, D_RNN), jnp.float32)
        y_lru, p_h = _lru_prompt(arag, conv0, jnp.zeros((bp, D_RNN), jnp.float32), lru_k, bp, sp, 256)
        p_conv = jnp.concatenate([conv0, rows(arag)[:, :, :D_RNN]], axis=1)[:, sp:]
        cmp_k, cmp_v = _compress_prompt(rows(kc), rows(vc), cmp_wk, cmp_wv)
        o_nsa = _nsa_prompt_attention(rows(q_pad), rows(ks), rows(vs), rows(kw), rows(vw), cmp_k, cmp_v, rows(ng))
        xn, hn, te, tr, tg, cnt = _merge_router(xp.reshape(bp * sp, D_MODEL), y_lru, o_nsa.reshape(bp * sp, D_Q),
                                                gab, *merge_w, 256)
        xp = _moe_layer(xn, hn, te, tr, tg, cnt, *moe_w, 256).reshape(bp, sp, D_MODEL)
        wbp = min(WINDOW, sp)
        p_states = (heads(kc), heads(vc), heads(ks), heads(vs), heads(kw)[:, -wbp:], heads(vw)[:, -wbp:],
                    p_h, p_conv)
        assert ss == 1
        pos_rows_s = jnp.broadcast_to(pos_s[None, :], (bs, ss)).reshape(-1)
        arag, q_pad, kc, vc, ks, vs, kw, vw, ng, gab = _input_projection(
            xs.reshape(bs, D_MODEL), pos_rows_s, bs, g_mix[l], w_pad, gq2, gk3)
        heads = lambda t: t.reshape(bs, ss, N_KV, HEAD_DIM)
        y_lru, s_h_l = _lru_step(arag, state_conv[l], state_h[l], lru_k)
        s_conv_l = jnp.concatenate([state_conv[l][:, 1:], arag[:, None, :D_RNN]], axis=1)
        o_nsa, s_kw_l, s_vw_l = _nsa_sample_paged(
            q_pad, ks, vs, kw, vw, ng, page_table, cache_k_cmp[l], cache_v_cmp[l], cache_k_slc[l], cache_v_slc[l],
            cache_k_win[l], cache_v_win[l], cmp_wk, cmp_wv)
        xn, hn, te, tr, tg, cnt = _merge_router(xs.reshape(bs, D_MODEL), y_lru, o_nsa, gab, *merge_w, bs)
        xs = _moe_layer(xn, hn, te, tr, tg, cnt, *moe_w, bs).reshape(bs, ss, D_MODEL)
        s_states = (heads(kc), heads(vc), heads(ks), heads(vs), s_kw_l, s_vw_l, s_h_l, s_conv_l)
        layer_states.append(p_states + s_states)
    (p_k_cmp, p_v_cmp, p_k_slc, p_v_slc, p_k_win, p_v_win, p_h, p_conv,
     s_k_cmp, s_v_cmp, s_k_slc, s_v_slc, s_k_win, s_v_win, s_h, s_conv) = [jnp.stack(z) for z in zip(*layer_states)]
    return (xp, xs, p_k_cmp, p_v_cmp, p_k_slc, p_v_slc, p_k_win, p_v_win, p_h, p_conv,
            s_k_cmp, s_v_cmp, s_k_slc, s_v_slc, s_k_win, s_v_win, s_h, s_conv)
```

```python
import functools
import math

import jax
import jax.numpy as jnp
import numpy as np
from jax import lax
from jax.experimental import pallas as pl
from jax.experimental.pallas import tpu as pltpu

D_MODEL = 1024
PAST_LEN = 8192
PAGE_SIZE = 128
D_RNN = D_MODEL
LRU_BLOCKS = 16
LRU_BW = D_RNN // LRU_BLOCKS
CONV_W = 4
LRU_C = 8.0
N_HEADS = 8
HEAD_DIM = 64
N_KV = 2
GROUP = N_HEADS // N_KV
D_Q = N_HEADS * HEAD_DIM
D_KV = N_KV * HEAD_DIM
ROPE_DIM = HEAD_DIM // 4
ROPE_THETA = 500000.0
CMP_LEN = 32
CMP_STRIDE = 16
CMP_HID = 256
SLC_BLOCK = 64
N_SEL = 16
WINDOW = 512
Q_BLOCK = 64
N_EXPERTS = 32
TOP_K = 4
D_FF = D_MODEL
SWIGLU_LIMIT = 7.0
SWIGLU_ALPHA = 1.702
EPS = 1e-6
NEG = -1e30
BIG = 1e30

LANES = 128
SUBLANES = 8
VMEM_LIMIT = 56 * 1024 * 1024

N_NG = 3 * N_HEADS
D_QP = N_HEADS * LANES
C_AR = 0
C_Q = 2 * D_RNN
C_KV = C_Q + D_QP
C_NG = C_KV + 6 * D_KV
C_GAB = C_NG + LANES
N_IN_PAD = C_GAB + 2 * D_MODEL


def _seg_sum64(x2, ones_blk):
    hi = x2.astype(jnp.bfloat16)
    lo = (x2 - hi.astype(jnp.float32)).astype(jnp.bfloat16)
    return (jnp.dot(hi, ones_blk, preferred_element_type=jnp.float32)
            + jnp.dot(lo, ones_blk, preferred_element_type=jnp.float32))


def _head_norm_rope(y, gain, ones_blk, rc, rs1, rs2):
    ss = _seg_sum64(y * y, ones_blk)
    yn = y * lax.rsqrt(ss * (1.0 / HEAD_DIM) + EPS) * gain
    half = ROPE_DIM // 2
    return yn * rc + pltpu.roll(yn, LANES - half, 1) * rs1 + pltpu.roll(yn, half, 1) * rs2


def _proj_kernel(x_ref, gmix_ref, w_ref, ones_ref, gq_ref, gk_ref, rc_ref, rs1_ref, rs2_ref,
                 arag_ref, q_ref, kc_ref, vc_ref, ks_ref, vs_ref, kw_ref, vw_ref, ng_ref, gab_ref):
    x = x_ref[...]
    xn = x * lax.rsqrt(jnp.mean(x * x, axis=-1, keepdims=True) + EPS) * gmix_ref[...]
    xb = xn.astype(jnp.bfloat16)

    def proj(c0, width):
        return jnp.dot(xb, w_ref[:, c0:c0 + width], preferred_element_type=jnp.float32)

    arag_ref[...] = proj(C_AR, 2 * D_RNN)
    gab_ref[...] = proj(C_GAB, 2 * D_MODEL)
    ng_ref[...] = proj(C_NG, LANES)
    ones_blk = ones_ref[...]
    rc, rs1, rs2 = rc_ref[...], rs1_ref[...], rs2_ref[...]
    scale = HEAD_DIM ** -0.5
    for j in range(N_HEADS):
        y = proj(C_Q + j * LANES, LANES)
        yr = _head_norm_rope(y, gq_ref[...], ones_blk, rc, rs1, rs2)
        q_ref[:, j * LANES:(j + 1) * LANES] = (yr * scale).astype(q_ref.dtype)
    k_outs = (kc_ref, ks_ref, kw_ref)
    v_outs = (vc_ref, vs_ref, vw_ref)
    for j in range(3):
        y = proj(C_KV + 2 * j * LANES, LANES)
        k_outs[j][...] = _head_norm_rope(y, gk_ref[j:j + 1, :], ones_blk, rc, rs1, rs2)
        v_outs[j][...] = proj(C_KV + (2 * j + 1) * LANES, LANES)


def _rope_tables(pos):
    half = ROPE_DIM // 2
    inv = ROPE_THETA ** (-jnp.arange(half, dtype=jnp.float32) / half)
    ang = pos.astype(jnp.float32)[:, None] * inv[None, :]
    cos, sin = jnp.cos(ang), jnp.sin(ang)
    p = pos.shape[0]
    ones = jnp.ones((p, HEAD_DIM - ROPE_DIM), jnp.float32)
    zeros = jnp.zeros((p, HEAD_DIM - ROPE_DIM), jnp.float32)
    zh = jnp.zeros((p, half), jnp.float32)
    rc = jnp.concatenate([cos, cos, ones], axis=-1)
    rs1 = jnp.concatenate([-sin, zh, zeros], axis=-1)
    rs2 = jnp.concatenate([zh, sin, zeros], axis=-1)
    tile2 = lambda t: jnp.concatenate([t, t], axis=-1)
    return tile2(rc), tile2(rs1), tile2(rs2)


def _ones_block64():
    i = np.arange(LANES)
    return jnp.asarray((i[:, None] // HEAD_DIM) == (i[None, :] // HEAD_DIM), dtype=jnp.bfloat16)


def _input_projection(x2d, pos_rows, tm, g_mix, w_pad, g_q, g_k3):
    m = x2d.shape[0]
    p = pos_rows.shape[0]
    assert m % tm == 0 and p % tm == 0
    rc, rs1, rs2 = _rope_tables(pos_rows)
    n_tab = p // tm
    row = lambda i: (i, 0)
    const = lambda i: (0, 0)
    tab = lambda i: (i % n_tab, 0)
    f32 = jnp.float32
    out_shape = (
        jax.ShapeDtypeStruct((m, 2 * D_RNN), f32),
        jax.ShapeDtypeStruct((m, D_QP), jnp.bfloat16),
    ) + tuple(jax.ShapeDtypeStruct((m, D_KV), f32) for _ in range(6)) + (
        jax.ShapeDtypeStruct((m, LANES), f32),
        jax.ShapeDtypeStruct((m, 2 * D_MODEL), f32),
    )
    out_specs = (
        pl.BlockSpec((tm, 2 * D_RNN), row),
        pl.BlockSpec((tm, D_QP), row),
    ) + tuple(pl.BlockSpec((tm, D_KV), row) for _ in range(6)) + (
        pl.BlockSpec((tm, LANES), row),
        pl.BlockSpec((tm, 2 * D_MODEL), row),
    )
    return pl.pallas_call(
        _proj_kernel,
        grid=(m // tm,),
        in_specs=[
            pl.BlockSpec((tm, D_MODEL), row),
            pl.BlockSpec((1, D_MODEL), const),
            pl.BlockSpec((D_MODEL, N_IN_PAD), const),
            pl.BlockSpec((LANES, LANES), const),
            pl.BlockSpec((1, LANES), const),
            pl.BlockSpec((3, LANES), const),
            pl.BlockSpec((tm, LANES), tab),
            pl.BlockSpec((tm, LANES), tab),
            pl.BlockSpec((tm, LANES), tab),
        ],
        out_specs=out_specs,
        out_shape=out_shape,
        compiler_params=pltpu.CompilerParams(dimension_semantics=("arbitrary",), vmem_limit_bytes=VMEM_LIMIT),
        name="input_projection",
    )(x2d, g_mix.reshape(1, D_MODEL), w_pad, _ones_block64(), g_q, g_k3, rc, rs1, rs2)


SLC_CHUNK = 1024
WIN_SPAN = WINDOW + LANES
ROWS = GROUP * Q_BLOCK


def _dot_nt(a, b):
    return lax.dot_general(a, b, (((1,), (1,)), ((), ())), preferred_element_type=jnp.float32)


def _softmax_rows(s, mask):
    s = jnp.where(mask, s, NEG)
    m = jnp.max(s, axis=-1, keepdims=True)
    p = jnp.where(mask, jnp.exp(s - m), 0.0)
    return p / jnp.maximum(jnp.sum(p, axis=-1, keepdims=True), 1e-30)


def _split3_bf16(x):
    hi = x.astype(jnp.bfloat16)
    r1 = x - hi.astype(jnp.float32)
    mid = r1.astype(jnp.bfloat16)
    lo = (r1 - mid.astype(jnp.float32)).astype(jnp.bfloat16)
    return hi, mid, lo


def _nsa_prompt_kernel(q_ref, ksa_ref, vs_ref, kw_ref, vw_ref, ck_ref, cv_ref, ng_ref, mmt_ref,
                       o_ref, sv_scr):
    f32, bf16 = jnp.float32, jnp.bfloat16
    qb = pl.program_id(1)
    row = lax.broadcasted_iota(jnp.int32, (ROWS, 1), 0)
    pos_t = qb * Q_BLOCK + (row & (Q_BLOCK - 1))
    ncp = ck_ref.shape[1]

    qh = [jnp.concatenate([q_ref[0, :, (h * GROUP + g) * LANES:(h * GROUP + g + 1) * LANES]
                           for g in range(GROUP)], axis=0) for h in range(N_KV)]

    ck, cv = ck_ref[0], cv_ref[0]
    c_end = lax.broadcasted_iota(jnp.int32, (1, ncp), 1) * CMP_STRIDE + (CMP_LEN - 1)
    cmask = c_end <= pos_t
    o_cmp, psum = [], []
    for h in range(N_KV):
        p = _softmax_rows(_dot_nt(qh[h], ck), cmask)
        o_cmp.append(jnp.dot(p.astype(bf16), cv, preferred_element_type=f32))
        psum.append(p[0:Q_BLOCK] + p[Q_BLOCK:2 * Q_BLOCK] + p[2 * Q_BLOCK:3 * Q_BLOCK] + p[3 * Q_BLOCK:])
    ps = jnp.concatenate(psum, axis=0)

    mmt = mmt_ref[...]
    sc = sum(_dot_nt(mmt, part) for part in _split3_bf16(ps))[0:Q_BLOCK]
    jio = lax.broadcasted_iota(jnp.int32, (Q_BLOCK, LANES), 0)
    forced = (jio == 0) | (jio == qb) | (jio == qb - 1)
    sv = jnp.where(forced, BIG, jnp.where(jio <= qb, sc, NEG))
    sv_scr[...] = sv

    def rank_body(i, cnt):
        r = sv_scr[pl.ds(i, 1), :]
        beats = (r > sv) | ((r == sv) & (jio > i))
        return cnt + jnp.where(beats, 1.0, 0.0)

    cnt = lax.fori_loop(0, qb + 1, rank_body, jnp.zeros((Q_BLOCK, LANES), f32))
    bias_lo = jnp.where((cnt < N_SEL) & (sv > NEG / 2), 0.0, NEG)
    lane = lax.broadcasted_iota(jnp.int32, (Q_BLOCK, LANES), 1)
    bias_t = jnp.concatenate([jnp.where(lane >= Q_BLOCK, bias_lo, 0.0),
                              jnp.where(lane < Q_BLOCK, bias_lo, 0.0)], axis=0)
    bias = bias_t.T.astype(bf16)

    kio = lax.broadcasted_iota(jnp.int32, (1, SLC_CHUNK), 1)
    wio = lax.broadcasted_iota(jnp.int32, (1, WIN_SPAN), 1)
    wbase = pl.multiple_of((jnp.maximum(qb - WINDOW // Q_BLOCK, 0) // 2) * LANES, LANES)
    dpos = pos_t - (wbase + wio)
    wmask = (dpos >= 0) & (dpos < WINDOW)
    n_chunks = qb // (SLC_CHUNK // SLC_BLOCK) + 1
    sig = jax.nn.sigmoid(ng_ref[0])
    qa = [qh[h] + jnp.concatenate([bias[h * Q_BLOCK:(h + 1) * Q_BLOCK]] * GROUP, axis=0) for h in range(N_KV)]

    def chunk(c, carry, causal):
        k0 = pl.multiple_of(c * SLC_CHUNK, SLC_CHUNK)
        v = vs_ref[0, pl.ds(k0, SLC_CHUNK), :]
        new = []
        for h in range(N_KV):
            m, l, acc = carry[h]
            s = _dot_nt(qa[h], ksa_ref[0, h, pl.ds(k0, SLC_CHUNK), :])
            if causal:
                s = jnp.where(k0 + kio <= pos_t, s, NEG)
            m_new = jnp.maximum(m, jnp.max(s, axis=-1, keepdims=True))
            alpha = jnp.exp(m - m_new)
            p = jnp.exp(s - m_new)
            l = alpha * l + jnp.sum(p, axis=-1, keepdims=True)
            acc = alpha * acc + jnp.dot(p.astype(bf16), v, preferred_element_type=f32)
            new.append((m_new, l, acc))
        return tuple(new)

    init = (jnp.full((ROWS, 1), NEG, f32), jnp.zeros((ROWS, 1), f32), jnp.zeros((ROWS, LANES), f32))
    carry = lax.fori_loop(0, n_chunks - 1, lambda c, cr: chunk(c, cr, False), (init, init))
    carry = chunk(n_chunks - 1, carry, True)

    out_h = []
    for h in range(N_KV):
        _, l, acc = carry[h]
        o_slc = acc / jnp.maximum(l, 1e-30)

        pw = _softmax_rows(_dot_nt(qh[h], kw_ref[0, pl.ds(wbase, WIN_SPAN), :]), wmask)
        o_win = jnp.dot(pw.astype(bf16), vw_ref[0, pl.ds(wbase, WIN_SPAN), :], preferred_element_type=f32)

        outs = []
        for g in range(GROUP):
            rs = slice(g * Q_BLOCK, (g + 1) * Q_BLOCK)
            c0 = (h * GROUP + g) * 3
            outs.append(sig[:, c0:c0 + 1] * o_cmp[h][rs] + sig[:, c0 + 1:c0 + 2] * o_slc[rs]
                        + sig[:, c0 + 2:c0 + 3] * o_win[rs])
        out_h.append(outs)
    for g in range(GROUP):
        o_ref[0, :, g * LANES:(g + 1) * LANES] = jnp.where(lane < HEAD_DIM, out_h[0][g], out_h[1][g]).astype(o_ref.dtype)


def _block_score_matrix(ncp):
    ratio = SLC_BLOCK // CMP_STRIDE
    ovl = CMP_LEN // CMP_STRIDE - 1
    j = np.arange(LANES)[:, None]
    c = np.arange(ncp)[None, :]
    m = (c >= ratio * j - ovl) & (c <= ratio * j + ratio - 1) & (j < Q_BLOCK) & (c < ncp - 1)
    return jnp.asarray(m, dtype=jnp.bfloat16)


def _nsa_prompt_attention(q_pad, ks, vs, kw, vw, cmp_k, cmp_v, ng):
    b, s, _ = q_pad.shape
    assert s % SLC_CHUNK == 0 and s // SLC_BLOCK <= Q_BLOCK and s >= WIN_SPAN and Q_BLOCK == SLC_BLOCK
    bf16 = jnp.bfloat16
    ncp = cmp_k.shape[1]
    onehot = jax.nn.one_hot(jnp.arange(s) // SLC_BLOCK, Q_BLOCK, dtype=bf16)
    onehot = jnp.broadcast_to(onehot[None], (b, s, Q_BLOCK))
    ksb = ks.astype(bf16)
    ksa = jnp.stack([jnp.concatenate([ksb[:, :, :HEAD_DIM], onehot], axis=-1),
                     jnp.concatenate([onehot, ksb[:, :, HEAD_DIM:]], axis=-1)], axis=1)
    per_b = lambda bi, i: (bi, 0, 0)
    return pl.pallas_call(
        _nsa_prompt_kernel,
        grid=(b, s // Q_BLOCK),
        in_specs=[
            pl.BlockSpec((1, Q_BLOCK, D_QP), lambda bi, i: (bi, i, 0)),
            pl.BlockSpec((1, N_KV, s, LANES), lambda bi, i: (bi, 0, 0, 0)),
            pl.BlockSpec((1, s, LANES), per_b),
            pl.BlockSpec((1, s, LANES), per_b),
            pl.BlockSpec((1, s, LANES), per_b),
            pl.BlockSpec((1, ncp, LANES), per_b),
            pl.BlockSpec((1, ncp, LANES), per_b),
            pl.BlockSpec((1, Q_BLOCK, LANES), lambda bi, i: (bi, i, 0)),
            pl.BlockSpec((LANES, ncp), lambda bi, i: (0, 0)),
        ],
        out_specs=pl.BlockSpec((1, Q_BLOCK, D_Q), lambda bi, i: (bi, i, 0)),
        out_shape=jax.ShapeDtypeStruct((b, s, D_Q), bf16),
        scratch_shapes=[pltpu.VMEM((Q_BLOCK, LANES), jnp.float32)],
        compiler_params=pltpu.CompilerParams(dimension_semantics=("arbitrary", "arbitrary"),
                                             vmem_limit_bytes=VMEM_LIMIT),
        name="nsa_prompt_attention",
    )(q_pad, ksa, vs.astype(bf16), kw.astype(bf16), vw.astype(bf16), cmp_k, cmp_v, ng, _block_score_matrix(ncp))


CHUNK_W = CMP_STRIDE * D_KV
CMP_PARTS = CMP_LEN // CMP_STRIDE
HID2 = N_KV * CMP_HID


def _compress_weights(pos_emb, w1, w2):
    bf16 = jnp.bfloat16
    eye = jnp.eye(N_KV, dtype=w1.dtype)
    w1b = w1.reshape(CMP_PARTS, CMP_STRIDE, HEAD_DIM, CMP_HID)
    w1p = jnp.einsum('pldf,hk->lhdpkf', w1b, eye).reshape(CHUNK_W, CMP_PARTS * HID2).astype(bf16)
    w2p = jnp.einsum('fd,hk->hfkd', w2, eye).reshape(HID2, D_KV).astype(bf16)
    pos8 = jnp.zeros((8, CMP_LEN * HEAD_DIM), bf16).at[0].set(pos_emb.reshape(-1).astype(bf16))
    w1f = w1.reshape(CMP_LEN * HEAD_DIM, CMP_HID).astype(bf16)
    return w1p, w2p, pos8, w1f


def _compress_rows(x, w1p, w2p, pos8, w1f):
    f32 = jnp.float32
    r = x.shape[0]
    part = jnp.dot(x, w1p, preferred_element_type=f32)
    bias = jnp.dot(pos8, w1f, preferred_element_type=f32)[0:1]
    bias = jnp.concatenate([bias] * N_KV, axis=-1)
    hid = bias + part[:, :HID2] + pltpu.roll(part[:, HID2:], r - 1, 0)
    out = jnp.dot(jax.nn.gelu(hid).astype(jnp.bfloat16), w2p, preferred_element_type=f32)
    rows = lax.broadcasted_iota(jnp.int32, (r, 1), 0)
    return jnp.where(rows < r - 1, out, 0.0)


def _compress_prompt_kernel(xk_ref, xv_ref, w1k_ref, w2k_ref, pk_ref, fk_ref, w1v_ref, w2v_ref, pv_ref, fv_ref,
                            ok_ref, ov_ref):
    bf16 = jnp.bfloat16
    ok_ref[0] = _compress_rows(xk_ref[0].astype(bf16), w1k_ref[...], w2k_ref[...], pk_ref[...],
                               fk_ref[...]).astype(ok_ref.dtype)
    ov_ref[0] = _compress_rows(xv_ref[0].astype(bf16), w1v_ref[...], w2v_ref[...], pv_ref[...],
                               fv_ref[...]).astype(ov_ref.dtype)


def _compress_prompt(kc, vc, wk, wv):
    b, s, _ = kc.shape
    r = s // CMP_STRIDE
    xspec = pl.BlockSpec((1, r, CHUNK_W), lambda i: (i, 0, 0))
    wspecs = [pl.BlockSpec(w.shape, lambda i: (0, 0)) for w in wk]
    ospec = pl.BlockSpec((1, r, D_KV), lambda i: (i, 0, 0))
    oshape = jax.ShapeDtypeStruct((b, r, D_KV), jnp.bfloat16)
    return pl.pallas_call(
        _compress_prompt_kernel,
        grid=(b,),
        in_specs=[xspec, xspec] + wspecs + wspecs,
        out_specs=(ospec, ospec),
        out_shape=(oshape, oshape),
        compiler_params=pltpu.CompilerParams(dimension_semantics=("arbitrary",), vmem_limit_bytes=VMEM_LIMIT),
        name="compress_prompt",
    )(kc.reshape(b, r, CHUNK_W), vc.reshape(b, r, CHUNK_W), *wk, *wv)


N_PAGES = PAST_LEN // PAGE_SIZE
PAST_CHUNKS = PAST_LEN // CMP_STRIDE
PAST_BLOCKS = PAST_LEN // SLC_BLOCK
BLOCKS_PER_PAGE = PAGE_SIZE // SLC_BLOCK
CHUNKS_PER_PAGE = PAGE_SIZE // CMP_STRIDE
SEL_ROWS = 256
QROWS = SUBLANES
REGROUP_UNROLL = 4


def _sample_cmp_kernel(pt_ref, q8_ref, kpool, vpool, w1k_ref, w2k_ref, pk_ref, fk_ref, w1v_ref, w2v_ref, pv_ref,
                       fv_ref, ocmp_ref, psum_ref, kbuf, vbuf, x_scr, t_scr, sem):
    f32, bf16 = jnp.float32, jnp.bfloat16
    n = pl.program_id(0)

    def fetch(s, slot):
        def body(j, _):
            pg = pt_ref[s, j]
            pltpu.make_async_copy(kpool.at[pg], kbuf.at[slot, j], sem.at[0, slot]).start()
            pltpu.make_async_copy(vpool.at[pg], vbuf.at[slot, j], sem.at[1, slot]).start()
            return 0
        lax.fori_loop(0, N_PAGES, body, 0)

    @pl.when(n == 0)
    def _():
        fetch(0, 0)

    @pl.when(n + 1 < pl.num_programs(0))
    def _():
        fetch(n + 1, (n + 1) % 2)

    slot = n % 2
    pltpu.make_async_copy(kbuf.at[slot], kbuf.at[slot], sem.at[0, slot]).wait()
    pltpu.make_async_copy(vbuf.at[slot], vbuf.at[slot], sem.at[1, slot]).wait()

    def regroup(buf):
        def body(jj, _):
            for u in range(REGROUP_UNROLL):
                j = jj * REGROUP_UNROLL + u
                t_scr[u] = buf[slot, j].T
                r0 = pl.multiple_of(j * CHUNKS_PER_PAGE, CHUNKS_PER_PAGE)
                for l in range(CMP_STRIDE):
                    x_scr[pl.ds(r0, CHUNKS_PER_PAGE), l * D_KV:(l + 1) * D_KV] = (
                        t_scr[u, pl.ds(l, CHUNKS_PER_PAGE, stride=CMP_STRIDE), :])
            return 0
        lax.fori_loop(0, N_PAGES // REGROUP_UNROLL, body, 0)
        return x_scr[...].astype(bf16)

    ck = _compress_rows(regroup(kbuf), w1k_ref[...], w2k_ref[...], pk_ref[...], fk_ref[...]).astype(bf16)
    cv = _compress_rows(regroup(vbuf), w1v_ref[...], w2v_ref[...], pv_ref[...], fv_ref[...]).astype(bf16)
    c_end = lax.broadcasted_iota(jnp.int32, (1, PAST_CHUNKS), 1) * CMP_STRIDE + (CMP_LEN - 1)
    cmask = c_end <= PAST_LEN
    rows = lax.broadcasted_iota(jnp.int32, (QROWS, 1), 0)
    ps = jnp.zeros((QROWS, PAST_CHUNKS), f32)
    for h in range(N_KV):
        p = _softmax_rows(_dot_nt(q8_ref[0, h], ck), cmask)
        ocmp_ref[0, h] = jnp.dot(p.astype(bf16), cv, preferred_element_type=f32)
        ps = jnp.where(rows == h, p[0:1] + p[1:2] + p[2:3] + p[3:4], ps)
    psum_ref[0] = ps


def _sample_cmp(page_table, q8, k_pool, v_pool, wk, wv):
    n = q8.shape[0]
    f32 = jnp.float32
    wspecs = [pl.BlockSpec(w.shape, lambda i, pt: (0, 0)) for w in wk]
    grid_spec = pltpu.PrefetchScalarGridSpec(
        num_scalar_prefetch=1,
        grid=(n,),
        in_specs=[pl.BlockSpec((1, N_KV, QROWS, LANES), lambda i, pt: (i, 0, 0, 0)),
                  pl.BlockSpec(memory_space=pl.ANY), pl.BlockSpec(memory_space=pl.ANY)] + wspecs + wspecs,
        out_specs=(pl.BlockSpec((1, N_KV, QROWS, LANES), lambda i, pt: (i, 0, 0, 0)),
                   pl.BlockSpec((1, QROWS, PAST_CHUNKS), lambda i, pt: (i, 0, 0))),
        scratch_shapes=[pltpu.VMEM((2, N_PAGES, D_KV, PAGE_SIZE), f32),
                        pltpu.VMEM((2, N_PAGES, D_KV, PAGE_SIZE), f32),
                        pltpu.VMEM((PAST_CHUNKS, CHUNK_W), f32),
                        pltpu.VMEM((REGROUP_UNROLL, PAGE_SIZE, D_KV), f32),
                        pltpu.SemaphoreType.DMA((2, 2))],
    )
    return pl.pallas_call(
        _sample_cmp_kernel,
        grid_spec=grid_spec,
        out_shape=(jax.ShapeDtypeStruct((n, N_KV, QROWS, LANES), f32),
                   jax.ShapeDtypeStruct((n, QROWS, PAST_CHUNKS), f32)),
        compiler_params=pltpu.CompilerParams(dimension_semantics=("arbitrary",), vmem_limit_bytes=VMEM_LIMIT),
        name="sample_compress_attention",
    )(page_table, q8, k_pool, v_pool, *wk, *wv)


def _sample_select_kernel(p_ref, mmt_ref, tri_ref, idx_ref, sv_scr):
    f32 = jnp.float32
    cols = p_ref.shape[0]
    cur = PAST_LEN // SLC_BLOCK
    mmt = mmt_ref[...]
    sc = sum(_dot_nt(mmt, part) for part in _split3_bf16(p_ref[...]))
    jio = lax.broadcasted_iota(jnp.int32, (SEL_ROWS, cols), 0)
    forced = (jio == 0) | (jio == cur) | (jio == cur - 1)
    sv = jnp.where(forced, BIG, jnp.where(jio <= cur, sc, NEG))
    sv_scr[...] = sv

    def rank_body(i, cnt):
        r = sv_scr[pl.ds(i, 1), :]
        beats = (r > sv) | ((r == sv) & (jio > i))
        return cnt + jnp.where(beats, 1.0, 0.0)

    cnt = lax.fori_loop(0, cur + 1, rank_body, jnp.zeros((SEL_ROWS, cols), f32))
    sel = (cnt < N_SEL) & (sv > NEG / 2)
    self = jnp.where(sel, 1.0, 0.0)
    before = jnp.dot(tri_ref[...], self.astype(jnp.bfloat16), preferred_element_type=f32)
    rows = []
    for k in range(N_SEL):
        rows.append(jnp.sum(jnp.where(sel & (before == k), jio, 0), axis=0, keepdims=True))
    idx_ref[...] = jnp.concatenate(rows, axis=0)


def _sample_select(p_sum):
    n = p_sum.shape[0]
    cols = n * N_KV
    p2 = p_sum[:, :N_KV, :].reshape(cols, PAST_CHUNKS)
    ratio = SLC_BLOCK // CMP_STRIDE
    ovl = CMP_LEN // CMP_STRIDE - 1
    j = np.arange(SEL_ROWS)[:, None]
    c = np.arange(PAST_CHUNKS)[None, :]
    mmt = (c >= ratio * j - ovl) & (c <= ratio * j + ratio - 1) & (j <= PAST_BLOCKS) & (c < PAST_CHUNKS - 1)
    tri = np.tril(np.ones((SEL_ROWS, SEL_ROWS), np.float32), -1)
    idx = pl.pallas_call(
        _sample_select_kernel,
        out_shape=jax.ShapeDtypeStruct((N_SEL, cols), jnp.int32),
        scratch_shapes=[pltpu.VMEM((SEL_ROWS, cols), jnp.float32)],
        compiler_params=pltpu.CompilerParams(vmem_limit_bytes=VMEM_LIMIT),
        name="sample_select",
    )(p2, jnp.asarray(mmt, dtype=jnp.bfloat16), jnp.asarray(tri, dtype=jnp.bfloat16))
    return idx.T.reshape(n, N_KV * N_SEL)


SLC_KEYS = N_SEL * SLC_BLOCK
EXT = LANES


def _sample_attend_kernel(pt_ref, idx_ref, q8_ref, kpool, vpool, knew_ref, vnew_ref, kwin_ref, vwin_ref, kwn_ref,
                          vwn_ref, ocmp_ref, ng_ref, o_ref, kwo_ref, vwo_ref, kg, vg, sem):
    f32, bf16 = jnp.float32, jnp.bfloat16
    n = pl.program_id(0)
    nblk = N_KV * N_SEL

    def fetch(s, slot):
        def body(i, _):
            jp = jnp.minimum(idx_ref[s, i], PAST_BLOCKS - 1)
            pg = pt_ref[s, jp // BLOCKS_PER_PAGE]
            sub = jp % BLOCKS_PER_PAGE
            pltpu.make_async_copy(kpool.at[pg, sub], kg.at[slot, i], sem.at[0, slot]).start()
            pltpu.make_async_copy(vpool.at[pg, sub], vg.at[slot, i], sem.at[1, slot]).start()
            return 0
        lax.fori_loop(0, nblk, body, 0)

    @pl.when(n == 0)
    def _():
        fetch(0, 0)

    @pl.when(n + 1 < pl.num_programs(0))
    def _():
        fetch(n + 1, (n + 1) % 2)

    slot = n % 2
    pltpu.make_async_copy(kg.at[slot], kg.at[slot], sem.at[0, slot]).wait()
    pltpu.make_async_copy(vg.at[slot], vg.at[slot], sem.at[1, slot]).wait()

    pos = PAST_LEN
    ext_row = lax.broadcasted_iota(jnp.int32, (EXT, 1), 0)
    ext = lambda ref: jnp.where(ext_row == 0, ref[0], 0.0).astype(bf16)
    col = lax.broadcasted_iota(jnp.int32, (1, SLC_KEYS + EXT), 1)
    wcol = lax.broadcasted_iota(jnp.int32, (1, WINDOW + EXT), 1)
    wpos = jnp.where(wcol < WINDOW, pos - WINDOW + wcol, pos + (wcol - WINDOW))
    wmask = (pos - wpos >= 0) & (pos - wpos < WINDOW)
    kwin, vwin = kwin_ref[0], vwin_ref[0]
    kw_all = jnp.concatenate([kwin.astype(bf16), ext(kwn_ref)], axis=0)
    vw_all = jnp.concatenate([vwin.astype(bf16), ext(vwn_ref)], axis=0)
    k_ext, v_ext = ext(knew_ref), ext(vnew_ref)
    lane = lax.broadcasted_iota(jnp.int32, (QROWS, LANES), 1)
    out = []
    for h in range(N_KV):
        q = q8_ref[0, h]
        new_sel = jnp.int32(0)
        valid = jnp.zeros((1, SLC_KEYS + EXT), jnp.int32)
        for p in range(N_SEL):
            j = idx_ref[n, h * N_SEL + p]
            valid = jnp.where((col >> 6) == p, (j < PAST_BLOCKS).astype(jnp.int32), valid)
            new_sel = jnp.maximum(new_sel, (j >= PAST_BLOCKS).astype(jnp.int32))
        kpos_new = PAST_LEN + (col - SLC_KEYS)
        valid = jnp.where(col >= SLC_KEYS, jnp.where(kpos_new <= pos, new_sel, 0), valid)
        kblk = kg[slot, h * N_SEL:(h + 1) * N_SEL].reshape(SLC_KEYS, LANES).astype(bf16)
        vblk = vg[slot, h * N_SEL:(h + 1) * N_SEL].reshape(SLC_KEYS, LANES).astype(bf16)
        ps = _softmax_rows(_dot_nt(q, jnp.concatenate([kblk, k_ext], axis=0)), valid > 0)
        o_slc = jnp.dot(ps.astype(bf16), jnp.concatenate([vblk, v_ext], axis=0), preferred_element_type=f32)
        pw = _softmax_rows(_dot_nt(q, kw_all), wmask)
        o_win = jnp.dot(pw.astype(bf16), vw_all, preferred_element_type=f32)
        sig = jax.nn.sigmoid(ng_ref[0, h])
        out.append(sig[0] * ocmp_ref[0, h] + sig[1] * o_slc + sig[2] * o_win)
    o_ref[0] = jnp.where(lane < HEAD_DIM, out[0], out[1]).astype(o_ref.dtype)

    last = lax.broadcasted_iota(jnp.int32, (WINDOW, 1), 0) == WINDOW - 1
    kwo_ref[0] = jnp.where(last, kwn_ref[0], pltpu.roll(kwin, WINDOW - 1, 0))
    vwo_ref[0] = jnp.where(last, vwn_ref[0], pltpu.roll(vwin, WINDOW - 1, 0))


def _sample_attend(page_table, idx, q8, k_pool, v_pool, ks_new, vs_new, k_win, v_win, kw_new, vw_new, o_cmp, ngb):
    n = q8.shape[0]
    f32 = jnp.float32
    n_phys = k_pool.shape[0]
    assert k_win.shape[1] == WINDOW
    view = lambda pool: pool.reshape(n_phys, BLOCKS_PER_PAGE, SLC_BLOCK, D_KV)
    row3 = lambda t: t.reshape(n, 1, D_KV)
    per_n3 = lambda i, pt, ix: (i, 0, 0)
    per_n4 = lambda i, pt, ix: (i, 0, 0, 0)
    new_spec = pl.BlockSpec((1, 1, D_KV), per_n3)
    win_spec = pl.BlockSpec((1, WINDOW, D_KV), per_n3)
    q_spec = pl.BlockSpec((1, N_KV, QROWS, LANES), per_n4)
    grid_spec = pltpu.PrefetchScalarGridSpec(
        num_scalar_prefetch=2,
        grid=(n,),
        in_specs=[q_spec, pl.BlockSpec(memory_space=pl.ANY), pl.BlockSpec(memory_space=pl.ANY),
                  new_spec, new_spec, win_spec, win_spec, new_spec, new_spec, q_spec,
                  pl.BlockSpec((1, N_KV, 3, QROWS, LANES), lambda i, pt, ix: (i, 0, 0, 0, 0))],
        out_specs=(pl.BlockSpec((1, QROWS, LANES), per_n3), win_spec, win_spec),
        scratch_shapes=[pltpu.VMEM((2, N_KV * N_SEL, SLC_BLOCK, D_KV), f32),
                        pltpu.VMEM((2, N_KV * N_SEL, SLC_BLOCK, D_KV), f32),
                        pltpu.SemaphoreType.DMA((2, 2))],
    )
    return pl.pallas_call(
        _sample_attend_kernel,
        grid_spec=grid_spec,
        out_shape=(jax.ShapeDtypeStruct((n, QROWS, LANES), jnp.bfloat16),
                   jax.ShapeDtypeStruct((n, WINDOW, D_KV), f32), jax.ShapeDtypeStruct((n, WINDOW, D_KV), f32)),
        compiler_params=pltpu.CompilerParams(dimension_semantics=("arbitrary",), vmem_limit_bytes=VMEM_LIMIT),
        name="sample_attention",
    )(page_table, idx, q8, view(k_pool), view(v_pool), row3(ks_new), row3(vs_new),
      k_win.reshape(n, WINDOW, D_KV), v_win.reshape(n, WINDOW, D_KV), row3(kw_new), row3(vw_new), o_cmp, ngb)


def _nsa_sample_kernels(q_pad, ks, vs, kw, vw, ng, page_table, ck_cmp, cv_cmp, ck_slc, cv_slc, ck_win, cv_win,
                        cmp_wk, cmp_wv):
    n = q_pad.shape[0]
    q8 = jnp.pad(q_pad.reshape(n, N_KV, GROUP, LANES), ((0, 0), (0, 0), (0, QROWS - GROUP), (0, 0)))
    o_cmp, p_sum = _sample_cmp(page_table, q8, ck_cmp, cv_cmp, cmp_wk, cmp_wv)
    idx = _sample_select(p_sum)
    ngb = ng[:, :N_NG].reshape(n, N_KV, GROUP, 3).transpose(0, 1, 3, 2)
    ngb = jnp.pad(ngb, ((0, 0), (0, 0), (0, 0), (0, QROWS - GROUP)))
    ngb = jnp.broadcast_to(ngb[..., None], (n, N_KV, 3, QROWS, LANES))
    o8, kwo, vwo = _sample_attend(page_table, idx, q8, ck_slc, cv_slc, ks, vs, ck_win, cv_win, kw, vw, o_cmp, ngb)
    return o8[:, :GROUP, :].reshape(n, D_Q), kwo, vwo


SLC_COLS = N_SEL * PAGE_SIZE


def _sample_attn_kernel(pt_ref, idx_ref, q_ref, kpool, vpool, knew_ref, vnew_ref, kwin_ref, vwin_ref, kwn_ref,
                        vwn_ref, ocmp_ref, ng_ref, o_ref, kwo_ref, vwo_ref, kg, vg, sem):
    f32, bf16 = jnp.float32, jnp.bfloat16
    n = pl.program_id(0)
    nblk = N_KV * N_SEL

    def fetch(s, slot):
        def body(i, _):
            jp = jnp.minimum(idx_ref[s, i], PAST_BLOCKS - 1)
            pg = pt_ref[s, jp // BLOCKS_PER_PAGE]
            h = i // N_SEL
            pltpu.make_async_copy(kpool.at[pg, h], kg.at[slot, i], sem.at[0, slot]).start()
            pltpu.make_async_copy(vpool.at[pg, h], vg.at[slot, i], sem.at[1, slot]).start()
            return 0
        lax.fori_loop(0, nblk, body, 0)

    @pl.when(n == 0)
    def _():
        fetch(0, 0)

    @pl.when(n + 1 < pl.num_programs(0))
    def _():
        fetch(n + 1, (n + 1) % 2)

    slot = n % 2
    pltpu.make_async_copy(kg.at[slot], kg.at[slot], sem.at[0, slot]).wait()
    pltpu.make_async_copy(vg.at[slot], vg.at[slot], sem.at[1, slot]).wait()

    pos = PAST_LEN
    col = lax.broadcasted_iota(jnp.int32, (1, SLC_COLS + LANES), 1)
    wcol = lax.broadcasted_iota(jnp.int32, (1, WINDOW + LANES), 1)
    wpos = jnp.where(wcol < WINDOW, pos - WINDOW + wcol, pos + (wcol - WINDOW))
    wmask = (pos - wpos >= 0) & (pos - wpos < WINDOW)
    wlane = lax.broadcasted_iota(jnp.int32, (1, WINDOW), 1)
    for h in range(N_KV):
        q = q_ref[0, h]
        new_sel = jnp.int32(0)
        lo = jnp.full((1, SLC_COLS + LANES), SLC_COLS + LANES, jnp.int32)
        for p in range(N_SEL):
            j = idx_ref[n, h * N_SEL + p]
            start = p * PAGE_SIZE + (j % BLOCKS_PER_PAGE) * SLC_BLOCK
            start = jnp.where(j < PAST_BLOCKS, start, SLC_COLS + LANES)
            lo = jnp.where((col >> 7) == p, start, lo)
            new_sel = jnp.maximum(new_sel, (j >= PAST_BLOCKS).astype(jnp.int32))
        valid = jnp.where((col >= lo) & (col < lo + SLC_BLOCK), 1, 0)
        kpos_new = PAST_LEN + (col - SLC_COLS)
        valid = jnp.where(col >= SLC_COLS, jnp.where(kpos_new <= pos, new_sel, 0), valid) > 0
        kt = jnp.concatenate([kg[slot, h * N_SEL + p] for p in range(N_SEL)] + [knew_ref[0, h]], axis=-1)
        vt = jnp.concatenate([vg[slot, h * N_SEL + p] for p in range(N_SEL)] + [vnew_ref[0, h]], axis=-1)
        ps = _softmax_rows(jnp.dot(q, kt.astype(bf16), preferred_element_type=f32), valid)
        o_slc = _dot_nt(ps.astype(bf16), vt.astype(bf16))
        kwin, vwin = kwin_ref[0, h], vwin_ref[0, h]
        kw_all = jnp.concatenate([kwin, kwn_ref[0, h]], axis=-1).astype(bf16)
        vw_all = jnp.concatenate([vwin, vwn_ref[0, h]], axis=-1).astype(bf16)
        pw = _softmax_rows(jnp.dot(q, kw_all, preferred_element_type=f32), wmask)
        o_win = _dot_nt(pw.astype(bf16), vw_all)
        sig = jax.nn.sigmoid(ng_ref[0, h])
        o_cmp = ocmp_ref[0, h][:, h * HEAD_DIM:(h + 1) * HEAD_DIM]
        o_ref[0, h] = sig[0] * o_cmp + sig[1] * o_slc + sig[2] * o_win
        kwo_ref[0, h] = jnp.where(wlane == WINDOW - 1, kwn_ref[0, h][:, 0:1], pltpu.roll(kwin, WINDOW - 1, 1))
        vwo_ref[0, h] = jnp.where(wlane == WINDOW - 1, vwn_ref[0, h][:, 0:1], pltpu.roll(vwin, WINDOW - 1, 1))


def _sample_attn(page_table, idx, q8u, k_pool_t, v_pool_t, knew_t, vnew_t, k_win_t, v_win_t, kwn_t, vwn_t, o_cmp, ngb):
    n = q8u.shape[0]
    f32 = jnp.float32
    assert k_win_t.shape[-1] == WINDOW
    per_n4 = lambda i, pt, ix: (i, 0, 0, 0)
    ext_spec = pl.BlockSpec((1, N_KV, HEAD_DIM, LANES), per_n4)
    win_spec = pl.BlockSpec((1, N_KV, HEAD_DIM, WINDOW), per_n4)
    q_spec = pl.BlockSpec((1, N_KV, QROWS, HEAD_DIM), per_n4)
    grid_spec = pltpu.PrefetchScalarGridSpec(
        num_scalar_prefetch=2,
        grid=(n,),
        in_specs=[q_spec, pl.BlockSpec(memory_space=pl.ANY), pl.BlockSpec(memory_space=pl.ANY),
                  ext_spec, ext_spec, win_spec, win_spec, ext_spec, ext_spec,
                  pl.BlockSpec((1, N_KV, QROWS, LANES), per_n4),
                  pl.BlockSpec((1, N_KV, 3, QROWS, HEAD_DIM), lambda i, pt, ix: (i, 0, 0, 0, 0))],
        out_specs=(q_spec, win_spec, win_spec),
        scratch_shapes=[pltpu.VMEM((2, N_KV * N_SEL, HEAD_DIM, PAGE_SIZE), f32),
                        pltpu.VMEM((2, N_KV * N_SEL, HEAD_DIM, PAGE_SIZE), f32),
                        pltpu.SemaphoreType.DMA((2, 2))],
    )
    return pl.pallas_call(
        _sample_attn_kernel,
        grid_spec=grid_spec,
        out_shape=(jax.ShapeDtypeStruct((n, N_KV, QROWS, HEAD_DIM), f32),
                   jax.ShapeDtypeStruct((n, N_KV, HEAD_DIM, WINDOW), f32),
                   jax.ShapeDtypeStruct((n, N_KV, HEAD_DIM, WINDOW), f32)),
        compiler_params=pltpu.CompilerParams(dimension_semantics=("arbitrary",), vmem_limit_bytes=VMEM_LIMIT),
        name="sample_attention",
    )(page_table, idx, q8u, k_pool_t, v_pool_t, knew_t, vnew_t, k_win_t, v_win_t, kwn_t, vwn_t, o_cmp, ngb)


def _nsa_sample_paged(q_pad, ks, vs, kw, vw, ng, page_table, ck_cmp, cv_cmp, ck_slc, cv_slc, ck_win, cv_win,
                      cmp_wk, cmp_wv):
    n = q_pad.shape[0]
    n_phys = ck_cmp.shape[0]
    pool_t = lambda p: p.transpose(0, 2, 3, 1)
    q4 = q_pad.reshape(n, N_KV, GROUP, N_KV, HEAD_DIM)
    q8 = jnp.pad(q_pad.reshape(n, N_KV, GROUP, LANES), ((0, 0), (0, 0), (0, QROWS - GROUP), (0, 0)))
    q8u = jnp.pad(jnp.stack([q4[:, h, :, h, :] for h in range(N_KV)], axis=1),
                  ((0, 0), (0, 0), (0, QROWS - GROUP), (0, 0)))
    o_cmp, p_sum = _sample_cmp(page_table, q8, pool_t(ck_cmp).reshape(n_phys, D_KV, PAGE_SIZE),
                               pool_t(cv_cmp).reshape(n_phys, D_KV, PAGE_SIZE), cmp_wk, cmp_wv)
    idx = _sample_select(p_sum)
    ngb = ng[:, :N_NG].reshape(n, N_KV, GROUP, 3).transpose(0, 1, 3, 2)
    ngb = jnp.pad(ngb, ((0, 0), (0, 0), (0, 0), (0, QROWS - GROUP)))
    ngb = jnp.broadcast_to(ngb[..., None], (n, N_KV, 3, QROWS, HEAD_DIM))
    col0 = lambda t: jnp.pad(t.reshape(n, N_KV, HEAD_DIM, 1), ((0, 0), (0, 0), (0, 0), (0, LANES - 1)))
    o8, kwo, vwo = _sample_attn(page_table, idx, q8u, pool_t(ck_slc), pool_t(cv_slc), col0(ks), col0(vs),
                                pool_t(ck_win), pool_t(cv_win), col0(kw), col0(vw), o_cmp, ngb)
    o = o8[:, :, :GROUP, :].transpose(0, 2, 1, 3).reshape(n, D_Q).astype(jnp.bfloat16)
    return o, kwo.transpose(0, 3, 1, 2), vwo.transpose(0, 3, 1, 2)


LRU_PAIRS = D_RNN // LANES


def _lru_gates(xc, wa_ref, ba, wx_ref, bx, lam):
    f32 = jnp.float32
    xb = xc.astype(jnp.bfloat16)
    rg = jnp.concatenate([jnp.dot(xb[:, j * LANES:(j + 1) * LANES], wa_ref[j], preferred_element_type=f32)
                          for j in range(LRU_PAIRS)], axis=-1)
    ig = jnp.concatenate([jnp.dot(xb[:, j * LANES:(j + 1) * LANES], wx_ref[j], preferred_element_type=f32)
                          for j in range(LRU_PAIRS)], axis=-1)
    r = jax.nn.sigmoid(rg + ba)
    i = jax.nn.sigmoid(ig + bx)
    log_a = -LRU_C * r * jax.nn.softplus(-lam)
    a = jnp.exp(log_a)
    u = jnp.sqrt(-jnp.tanh(log_a) * (a * a + 1.0)) * (i * xc)
    return a, u


def _lru_prompt_kernel(arag_ref, cbuf_ref, h0_ref, cw_ref, cb_ref, wa_ref, ba_ref, wx_ref, bx_ref, lam_ref,
                       y_ref, hl_ref, prev_scr, h_scr, hbuf):
    f32 = jnp.float32
    t = arag_ref.shape[0]

    @pl.when(pl.program_id(1) == 0)
    def _():
        prev_scr[...] = cbuf_ref[0]
        h_scr[...] = h0_ref[0]

    ar = arag_ref[:, :D_RNN]
    row8 = lax.broadcasted_iota(jnp.int32, (SUBLANES, 1), 0)
    prev = prev_scr[...]
    xc = cb_ref[...] + cw_ref[CONV_W - 1:CONV_W, :] * ar
    for k in range(1, CONV_W):
        rolled = pltpu.roll(ar, k, 0)
        head = jnp.where(row8 < k, pltpu.roll(prev, k, 0), rolled[0:SUBLANES])
        shifted = jnp.concatenate([head, rolled[SUBLANES:]], axis=0)
        xc = xc + cw_ref[CONV_W - 1 - k:CONV_W - k, :] * shifted
    prev_scr[...] = ar[t - SUBLANES:]

    a, u = _lru_gates(xc, wa_ref, ba_ref[...], wx_ref, bx_ref[...], lam_ref[...])

    rowm = lax.broadcasted_iota(jnp.int32, (t, 1), 0) & (SUBLANES - 1)
    for s in (1, 2, 4):
        ok = rowm >= s
        u = jnp.where(ok, a * pltpu.roll(u, s, 0) + u, u)
        a = jnp.where(ok, a * pltpu.roll(a, s, 0), a)
    h = h_scr[...]
    for g in range(t // SUBLANES):
        rs = slice(g * SUBLANES, (g + 1) * SUBLANES)
        hg = a[rs] * h + u[rs]
        hbuf[rs, :] = hg
        h = jnp.broadcast_to(hg[SUBLANES - 1:SUBLANES, :], (SUBLANES, D_RNN))
    h_scr[...] = h
    hl_ref[0] = h
    y_ref[...] = (hbuf[...] * jax.nn.gelu(arag_ref[:, D_RNN:])).astype(y_ref.dtype)


def _lru_pair_weights(w):
    w4 = w.reshape(LRU_PAIRS, 2, LRU_BW, LRU_BW)
    eye = jnp.eye(2, dtype=w.dtype)
    return jnp.einsum('jaio,ab->jaibo', w4, eye).reshape(LRU_PAIRS, LANES, LANES).astype(jnp.bfloat16)


def _lru_weights(conv_w, conv_b, wa, ba, wx, bx, lam):
    row = lambda v: v.reshape(1, D_RNN).astype(jnp.float32)
    return (conv_w.astype(jnp.float32), row(conv_b), _lru_pair_weights(wa), row(ba), _lru_pair_weights(wx), row(bx),
            row(lam))


def _lru_prompt(arag, conv_buf, h0, lw, b, s, t):
    assert s % t == 0 and t % SUBLANES == 0
    nt = s // t
    cbuf8 = jnp.pad(conv_buf.astype(jnp.float32), ((0, 0), (SUBLANES - (CONV_W - 1), 0), (0, 0)))
    h08 = jnp.broadcast_to(h0.astype(jnp.float32)[:, None, :], (b, SUBLANES, D_RNN))
    const2 = lambda bi, i: (0, 0)
    const3 = lambda bi, i: (0, 0, 0)
    per_b = lambda bi, i: (bi, 0, 0)
    y, hl = pl.pallas_call(
        _lru_prompt_kernel,
        grid=(b, nt),
        in_specs=[
            pl.BlockSpec((t, 2 * D_RNN), lambda bi, i: (bi * nt + i, 0)),
            pl.BlockSpec((1, SUBLANES, D_RNN), per_b),
            pl.BlockSpec((1, SUBLANES, D_RNN), per_b),
            pl.BlockSpec((CONV_W, D_RNN), const2),
            pl.BlockSpec((1, D_RNN), const2),
            pl.BlockSpec((LRU_PAIRS, LANES, LANES), const3),
            pl.BlockSpec((1, D_RNN), const2),
            pl.BlockSpec((LRU_PAIRS, LANES, LANES), const3),
            pl.BlockSpec((1, D_RNN), const2),
            pl.BlockSpec((1, D_RNN), const2),
        ],
        out_specs=(pl.BlockSpec((t, D_RNN), lambda bi, i: (bi * nt + i, 0)),
                   pl.BlockSpec((1, SUBLANES, D_RNN), per_b)),
        out_shape=(jax.ShapeDtypeStruct((b * s, D_RNN), jnp.bfloat16),
                   jax.ShapeDtypeStruct((b, SUBLANES, D_RNN), jnp.float32)),
        scratch_shapes=[pltpu.VMEM((SUBLANES, D_RNN), jnp.float32), pltpu.VMEM((SUBLANES, D_RNN), jnp.float32),
                        pltpu.VMEM((t, D_RNN), jnp.float32)],
        compiler_params=pltpu.CompilerParams(dimension_semantics=("arbitrary", "arbitrary"),
                                             vmem_limit_bytes=VMEM_LIMIT),
        name="lru_prompt",
    )(arag, cbuf8, h08, *lw)
    return y, hl[:, 0, :]


def _lru_step_kernel(arag_ref, c0_ref, c1_ref, c2_ref, h0_ref, cw_ref, cb_ref, wa_ref, ba_ref, wx_ref, bx_ref,
                     lam_ref, y_ref, h_ref):
    ar = arag_ref[:, :D_RNN]
    xc = (cb_ref[...] + cw_ref[0:1, :] * c0_ref[...] + cw_ref[1:2, :] * c1_ref[...] + cw_ref[2:3, :] * c2_ref[...]
          + cw_ref[3:4, :] * ar)
    a, u = _lru_gates(xc, wa_ref, ba_ref[...], wx_ref, bx_ref[...], lam_ref[...])
    h = a * h0_ref[...] + u
    h_ref[...] = h
    y_ref[...] = (h * jax.nn.gelu(arag_ref[:, D_RNN:])).astype(y_ref.dtype)


def _lru_step(arag, conv_buf, h0, lw):
    n = arag.shape[0]
    f32 = jnp.float32
    cb = conv_buf.astype(f32)
    return pl.pallas_call(
        _lru_step_kernel,
        out_shape=(jax.ShapeDtypeStruct((n, D_RNN), jnp.bfloat16), jax.ShapeDtypeStruct((n, D_RNN), f32)),
        compiler_params=pltpu.CompilerParams(vmem_limit_bytes=VMEM_LIMIT),
        name="lru_step",
    )(arag, cb[:, 0], cb[:, 1], cb[:, 2], h0.astype(f32), *lw)


def _merge_router_kernel(x_ref, ylru_ref, onsa_ref, gab_ref, wa_ref, wb_ref, wo_ref, gffn_ref, rw_ref, rb_ref,
                         tri_ref, xn_ref, hn_ref, te_ref, tr_ref, tg_ref, cnt_ref, run_scr):
    f32, bf16 = jnp.float32, jnp.bfloat16
    tm = x_ref.shape[0]

    @pl.when(pl.program_id(0) == 0)
    def _():
        run_scr[...] = jnp.zeros_like(run_scr)

    ya = jnp.dot(ylru_ref[...], wa_ref[...], preferred_element_type=f32)
    yb = jnp.dot(onsa_ref[...], wb_ref[...], preferred_element_type=f32)
    mix = jax.nn.sigmoid(gab_ref[:, :D_MODEL]) * ya + jax.nn.sigmoid(gab_ref[:, D_MODEL:]) * yb
    xn = x_ref[...] + jnp.dot(mix.astype(bf16), wo_ref[...], preferred_element_type=f32)
    xn_ref[...] = xn
    hn = xn * lax.rsqrt(jnp.mean(xn * xn, axis=-1, keepdims=True) + EPS) * gffn_ref[...]
    hn_ref[...] = hn

    hn_hi = hn.astype(bf16)
    hn_lo = (hn - hn_hi.astype(f32)).astype(bf16)
    logits = (jnp.dot(hn_hi, rw_ref[0], preferred_element_type=f32)
              + jnp.dot(hn_lo, rw_ref[0], preferred_element_type=f32)
              + jnp.dot(hn_hi, rw_ref[1], preferred_element_type=f32)) + rb_ref[...]
    lane = lax.broadcasted_iota(jnp.int32, (tm, LANES), 1)
    ninf = -jnp.inf
    l = jnp.where(lane < N_EXPERTS, logits, ninf)
    onehots, m0 = [], None
    for k in range(TOP_K):
        m = jnp.max(l, axis=-1, keepdims=True)
        idx = jnp.min(jnp.where(l == m, lane, LANES), axis=-1, keepdims=True)
        oh = lane == idx
        onehots.append((oh, idx))
        m0 = m if k == 0 else m0
        l = jnp.where(oh, ninf, l)
    sel = onehots[0][0] | onehots[1][0] | onehots[2][0] | onehots[3][0]
    e = jnp.where(sel, jnp.exp(logits - m0), 0.0)
    gates = e / jnp.sum(e, axis=-1, keepdims=True)

    self = jnp.where(sel, 1.0, 0.0)
    incl = jnp.dot(tri_ref[...], self.astype(bf16), preferred_element_type=f32)
    run = run_scr[0:1, :]
    rank = run + incl - self
    total = run + incl[tm - 1:tm, :]
    run_scr[...] = jnp.broadcast_to(total, run_scr.shape)
    cnt_ref[...] = jnp.broadcast_to(total, cnt_ref.shape)

    te = jnp.zeros((tm, LANES), jnp.int32)
    tr = jnp.zeros((tm, LANES), f32)
    tg = jnp.zeros((tm, LANES), f32)
    for k, (oh, idx) in enumerate(onehots):
        te = jnp.where(lane == k, idx, te)
        tr = jnp.where(lane == k, jnp.sum(jnp.where(oh, rank, 0.0), axis=-1, keepdims=True), tr)
        tg = jnp.where(lane == k, jnp.sum(jnp.where(oh, gates, 0.0), axis=-1, keepdims=True), tg)
    te_ref[...] = te
    tr_ref[...] = tr.astype(jnp.int32)
    tg_ref[...] = tg


def _merge_router(x2d, ylru, onsa, gab, wa, wb, wo, g_ffn, rw_pad, rb_pad, tm):
    m = x2d.shape[0]
    assert m % tm == 0
    f32 = jnp.float32
    row = lambda i: (i, 0)
    const = lambda i: (0, 0)
    tri = jnp.asarray(np.tril(np.ones((tm, tm), np.float32)), dtype=jnp.bfloat16)
    lane_out = lambda dt: (jax.ShapeDtypeStruct((m, LANES), dt), pl.BlockSpec((tm, LANES), row))
    outs = [
        (jax.ShapeDtypeStruct((m, D_MODEL), f32), pl.BlockSpec((tm, D_MODEL), row)),
        (jax.ShapeDtypeStruct((m, D_MODEL), f32), pl.BlockSpec((tm, D_MODEL), row)),
        lane_out(jnp.int32), lane_out(jnp.int32), lane_out(f32),
        (jax.ShapeDtypeStruct((8, LANES), f32), pl.BlockSpec((8, LANES), const)),
    ]
    return pl.pallas_call(
        _merge_router_kernel,
        grid=(m // tm,),
        in_specs=[
            pl.BlockSpec((tm, D_MODEL), row),
            pl.BlockSpec((tm, D_RNN), row),
            pl.BlockSpec((tm, D_Q), row),
            pl.BlockSpec((tm, 2 * D_MODEL), row),
            pl.BlockSpec((D_RNN, D_MODEL), const),
            pl.BlockSpec((D_Q, D_MODEL), const),
            pl.BlockSpec((D_MODEL, D_MODEL), const),
            pl.BlockSpec((1, D_MODEL), const),
            pl.BlockSpec((2, D_MODEL, LANES), lambda i: (0, 0, 0)),
            pl.BlockSpec((1, LANES), const),
            pl.BlockSpec((tm, tm), const),
        ],
        out_specs=tuple(o[1] for o in outs),
        out_shape=tuple(o[0] for o in outs),
        scratch_shapes=[pltpu.VMEM((8, LANES), f32)],
        compiler_params=pltpu.CompilerParams(dimension_semantics=("arbitrary",), vmem_limit_bytes=VMEM_LIMIT),
        name="merge_router",
    )(x2d, ylru, onsa, gab, wa, wb, wo, g_ffn.reshape(1, D_MODEL), rw_pad, rb_pad, tri)


MOE_BLK = 512


def _moe_dispatch_kernel(dest_ref, hn_ref, xs_in_ref, xs_ref, sem):
    del xs_in_ref
    tm = hn_ref.shape[0]

    def issue(r, _):
        for k in range(TOP_K):
            d = dest_ref[0, 0, r * TOP_K + k]
            pltpu.make_async_copy(hn_ref.at[pl.ds(r, 1)], xs_ref.at[pl.ds(d, 1)], sem).start()
        return 0

    lax.fori_loop(0, tm, issue, 0)
    pltpu.make_async_copy(xs_ref.at[pl.ds(0, tm * TOP_K)], xs_ref.at[pl.ds(0, tm * TOP_K)], sem).wait()


def _moe_dispatch(hn, dest, n_slots, tm):
    m = hn.shape[0]
    nt = m // tm
    xs0 = jnp.zeros((n_slots, D_MODEL), hn.dtype)
    return pl.pallas_call(
        _moe_dispatch_kernel,
        grid=(nt,),
        in_specs=[
            pl.BlockSpec((1, 1, tm * TOP_K), lambda i: (i, 0, 0), memory_space=pltpu.SMEM),
            pl.BlockSpec((tm, D_MODEL), lambda i: (i, 0)),
            pl.BlockSpec(memory_space=pl.ANY),
        ],
        out_specs=pl.BlockSpec(memory_space=pl.ANY),
        out_shape=jax.ShapeDtypeStruct((n_slots, D_MODEL), hn.dtype),
        scratch_shapes=[pltpu.SemaphoreType.DMA(())],
        input_output_aliases={2: 0},
        compiler_params=pltpu.CompilerParams(dimension_semantics=("arbitrary",), vmem_limit_bytes=VMEM_LIMIT),
        name="moe_dispatch",
    )(dest.reshape(nt, 1, tm * TOP_K), hn, xs0)


def _moe_experts_kernel(te_ref, nused_ref, xs_ref, wup_ref, bup_ref, wdn_ref, bdn_ref, ys_ref, wup_scr, wdn_scr):
    f32, bf16 = jnp.float32, jnp.bfloat16
    j = pl.program_id(0)
    changed = jnp.logical_or(j == 0, te_ref[j] != te_ref[jnp.maximum(j - 1, 0)])

    @pl.when(jnp.logical_and(j < nused_ref[0], changed))
    def _():
        wup_scr[...] = wup_ref[0].astype(bf16)
        wdn_scr[...] = wdn_ref[0].astype(bf16)

    @pl.when(j < nused_ref[0])
    def _():
        x = xs_ref[...].astype(bf16)
        gu = jnp.dot(x, wup_scr[...], preferred_element_type=f32) + bup_ref[0]
        a = jnp.minimum(gu[:, :D_FF], SWIGLU_LIMIT)
        u = jnp.clip(gu[:, D_FF:], -SWIGLU_LIMIT, SWIGLU_LIMIT)
        act = (u + 1.0) * a * jax.nn.sigmoid(SWIGLU_ALPHA * a)
        ys_ref[...] = jnp.dot(act.astype(bf16), wdn_scr[...], preferred_element_type=f32) + bdn_ref[0]


def _moe_experts(xs, tile_expert, n_used, w_up, b_up, w_down, b_down):
    n_slots = xs.shape[0]
    nt = n_slots // MOE_BLK
    tile = lambda j, te, nu: (jnp.minimum(j, nu[0] - 1), 0)
    wsel = lambda j, te, nu: (te[jnp.minimum(j, nu[0] - 1)], 0, 0)
    grid_spec = pltpu.PrefetchScalarGridSpec(
        num_scalar_prefetch=2,
        grid=(nt,),
        in_specs=[
            pl.BlockSpec((MOE_BLK, D_MODEL), tile),
            pl.BlockSpec((1, D_MODEL, 2 * D_FF), wsel),
            pl.BlockSpec((1, 1, 2 * D_FF), wsel),
            pl.BlockSpec((1, D_FF, D_MODEL), wsel),
            pl.BlockSpec((1, 1, D_MODEL), wsel),
        ],
        out_specs=pl.BlockSpec((MOE_BLK, D_MODEL), tile),
        scratch_shapes=[pltpu.VMEM((D_MODEL, 2 * D_FF), jnp.bfloat16), pltpu.VMEM((D_FF, D_MODEL), jnp.bfloat16)],
    )
    return pl.pallas_call(
        _moe_experts_kernel,
        grid_spec=grid_spec,
        out_shape=jax.ShapeDtypeStruct((n_slots, D_MODEL), jnp.float32),
        compiler_params=pltpu.CompilerParams(dimension_semantics=("arbitrary",), vmem_limit_bytes=VMEM_LIMIT),
        name="moe_experts",
    )(tile_expert, n_used, xs, w_up, b_up.reshape(N_EXPERTS, 1, 2 * D_FF), w_down,
      b_down.reshape(N_EXPERTS, 1, D_MODEL))


def _moe_combine_kernel(dest_ref, xn_ref, tg_ref, ys_ref, y_ref, buf, sem):
    tm = xn_ref.shape[0]

    def issue(r, _):
        for k in range(TOP_K):
            d = dest_ref[0, 0, r * TOP_K + k]
            pltpu.make_async_copy(ys_ref.at[pl.ds(d, 1)], buf.at[k, pl.ds(r, 1)], sem).start()
        return 0

    lax.fori_loop(0, tm, issue, 0)
    pltpu.make_async_copy(ys_ref.at[pl.ds(0, tm * TOP_K)], ys_ref.at[pl.ds(0, tm * TOP_K)], sem).wait()
    tg = tg_ref[...]
    acc = xn_ref[...]
    for k in range(TOP_K):
        acc = acc + tg[:, k:k + 1] * buf[k]
    y_ref[...] = acc


def _moe_combine(xn, tg, ys, dest, tm):
    m = xn.shape[0]
    nt = m // tm
    return pl.pallas_call(
        _moe_combine_kernel,
        grid=(nt,),
        in_specs=[
            pl.BlockSpec((1, 1, tm * TOP_K), lambda i: (i, 0, 0), memory_space=pltpu.SMEM),
            pl.BlockSpec((tm, D_MODEL), lambda i: (i, 0)),
            pl.BlockSpec((tm, LANES), lambda i: (i, 0)),
            pl.BlockSpec(memory_space=pl.ANY),
        ],
        out_specs=pl.BlockSpec((tm, D_MODEL), lambda i: (i, 0)),
        out_shape=jax.ShapeDtypeStruct((m, D_MODEL), jnp.float32),
        scratch_shapes=[pltpu.VMEM((TOP_K, tm, D_MODEL), jnp.float32), pltpu.SemaphoreType.DMA(())],
        compiler_params=pltpu.CompilerParams(dimension_semantics=("arbitrary",), vmem_limit_bytes=VMEM_LIMIT),
        name="moe_combine",
    )(dest.reshape(nt, 1, tm * TOP_K), xn, tg, ys)


def _moe_layer(xn, hn, te, tr, tg, cnt, w_up, b_up, w_down, b_down, tm):
    m = xn.shape[0]
    n_tiles = -(-m * TOP_K // MOE_BLK) + N_EXPERTS
    counts = cnt[0, :N_EXPERTS].astype(jnp.int32)
    padded = (counts + MOE_BLK - 1) // MOE_BLK * MOE_BLK
    pad_end = jnp.cumsum(padded)
    pad_start = pad_end - padded
    dest = (pad_start[te[:, :TOP_K]] + tr[:, :TOP_K]).astype(jnp.int32)
    tile_start = jnp.arange(n_tiles, dtype=jnp.int32) * MOE_BLK
    tile_expert = jnp.minimum(jnp.sum(pad_end[None, :] <= tile_start[:, None], axis=1), N_EXPERTS - 1).astype(jnp.int32)
    n_used = (pad_end[-1:] // MOE_BLK).astype(jnp.int32)
    xs = _moe_dispatch(hn, dest, n_tiles * MOE_BLK, tm)
    ys = _moe_experts(xs, tile_expert, n_used, w_up, b_up, w_down, b_down)
    return _moe_combine(xn, tg, ys, dest, tm)


def _masked_softmax(s, mask):
    s = jnp.where(mask, s.astype(jnp.float32), NEG)
    m = jnp.max(s, axis=-1, keepdims=True)
    p = jnp.where(mask, jnp.exp(s - m), 0.0)
    return p / jnp.maximum(jnp.sum(p, axis=-1, keepdims=True), 1e-30)


def _merge_branches(y_lru, o_nsa, ga, gb, w_proj_a, w_proj_b, w_out):
    return (jax.nn.sigmoid(ga) * (y_lru @ w_proj_a) + jax.nn.sigmoid(gb) * (o_nsa @ w_proj_b)) @ w_out


def _lin_combine(c1, c2):
    a1, b1 = c1
    a2, b2 = c2
    return a1 * a2, a2 * b1 + b2


def _rglru(xr, gate_in, conv_buf, h0, conv_w, conv_b, wa, ba, wx, bx, lam):
    n, s, _ = xr.shape
    xp = jnp.concatenate([conv_buf.astype(xr.dtype), xr], axis=1)
    xc = conv_b + sum(conv_w[k] * xp[:, k:k + s] for k in range(CONV_W))
    xb = xc.reshape(n, s, LRU_BLOCKS, LRU_BW)
    r = jax.nn.sigmoid((jnp.einsum('nsbi,bij->nsbj', xb, wa) + ba).astype(jnp.float32)).reshape(n, s, D_RNN)
    i = jax.nn.sigmoid((jnp.einsum('nsbi,bij->nsbj', xb, wx) + bx).astype(jnp.float32)).reshape(n, s, D_RNN)
    log_a = -LRU_C * r * jax.nn.softplus(-lam.astype(jnp.float32))
    a = jnp.exp(log_a)
    u = jnp.sqrt(-jnp.expm1(2.0 * log_a)) * (i * xc.astype(jnp.float32))
    u = u.at[:, 0].add(a[:, 0] * h0.astype(jnp.float32))
    _, h = lax.associative_scan(_lin_combine, (a, u), axis=1)
    y = (h * jax.nn.gelu(gate_in.astype(jnp.float32))).astype(xr.dtype)
    return y, xp[:, s:], h[:, -1]


def _compress(rows, pos_emb, w1, w2):
    n, l = rows.shape[:2]
    r_ = CMP_LEN // CMP_STRIDE
    nch = l // CMP_STRIDE
    ch = rows[:, :nch * CMP_STRIDE].reshape(n, nch, CMP_STRIDE, N_KV, HEAD_DIM)
    w1b = w1.reshape(r_, CMP_STRIDE, HEAD_DIM, CMP_HID)
    part = jnp.einsum('nclhd,pldf->pnchf', ch, w1b)
    nc = nch - r_ + 1
    hid = jnp.einsum('ld,ldf->f', pos_emb, w1) + sum(part[p, :, p:p + nc] for p in range(r_))
    return jnp.einsum('nchf,fd->nchd', jax.nn.gelu(hid), w2)


def _block_scores(p_cmp, nsb):
    nc = p_cmp.shape[-1]
    ratio = SLC_BLOCK // CMP_STRIDE
    ovl = CMP_LEN // CMP_STRIDE - 1
    widths = [(0, 0)] * (p_cmp.ndim - 1) + [(ovl, ratio * (nsb + 1) - nc - ovl)]
    pp = jnp.pad(p_cmp, widths).reshape(*p_cmp.shape[:-1], nsb + 1, ratio)
    return pp[..., :nsb, :].sum(-1) + pp[..., 1:, :ovl].sum(-1)


def _select_blocks(scores, pos, nsb):
    j = jnp.arange(nsb, dtype=jnp.int32)
    cur = (pos // SLC_BLOCK)[:, None, None]
    forced = (j == 0) | (j == cur) | (j == cur - 1)
    s = jnp.where(forced, BIG, jnp.where(j <= cur, scores, NEG))
    top, idx = lax.top_k(s, min(N_SEL, nsb))
    return idx, top > NEG / 2


def _nsa_attend(q, pos, cmp_k, cmp_v, gather_slc, nsb, win_k, win_v, win_pos, gates):
    n, qn = q.shape[:2]
    nc = cmp_k.shape[1]
    s = jnp.einsum('nqhgd,nchd->nqhgc', q, cmp_k)
    cmp_end = jnp.arange(nc, dtype=jnp.int32) * CMP_STRIDE + CMP_LEN - 1
    p_c = _masked_softmax(s, cmp_end <= pos[:, None, None, None])
    o_cmp = jnp.einsum('nqhgc,nchd->nqhgd', p_c.astype(cmp_v.dtype), cmp_v)
    idx, valid = _select_blocks(_block_scores(p_c.sum(3), nsb), pos, nsb)
    k_g, v_g = gather_slc(idx)
    kk = idx.shape[-1] * SLC_BLOCK
    kpos = idx[..., None] * SLC_BLOCK + jnp.arange(SLC_BLOCK, dtype=jnp.int32)
    smask = (valid[..., None] & (kpos <= pos[:, None, None, None])).reshape(n, qn, N_KV, 1, kk)
    s = jnp.einsum('nqhgd,nqhkd->nqhgk', q, k_g.reshape(n, qn, N_KV, kk, HEAD_DIM))
    p_s = _masked_softmax(s, smask)
    o_slc = jnp.einsum('nqhgk,nqhkd->nqhgd', p_s.astype(v_g.dtype), v_g.reshape(n, qn, N_KV, kk, HEAD_DIM))
    s = jnp.einsum('nqhgd,nkhd->nqhgk', q, win_k)
    dpos = pos[:, None] - win_pos[None, :]
    wmask = (dpos >= 0) & (dpos < WINDOW) & (win_pos >= 0)[None, :]
    p_w = _masked_softmax(s, wmask[:, None, None, :])
    o_win = jnp.einsum('nqhgk,nkhd->nqhgd', p_w.astype(win_v.dtype), win_v)
    g = jax.nn.sigmoid(gates)
    return g[..., 0:1] * o_cmp + g[..., 1:2] * o_slc + g[..., 2:3] * o_win


def _nsa_prompt(q, kc, vc, ks, vs, kw, vw, ng, pos_k, w1_k, w2_k, pos_v, w1_v, w2_v):
    n, s = q.shape[:2]
    cmp_k = _compress(kc, pos_k, w1_k, w2_k)
    cmp_v = _compress(vc, pos_v, w1_v, w2_v)
    nsb = s // SLC_BLOCK
    kb = ks.reshape(n, nsb, SLC_BLOCK, N_KV, HEAD_DIM)
    vb = vs.reshape(n, nsb, SLC_BLOCK, N_KV, HEAD_DIM)
    bi = jnp.arange(n)[:, None, None, None]
    hi = jnp.arange(N_KV)[None, None, :, None]

    def gather(idx):
        return kb[bi, idx, :, hi], vb[bi, idx, :, hi]

    pad = ((0, 0), (WINDOW, 0), (0, 0), (0, 0))
    kw_pad, vw_pad = jnp.pad(kw, pad), jnp.pad(vw, pad)

    def block(i):
        start = i * Q_BLOCK
        sl = lambda t, size: lax.dynamic_slice_in_dim(t, start, size, axis=1)
        pos = start + jnp.arange(Q_BLOCK, dtype=jnp.int32)
        wpos = start - WINDOW + jnp.arange(WINDOW + Q_BLOCK, dtype=jnp.int32)
        return _nsa_attend(sl(q, Q_BLOCK), pos, cmp_k, cmp_v, gather, nsb,
                           sl(kw_pad, WINDOW + Q_BLOCK), sl(vw_pad, WINDOW + Q_BLOCK), wpos, sl(ng, Q_BLOCK))

    o = lax.map(block, jnp.arange(s // Q_BLOCK, dtype=jnp.int32))
    return jnp.moveaxis(o, 0, 1).reshape(n, s, D_Q)


def _nsa_sample(q, kc, vc, ks, vs, kw, vw, ng, page_table, ck_cmp, cv_cmp, ck_slc, cv_slc, ck_win, cv_win,
                pos_k, w1_k, w2_k, pos_v, w1_v, w2_v):
    n, ds = q.shape[:2]
    pos = PAST_LEN + jnp.arange(ds, dtype=jnp.int32)

    def past(pool):
        return pool[page_table].reshape(n, PAST_LEN, N_KV, HEAD_DIM)

    cmp_k = _compress(jnp.concatenate([past(ck_cmp), kc], axis=1), pos_k, w1_k, w2_k)
    cmp_v = _compress(jnp.concatenate([past(cv_cmp), vc], axis=1), pos_v, w1_v, w2_v)
    past_blocks = PAST_LEN // SLC_BLOCK
    new_blocks = -(-ds // SLC_BLOCK)
    nsb = past_blocks + new_blocks
    bpp = PAGE_SIZE // SLC_BLOCK
    view = lambda pool: pool.reshape(pool.shape[0], bpp, SLC_BLOCK, N_KV, HEAD_DIM)
    pk, pv = view(ck_slc), view(cv_slc)
    padn = ((0, 0), (0, new_blocks * SLC_BLOCK - ds), (0, 0), (0, 0))
    nk = jnp.pad(ks, padn).reshape(n, new_blocks, SLC_BLOCK, N_KV, HEAD_DIM)
    nv = jnp.pad(vs, padn).reshape(n, new_blocks, SLC_BLOCK, N_KV, HEAD_DIM)
    bi = jnp.arange(n)[:, None, None, None]
    hi = jnp.arange(N_KV)[None, None, :, None]

    def gather(idx):
        jp = jnp.minimum(idx, past_blocks - 1)
        phys = page_table[bi, jp // bpp]
        sub = jp % bpp
        jn = jnp.clip(idx - past_blocks, 0, new_blocks - 1)
        is_new = (idx >= past_blocks)[..., None, None]
        return (jnp.where(is_new, nk[bi, jn, :, hi], pk[phys, sub, :, hi]),
                jnp.where(is_new, nv[bi, jn, :, hi], pv[phys, sub, :, hi]))

    wb = ck_win.shape[1]
    wk = jnp.concatenate([ck_win, kw], axis=1)
    wv = jnp.concatenate([cv_win, vw], axis=1)
    wpos = PAST_LEN - wb + jnp.arange(wb + ds, dtype=jnp.int32)
    o = _nsa_attend(q, pos, cmp_k, cmp_v, gather, nsb, wk, wv, wpos, ng)
    return o.reshape(n, ds, D_Q), wk[:, -wb:], wv[:, -wb:]


def _rms_norm(x, g):
    xf = x.astype(jnp.float32)
    y = xf * lax.rsqrt(jnp.mean(xf * xf, axis=-1, keepdims=True) + EPS)
    return (y * g.astype(jnp.float32)).astype(x.dtype)


def _moe(h, router_w, router_b, w_up, b_up, w_down, b_down):
    shp = h.shape
    t = h.reshape(-1, D_MODEL)
    n_tok = t.shape[0]
    logits = (t @ router_w).astype(jnp.float32) + router_b.astype(jnp.float32)
    top_v, top_e = lax.top_k(logits, TOP_K)
    gate = jax.nn.softmax(top_v, axis=-1)
    n_asg = n_tok * TOP_K
    blk = int(min(512, max(8, n_asg // N_EXPERTS)))
    n_blocks = -(-n_asg // blk) + N_EXPERTS
    e_flat = top_e.reshape(-1)
    tok_flat = jnp.repeat(jnp.arange(n_tok, dtype=jnp.int32), TOP_K)
    g_flat = gate.reshape(-1)
    order = jnp.argsort(e_flat)
    e_sorted = e_flat[order]
    counts = jnp.bincount(e_flat, length=N_EXPERTS)
    padded = (counts + blk - 1) // blk * blk
    pad_end = jnp.cumsum(padded)
    pad_start = pad_end - padded
    starts = jnp.cumsum(counts) - counts
    dest = pad_start[e_sorted] + jnp.arange(n_asg, dtype=jnp.int32) - starts[e_sorted]
    slot_tok = jnp.full((n_blocks * blk,), n_tok, jnp.int32).at[dest].set(tok_flat[order])
    slot_gate = jnp.zeros((n_blocks * blk,), jnp.float32).at[dest].set(g_flat[order])
    blk_expert = jnp.minimum(jnp.searchsorted(pad_end, jnp.arange(n_blocks, dtype=jnp.int32) * blk, side='right'),
                             N_EXPERTS - 1)
    t_pad = jnp.concatenate([t, jnp.zeros((1, D_MODEL), t.dtype)], axis=0)

    def run_block(args):
        tok, g, e = args
        gu = t_pad[tok] @ w_up[e] + b_up[e]
        a = jnp.minimum(gu[:, :D_FF], SWIGLU_LIMIT)
        u = jnp.clip(gu[:, D_FF:], -SWIGLU_LIMIT, SWIGLU_LIMIT)
        y = ((u + 1.0) * a * jax.nn.sigmoid(SWIGLU_ALPHA * a)) @ w_down[e] + b_down[e]
        return (y * g[:, None]).astype(t.dtype)

    ys = lax.map(run_block, (slot_tok.reshape(n_blocks, blk), slot_gate.reshape(n_blocks, blk), blk_expert))
    out = jnp.zeros((n_tok + 1, D_MODEL), t.dtype).at[slot_tok].add(ys.reshape(-1, D_MODEL))
    return out[:n_tok].reshape(shp)


def _pad_w_in(w):
    c0 = 2 * D_RNN
    wq = w[:, c0:c0 + D_Q].reshape(D_MODEL, N_KV, GROUP, 1, HEAD_DIM)
    sel = jnp.eye(N_KV, dtype=w.dtype).reshape(N_KV, 1, N_KV, 1)
    wq = (wq * sel).reshape(D_MODEL, D_QP)
    c1 = c0 + D_Q
    c2 = c1 + 6 * D_KV + N_NG
    pad = jnp.zeros((D_MODEL, LANES - N_NG), w.dtype)
    return jnp.concatenate([w[:, :c0], wq, w[:, c1:c2], pad, w[:, c2:]], axis=1).astype(jnp.bfloat16)


def _unpad_q(q_pad, n, s):
    qp = q_pad.reshape(n, s, N_KV, GROUP, N_KV, HEAD_DIM)
    return jnp.stack([qp[:, :, h, :, h, :] for h in range(N_KV)], axis=2)


def _project(x, pos_rows, tm, g_mix, w_pad, g_q, g_k3):
    n, s, _ = x.shape
    outs = _input_projection(x.reshape(n * s, D_MODEL), pos_rows, tm, g_mix, w_pad, g_q, g_k3)
    arag, q, kc, vc, ks, vs, kw, vw, ng, gab = outs
    heads = lambda t: t.reshape(n, s, N_KV, HEAD_DIM)
    ar = arag[:, :D_RNN].reshape(n, s, D_RNN)
    ag = arag[:, D_RNN:].reshape(n, s, D_RNN)
    q = _unpad_q(q, n, s)
    ng = ng[:, :N_NG].reshape(n, s, N_KV, GROUP, 3)
    ga = gab[:, :D_MODEL].reshape(n, s, D_MODEL)
    gb = gab[:, D_MODEL:].reshape(n, s, D_MODEL)
    return ar, ag, q, heads(kc), heads(vc), heads(ks), heads(vs), heads(kw), heads(vw), ng, ga, gb


def kernel(x_prompt, x_sample, page_table, cache_k_cmp, cache_v_cmp, cache_k_slc, cache_v_slc,
           cache_k_win, cache_v_win, state_h, state_conv, g_mix, w_in, conv_w, conv_b, lru_wa, lru_ba,
           lru_wx, lru_bx, lru_lambda, g_q, g_k_cmp, g_k_slc, g_k_win, cmp_pos_k, cmp_w1_k, cmp_w2_k,
           cmp_pos_v, cmp_w1_v, cmp_w2_v, w_proj_a, w_proj_b, w_out, g_ffn, router_w, router_b,
           exp_w_up, exp_b_up, exp_w_down, exp_b_down):
    xp, xs = x_prompt, x_sample
    bp, sp = xp.shape[:2]
    bs, ss = xs.shape[:2]
    pos_p = jnp.arange(sp, dtype=jnp.int32)
    pos_s = PAST_LEN + jnp.arange(ss, dtype=jnp.int32)
    depth = w_in.shape[0]
    layer_states = []
    tile2 = lambda g: jnp.concatenate([g, g], axis=-1)
    for l in range(depth):
        cmp_w = (cmp_pos_k[l], cmp_w1_k[l], cmp_w2_k[l], cmp_pos_v[l], cmp_w1_v[l], cmp_w2_v[l])
        lru_w = (conv_w[l], conv_b[l], lru_wa[l], lru_ba[l], lru_wx[l], lru_bx[l], lru_lambda[l])
        w_pad = _pad_w_in(w_in[l])
        gq2 = tile2(g_q[l]).reshape(1, LANES)
        gk3 = jnp.stack([tile2(g_k_cmp[l]), tile2(g_k_slc[l]), tile2(g_k_win[l])])
        cmp_wk = _compress_weights(cmp_pos_k[l], cmp_w1_k[l], cmp_w2_k[l])
        cmp_wv = _compress_weights(cmp_pos_v[l], cmp_w1_v[l], cmp_w2_v[l])
        wb_perm = w_proj_b[l].reshape(N_KV, GROUP, HEAD_DIM, D_MODEL).transpose(1, 0, 2, 3).reshape(D_Q, D_MODEL)
        bf16 = jnp.bfloat16
        lru_k = _lru_weights(*lru_w)
        rw_pad = jnp.pad(router_w[l].astype(jnp.float32), ((0, 0), (0, LANES - N_EXPERTS)))
        rw_hi = rw_pad.astype(bf16)
        rw_pad = jnp.stack([rw_hi, (rw_pad - rw_hi.astype(jnp.float32)).astype(bf16)])
        rb_pad = jnp.pad(router_b[l].astype(jnp.float32), (0, LANES - N_EXPERTS)).reshape(1, LANES)
        merge_w = (w_proj_a[l].astype(bf16), wb_perm.astype(bf16), w_out[l].astype(bf16), g_ffn[l], rw_pad, rb_pad)
        moe_w = (exp_w_up[l], exp_b_up[l], exp_w_down[l], exp_b_down[l])
        arag, q_pad, kc, vc, ks, vs, kw, vw, ng, gab = _input_projection(
            xp.reshape(bp * sp, D_MODEL), pos_p, 256, g_mix[l], w_pad, gq2, gk3)
        rows = lambda t: t.reshape(bp, sp, t.shape[-1])
        heads = lambda t: t.reshape(bp, sp, N_KV, HEAD_DIM)
        conv0 = jnp.zeros((bp, CONV_W - 1, D_RNN), jnp.float32)
        y_lru, p_h = _lru_prompt(arag, conv0, jnp.zeros((bp, D_RNN), jnp.float32), lru_k, bp, sp, 256)
        p_conv = jnp.concatenate([conv0, rows(arag)[:, :, :D_RNN]], axis=1)[:, sp:]
        cmp_k, cmp_v = _compress_prompt(rows(kc), rows(vc), cmp_wk, cmp_wv)
        o_nsa = _nsa_prompt_attention(rows(q_pad), rows(ks), rows(vs), rows(kw), rows(vw), cmp_k, cmp_v, rows(ng))
        xn, hn, te, tr, tg, cnt = _merge_router(xp.reshape(bp * sp, D_MODEL), y_lru, o_nsa.reshape(bp * sp, D_Q),
                                                gab, *merge_w, 256)
        xp = _moe_layer(xn, hn, te, tr, tg, cnt, *moe_w, 256).reshape(bp, sp, D_MODEL)
        wbp = min(WINDOW, sp)
        p_states = (heads(kc), heads(vc), heads(ks), heads(vs), heads(kw)[:, -wbp:], heads(vw)[:, -wbp:],
                    p_h, p_conv)
        assert ss == 1
        pos_rows_s = jnp.broadcast_to(pos_s[None, :], (bs, ss)).reshape(-1)
        arag, q_pad, kc, vc, ks, vs, kw, vw, ng, gab = _input_projection(
            xs.reshape(bs, D_MODEL), pos_rows_s, bs, g_mix[l], w_pad, gq2, gk3)
        heads = lambda t: t.reshape(bs, ss, N_KV, HEAD_DIM)
        y_lru, s_h_l = _lru_step(arag, state_conv[l], state_h[l], lru_k)
        s_conv_l = jnp.concatenate([state_conv[l][:, 1:], arag[:, None, :D_RNN]], axis=1)
        o_nsa, s_kw_l, s_vw_l = _nsa_sample_paged(
            q_pad, ks, vs, kw, vw, ng, page_table, cache_k_cmp[l], cache_v_cmp[l], cache_k_slc[l], cache_v_slc[l],
            cache_k_win[l], cache_v_win[l], cmp_wk, cmp_wv)
        xn, hn, te, tr, tg, cnt = _merge_router(xs.reshape(bs, D_MODEL), y_lru, o_nsa, gab, *merge_w, bs)
        xs = _moe_layer(xn, hn, te, tr, tg, cnt, *moe_w, bs).reshape(bs, ss, D_MODEL)
        s_states = (heads(kc), heads(vc), heads(ks), heads(vs), s_kw_l, s_vw_l, s_h_l, s_conv_l)
        layer_states.append(p_states + s_states)
    (p_k_cmp, p_v_cmp, p_k_slc, p_v_slc, p_k_win, p_v_win, p_h, p_conv,
     s_k_cmp, s_v_cmp, s_k_slc, s_v_slc, s_k_win, s_v_win, s_h, s_conv) = [jnp.stack(z) for z in zip(*layer_states)]
    return (xp, xs, p_k_cmp, p_v_cmp, p_k_slc, p_v_slc, p_k_win, p_v_win, p_h, p_conv,
            s_k_cmp, s_v_cmp, s_k_slc, s_v_slc, s_k_win, s_v_win, s_h, s_conv)
```

```python
import functools
import math

import jax
import jax.numpy as jnp
import numpy as np
from jax import lax
from jax.experimental import pallas as pl
from jax.experimental.pallas import tpu as pltpu

D_MODEL = 1024
PAST_LEN = 8192
PAGE_SIZE = 128
D_RNN = D_MODEL
LRU_BLOCKS = 16
LRU_BW = D_RNN // LRU_BLOCKS
CONV_W = 4
LRU_C = 8.0
N_HEADS = 8
HEAD_DIM = 64
N_KV = 2
GROUP = N_HEADS // N_KV
D_Q = N_HEADS * HEAD_DIM
D_KV = N_KV * HEAD_DIM
ROPE_DIM = HEAD_DIM // 4
ROPE_THETA = 500000.0
CMP_LEN = 32
CMP_STRIDE = 16
CMP_HID = 256
SLC_BLOCK = 64
N_SEL = 16
WINDOW = 512
Q_BLOCK = 64
N_EXPERTS = 32
TOP_K = 4
D_FF = D_MODEL
SWIGLU_LIMIT = 7.0
SWIGLU_ALPHA = 1.702
EPS = 1e-6
NEG = -1e30
BIG = 1e30

LANES = 128
SUBLANES = 8
VMEM_LIMIT = 56 * 1024 * 1024

N_NG = 3 * N_HEADS
D_QP = N_HEADS * LANES
C_AR = 0
C_Q = 2 * D_RNN
C_KV = C_Q + D_QP
C_NG = C_KV + 6 * D_KV
C_GAB = C_NG + LANES
N_IN_PAD = C_GAB + 2 * D_MODEL


def _seg_sum64(x2, ones_blk):
    hi = x2.astype(jnp.bfloat16)
    lo = (x2 - hi.astype(jnp.float32)).astype(jnp.bfloat16)
    return (jnp.dot(hi, ones_blk, preferred_element_type=jnp.float32)
            + jnp.dot(lo, ones_blk, preferred_element_type=jnp.float32))


def _head_norm_rope(y, gain, ones_blk, rc, rs1, rs2):
    ss = _seg_sum64(y * y, ones_blk)
    yn = y * lax.rsqrt(ss * (1.0 / HEAD_DIM) + EPS) * gain
    half = ROPE_DIM // 2
    return yn * rc + pltpu.roll(yn, LANES - half, 1) * rs1 + pltpu.roll(yn, half, 1) * rs2


def _proj_kernel(x_ref, gmix_ref, w_ref, ones_ref, gq_ref, gk_ref, rc_ref, rs1_ref, rs2_ref,
                 arag_ref, q_ref, kc_ref, vc_ref, ks_ref, vs_ref, kw_ref, vw_ref, ng_ref, gab_ref, *kv_t_refs):
    x = x_ref[...]
    xn = x * lax.rsqrt(jnp.mean(x * x, axis=-1, keepdims=True) + EPS) * gmix_ref[...]
    xb = xn.astype(jnp.bfloat16)

    def proj(c0, width):
        return jnp.dot(xb, w_ref[:, c0:c0 + width], preferred_element_type=jnp.float32)

    arag_ref[...] = proj(C_AR, 2 * D_RNN)
    gab_ref[...] = proj(C_GAB, 2 * D_MODEL)
    ng_ref[...] = proj(C_NG, LANES)
    ones_blk = ones_ref[...]
    rc, rs1, rs2 = rc_ref[...], rs1_ref[...], rs2_ref[...]
    scale = HEAD_DIM ** -0.5
    for j in range(N_HEADS):
        y = proj(C_Q + j * LANES, LANES)
        yr = _head_norm_rope(y, gq_ref[...], ones_blk, rc, rs1, rs2)
        q_ref[:, j * LANES:(j + 1) * LANES] = (yr * scale).astype(q_ref.dtype)
    kv_outs = (kc_ref, vc_ref, ks_ref, vs_ref, kw_ref, vw_ref)
    for j in range(3):
        yk = _head_norm_rope(proj(C_KV + 2 * j * LANES, LANES), gk_ref[j:j + 1, :], ones_blk, rc, rs1, rs2)
        yv = proj(C_KV + (2 * j + 1) * LANES, LANES)
        for i, y in ((2 * j, yk), (2 * j + 1, yv)):
            kv_outs[i][...] = y.astype(kv_outs[i].dtype)
            if kv_t_refs:
                kv_t_refs[i][0] = y.T


def _rope_tables(pos):
    half = ROPE_DIM // 2
    inv = ROPE_THETA ** (-jnp.arange(half, dtype=jnp.float32) / half)
    ang = pos.astype(jnp.float32)[:, None] * inv[None, :]
    cos, sin = jnp.cos(ang), jnp.sin(ang)
    p = pos.shape[0]
    ones = jnp.ones((p, HEAD_DIM - ROPE_DIM), jnp.float32)
    zeros = jnp.zeros((p, HEAD_DIM - ROPE_DIM), jnp.float32)
    zh = jnp.zeros((p, half), jnp.float32)
    rc = jnp.concatenate([cos, cos, ones], axis=-1)
    rs1 = jnp.concatenate([-sin, zh, zeros], axis=-1)
    rs2 = jnp.concatenate([zh, sin, zeros], axis=-1)
    tile2 = lambda t: jnp.concatenate([t, t], axis=-1)
    return tile2(rc), tile2(rs1), tile2(rs2)


def _ones_block64():
    i = np.arange(LANES)
    return jnp.asarray((i[:, None] // HEAD_DIM) == (i[None, :] // HEAD_DIM), dtype=jnp.bfloat16)


def _input_projection(x2d, pos_rows, tm, g_mix, w_pad, g_q, g_k3, kv_transposed=False):
    m = x2d.shape[0]
    p = pos_rows.shape[0]
    assert m % tm == 0 and p % tm == 0
    rc, rs1, rs2 = _rope_tables(pos_rows)
    n_tab = p // tm
    row = lambda i: (i, 0)
    const = lambda i: (0, 0)
    tab = lambda i: (i % n_tab, 0)
    f32 = jnp.float32
    kv_dt = jnp.bfloat16 if kv_transposed else f32
    out_shape = (
        jax.ShapeDtypeStruct((m, 2 * D_RNN), f32),
        jax.ShapeDtypeStruct((m, D_QP), jnp.bfloat16),
    ) + tuple(jax.ShapeDtypeStruct((m, D_KV), kv_dt) for _ in range(6)) + (
        jax.ShapeDtypeStruct((m, LANES), f32),
        jax.ShapeDtypeStruct((m, 2 * D_MODEL), f32),
    )
    out_specs = (
        pl.BlockSpec((tm, 2 * D_RNN), row),
        pl.BlockSpec((tm, D_QP), row),
    ) + tuple(pl.BlockSpec((tm, D_KV), row) for _ in range(6)) + (
        pl.BlockSpec((tm, LANES), row),
        pl.BlockSpec((tm, 2 * D_MODEL), row),
    )
    if kv_transposed:
        out_shape += tuple(jax.ShapeDtypeStruct((m // p, D_KV, p), f32) for _ in range(6))
        out_specs += tuple(pl.BlockSpec((1, D_KV, tm), lambda i: (i // n_tab, 0, i % n_tab)) for _ in range(6))
    return pl.pallas_call(
        _proj_kernel,
        grid=(m // tm,),
        in_specs=[
            pl.BlockSpec((tm, D_MODEL), row),
            pl.BlockSpec((1, D_MODEL), const),
            pl.BlockSpec((D_MODEL, N_IN_PAD), const),
            pl.BlockSpec((LANES, LANES), const),
            pl.BlockSpec((1, LANES), const),
            pl.BlockSpec((3, LANES), const),
            pl.BlockSpec((tm, LANES), tab),
            pl.BlockSpec((tm, LANES), tab),
            pl.BlockSpec((tm, LANES), tab),
        ],
        out_specs=out_specs,
        out_shape=out_shape,
        compiler_params=pltpu.CompilerParams(dimension_semantics=("arbitrary",), vmem_limit_bytes=VMEM_LIMIT),
        name="input_projection",
    )(x2d, g_mix.reshape(1, D_MODEL), w_pad, _ones_block64(), g_q, g_k3, rc, rs1, rs2)


SLC_CHUNK = 1024
WIN_SPAN = WINDOW + LANES
ROWS = GROUP * Q_BLOCK


def _dot_nt(a, b):
    return lax.dot_general(a, b, (((1,), (1,)), ((), ())), preferred_element_type=jnp.float32)


def _softmax_rows(s, mask):
    s = jnp.where(mask, s, NEG)
    m = jnp.max(s, axis=-1, keepdims=True)
    p = jnp.where(mask, jnp.exp(s - m), 0.0)
    return p / jnp.maximum(jnp.sum(p, axis=-1, keepdims=True), 1e-30)


def _split3_bf16(x):
    hi = x.astype(jnp.bfloat16)
    r1 = x - hi.astype(jnp.float32)
    mid = r1.astype(jnp.bfloat16)
    lo = (r1 - mid.astype(jnp.float32)).astype(jnp.bfloat16)
    return hi, mid, lo


def _nsa_prompt_kernel(q_ref, ksa_ref, vs_ref, kw_ref, vw_ref, ck_ref, cv_ref, ng_ref, mmt_ref,
                       o_ref, sv_scr):
    f32, bf16 = jnp.float32, jnp.bfloat16
    qb = pl.program_id(1)
    row = lax.broadcasted_iota(jnp.int32, (ROWS, 1), 0)
    pos_t = qb * Q_BLOCK + (row & (Q_BLOCK - 1))
    ncp = ck_ref.shape[1]

    qh = [jnp.concatenate([q_ref[0, :, (h * GROUP + g) * LANES:(h * GROUP + g + 1) * LANES]
                           for g in range(GROUP)], axis=0) for h in range(N_KV)]

    ck, cv = ck_ref[0], cv_ref[0]
    c_end = lax.broadcasted_iota(jnp.int32, (1, ncp), 1) * CMP_STRIDE + (CMP_LEN - 1)
    cmask = c_end <= pos_t
    o_cmp, psum = [], []
    for h in range(N_KV):
        p = _softmax_rows(_dot_nt(qh[h], ck), cmask)
        o_cmp.append(jnp.dot(p.astype(bf16), cv, preferred_element_type=f32))
        psum.append(p[0:Q_BLOCK] + p[Q_BLOCK:2 * Q_BLOCK] + p[2 * Q_BLOCK:3 * Q_BLOCK] + p[3 * Q_BLOCK:])
    ps = jnp.concatenate(psum, axis=0)

    mmt = mmt_ref[...]
    sc = sum(_dot_nt(mmt, part) for part in _split3_bf16(ps))[0:Q_BLOCK]
    jio = lax.broadcasted_iota(jnp.int32, (Q_BLOCK, LANES), 0)
    forced = (jio == 0) | (jio == qb) | (jio == qb - 1)
    sv = jnp.where(forced, BIG, jnp.where(jio <= qb, sc, NEG))
    sv_scr[...] = sv

    def rank_body(i, cnt):
        r = sv_scr[pl.ds(i, 1), :]
        beats = (r > sv) | ((r == sv) & (jio > i))
        return cnt + jnp.where(beats, 1.0, 0.0)

    cnt = lax.fori_loop(0, qb + 1, rank_body, jnp.zeros((Q_BLOCK, LANES), f32))
    bias_lo = jnp.where((cnt < N_SEL) & (sv > NEG / 2), 0.0, NEG)
    lane = lax.broadcasted_iota(jnp.int32, (Q_BLOCK, LANES), 1)
    bias_t = jnp.concatenate([jnp.where(lane >= Q_BLOCK, bias_lo, 0.0),
                              jnp.where(lane < Q_BLOCK, bias_lo, 0.0)], axis=0)
    bias = bias_t.T.astype(bf16)

    kio = lax.broadcasted_iota(jnp.int32, (1, SLC_CHUNK), 1)
    wio = lax.broadcasted_iota(jnp.int32, (1, WIN_SPAN), 1)
    wbase = pl.multiple_of((jnp.maximum(qb - WINDOW // Q_BLOCK, 0) // 2) * LANES, LANES)
    dpos = pos_t - (wbase + wio)
    wmask = (dpos >= 0) & (dpos < WINDOW)
    n_chunks = qb // (SLC_CHUNK // SLC_BLOCK) + 1
    sig = jax.nn.sigmoid(ng_ref[0])
    qa = [qh[h] + jnp.concatenate([bias[h * Q_BLOCK:(h + 1) * Q_BLOCK]] * GROUP, axis=0) for h in range(N_KV)]

    def chunk(c, carry, causal):
        k0 = pl.multiple_of(c * SLC_CHUNK, SLC_CHUNK)
        v = vs_ref[0, pl.ds(k0, SLC_CHUNK), :]
        new = []
        for h in range(N_KV):
            m, l, acc = carry[h]
            s = _dot_nt(qa[h], ksa_ref[0, h, pl.ds(k0, SLC_CHUNK), :])
            if causal:
                s = jnp.where(k0 + kio <= pos_t, s, NEG)
            m_new = jnp.maximum(m, jnp.max(s, axis=-1, keepdims=True))
            alpha = jnp.exp(m - m_new)
            p = jnp.exp(s - m_new)
            l = alpha * l + jnp.sum(p, axis=-1, keepdims=True)
            acc = alpha * acc + jnp.dot(p.astype(bf16), v, preferred_element_type=f32)
            new.append((m_new, l, acc))
        return tuple(new)

    init = (jnp.full((ROWS, 1), NEG, f32), jnp.zeros((ROWS, 1), f32), jnp.zeros((ROWS, LANES), f32))
    carry = lax.fori_loop(0, n_chunks - 1, lambda c, cr: chunk(c, cr, False), (init, init))
    carry = chunk(n_chunks - 1, carry, True)

    out_h = []
    for h in range(N_KV):
        _, l, acc = carry[h]
        o_slc = acc / jnp.maximum(l, 1e-30)

        pw = _softmax_rows(_dot_nt(qh[h], kw_ref[0, pl.ds(wbase, WIN_SPAN), :]), wmask)
        o_win = jnp.dot(pw.astype(bf16), vw_ref[0, pl.ds(wbase, WIN_SPAN), :], preferred_element_type=f32)

        outs = []
        for g in range(GROUP):
            rs = slice(g * Q_BLOCK, (g + 1) * Q_BLOCK)
            c0 = (h * GROUP + g) * 3
            outs.append(sig[:, c0:c0 + 1] * o_cmp[h][rs] + sig[:, c0 + 1:c0 + 2] * o_slc[rs]
                        + sig[:, c0 + 2:c0 + 3] * o_win[rs])
        out_h.append(outs)
    for g in range(GROUP):
        o_ref[0, :, g * LANES:(g + 1) * LANES] = jnp.where(lane < HEAD_DIM, out_h[0][g], out_h[1][g]).astype(o_ref.dtype)


def _block_score_matrix(ncp):
    ratio = SLC_BLOCK // CMP_STRIDE
    ovl = CMP_LEN // CMP_STRIDE - 1
    j = np.arange(LANES)[:, None]
    c = np.arange(ncp)[None, :]
    m = (c >= ratio * j - ovl) & (c <= ratio * j + ratio - 1) & (j < Q_BLOCK) & (c < ncp - 1)
    return jnp.asarray(m, dtype=jnp.bfloat16)


def _nsa_prompt_attention(q_pad, ks, vs, kw, vw, cmp_k, cmp_v, ng):
    b, s, _ = q_pad.shape
    assert s % SLC_CHUNK == 0 and s // SLC_BLOCK <= Q_BLOCK and s >= WIN_SPAN and Q_BLOCK == SLC_BLOCK
    bf16 = jnp.bfloat16
    ncp = cmp_k.shape[1]
    onehot = jax.nn.one_hot(jnp.arange(s) // SLC_BLOCK, Q_BLOCK, dtype=bf16)
    onehot = jnp.broadcast_to(onehot[None], (b, s, Q_BLOCK))
    ksb = ks.astype(bf16)
    ksa = jnp.stack([jnp.concatenate([ksb[:, :, :HEAD_DIM], onehot], axis=-1),
                     jnp.concatenate([onehot, ksb[:, :, HEAD_DIM:]], axis=-1)], axis=1)
    per_b = lambda bi, i: (bi, 0, 0)
    return pl.pallas_call(
        _nsa_prompt_kernel,
        grid=(b, s // Q_BLOCK),
        in_specs=[
            pl.BlockSpec((1, Q_BLOCK, D_QP), lambda bi, i: (bi, i, 0)),
            pl.BlockSpec((1, N_KV, s, LANES), lambda bi, i: (bi, 0, 0, 0)),
            pl.BlockSpec((1, s, LANES), per_b),
            pl.BlockSpec((1, s, LANES), per_b),
            pl.BlockSpec((1, s, LANES), per_b),
            pl.BlockSpec((1, ncp, LANES), per_b),
            pl.BlockSpec((1, ncp, LANES), per_b),
            pl.BlockSpec((1, Q_BLOCK, LANES), lambda bi, i: (bi, i, 0)),
            pl.BlockSpec((LANES, ncp), lambda bi, i: (0, 0)),
        ],
        out_specs=pl.BlockSpec((1, Q_BLOCK, D_Q), lambda bi, i: (bi, i, 0)),
        out_shape=jax.ShapeDtypeStruct((b, s, D_Q), bf16),
        scratch_shapes=[pltpu.VMEM((Q_BLOCK, LANES), jnp.float32)],
        compiler_params=pltpu.CompilerParams(dimension_semantics=("arbitrary", "arbitrary"),
                                             vmem_limit_bytes=VMEM_LIMIT),
        name="nsa_prompt_attention",
    )(q_pad, ksa, vs.astype(bf16), kw.astype(bf16), vw.astype(bf16), cmp_k, cmp_v, ng, _block_score_matrix(ncp))


CHUNK_W = CMP_STRIDE * D_KV
CMP_PARTS = CMP_LEN // CMP_STRIDE
HID2 = N_KV * CMP_HID


def _compress_weights(pos_emb, w1, w2):
    bf16 = jnp.bfloat16
    eye = jnp.eye(N_KV, dtype=w1.dtype)
    w1b = w1.reshape(CMP_PARTS, CMP_STRIDE, HEAD_DIM, CMP_HID)
    w1p = jnp.einsum('pldf,hk->lhdpkf', w1b, eye).reshape(CHUNK_W, CMP_PARTS * HID2).astype(bf16)
    w2p = jnp.einsum('fd,hk->hfkd', w2, eye).reshape(HID2, D_KV).astype(bf16)
    pos8 = jnp.zeros((8, CMP_LEN * HEAD_DIM), bf16).at[0].set(pos_emb.reshape(-1).astype(bf16))
    w1f = w1.reshape(CMP_LEN * HEAD_DIM, CMP_HID).astype(bf16)
    return w1p, w2p, pos8, w1f


def _compress_rows(x, w1p, w2p, pos8, w1f):
    f32 = jnp.float32
    r = x.shape[0]
    part = jnp.dot(x, w1p, preferred_element_type=f32)
    bias = jnp.dot(pos8, w1f, preferred_element_type=f32)[0:1]
    bias = jnp.concatenate([bias] * N_KV, axis=-1)
    hid = bias + part[:, :HID2] + pltpu.roll(part[:, HID2:], r - 1, 0)
    out = jnp.dot(jax.nn.gelu(hid).astype(jnp.bfloat16), w2p, preferred_element_type=f32)
    rows = lax.broadcasted_iota(jnp.int32, (r, 1), 0)
    return jnp.where(rows < r - 1, out, 0.0)


def _compress_prompt_kernel(xk_ref, xv_ref, w1k_ref, w2k_ref, pk_ref, fk_ref, w1v_ref, w2v_ref, pv_ref, fv_ref,
                            ok_ref, ov_ref):
    bf16 = jnp.bfloat16
    ok_ref[0] = _compress_rows(xk_ref[0].astype(bf16), w1k_ref[...], w2k_ref[...], pk_ref[...],
                               fk_ref[...]).astype(ok_ref.dtype)
    ov_ref[0] = _compress_rows(xv_ref[0].astype(bf16), w1v_ref[...], w2v_ref[...], pv_ref[...],
                               fv_ref[...]).astype(ov_ref.dtype)


def _compress_prompt(kc, vc, wk, wv):
    b, s, _ = kc.shape
    r = s // CMP_STRIDE
    xspec = pl.BlockSpec((1, r, CHUNK_W), lambda i: (i, 0, 0))
    wspecs = [pl.BlockSpec(w.shape, lambda i: (0, 0)) for w in wk]
    ospec = pl.BlockSpec((1, r, D_KV), lambda i: (i, 0, 0))
    oshape = jax.ShapeDtypeStruct((b, r, D_KV), jnp.bfloat16)
    return pl.pallas_call(
        _compress_prompt_kernel,
        grid=(b,),
        in_specs=[xspec, xspec] + wspecs + wspecs,
        out_specs=(ospec, ospec),
        out_shape=(oshape, oshape),
        compiler_params=pltpu.CompilerParams(dimension_semantics=("arbitrary",), vmem_limit_bytes=VMEM_LIMIT),
        name="compress_prompt",
    )(kc.reshape(b, r, CHUNK_W), vc.reshape(b, r, CHUNK_W), *wk, *wv)


N_PAGES = PAST_LEN // PAGE_SIZE
PAST_CHUNKS = PAST_LEN // CMP_STRIDE
PAST_BLOCKS = PAST_LEN // SLC_BLOCK
BLOCKS_PER_PAGE = PAGE_SIZE // SLC_BLOCK
CHUNKS_PER_PAGE = PAGE_SIZE // CMP_STRIDE
SEL_ROWS = 256
QROWS = SUBLANES
REGROUP_UNROLL = 4


def _sample_cmp_kernel(pt_ref, q8_ref, kpool, vpool, w1k_ref, w2k_ref, pk_ref, fk_ref, w1v_ref, w2v_ref, pv_ref,
                       fv_ref, ocmp_ref, psum_ref, kbuf, vbuf, x_scr, t_scr, sem):
    f32, bf16 = jnp.float32, jnp.bfloat16
    n = pl.program_id(0)

    def fetch(s, slot):
        def body(j, _):
            pg = pt_ref[s, j]
            pltpu.make_async_copy(kpool.at[pg], kbuf.at[slot, j], sem.at[0, slot]).start()
            pltpu.make_async_copy(vpool.at[pg], vbuf.at[slot, j], sem.at[1, slot]).start()
            return 0
        lax.fori_loop(0, N_PAGES, body, 0)

    @pl.when(n == 0)
    def _():
        fetch(0, 0)

    @pl.when(n + 1 < pl.num_programs(0))
    def _():
        fetch(n + 1, (n + 1) % 2)

    slot = n % 2
    pltpu.make_async_copy(kbuf.at[slot], kbuf.at[slot], sem.at[0, slot]).wait()
    pltpu.make_async_copy(vbuf.at[slot], vbuf.at[slot], sem.at[1, slot]).wait()

    def regroup(buf):
        def body(jj, _):
            for u in range(REGROUP_UNROLL):
                j = jj * REGROUP_UNROLL + u
                t_scr[u] = buf[slot, j].T
                r0 = pl.multiple_of(j * CHUNKS_PER_PAGE, CHUNKS_PER_PAGE)
                for l in range(CMP_STRIDE):
                    x_scr[pl.ds(r0, CHUNKS_PER_PAGE), l * D_KV:(l + 1) * D_KV] = (
                        t_scr[u, pl.ds(l, CHUNKS_PER_PAGE, stride=CMP_STRIDE), :])
            return 0
        lax.fori_loop(0, N_PAGES // REGROUP_UNROLL, body, 0)
        return x_scr[...].astype(bf16)

    ck = _compress_rows(regroup(kbuf), w1k_ref[...], w2k_ref[...], pk_ref[...], fk_ref[...]).astype(bf16)
    cv = _compress_rows(regroup(vbuf), w1v_ref[...], w2v_ref[...], pv_ref[...], fv_ref[...]).astype(bf16)
    c_end = lax.broadcasted_iota(jnp.int32, (1, PAST_CHUNKS), 1) * CMP_STRIDE + (CMP_LEN - 1)
    cmask = c_end <= PAST_LEN
    rows = lax.broadcasted_iota(jnp.int32, (QROWS, 1), 0)
    ps = jnp.zeros((QROWS, PAST_CHUNKS), f32)
    for h in range(N_KV):
        p = _softmax_rows(_dot_nt(q8_ref[0, h], ck), cmask)
        ocmp_ref[0, h] = jnp.dot(p.astype(bf16), cv, preferred_element_type=f32)
        ps = jnp.where(rows == h, p[0:1] + p[1:2] + p[2:3] + p[3:4], ps)
    psum_ref[0] = ps


def _sample_cmp(page_table, q8, k_pool, v_pool, wk, wv):
    n = q8.shape[0]
    f32 = jnp.float32
    wspecs = [pl.BlockSpec(w.shape, lambda i, pt: (0, 0)) for w in wk]
    grid_spec = pltpu.PrefetchScalarGridSpec(
        num_scalar_prefetch=1,
        grid=(n,),
        in_specs=[pl.BlockSpec((1, N_KV, QROWS, LANES), lambda i, pt: (i, 0, 0, 0)),
                  pl.BlockSpec(memory_space=pl.ANY), pl.BlockSpec(memory_space=pl.ANY)] + wspecs + wspecs,
        out_specs=(pl.BlockSpec((1, N_KV, QROWS, LANES), lambda i, pt: (i, 0, 0, 0)),
                   pl.BlockSpec((1, QROWS, PAST_CHUNKS), lambda i, pt: (i, 0, 0))),
        scratch_shapes=[pltpu.VMEM((2, N_PAGES, D_KV, PAGE_SIZE), f32),
                        pltpu.VMEM((2, N_PAGES, D_KV, PAGE_SIZE), f32),
                        pltpu.VMEM((PAST_CHUNKS, CHUNK_W), f32),
                        pltpu.VMEM((REGROUP_UNROLL, PAGE_SIZE, D_KV), f32),
                        pltpu.SemaphoreType.DMA((2, 2))],
    )
    return pl.pallas_call(
        _sample_cmp_kernel,
        grid_spec=grid_spec,
        out_shape=(jax.ShapeDtypeStruct((n, N_KV, QROWS, LANES), f32),
                   jax.ShapeDtypeStruct((n, QROWS, PAST_CHUNKS), f32)),
        compiler_params=pltpu.CompilerParams(dimension_semantics=("arbitrary",), vmem_limit_bytes=VMEM_LIMIT),
        name="sample_compress_attention",
    )(page_table, q8, k_pool, v_pool, *wk, *wv)


def _sample_select_kernel(p_ref, mmt_ref, tri_ref, idx_ref, sv_scr):
    f32 = jnp.float32
    cols = p_ref.shape[0]
    cur = PAST_LEN // SLC_BLOCK
    mmt = mmt_ref[...]
    sc = sum(_dot_nt(mmt, part) for part in _split3_bf16(p_ref[...]))
    jio = lax.broadcasted_iota(jnp.int32, (SEL_ROWS, cols), 0)
    forced = (jio == 0) | (jio == cur) | (jio == cur - 1)
    sv = jnp.where(forced, BIG, jnp.where(jio <= cur, sc, NEG))
    sv_scr[...] = sv

    def rank_body(i, cnt):
        r = sv_scr[pl.ds(i, 1), :]
        beats = (r > sv) | ((r == sv) & (jio > i))
        return cnt + jnp.where(beats, 1.0, 0.0)

    cnt = lax.fori_loop(0, cur + 1, rank_body, jnp.zeros((SEL_ROWS, cols), f32))
    sel = (cnt < N_SEL) & (sv > NEG / 2)
    self = jnp.where(sel, 1.0, 0.0)
    before = jnp.dot(tri_ref[...], self.astype(jnp.bfloat16), preferred_element_type=f32)
    rows = []
    for k in range(N_SEL):
        rows.append(jnp.sum(jnp.where(sel & (before == k), jio, 0), axis=0, keepdims=True))
    idx_ref[...] = jnp.concatenate(rows, axis=0)


def _sample_select(p_sum):
    n = p_sum.shape[0]
    cols = n * N_KV
    p2 = p_sum[:, :N_KV, :].reshape(cols, PAST_CHUNKS)
    ratio = SLC_BLOCK // CMP_STRIDE
    ovl = CMP_LEN // CMP_STRIDE - 1
    j = np.arange(SEL_ROWS)[:, None]
    c = np.arange(PAST_CHUNKS)[None, :]
    mmt = (c >= ratio * j - ovl) & (c <= ratio * j + ratio - 1) & (j <= PAST_BLOCKS) & (c < PAST_CHUNKS - 1)
    tri = np.tril(np.ones((SEL_ROWS, SEL_ROWS), np.float32), -1)
    idx = pl.pallas_call(
        _sample_select_kernel,
        out_shape=jax.ShapeDtypeStruct((N_SEL, cols), jnp.int32),
        scratch_shapes=[pltpu.VMEM((SEL_ROWS, cols), jnp.float32)],
        compiler_params=pltpu.CompilerParams(vmem_limit_bytes=VMEM_LIMIT),
        name="sample_select",
    )(p2, jnp.asarray(mmt, dtype=jnp.bfloat16), jnp.asarray(tri, dtype=jnp.bfloat16))
    return idx.T.reshape(n, N_KV * N_SEL)


SLC_KEYS = N_SEL * SLC_BLOCK
EXT = LANES


def _sample_attend_kernel(pt_ref, idx_ref, q8_ref, kpool, vpool, knew_ref, vnew_ref, kwin_ref, vwin_ref, kwn_ref,
                          vwn_ref, ocmp_ref, ng_ref, o_ref, kwo_ref, vwo_ref, kg, vg, sem):
    f32, bf16 = jnp.float32, jnp.bfloat16
    n = pl.program_id(0)
    nblk = N_KV * N_SEL

    def fetch(s, slot):
        def body(i, _):
            jp = jnp.minimum(idx_ref[s, i], PAST_BLOCKS - 1)
            pg = pt_ref[s, jp // BLOCKS_PER_PAGE]
            sub = jp % BLOCKS_PER_PAGE
            pltpu.make_async_copy(kpool.at[pg, sub], kg.at[slot, i], sem.at[0, slot]).start()
            pltpu.make_async_copy(vpool.at[pg, sub], vg.at[slot, i], sem.at[1, slot]).start()
            return 0
        lax.fori_loop(0, nblk, body, 0)

    @pl.when(n == 0)
    def _():
        fetch(0, 0)

    @pl.when(n + 1 < pl.num_programs(0))
    def _():
        fetch(n + 1, (n + 1) % 2)

    slot = n % 2
    pltpu.make_async_copy(kg.at[slot], kg.at[slot], sem.at[0, slot]).wait()
    pltpu.make_async_copy(vg.at[slot], vg.at[slot], sem.at[1, slot]).wait()

    pos = PAST_LEN
    ext_row = lax.broadcasted_iota(jnp.int32, (EXT, 1), 0)
    ext = lambda ref: jnp.where(ext_row == 0, ref[0], 0.0).astype(bf16)
    col = lax.broadcasted_iota(jnp.int32, (1, SLC_KEYS + EXT), 1)
    wcol = lax.broadcasted_iota(jnp.int32, (1, WINDOW + EXT), 1)
    wpos = jnp.where(wcol < WINDOW, pos - WINDOW + wcol, pos + (wcol - WINDOW))
    wmask = (pos - wpos >= 0) & (pos - wpos < WINDOW)
    kwin, vwin = kwin_ref[0], vwin_ref[0]
    kw_all = jnp.concatenate([kwin.astype(bf16), ext(kwn_ref)], axis=0)
    vw_all = jnp.concatenate([vwin.astype(bf16), ext(vwn_ref)], axis=0)
    k_ext, v_ext = ext(knew_ref), ext(vnew_ref)
    lane = lax.broadcasted_iota(jnp.int32, (QROWS, LANES), 1)
    out = []
    for h in range(N_KV):
        q = q8_ref[0, h]
        new_sel = jnp.int32(0)
        valid = jnp.zeros((1, SLC_KEYS + EXT), jnp.int32)
        for p in range(N_SEL):
            j = idx_ref[n, h * N_SEL + p]
            valid = jnp.where((col >> 6) == p, (j < PAST_BLOCKS).astype(jnp.int32), valid)
            new_sel = jnp.maximum(new_sel, (j >= PAST_BLOCKS).astype(jnp.int32))
        kpos_new = PAST_LEN + (col - SLC_KEYS)
        valid = jnp.where(col >= SLC_KEYS, jnp.where(kpos_new <= pos, new_sel, 0), valid)
        kblk = kg[slot, h * N_SEL:(h + 1) * N_SEL].reshape(SLC_KEYS, LANES).astype(bf16)
        vblk = vg[slot, h * N_SEL:(h + 1) * N_SEL].reshape(SLC_KEYS, LANES).astype(bf16)
        ps = _softmax_rows(_dot_nt(q, jnp.concatenate([kblk, k_ext], axis=0)), valid > 0)
        o_slc = jnp.dot(ps.astype(bf16), jnp.concatenate([vblk, v_ext], axis=0), preferred_element_type=f32)
        pw = _softmax_rows(_dot_nt(q, kw_all), wmask)
        o_win = jnp.dot(pw.astype(bf16), vw_all, preferred_element_type=f32)
        sig = jax.nn.sigmoid(ng_ref[0, h])
        out.append(sig[0] * ocmp_ref[0, h] + sig[1] * o_slc + sig[2] * o_win)
    o_ref[0] = jnp.where(lane < HEAD_DIM, out[0], out[1]).astype(o_ref.dtype)

    last = lax.broadcasted_iota(jnp.int32, (WINDOW, 1), 0) == WINDOW - 1
    kwo_ref[0] = jnp.where(last, kwn_ref[0], pltpu.roll(kwin, WINDOW - 1, 0))
    vwo_ref[0] = jnp.where(last, vwn_ref[0], pltpu.roll(vwin, WINDOW - 1, 0))


def _sample_attend(page_table, idx, q8, k_pool, v_pool, ks_new, vs_new, k_win, v_win, kw_new, vw_new, o_cmp, ngb):
    n = q8.shape[0]
    f32 = jnp.float32
    n_phys = k_pool.shape[0]
    assert k_win.shape[1] == WINDOW
    view = lambda pool: pool.reshape(n_phys, BLOCKS_PER_PAGE, SLC_BLOCK, D_KV)
    row3 = lambda t: t.reshape(n, 1, D_KV)
    per_n3 = lambda i, pt, ix: (i, 0, 0)
    per_n4 = lambda i, pt, ix: (i, 0, 0, 0)
    new_spec = pl.BlockSpec((1, 1, D_KV), per_n3)
    win_spec = pl.BlockSpec((1, WINDOW, D_KV), per_n3)
    q_spec = pl.BlockSpec((1, N_KV, QROWS, LANES), per_n4)
    grid_spec = pltpu.PrefetchScalarGridSpec(
        num_scalar_prefetch=2,
        grid=(n,),
        in_specs=[q_spec, pl.BlockSpec(memory_space=pl.ANY), pl.BlockSpec(memory_space=pl.ANY),
                  new_spec, new_spec, win_spec, win_spec, new_spec, new_spec, q_spec,
                  pl.BlockSpec((1, N_KV, 3, QROWS, LANES), lambda i, pt, ix: (i, 0, 0, 0, 0))],
        out_specs=(pl.BlockSpec((1, QROWS, LANES), per_n3), win_spec, win_spec),
        scratch_shapes=[pltpu.VMEM((2, N_KV * N_SEL, SLC_BLOCK, D_KV), f32),
                        pltpu.VMEM((2, N_KV * N_SEL, SLC_BLOCK, D_KV), f32),
                        pltpu.SemaphoreType.DMA((2, 2))],
    )
    return pl.pallas_call(
        _sample_attend_kernel,
        grid_spec=grid_spec,
        out_shape=(jax.ShapeDtypeStruct((n, QROWS, LANES), jnp.bfloat16),
                   jax.ShapeDtypeStruct((n, WINDOW, D_KV), f32), jax.ShapeDtypeStruct((n, WINDOW, D_KV), f32)),
        compiler_params=pltpu.CompilerParams(dimension_semantics=("arbitrary",), vmem_limit_bytes=VMEM_LIMIT),
        name="sample_attention",
    )(page_table, idx, q8, view(k_pool), view(v_pool), row3(ks_new), row3(vs_new),
      k_win.reshape(n, WINDOW, D_KV), v_win.reshape(n, WINDOW, D_KV), row3(kw_new), row3(vw_new), o_cmp, ngb)


def _nsa_sample_kernels(q_pad, ks, vs, kw, vw, ng, page_table, ck_cmp, cv_cmp, ck_slc, cv_slc, ck_win, cv_win,
                        cmp_wk, cmp_wv):
    n = q_pad.shape[0]
    q8 = jnp.pad(q_pad.reshape(n, N_KV, GROUP, LANES), ((0, 0), (0, 0), (0, QROWS - GROUP), (0, 0)))
    o_cmp, p_sum = _sample_cmp(page_table, q8, ck_cmp, cv_cmp, cmp_wk, cmp_wv)
    idx = _sample_select(p_sum)
    ngb = ng[:, :N_NG].reshape(n, N_KV, GROUP, 3).transpose(0, 1, 3, 2)
    ngb = jnp.pad(ngb, ((0, 0), (0, 0), (0, 0), (0, QROWS - GROUP)))
    ngb = jnp.broadcast_to(ngb[..., None], (n, N_KV, 3, QROWS, LANES))
    o8, kwo, vwo = _sample_attend(page_table, idx, q8, ck_slc, cv_slc, ks, vs, ck_win, cv_win, kw, vw, o_cmp, ngb)
    return o8[:, :GROUP, :].reshape(n, D_Q), kwo, vwo


SLC_COLS = N_SEL * PAGE_SIZE


def _sample_attn_kernel(pt_ref, idx_ref, q_ref, kpool, vpool, knew_ref, vnew_ref, kwin_ref, vwin_ref, kwn_ref,
                        vwn_ref, ocmp_ref, ng_ref, o_ref, kwo_ref, vwo_ref, kg, vg, sem):
    f32, bf16 = jnp.float32, jnp.bfloat16
    n = pl.program_id(0)
    nblk = N_KV * N_SEL

    def fetch(s, slot):
        def body(i, _):
            jp = jnp.minimum(idx_ref[s, i], PAST_BLOCKS - 1)
            pg = pt_ref[s, jp // BLOCKS_PER_PAGE]
            h = i // N_SEL
            pltpu.make_async_copy(kpool.at[pg, h], kg.at[slot, i], sem.at[0, slot]).start()
            pltpu.make_async_copy(vpool.at[pg, h], vg.at[slot, i], sem.at[1, slot]).start()
            return 0
        lax.fori_loop(0, nblk, body, 0)

    @pl.when(n == 0)
    def _():
        fetch(0, 0)

    @pl.when(n + 1 < pl.num_programs(0))
    def _():
        fetch(n + 1, (n + 1) % 2)

    slot = n % 2
    pltpu.make_async_copy(kg.at[slot], kg.at[slot], sem.at[0, slot]).wait()
    pltpu.make_async_copy(vg.at[slot], vg.at[slot], sem.at[1, slot]).wait()

    pos = PAST_LEN
    col = lax.broadcasted_iota(jnp.int32, (1, SLC_COLS + LANES), 1)
    wcol = lax.broadcasted_iota(jnp.int32, (1, WINDOW + LANES), 1)
    wpos = jnp.where(wcol < WINDOW, pos - WINDOW + wcol, pos + (wcol - WINDOW))
    wmask = (pos - wpos >= 0) & (pos - wpos < WINDOW)
    wlane = lax.broadcasted_iota(jnp.int32, (1, WINDOW), 1)
    for h in range(N_KV):
        q = q_ref[0, h]
        new_sel = jnp.int32(0)
        lo = jnp.full((1, SLC_COLS + LANES), SLC_COLS + LANES, jnp.int32)
        for p in range(N_SEL):
            j = idx_ref[n, h * N_SEL + p]
            start = p * PAGE_SIZE + (j % BLOCKS_PER_PAGE) * SLC_BLOCK
            start = jnp.where(j < PAST_BLOCKS, start, SLC_COLS + LANES)
            lo = jnp.where((col >> 7) == p, start, lo)
            new_sel = jnp.maximum(new_sel, (j >= PAST_BLOCKS).astype(jnp.int32))
        valid = jnp.where((col >= lo) & (col < lo + SLC_BLOCK), 1, 0)
        kpos_new = PAST_LEN + (col - SLC_COLS)
        valid = jnp.where(col >= SLC_COLS, jnp.where(kpos_new <= pos, new_sel, 0), valid) > 0
        kt = jnp.concatenate([kg[slot, h * N_SEL + p] for p in range(N_SEL)] + [knew_ref[0, h]], axis=-1)
        vt = jnp.concatenate([vg[slot, h * N_SEL + p] for p in range(N_SEL)] + [vnew_ref[0, h]], axis=-1)
        ps = _softmax_rows(jnp.dot(q, kt.astype(bf16), preferred_element_type=f32), valid)
        o_slc = _dot_nt(ps.astype(bf16), vt.astype(bf16))
        kwin, vwin = kwin_ref[0, h], vwin_ref[0, h]
        kw_all = jnp.concatenate([kwin, kwn_ref[0, h]], axis=-1).astype(bf16)
        vw_all = jnp.concatenate([vwin, vwn_ref[0, h]], axis=-1).astype(bf16)
        pw = _softmax_rows(jnp.dot(q, kw_all, preferred_element_type=f32), wmask)
        o_win = _dot_nt(pw.astype(bf16), vw_all)
        sig = jax.nn.sigmoid(ng_ref[0, h])
        o_cmp = ocmp_ref[0, h][:, h * HEAD_DIM:(h + 1) * HEAD_DIM]
        o_ref[0, h] = sig[0] * o_cmp + sig[1] * o_slc + sig[2] * o_win
        kwo_ref[0, h] = jnp.where(wlane == WINDOW - 1, kwn_ref[0, h][:, 0:1], pltpu.roll(kwin, WINDOW - 1, 1))
        vwo_ref[0, h] = jnp.where(wlane == WINDOW - 1, vwn_ref[0, h][:, 0:1], pltpu.roll(vwin, WINDOW - 1, 1))


def _sample_attn(page_table, idx, q8u, k_pool_t, v_pool_t, knew_t, vnew_t, k_win_t, v_win_t, kwn_t, vwn_t, o_cmp, ngb):
    n = q8u.shape[0]
    f32 = jnp.float32
    assert k_win_t.shape[-1] == WINDOW
    per_n4 = lambda i, pt, ix: (i, 0, 0, 0)
    ext_spec = pl.BlockSpec((1, N_KV, HEAD_DIM, LANES), per_n4)
    win_spec = pl.BlockSpec((1, N_KV, HEAD_DIM, WINDOW), per_n4)
    q_spec = pl.BlockSpec((1, N_KV, QROWS, HEAD_DIM), per_n4)
    grid_spec = pltpu.PrefetchScalarGridSpec(
        num_scalar_prefetch=2,
        grid=(n,),
        in_specs=[q_spec, pl.BlockSpec(memory_space=pl.ANY), pl.BlockSpec(memory_space=pl.ANY),
                  ext_spec, ext_spec, win_spec, win_spec, ext_spec, ext_spec,
                  pl.BlockSpec((1, N_KV, QROWS, LANES), per_n4),
                  pl.BlockSpec((1, N_KV, 3, QROWS, HEAD_DIM), lambda i, pt, ix: (i, 0, 0, 0, 0))],
        out_specs=(q_spec, win_spec, win_spec),
        scratch_shapes=[pltpu.VMEM((2, N_KV * N_SEL, HEAD_DIM, PAGE_SIZE), f32),
                        pltpu.VMEM((2, N_KV * N_SEL, HEAD_DIM, PAGE_SIZE), f32),
                        pltpu.SemaphoreType.DMA((2, 2))],
    )
    return pl.pallas_call(
        _sample_attn_kernel,
        grid_spec=grid_spec,
        out_shape=(jax.ShapeDtypeStruct((n, N_KV, QROWS, HEAD_DIM), f32),
                   jax.ShapeDtypeStruct((n, N_KV, HEAD_DIM, WINDOW), f32),
                   jax.ShapeDtypeStruct((n, N_KV, HEAD_DIM, WINDOW), f32)),
        compiler_params=pltpu.CompilerParams(dimension_semantics=("arbitrary",), vmem_limit_bytes=VMEM_LIMIT),
        name="sample_attention",
    )(page_table, idx, q8u, k_pool_t, v_pool_t, knew_t, vnew_t, k_win_t, v_win_t, kwn_t, vwn_t, o_cmp, ngb)


def _nsa_sample_paged(q_pad, ks, vs, kw, vw, ng, page_table, ck_cmp, cv_cmp, ck_slc, cv_slc, ck_win, cv_win,
                      cmp_wk, cmp_wv):
    n = q_pad.shape[0]
    n_phys = ck_cmp.shape[0]
    pool_t = lambda p: p.transpose(0, 2, 3, 1)
    q4 = q_pad.reshape(n, N_KV, GROUP, N_KV, HEAD_DIM)
    q8 = jnp.pad(q_pad.reshape(n, N_KV, GROUP, LANES), ((0, 0), (0, 0), (0, QROWS - GROUP), (0, 0)))
    q8u = jnp.pad(jnp.stack([q4[:, h, :, h, :] for h in range(N_KV)], axis=1),
                  ((0, 0), (0, 0), (0, QROWS - GROUP), (0, 0)))
    o_cmp, p_sum = _sample_cmp(page_table, q8, pool_t(ck_cmp).reshape(n_phys, D_KV, PAGE_SIZE),
                               pool_t(cv_cmp).reshape(n_phys, D_KV, PAGE_SIZE), cmp_wk, cmp_wv)
    idx = _sample_select(p_sum)
    ngb = ng[:, :N_NG].reshape(n, N_KV, GROUP, 3).transpose(0, 1, 3, 2)
    ngb = jnp.pad(ngb, ((0, 0), (0, 0), (0, 0), (0, QROWS - GROUP)))
    ngb = jnp.broadcast_to(ngb[..., None], (n, N_KV, 3, QROWS, HEAD_DIM))
    col0 = lambda t: jnp.pad(t.reshape(n, N_KV, HEAD_DIM, 1), ((0, 0), (0, 0), (0, 0), (0, LANES - 1)))
    o8, kwo, vwo = _sample_attn(page_table, idx, q8u, pool_t(ck_slc), pool_t(cv_slc), col0(ks), col0(vs),
                                pool_t(ck_win), pool_t(cv_win), col0(kw), col0(vw), o_cmp, ngb)
    o = o8[:, :, :GROUP, :].transpose(0, 2, 1, 3).reshape(n, D_Q).astype(jnp.bfloat16)
    return o, kwo.transpose(0, 3, 1, 2), vwo.transpose(0, 3, 1, 2)


LRU_PAIRS = D_RNN // LANES


def _lru_gates(xc, wa_ref, ba, wx_ref, bx, lam):
    f32 = jnp.float32
    xb = xc.astype(jnp.bfloat16)
    rg = jnp.concatenate([jnp.dot(xb[:, j * LANES:(j + 1) * LANES], wa_ref[j], preferred_element_type=f32)
                          for j in range(LRU_PAIRS)], axis=-1)
    ig = jnp.concatenate([jnp.dot(xb[:, j * LANES:(j + 1) * LANES], wx_ref[j], preferred_element_type=f32)
                          for j in range(LRU_PAIRS)], axis=-1)
    r = jax.nn.sigmoid(rg + ba)
    i = jax.nn.sigmoid(ig + bx)
    log_a = -LRU_C * r * jax.nn.softplus(-lam)
    a = jnp.exp(log_a)
    u = jnp.sqrt(-jnp.tanh(log_a) * (a * a + 1.0)) * (i * xc)
    return a, u


def _lru_prompt_kernel(arag_ref, cbuf_ref, h0_ref, cw_ref, cb_ref, wa_ref, ba_ref, wx_ref, bx_ref, lam_ref,
                       y_ref, hl_ref, prev_scr, h_scr, hbuf):
    f32 = jnp.float32
    t = arag_ref.shape[0]

    @pl.when(pl.program_id(1) == 0)
    def _():
        prev_scr[...] = cbuf_ref[0]
        h_scr[...] = h0_ref[0]

    ar = arag_ref[:, :D_RNN]
    row8 = lax.broadcasted_iota(jnp.int32, (SUBLANES, 1), 0)
    prev = prev_scr[...]
    xc = cb_ref[...] + cw_ref[CONV_W - 1:CONV_W, :] * ar
    for k in range(1, CONV_W):
        rolled = pltpu.roll(ar, k, 0)
        head = jnp.where(row8 < k, pltpu.roll(prev, k, 0), rolled[0:SUBLANES])
        shifted = jnp.concatenate([head, rolled[SUBLANES:]], axis=0)
        xc = xc + cw_ref[CONV_W - 1 - k:CONV_W - k, :] * shifted
    prev_scr[...] = ar[t - SUBLANES:]

    a, u = _lru_gates(xc, wa_ref, ba_ref[...], wx_ref, bx_ref[...], lam_ref[...])

    rowm = lax.broadcasted_iota(jnp.int32, (t, 1), 0) & (SUBLANES - 1)
    for s in (1, 2, 4):
        ok = rowm >= s
        u = jnp.where(ok, a * pltpu.roll(u, s, 0) + u, u)
        a = jnp.where(ok, a * pltpu.roll(a, s, 0), a)
    h = h_scr[...]
    for g in range(t // SUBLANES):
        rs = slice(g * SUBLANES, (g + 1) * SUBLANES)
        hg = a[rs] * h + u[rs]
        hbuf[rs, :] = hg
        h = jnp.broadcast_to(hg[SUBLANES - 1:SUBLANES, :], (SUBLANES, D_RNN))
    h_scr[...] = h
    hl_ref[0] = h
    y_ref[...] = (hbuf[...] * jax.nn.gelu(arag_ref[:, D_RNN:])).astype(y_ref.dtype)


def _lru_pair_weights(w):
    w4 = w.reshape(LRU_PAIRS, 2, LRU_BW, LRU_BW)
    eye = jnp.eye(2, dtype=w.dtype)
    return jnp.einsum('jaio,ab->jaibo', w4, eye).reshape(LRU_PAIRS, LANES, LANES).astype(jnp.bfloat16)


def _lru_weights(conv_w, conv_b, wa, ba, wx, bx, lam):
    row = lambda v: v.reshape(1, D_RNN).astype(jnp.float32)
    return (conv_w.astype(jnp.float32), row(conv_b), _lru_pair_weights(wa), row(ba), _lru_pair_weights(wx), row(bx),
            row(lam))


def _lru_prompt(arag, conv_buf, h0, lw, b, s, t):
    assert s % t == 0 and t % SUBLANES == 0
    nt = s // t
    cbuf8 = jnp.pad(conv_buf.astype(jnp.float32), ((0, 0), (SUBLANES - (CONV_W - 1), 0), (0, 0)))
    h08 = jnp.broadcast_to(h0.astype(jnp.float32)[:, None, :], (b, SUBLANES, D_RNN))
    const2 = lambda bi, i: (0, 0)
    const3 = lambda bi, i: (0, 0, 0)
    per_b = lambda bi, i: (bi, 0, 0)
    y, hl = pl.pallas_call(
        _lru_prompt_kernel,
        grid=(b, nt),
        in_specs=[
            pl.BlockSpec((t, 2 * D_RNN), lambda bi, i: (bi * nt + i, 0)),
            pl.BlockSpec((1, SUBLANES, D_RNN), per_b),
            pl.BlockSpec((1, SUBLANES, D_RNN), per_b),
            pl.BlockSpec((CONV_W, D_RNN), const2),
            pl.BlockSpec((1, D_RNN), const2),
            pl.BlockSpec((LRU_PAIRS, LANES, LANES), const3),
            pl.BlockSpec((1, D_RNN), const2),
            pl.BlockSpec((LRU_PAIRS, LANES, LANES), const3),
            pl.BlockSpec((1, D_RNN), const2),
            pl.BlockSpec((1, D_RNN), const2),
        ],
        out_specs=(pl.BlockSpec((t, D_RNN), lambda bi, i: (bi * nt + i, 0)),
                   pl.BlockSpec((1, SUBLANES, D_RNN), per_b)),
        out_shape=(jax.ShapeDtypeStruct((b * s, D_RNN), jnp.bfloat16),
                   jax.ShapeDtypeStruct((b, SUBLANES, D_RNN), jnp.float32)),
        scratch_shapes=[pltpu.VMEM((SUBLANES, D_RNN), jnp.float32), pltpu.VMEM((SUBLANES, D_RNN), jnp.float32),
                        pltpu.VMEM((t, D_RNN), jnp.float32)],
        compiler_params=pltpu.CompilerParams(dimension_semantics=("arbitrary", "arbitrary"),
                                             vmem_limit_bytes=VMEM_LIMIT),
        name="lru_prompt",
    )(arag, cbuf8, h08, *lw)
    return y, hl[:, 0, :]


def _lru_step_kernel(arag_ref, c0_ref, c1_ref, c2_ref, h0_ref, cw_ref, cb_ref, wa_ref, ba_ref, wx_ref, bx_ref,
                     lam_ref, y_ref, h_ref):
    ar = arag_ref[:, :D_RNN]
    xc = (cb_ref[...] + cw_ref[0:1, :] * c0_ref[...] + cw_ref[1:2, :] * c1_ref[...] + cw_ref[2:3, :] * c2_ref[...]
          + cw_ref[3:4, :] * ar)
    a, u = _lru_gates(xc, wa_ref, ba_ref[...], wx_ref, bx_ref[...], lam_ref[...])
    h = a * h0_ref[...] + u
    h_ref[...] = h
    y_ref[...] = (h * jax.nn.gelu(arag_ref[:, D_RNN:])).astype(y_ref.dtype)


def _lru_step(arag, conv_buf, h0, lw):
    n = arag.shape[0]
    f32 = jnp.float32
    cb = conv_buf.astype(f32)
    return pl.pallas_call(
        _lru_step_kernel,
        out_shape=(jax.ShapeDtypeStruct((n, D_RNN), jnp.bfloat16), jax.ShapeDtypeStruct((n, D_RNN), f32)),
        compiler_params=pltpu.CompilerParams(vmem_limit_bytes=VMEM_LIMIT),
        name="lru_step",
    )(arag, cb[:, 0], cb[:, 1], cb[:, 2], h0.astype(f32), *lw)


def _merge_router_kernel(x_ref, ylru_ref, onsa_ref, gab_ref, wa_ref, wb_ref, wo_ref, gffn_ref, rw_ref, rb_ref,
                         tri_ref, xn_ref, hn_ref, te_ref, tr_ref, tg_ref, cnt_ref, run_scr):
    f32, bf16 = jnp.float32, jnp.bfloat16
    tm = x_ref.shape[0]

    @pl.when(pl.program_id(0) == 0)
    def _():
        run_scr[...] = jnp.zeros_like(run_scr)

    ya = jnp.dot(ylru_ref[...], wa_ref[...], preferred_element_type=f32)
    yb = jnp.dot(onsa_ref[...], wb_ref[...], preferred_element_type=f32)
    mix = jax.nn.sigmoid(gab_ref[:, :D_MODEL]) * ya + jax.nn.sigmoid(gab_ref[:, D_MODEL:]) * yb
    xn = x_ref[...] + jnp.dot(mix.astype(bf16), wo_ref[...], preferred_element_type=f32)
    xn_ref[...] = xn
    hn = xn * lax.rsqrt(jnp.mean(xn * xn, axis=-1, keepdims=True) + EPS) * gffn_ref[...]
    hn_ref[...] = hn

    hn_hi = hn.astype(bf16)
    hn_lo = (hn - hn_hi.astype(f32)).astype(bf16)
    logits = (jnp.dot(hn_hi, rw_ref[0], preferred_element_type=f32)
              + jnp.dot(hn_lo, rw_ref[0], preferred_element_type=f32)
              + jnp.dot(hn_hi, rw_ref[1], preferred_element_type=f32)) + rb_ref[...]
    lane = lax.broadcasted_iota(jnp.int32, (tm, LANES), 1)
    ninf = -jnp.inf
    l = jnp.where(lane < N_EXPERTS, logits, ninf)
    onehots, m0 = [], None
    for k in range(TOP_K):
        m = jnp.max(l, axis=-1, keepdims=True)
        idx = jnp.min(jnp.where(l == m, lane, LANES), axis=-1, keepdims=True)
        oh = lane == idx
        onehots.append((oh, idx))
        m0 = m if k == 0 else m0
        l = jnp.where(oh, ninf, l)
    sel = onehots[0][0] | onehots[1][0] | onehots[2][0] | onehots[3][0]
    e = jnp.where(sel, jnp.exp(logits - m0), 0.0)
    gates = e / jnp.sum(e, axis=-1, keepdims=True)

    self = jnp.where(sel, 1.0, 0.0)
    incl = jnp.dot(tri_ref[...], self.astype(bf16), preferred_element_type=f32)
    run = run_scr[0:1, :]
    rank = run + incl - self
    total = run + incl[tm - 1:tm, :]
    run_scr[...] = jnp.broadcast_to(total, run_scr.shape)
    cnt_ref[...] = jnp.broadcast_to(total, cnt_ref.shape)

    te = jnp.zeros((tm, LANES), jnp.int32)
    tr = jnp.zeros((tm, LANES), f32)
    tg = jnp.zeros((tm, LANES), f32)
    for k, (oh, idx) in enumerate(onehots):
        te = jnp.where(lane == k, idx, te)
        tr = jnp.where(lane == k, jnp.sum(jnp.where(oh, rank, 0.0), axis=-1, keepdims=True), tr)
        tg = jnp.where(lane == k, jnp.sum(jnp.where(oh, gates, 0.0), axis=-1, keepdims=True), tg)
    te_ref[...] = te
    tr_ref[...] = tr.astype(jnp.int32)
    tg_ref[...] = tg


def _merge_router(x2d, ylru, onsa, gab, wa, wb, wo, g_ffn, rw_pad, rb_pad, tm):
    m = x2d.shape[0]
    assert m % tm == 0
    f32 = jnp.float32
    row = lambda i: (i, 0)
    const = lambda i: (0, 0)
    tri = jnp.asarray(np.tril(np.ones((tm, tm), np.float32)), dtype=jnp.bfloat16)
    lane_out = lambda dt: (jax.ShapeDtypeStruct((m, LANES), dt), pl.BlockSpec((tm, LANES), row))
    outs = [
        (jax.ShapeDtypeStruct((m, D_MODEL), f32), pl.BlockSpec((tm, D_MODEL), row)),
        (jax.ShapeDtypeStruct((m, D_MODEL), f32), pl.BlockSpec((tm, D_MODEL), row)),
        lane_out(jnp.int32), lane_out(jnp.int32), lane_out(f32),
        (jax.ShapeDtypeStruct((8, LANES), f32), pl.BlockSpec((8, LANES), const)),
    ]
    return pl.pallas_call(
        _merge_router_kernel,
        grid=(m // tm,),
        in_specs=[
            pl.BlockSpec((tm, D_MODEL), row),
            pl.BlockSpec((tm, D_RNN), row),
            pl.BlockSpec((tm, D_Q), row),
            pl.BlockSpec((tm, 2 * D_MODEL), row),
            pl.BlockSpec((D_RNN, D_MODEL), const),
            pl.BlockSpec((D_Q, D_MODEL), const),
            pl.BlockSpec((D_MODEL, D_MODEL), const),
            pl.BlockSpec((1, D_MODEL), const),
            pl.BlockSpec((2, D_MODEL, LANES), lambda i: (0, 0, 0)),
            pl.BlockSpec((1, LANES), const),
            pl.BlockSpec((tm, tm), const),
        ],
        out_specs=tuple(o[1] for o in outs),
        out_shape=tuple(o[0] for o in outs),
        scratch_shapes=[pltpu.VMEM((8, LANES), f32)],
        compiler_params=pltpu.CompilerParams(dimension_semantics=("arbitrary",), vmem_limit_bytes=VMEM_LIMIT),
        name="merge_router",
    )(x2d, ylru, onsa, gab, wa, wb, wo, g_ffn.reshape(1, D_MODEL), rw_pad, rb_pad, tri)


MOE_BLK = 512


def _moe_dispatch_kernel(last_ref, dest_ref, hn_ref, xs_ref, zero_scr, sem, zsem):
    tm = hn_ref.shape[0]

    @pl.when(pl.program_id(0) == 0)
    def _():
        zero_scr[...] = jnp.zeros_like(zero_scr)
        for e in range(N_EXPERTS):
            @pl.when(last_ref[e] >= 0)
            def _():
                start = pl.multiple_of(last_ref[e], MOE_BLK)
                cp = pltpu.make_async_copy(zero_scr, xs_ref.at[pl.ds(start, MOE_BLK)], zsem)
                cp.start()
                cp.wait()

    def issue(r, _):
        for k in range(TOP_K):
            d = dest_ref[0, 0, r * TOP_K + k]
            pltpu.make_async_copy(hn_ref.at[pl.ds(r, 1)], xs_ref.at[pl.ds(d, 1)], sem).start()
        return 0

    lax.fori_loop(0, tm, issue, 0)
    pltpu.make_async_copy(xs_ref.at[pl.ds(0, tm * TOP_K)], xs_ref.at[pl.ds(0, tm * TOP_K)], sem).wait()


def _moe_dispatch(hn, dest, last_tile, n_slots, tm):
    m = hn.shape[0]
    nt = m // tm
    grid_spec = pltpu.PrefetchScalarGridSpec(
        num_scalar_prefetch=1,
        grid=(nt,),
        in_specs=[
            pl.BlockSpec((1, 1, tm * TOP_K), lambda i, lt: (i, 0, 0), memory_space=pltpu.SMEM),
            pl.BlockSpec((tm, D_MODEL), lambda i, lt: (i, 0)),
        ],
        out_specs=pl.BlockSpec(memory_space=pl.ANY),
        scratch_shapes=[pltpu.VMEM((MOE_BLK, D_MODEL), hn.dtype), pltpu.SemaphoreType.DMA(()),
                        pltpu.SemaphoreType.DMA(())],
    )
    return pl.pallas_call(
        _moe_dispatch_kernel,
        grid_spec=grid_spec,
        out_shape=jax.ShapeDtypeStruct((n_slots, D_MODEL), hn.dtype),
        compiler_params=pltpu.CompilerParams(dimension_semantics=("arbitrary",), vmem_limit_bytes=VMEM_LIMIT),
        name="moe_dispatch",
    )(last_tile, dest.reshape(nt, 1, tm * TOP_K), hn)


def _moe_experts_kernel(te_ref, nused_ref, xs_ref, wup_ref, bup_ref, wdn_ref, bdn_ref, ys_ref, wup_scr, wdn_scr):
    f32, bf16 = jnp.float32, jnp.bfloat16
    j = pl.program_id(0)
    changed = jnp.logical_or(j == 0, te_ref[j] != te_ref[jnp.maximum(j - 1, 0)])

    @pl.when(jnp.logical_and(j < nused_ref[0], changed))
    def _():
        wup_scr[...] = wup_ref[0].astype(bf16)
        wdn_scr[...] = wdn_ref[0].astype(bf16)

    @pl.when(j < nused_ref[0])
    def _():
        x = xs_ref[...].astype(bf16)
        gu = jnp.dot(x, wup_scr[...], preferred_element_type=f32) + bup_ref[0]
        a = jnp.minimum(gu[:, :D_FF], SWIGLU_LIMIT)
        u = jnp.clip(gu[:, D_FF:], -SWIGLU_LIMIT, SWIGLU_LIMIT)
        act = (u + 1.0) * a * jax.nn.sigmoid(SWIGLU_ALPHA * a)
        ys_ref[...] = jnp.dot(act.astype(bf16), wdn_scr[...], preferred_element_type=f32) + bdn_ref[0]


def _moe_experts(xs, tile_expert, n_used, w_up, b_up, w_down, b_down):
    n_slots = xs.shape[0]
    nt = n_slots // MOE_BLK
    tile = lambda j, te, nu: (jnp.minimum(j, nu[0] - 1), 0)
    wsel = lambda j, te, nu: (te[jnp.minimum(j, nu[0] - 1)], 0, 0)
    grid_spec = pltpu.PrefetchScalarGridSpec(
        num_scalar_prefetch=2,
        grid=(nt,),
        in_specs=[
            pl.BlockSpec((MOE_BLK, D_MODEL), tile),
            pl.BlockSpec((1, D_MODEL, 2 * D_FF), wsel),
            pl.BlockSpec((1, 1, 2 * D_FF), wsel),
            pl.BlockSpec((1, D_FF, D_MODEL), wsel),
            pl.BlockSpec((1, 1, D_MODEL), wsel),
        ],
        out_specs=pl.BlockSpec((MOE_BLK, D_MODEL), tile),
        scratch_shapes=[pltpu.VMEM((D_MODEL, 2 * D_FF), jnp.bfloat16), pltpu.VMEM((D_FF, D_MODEL), jnp.bfloat16)],
    )
    return pl.pallas_call(
        _moe_experts_kernel,
        grid_spec=grid_spec,
        out_shape=jax.ShapeDtypeStruct((n_slots, D_MODEL), jnp.float32),
        compiler_params=pltpu.CompilerParams(dimension_semantics=("arbitrary",), vmem_limit_bytes=VMEM_LIMIT),
        name="moe_experts",
    )(tile_expert, n_used, xs, w_up, b_up.reshape(N_EXPERTS, 1, 2 * D_FF), w_down,
      b_down.reshape(N_EXPERTS, 1, D_MODEL))


def _moe_combine_kernel(dest_ref, xn_ref, tg_ref, ys_ref, y_ref, buf, sem):
    tm = xn_ref.shape[0]

    def issue(r, _):
        for k in range(TOP_K):
            d = dest_ref[0, 0, r * TOP_K + k]
            pltpu.make_async_copy(ys_ref.at[pl.ds(d, 1)], buf.at[k, pl.ds(r, 1)], sem).start()
        return 0

    lax.fori_loop(0, tm, issue, 0)
    pltpu.make_async_copy(ys_ref.at[pl.ds(0, tm * TOP_K)], ys_ref.at[pl.ds(0, tm * TOP_K)], sem).wait()
    tg = tg_ref[...]
    acc = xn_ref[...]
    for k in range(TOP_K):
        acc = acc + tg[:, k:k + 1] * buf[k]
    y_ref[...] = acc


def _moe_combine(xn, tg, ys, dest, tm):
    m = xn.shape[0]
    nt = m // tm
    return pl.pallas_call(
        _moe_combine_kernel,
        grid=(nt,),
        in_specs=[
            pl.BlockSpec((1, 1, tm * TOP_K), lambda i: (i, 0, 0), memory_space=pltpu.SMEM),
            pl.BlockSpec((tm, D_MODEL), lambda i: (i, 0)),
            pl.BlockSpec((tm, LANES), lambda i: (i, 0)),
            pl.BlockSpec(memory_space=pl.ANY),
        ],
        out_specs=pl.BlockSpec((tm, D_MODEL), lambda i: (i, 0)),
        out_shape=jax.ShapeDtypeStruct((m, D_MODEL), jnp.float32),
        scratch_shapes=[pltpu.VMEM((TOP_K, tm, D_MODEL), jnp.float32), pltpu.SemaphoreType.DMA(())],
        compiler_params=pltpu.CompilerParams(dimension_semantics=("arbitrary",), vmem_limit_bytes=VMEM_LIMIT),
        name="moe_combine",
    )(dest.reshape(nt, 1, tm * TOP_K), xn, tg, ys)


def _moe_layer(xn, hn, te, tr, tg, cnt, w_up, b_up, w_down, b_down, tm):
    m = xn.shape[0]
    n_tiles = -(-m * TOP_K // MOE_BLK) + N_EXPERTS
    counts = cnt[0, :N_EXPERTS].astype(jnp.int32)
    padded = (counts + MOE_BLK - 1) // MOE_BLK * MOE_BLK
    pad_end = jnp.cumsum(padded)
    pad_start = pad_end - padded
    dest = (pad_start[te[:, :TOP_K]] + tr[:, :TOP_K]).astype(jnp.int32)
    tile_start = jnp.arange(n_tiles, dtype=jnp.int32) * MOE_BLK
    tile_expert = jnp.minimum(jnp.sum(pad_end[None, :] <= tile_start[:, None], axis=1), N_EXPERTS - 1).astype(jnp.int32)
    n_used = (pad_end[-1:] // MOE_BLK).astype(jnp.int32)
    last_tile = jnp.where(padded > 0, pad_end - MOE_BLK, -1).astype(jnp.int32)
    xs = _moe_dispatch(hn, dest, last_tile, n_tiles * MOE_BLK, tm)
    ys = _moe_experts(xs, tile_expert, n_used, w_up, b_up, w_down, b_down)
    return _moe_combine(xn, tg, ys, dest, tm)


def _masked_softmax(s, mask):
    s = jnp.where(mask, s.astype(jnp.float32), NEG)
    m = jnp.max(s, axis=-1, keepdims=True)
    p = jnp.where(mask, jnp.exp(s - m), 0.0)
    return p / jnp.maximum(jnp.sum(p, axis=-1, keepdims=True), 1e-30)


def _merge_branches(y_lru, o_nsa, ga, gb, w_proj_a, w_proj_b, w_out):
    return (jax.nn.sigmoid(ga) * (y_lru @ w_proj_a) + jax.nn.sigmoid(gb) * (o_nsa @ w_proj_b)) @ w_out


def _lin_combine(c1, c2):
    a1, b1 = c1
    a2, b2 = c2
    return a1 * a2, a2 * b1 + b2


def _rglru(xr, gate_in, conv_buf, h0, conv_w, conv_b, wa, ba, wx, bx, lam):
    n, s, _ = xr.shape
    xp = jnp.concatenate([conv_buf.astype(xr.dtype), xr], axis=1)
    xc = conv_b + sum(conv_w[k] * xp[:, k:k + s] for k in range(CONV_W))
    xb = xc.reshape(n, s, LRU_BLOCKS, LRU_BW)
    r = jax.nn.sigmoid((jnp.einsum('nsbi,bij->nsbj', xb, wa) + ba).astype(jnp.float32)).reshape(n, s, D_RNN)
    i = jax.nn.sigmoid((jnp.einsum('nsbi,bij->nsbj', xb, wx) + bx).astype(jnp.float32)).reshape(n, s, D_RNN)
    log_a = -LRU_C * r * jax.nn.softplus(-lam.astype(jnp.float32))
    a = jnp.exp(log_a)
    u = jnp.sqrt(-jnp.expm1(2.0 * log_a)) * (i * xc.astype(jnp.float32))
    u = u.at[:, 0].add(a[:, 0] * h0.astype(jnp.float32))
    _, h = lax.associative_scan(_lin_combine, (a, u), axis=1)
    y = (h * jax.nn.gelu(gate_in.astype(jnp.float32))).astype(xr.dtype)
    return y, xp[:, s:], h[:, -1]


def _compress(rows, pos_emb, w1, w2):
    n, l = rows.shape[:2]
    r_ = CMP_LEN // CMP_STRIDE
    nch = l // CMP_STRIDE
    ch = rows[:, :nch * CMP_STRIDE].reshape(n, nch, CMP_STRIDE, N_KV, HEAD_DIM)
    w1b = w1.reshape(r_, CMP_STRIDE, HEAD_DIM, CMP_HID)
    part = jnp.einsum('nclhd,pldf->pnchf', ch, w1b)
    nc = nch - r_ + 1
    hid = jnp.einsum('ld,ldf->f', pos_emb, w1) + sum(part[p, :, p:p + nc] for p in range(r_))
    return jnp.einsum('nchf,fd->nchd', jax.nn.gelu(hid), w2)


def _block_scores(p_cmp, nsb):
    nc = p_cmp.shape[-1]
    ratio = SLC_BLOCK // CMP_STRIDE
    ovl = CMP_LEN // CMP_STRIDE - 1
    widths = [(0, 0)] * (p_cmp.ndim - 1) + [(ovl, ratio * (nsb + 1) - nc - ovl)]
    pp = jnp.pad(p_cmp, widths).reshape(*p_cmp.shape[:-1], nsb + 1, ratio)
    return pp[..., :nsb, :].sum(-1) + pp[..., 1:, :ovl].sum(-1)


def _select_blocks(scores, pos, nsb):
    j = jnp.arange(nsb, dtype=jnp.int32)
    cur = (pos // SLC_BLOCK)[:, None, None]
    forced = (j == 0) | (j == cur) | (j == cur - 1)
    s = jnp.where(forced, BIG, jnp.where(j <= cur, scores, NEG))
    top, idx = lax.top_k(s, min(N_SEL, nsb))
    return idx, top > NEG / 2


def _nsa_attend(q, pos, cmp_k, cmp_v, gather_slc, nsb, win_k, win_v, win_pos, gates):
    n, qn = q.shape[:2]
    nc = cmp_k.shape[1]
    s = jnp.einsum('nqhgd,nchd->nqhgc', q, cmp_k)
    cmp_end = jnp.arange(nc, dtype=jnp.int32) * CMP_STRIDE + CMP_LEN - 1
    p_c = _masked_softmax(s, cmp_end <= pos[:, None, None, None])
    o_cmp = jnp.einsum('nqhgc,nchd->nqhgd', p_c.astype(cmp_v.dtype), cmp_v)
    idx, valid = _select_blocks(_block_scores(p_c.sum(3), nsb), pos, nsb)
    k_g, v_g = gather_slc(idx)
    kk = idx.shape[-1] * SLC_BLOCK
    kpos = idx[..., None] * SLC_BLOCK + jnp.arange(SLC_BLOCK, dtype=jnp.int32)
    smask = (valid[..., None] & (kpos <= pos[:, None, None, None])).reshape(n, qn, N_KV, 1, kk)
    s = jnp.einsum('nqhgd,nqhkd->nqhgk', q, k_g.reshape(n, qn, N_KV, kk, HEAD_DIM))
    p_s = _masked_softmax(s, smask)
    o_slc = jnp.einsum('nqhgk,nqhkd->nqhgd', p_s.astype(v_g.dtype), v_g.reshape(n, qn, N_KV, kk, HEAD_DIM))
    s = jnp.einsum('nqhgd,nkhd->nqhgk', q, win_k)
    dpos = pos[:, None] - win_pos[None, :]
    wmask = (dpos >= 0) & (dpos < WINDOW) & (win_pos >= 0)[None, :]
    p_w = _masked_softmax(s, wmask[:, None, None, :])
    o_win = jnp.einsum('nqhgk,nkhd->nqhgd', p_w.astype(win_v.dtype), win_v)
    g = jax.nn.sigmoid(gates)
    return g[..., 0:1] * o_cmp + g[..., 1:2] * o_slc + g[..., 2:3] * o_win


def _nsa_prompt(q, kc, vc, ks, vs, kw, vw, ng, pos_k, w1_k, w2_k, pos_v, w1_v, w2_v):
    n, s = q.shape[:2]
    cmp_k = _compress(kc, pos_k, w1_k, w2_k)
    cmp_v = _compress(vc, pos_v, w1_v, w2_v)
    nsb = s // SLC_BLOCK
    kb = ks.reshape(n, nsb, SLC_BLOCK, N_KV, HEAD_DIM)
    vb = vs.reshape(n, nsb, SLC_BLOCK, N_KV, HEAD_DIM)
    bi = jnp.arange(n)[:, None, None, None]
    hi = jnp.arange(N_KV)[None, None, :, None]

    def gather(idx):
        return kb[bi, idx, :, hi], vb[bi, idx, :, hi]

    pad = ((0, 0), (WINDOW, 0), (0, 0), (0, 0))
    kw_pad, vw_pad = jnp.pad(kw, pad), jnp.pad(vw, pad)

    def block(i):
        start = i * Q_BLOCK
        sl = lambda t, size: lax.dynamic_slice_in_dim(t, start, size, axis=1)
        pos = start + jnp.arange(Q_BLOCK, dtype=jnp.int32)
        wpos = start - WINDOW + jnp.arange(WINDOW + Q_BLOCK, dtype=jnp.int32)
        return _nsa_attend(sl(q, Q_BLOCK), pos, cmp_k, cmp_v, gather, nsb,
                           sl(kw_pad, WINDOW + Q_BLOCK), sl(vw_pad, WINDOW + Q_BLOCK), wpos, sl(ng, Q_BLOCK))

    o = lax.map(block, jnp.arange(s // Q_BLOCK, dtype=jnp.int32))
    return jnp.moveaxis(o, 0, 1).reshape(n, s, D_Q)


def _nsa_sample(q, kc, vc, ks, vs, kw, vw, ng, page_table, ck_cmp, cv_cmp, ck_slc, cv_slc, ck_win, cv_win,
                pos_k, w1_k, w2_k, pos_v, w1_v, w2_v):
    n, ds = q.shape[:2]
    pos = PAST_LEN + jnp.arange(ds, dtype=jnp.int32)

    def past(pool):
        return pool[page_table].reshape(n, PAST_LEN, N_KV, HEAD_DIM)

    cmp_k = _compress(jnp.concatenate([past(ck_cmp), kc], axis=1), pos_k, w1_k, w2_k)
    cmp_v = _compress(jnp.concatenate([past(cv_cmp), vc], axis=1), pos_v, w1_v, w2_v)
    past_blocks = PAST_LEN // SLC_BLOCK
    new_blocks = -(-ds // SLC_BLOCK)
    nsb = past_blocks + new_blocks
    bpp = PAGE_SIZE // SLC_BLOCK
    view = lambda pool: pool.reshape(pool.shape[0], bpp, SLC_BLOCK, N_KV, HEAD_DIM)
    pk, pv = view(ck_slc), view(cv_slc)
    padn = ((0, 0), (0, new_blocks * SLC_BLOCK - ds), (0, 0), (0, 0))
    nk = jnp.pad(ks, padn).reshape(n, new_blocks, SLC_BLOCK, N_KV, HEAD_DIM)
    nv = jnp.pad(vs, padn).reshape(n, new_blocks, SLC_BLOCK, N_KV, HEAD_DIM)
    bi = jnp.arange(n)[:, None, None, None]
    hi = jnp.arange(N_KV)[None, None, :, None]

    def gather(idx):
        jp = jnp.minimum(idx, past_blocks - 1)
        phys = page_table[bi, jp // bpp]
        sub = jp % bpp
        jn = jnp.clip(idx - past_blocks, 0, new_blocks - 1)
        is_new = (idx >= past_blocks)[..., None, None]
        return (jnp.where(is_new, nk[bi, jn, :, hi], pk[phys, sub, :, hi]),
                jnp.where(is_new, nv[bi, jn, :, hi], pv[phys, sub, :, hi]))

    wb = ck_win.shape[1]
    wk = jnp.concatenate([ck_win, kw], axis=1)
    wv = jnp.concatenate([cv_win, vw], axis=1)
    wpos = PAST_LEN - wb + jnp.arange(wb + ds, dtype=jnp.int32)
    o = _nsa_attend(q, pos, cmp_k, cmp_v, gather, nsb, wk, wv, wpos, ng)
    return o.reshape(n, ds, D_Q), wk[:, -wb:], wv[:, -wb:]


def _rms_norm(x, g):
    xf = x.astype(jnp.float32)
    y = xf * lax.rsqrt(jnp.mean(xf * xf, axis=-1, keepdims=True) + EPS)
    return (y * g.astype(jnp.float32)).astype(x.dtype)


def _moe(h, router_w, router_b, w_up, b_up, w_down, b_down):
    shp = h.shape
    t = h.reshape(-1, D_MODEL)
    n_tok = t.shape[0]
    logits = (t @ router_w).astype(jnp.float32) + router_b.astype(jnp.float32)
    top_v, top_e = lax.top_k(logits, TOP_K)
    gate = jax.nn.softmax(top_v, axis=-1)
    n_asg = n_tok * TOP_K
    blk = int(min(512, max(8, n_asg // N_EXPERTS)))
    n_blocks = -(-n_asg // blk) + N_EXPERTS
    e_flat = top_e.reshape(-1)
    tok_flat = jnp.repeat(jnp.arange(n_tok, dtype=jnp.int32), TOP_K)
    g_flat = gate.reshape(-1)
    order = jnp.argsort(e_flat)
    e_sorted = e_flat[order]
    counts = jnp.bincount(e_flat, length=N_EXPERTS)
    padded = (counts + blk - 1) // blk * blk
    pad_end = jnp.cumsum(padded)
    pad_start = pad_end - padded
    starts = jnp.cumsum(counts) - counts
    dest = pad_start[e_sorted] + jnp.arange(n_asg, dtype=jnp.int32) - starts[e_sorted]
    slot_tok = jnp.full((n_blocks * blk,), n_tok, jnp.int32).at[dest].set(tok_flat[order])
    slot_gate = jnp.zeros((n_blocks * blk,), jnp.float32).at[dest].set(g_flat[order])
    blk_expert = jnp.minimum(jnp.searchsorted(pad_end, jnp.arange(n_blocks, dtype=jnp.int32) * blk, side='right'),
                             N_EXPERTS - 1)
    t_pad = jnp.concatenate([t, jnp.zeros((1, D_MODEL), t.dtype)], axis=0)

    def run_block(args):
        tok, g, e = args
        gu = t_pad[tok] @ w_up[e] + b_up[e]
        a = jnp.minimum(gu[:, :D_FF], SWIGLU_LIMIT)
        u = jnp.clip(gu[:, D_FF:], -SWIGLU_LIMIT, SWIGLU_LIMIT)
        y = ((u + 1.0) * a * jax.nn.sigmoid(SWIGLU_ALPHA * a)) @ w_down[e] + b_down[e]
        return (y * g[:, None]).astype(t.dtype)

    ys = lax.map(run_block, (slot_tok.reshape(n_blocks, blk), slot_gate.reshape(n_blocks, blk), blk_expert))
    out = jnp.zeros((n_tok + 1, D_MODEL), t.dtype).at[slot_tok].add(ys.reshape(-1, D_MODEL))
    return out[:n_tok].reshape(shp)


def _pad_w_in(w):
    c0 = 2 * D_RNN
    wq = w[:, c0:c0 + D_Q].reshape(D_MODEL, N_KV, GROUP, 1, HEAD_DIM)
    sel = jnp.eye(N_KV, dtype=w.dtype).reshape(N_KV, 1, N_KV, 1)
    wq = (wq * sel).reshape(D_MODEL, D_QP)
    c1 = c0 + D_Q
    c2 = c1 + 6 * D_KV + N_NG
    pad = jnp.zeros((D_MODEL, LANES - N_NG), w.dtype)
    return jnp.concatenate([w[:, :c0], wq, w[:, c1:c2], pad, w[:, c2:]], axis=1).astype(jnp.bfloat16)


def _unpad_q(q_pad, n, s):
    qp = q_pad.reshape(n, s, N_KV, GROUP, N_KV, HEAD_DIM)
    return jnp.stack([qp[:, :, h, :, h, :] for h in range(N_KV)], axis=2)


def _project(x, pos_rows, tm, g_mix, w_pad, g_q, g_k3):
    n, s, _ = x.shape
    outs = _input_projection(x.reshape(n * s, D_MODEL), pos_rows, tm, g_mix, w_pad, g_q, g_k3)
    arag, q, kc, vc, ks, vs, kw, vw, ng, gab = outs
    heads = lambda t: t.reshape(n, s, N_KV, HEAD_DIM)
    ar = arag[:, :D_RNN].reshape(n, s, D_RNN)
    ag = arag[:, D_RNN:].reshape(n, s, D_RNN)
    q = _unpad_q(q, n, s)
    ng = ng[:, :N_NG].reshape(n, s, N_KV, GROUP, 3)
    ga = gab[:, :D_MODEL].reshape(n, s, D_MODEL)
    gb = gab[:, D_MODEL:].reshape(n, s, D_MODEL)
    return ar, ag, q, heads(kc), heads(vc), heads(ks), heads(vs), heads(kw), heads(vw), ng, ga, gb


def kernel(x_prompt, x_sample, page_table, cache_k_cmp, cache_v_cmp, cache_k_slc, cache_v_slc,
           cache_k_win, cache_v_win, state_h, state_conv, g_mix, w_in, conv_w, conv_b, lru_wa, lru_ba,
           lru_wx, lru_bx, lru_lambda, g_q, g_k_cmp, g_k_slc, g_k_win, cmp_pos_k, cmp_w1_k, cmp_w2_k,
           cmp_pos_v, cmp_w1_v, cmp_w2_v, w_proj_a, w_proj_b, w_out, g_ffn, router_w, router_b,
           exp_w_up, exp_b_up, exp_w_down, exp_b_down):
    xp, xs = x_prompt, x_sample
    bp, sp = xp.shape[:2]
    bs, ss = xs.shape[:2]
    pos_p = jnp.arange(sp, dtype=jnp.int32)
    pos_s = PAST_LEN + jnp.arange(ss, dtype=jnp.int32)
    depth = w_in.shape[0]
    layer_states = []
    tile2 = lambda g: jnp.concatenate([g, g], axis=-1)
    for l in range(depth):
        cmp_w = (cmp_pos_k[l], cmp_w1_k[l], cmp_w2_k[l], cmp_pos_v[l], cmp_w1_v[l], cmp_w2_v[l])
        lru_w = (conv_w[l], conv_b[l], lru_wa[l], lru_ba[l], lru_wx[l], lru_bx[l], lru_lambda[l])
        w_pad = _pad_w_in(w_in[l])
        gq2 = tile2(g_q[l]).reshape(1, LANES)
        gk3 = jnp.stack([tile2(g_k_cmp[l]), tile2(g_k_slc[l]), tile2(g_k_win[l])])
        cmp_wk = _compress_weights(cmp_pos_k[l], cmp_w1_k[l], cmp_w2_k[l])
        cmp_wv = _compress_weights(cmp_pos_v[l], cmp_w1_v[l], cmp_w2_v[l])
        wb_perm = w_proj_b[l].reshape(N_KV, GROUP, HEAD_DIM, D_MODEL).transpose(1, 0, 2, 3).reshape(D_Q, D_MODEL)
        bf16 = jnp.bfloat16
        lru_k = _lru_weights(*lru_w)
        rw_pad = jnp.pad(router_w[l].astype(jnp.float32), ((0, 0), (0, LANES - N_EXPERTS)))
        rw_hi = rw_pad.astype(bf16)
        rw_pad = jnp.stack([rw_hi, (rw_pad - rw_hi.astype(jnp.float32)).astype(bf16)])
        rb_pad = jnp.pad(router_b[l].astype(jnp.float32), (0, LANES - N_EXPERTS)).reshape(1, LANES)
        merge_w = (w_proj_a[l].astype(bf16), wb_perm.astype(bf16), w_out[l].astype(bf16), g_ffn[l], rw_pad, rb_pad)
        moe_w = (exp_w_up[l], exp_b_up[l], exp_w_down[l], exp_b_down[l])
        arag, q_pad, kc, vc, ks, vs, kw, vw, ng, gab, *kv_t = _input_projection(
            xp.reshape(bp * sp, D_MODEL), pos_p, 256, g_mix[l], w_pad, gq2, gk3, kv_transposed=True)
        rows = lambda t: t.reshape(bp, sp, t.shape[-1])
        heads_t = lambda t: t.reshape(bp, N_KV, HEAD_DIM, t.shape[-1]).transpose(0, 3, 1, 2)
        conv0 = jnp.zeros((bp, CONV_W - 1, D_RNN), jnp.float32)
        y_lru, p_h = _lru_prompt(arag, conv0, jnp.zeros((bp, D_RNN), jnp.float32), lru_k, bp, sp, 256)
        p_conv = jnp.concatenate([conv0, rows(arag)[:, :, :D_RNN]], axis=1)[:, sp:]
        cmp_k, cmp_v = _compress_prompt(rows(kc), rows(vc), cmp_wk, cmp_wv)
        o_nsa = _nsa_prompt_attention(rows(q_pad), rows(ks), rows(vs), rows(kw), rows(vw), cmp_k, cmp_v, rows(ng))
        xn, hn, te, tr, tg, cnt = _merge_router(xp.reshape(bp * sp, D_MODEL), y_lru, o_nsa.reshape(bp * sp, D_Q),
                                                gab, *merge_w, 256)
        xp = _moe_layer(xn, hn, te, tr, tg, cnt, *moe_w, 256).reshape(bp, sp, D_MODEL)
        wbp = min(WINDOW, sp)
        p_states = tuple(heads_t(t) for t in kv_t[:4]) + tuple(heads_t(t[:, :, sp - wbp:]) for t in kv_t[4:]) + (
            p_h, p_conv)
        assert ss == 1
        pos_rows_s = jnp.broadcast_to(pos_s[None, :], (bs, ss)).reshape(-1)
        arag, q_pad, kc, vc, ks, vs, kw, vw, ng, gab = _input_projection(
            xs.reshape(bs, D_MODEL), pos_rows_s, bs, g_mix[l], w_pad, gq2, gk3)
        heads = lambda t: t.reshape(bs, ss, N_KV, HEAD_DIM)
        y_lru, s_h_l = _lru_step(arag, state_conv[l], state_h[l], lru_k)
        s_conv_l = jnp.concatenate([state_conv[l][:, 1:], arag[:, None, :D_RNN]], axis=1)
        o_nsa, s_kw_l, s_vw_l = _nsa_sample_paged(
            q_pad, ks, vs, kw, vw, ng, page_table, cache_k_cmp[l], cache_v_cmp[l], cache_k_slc[l], cache_v_slc[l],
            cache_k_win[l], cache_v_win[l], cmp_wk, cmp_wv)
        xn, hn, te, tr, tg, cnt = _merge_router(xs.reshape(bs, D_MODEL), y_lru, o_nsa, gab, *merge_w, bs)
        xs = _moe_layer(xn, hn, te, tr, tg, cnt, *moe_w, bs).reshape(bs, ss, D_MODEL)
        s_states = (heads(kc), heads(vc), heads(ks), heads(vs), s_kw_l, s_vw_l, s_h_l, s_conv_l)
        layer_states.append(p_states + s_states)
    (p_k_cmp, p_v_cmp, p_k_slc, p_v_slc, p_k_win, p_v_win, p_h, p_conv,
     s_k_cmp, s_v_cmp, s_k_slc, s_v_slc, s_k_win, s_v_win, s_h, s_conv) = [jnp.stack(z) for z in zip(*layer_states)]
    return (xp, xs, p_k_cmp, p_v_cmp, p_k_slc, p_v_slc, p_k_win, p_v_win, p_h, p_conv,
            s_k_cmp, s_v_cmp, s_k_slc, s_v_slc, s_k_win, s_v_win, s_h, s_conv)
```

```python
import jax
import jax.numpy as jnp
import numpy as np
from jax import lax
from jax.experimental import pallas as pl
from jax.experimental.pallas import tpu as pltpu

D_MODEL = 1024
PAST_LEN = 8192
PAGE_SIZE = 128
D_RNN = D_MODEL
LRU_BLOCKS = 16
LRU_BW = D_RNN // LRU_BLOCKS
CONV_W = 4
LRU_C = 8.0
N_HEADS = 8
HEAD_DIM = 64
N_KV = 2
GROUP = N_HEADS // N_KV
D_Q = N_HEADS * HEAD_DIM
D_KV = N_KV * HEAD_DIM
ROPE_DIM = HEAD_DIM // 4
ROPE_THETA = 500000.0
CMP_LEN = 32
CMP_STRIDE = 16
CMP_HID = 256
SLC_BLOCK = 64
N_SEL = 16
WINDOW = 512
Q_BLOCK = 64
N_EXPERTS = 32
TOP_K = 4
D_FF = D_MODEL
SWIGLU_LIMIT = 7.0
SWIGLU_ALPHA = 1.702
EPS = 1e-6
NEG = -1e30
BIG = 1e30

LANES = 128
SUBLANES = 8
VMEM_LIMIT = 56 * 1024 * 1024

N_NG = 3 * N_HEADS
D_QP = N_HEADS * LANES
C_AR = 0
C_Q = 2 * D_RNN
C_KV = C_Q + D_QP
C_NG = C_KV + 6 * D_KV
C_GAB = C_NG + LANES
N_IN_PAD = C_GAB + 2 * D_MODEL


def _seg_sum64(x2, ones_blk):
    hi = x2.astype(jnp.bfloat16)
    lo = (x2 - hi.astype(jnp.float32)).astype(jnp.bfloat16)
    return (jnp.dot(hi, ones_blk, preferred_element_type=jnp.float32)
            + jnp.dot(lo, ones_blk, preferred_element_type=jnp.float32))


def _head_norm_rope(y, gain, ones_blk, rc, rs1, rs2):
    ss = _seg_sum64(y * y, ones_blk)
    yn = y * lax.rsqrt(ss * (1.0 / HEAD_DIM) + EPS) * gain
    half = ROPE_DIM // 2
    return yn * rc + pltpu.roll(yn, LANES - half, 1) * rs1 + pltpu.roll(yn, half, 1) * rs2


def _proj_kernel(x_ref, gmix_ref, w_ref, ones_ref, gq_ref, gk_ref, rc_ref, rs1_ref, rs2_ref,
                 arag_ref, q_ref, kc_ref, vc_ref, ks_ref, vs_ref, kw_ref, vw_ref, ng_ref, gab_ref, *kv_t_refs):
    x = x_ref[...]
    xn = x * lax.rsqrt(jnp.mean(x * x, axis=-1, keepdims=True) + EPS) * gmix_ref[...]
    xb = xn.astype(jnp.bfloat16)

    def proj(c0, width):
        return jnp.dot(xb, w_ref[:, c0:c0 + width], preferred_element_type=jnp.float32)

    arag_ref[...] = proj(C_AR, 2 * D_RNN)
    gab_ref[...] = proj(C_GAB, 2 * D_MODEL)
    ng_ref[...] = proj(C_NG, LANES)
    ones_blk = ones_ref[...]
    rc, rs1, rs2 = rc_ref[...], rs1_ref[...], rs2_ref[...]
    scale = HEAD_DIM ** -0.5
    for j in range(N_HEADS):
        y = proj(C_Q + j * LANES, LANES)
        yr = _head_norm_rope(y, gq_ref[...], ones_blk, rc, rs1, rs2)
        q_ref[:, j * LANES:(j + 1) * LANES] = (yr * scale).astype(q_ref.dtype)
    kv_outs = (kc_ref, vc_ref, ks_ref, vs_ref, kw_ref, vw_ref)
    for j in range(3):
        yk = _head_norm_rope(proj(C_KV + 2 * j * LANES, LANES), gk_ref[j:j + 1, :], ones_blk, rc, rs1, rs2)
        yv = proj(C_KV + (2 * j + 1) * LANES, LANES)
        for i, y in ((2 * j, yk), (2 * j + 1, yv)):
            kv_outs[i][...] = y.astype(kv_outs[i].dtype)
            if kv_t_refs:
                kv_t_refs[i][0] = y.T


def _rope_tables(pos):
    half = ROPE_DIM // 2
    inv = ROPE_THETA ** (-jnp.arange(half, dtype=jnp.float32) / half)
    ang = pos.astype(jnp.float32)[:, None] * inv[None, :]
    cos, sin = jnp.cos(ang), jnp.sin(ang)
    p = pos.shape[0]
    ones = jnp.ones((p, HEAD_DIM - ROPE_DIM), jnp.float32)
    zeros = jnp.zeros((p, HEAD_DIM - ROPE_DIM), jnp.float32)
    zh = jnp.zeros((p, half), jnp.float32)
    rc = jnp.concatenate([cos, cos, ones], axis=-1)
    rs1 = jnp.concatenate([-sin, zh, zeros], axis=-1)
    rs2 = jnp.concatenate([zh, sin, zeros], axis=-1)
    tile2 = lambda t: jnp.concatenate([t, t], axis=-1)
    return tile2(rc), tile2(rs1), tile2(rs2)


def _ones_block64():
    i = np.arange(LANES)
    return jnp.asarray((i[:, None] // HEAD_DIM) == (i[None, :] // HEAD_DIM), dtype=jnp.bfloat16)


def _input_projection(x2d, pos_rows, tm, g_mix, w_pad, g_q, g_k3, kv_transposed=False):
    m = x2d.shape[0]
    p = pos_rows.shape[0]
    assert m % tm == 0 and p % tm == 0
    rc, rs1, rs2 = _rope_tables(pos_rows)
    n_tab = p // tm
    row = lambda i: (i, 0)
    const = lambda i: (0, 0)
    tab = lambda i: (i % n_tab, 0)
    f32 = jnp.float32
    kv_dt = jnp.bfloat16 if kv_transposed else f32
    out_shape = (
        jax.ShapeDtypeStruct((m, 2 * D_RNN), f32),
        jax.ShapeDtypeStruct((m, D_QP), jnp.bfloat16),
    ) + tuple(jax.ShapeDtypeStruct((m, D_KV), kv_dt) for _ in range(6)) + (
        jax.ShapeDtypeStruct((m, LANES), f32),
        jax.ShapeDtypeStruct((m, 2 * D_MODEL), f32),
    )
    out_specs = (
        pl.BlockSpec((tm, 2 * D_RNN), row),
        pl.BlockSpec((tm, D_QP), row),
    ) + tuple(pl.BlockSpec((tm, D_KV), row) for _ in range(6)) + (
        pl.BlockSpec((tm, LANES), row),
        pl.BlockSpec((tm, 2 * D_MODEL), row),
    )
    if kv_transposed:
        out_shape += tuple(jax.ShapeDtypeStruct((m // p, D_KV, p), f32) for _ in range(6))
        out_specs += tuple(pl.BlockSpec((1, D_KV, tm), lambda i: (i // n_tab, 0, i % n_tab)) for _ in range(6))
    return pl.pallas_call(
        _proj_kernel,
        grid=(m // tm,),
        in_specs=[
            pl.BlockSpec((tm, D_MODEL), row),
            pl.BlockSpec((1, D_MODEL), const),
            pl.BlockSpec((D_MODEL, N_IN_PAD), const),
            pl.BlockSpec((LANES, LANES), const),
            pl.BlockSpec((1, LANES), const),
            pl.BlockSpec((3, LANES), const),
            pl.BlockSpec((tm, LANES), tab),
            pl.BlockSpec((tm, LANES), tab),
            pl.BlockSpec((tm, LANES), tab),
        ],
        out_specs=out_specs,
        out_shape=out_shape,
        compiler_params=pltpu.CompilerParams(dimension_semantics=("arbitrary",), vmem_limit_bytes=VMEM_LIMIT),
        name="input_projection",
    )(x2d, g_mix.reshape(1, D_MODEL), w_pad, _ones_block64(), g_q, g_k3, rc, rs1, rs2)


SLC_CHUNK = 1024
WIN_SPAN = WINDOW + LANES
ROWS = GROUP * Q_BLOCK


def _dot_nt(a, b):
    return lax.dot_general(a, b, (((1,), (1,)), ((), ())), preferred_element_type=jnp.float32)


def _softmax_rows(s, mask):
    s = jnp.where(mask, s, NEG)
    m = jnp.max(s, axis=-1, keepdims=True)
    p = jnp.where(mask, jnp.exp(s - m), 0.0)
    return p / jnp.maximum(jnp.sum(p, axis=-1, keepdims=True), 1e-30)


def _split3_bf16(x):
    hi = x.astype(jnp.bfloat16)
    r1 = x - hi.astype(jnp.float32)
    mid = r1.astype(jnp.bfloat16)
    lo = (r1 - mid.astype(jnp.float32)).astype(jnp.bfloat16)
    return hi, mid, lo


def _nsa_prompt_kernel(q_ref, ksa_ref, vs_ref, kw_ref, vw_ref, ck_ref, cv_ref, ng_ref, mmt_ref,
                       o_ref, sv_scr):
    f32, bf16 = jnp.float32, jnp.bfloat16
    qb = pl.program_id(1)
    row = lax.broadcasted_iota(jnp.int32, (ROWS, 1), 0)
    pos_t = qb * Q_BLOCK + (row & (Q_BLOCK - 1))
    ncp = ck_ref.shape[1]

    qh = [jnp.concatenate([q_ref[0, :, (h * GROUP + g) * LANES:(h * GROUP + g + 1) * LANES]
                           for g in range(GROUP)], axis=0) for h in range(N_KV)]

    ck, cv = ck_ref[0], cv_ref[0]
    c_end = lax.broadcasted_iota(jnp.int32, (1, ncp), 1) * CMP_STRIDE + (CMP_LEN - 1)
    cmask = c_end <= pos_t
    o_cmp, psum = [], []
    for h in range(N_KV):
        p = _softmax_rows(_dot_nt(qh[h], ck), cmask)
        o_cmp.append(jnp.dot(p.astype(bf16), cv, preferred_element_type=f32))
        psum.append(p[0:Q_BLOCK] + p[Q_BLOCK:2 * Q_BLOCK] + p[2 * Q_BLOCK:3 * Q_BLOCK] + p[3 * Q_BLOCK:])
    ps = jnp.concatenate(psum, axis=0)

    mmt = mmt_ref[...]
    sc = sum(_dot_nt(mmt, part) for part in _split3_bf16(ps))[0:Q_BLOCK]
    jio = lax.broadcasted_iota(jnp.int32, (Q_BLOCK, LANES), 0)
    forced = (jio == 0) | (jio == qb) | (jio == qb - 1)
    sv = jnp.where(forced, BIG, jnp.where(jio <= qb, sc, NEG))
    sv_scr[...] = sv

    def rank_body(i, cnt):
        r = sv_scr[pl.ds(i, 1), :]
        beats = (r > sv) | ((r == sv) & (jio > i))
        return cnt + jnp.where(beats, 1.0, 0.0)

    cnt = lax.fori_loop(0, qb + 1, rank_body, jnp.zeros((Q_BLOCK, LANES), f32))
    bias_lo = jnp.where((cnt < N_SEL) & (sv > NEG / 2), 0.0, NEG)
    lane = lax.broadcasted_iota(jnp.int32, (Q_BLOCK, LANES), 1)
    bias_t = jnp.concatenate([jnp.where(lane >= Q_BLOCK, bias_lo, 0.0),
                              jnp.where(lane < Q_BLOCK, bias_lo, 0.0)], axis=0)
    bias = bias_t.T.astype(bf16)

    kio = lax.broadcasted_iota(jnp.int32, (1, SLC_CHUNK), 1)
    wio = lax.broadcasted_iota(jnp.int32, (1, WIN_SPAN), 1)
    wbase = pl.multiple_of((jnp.maximum(qb - WINDOW // Q_BLOCK, 0) // 2) * LANES, LANES)
    dpos = pos_t - (wbase + wio)
    wmask = (dpos >= 0) & (dpos < WINDOW)
    n_chunks = qb // (SLC_CHUNK // SLC_BLOCK) + 1
    sig = jax.nn.sigmoid(ng_ref[0])
    qa = [qh[h] + jnp.concatenate([bias[h * Q_BLOCK:(h + 1) * Q_BLOCK]] * GROUP, axis=0) for h in range(N_KV)]

    def chunk(c, carry, causal):
        k0 = pl.multiple_of(c * SLC_CHUNK, SLC_CHUNK)
        v = vs_ref[0, pl.ds(k0, SLC_CHUNK), :]
        new = []
        for h in range(N_KV):
            m, l, acc = carry[h]
            s = _dot_nt(qa[h], ksa_ref[0, h, pl.ds(k0, SLC_CHUNK), :])
            if causal:
                s = jnp.where(k0 + kio <= pos_t, s, NEG)
            m_new = jnp.maximum(m, jnp.max(s, axis=-1, keepdims=True))
            alpha = jnp.exp(m - m_new)
            p = jnp.exp(s - m_new)
            l = alpha * l + jnp.sum(p, axis=-1, keepdims=True)
            acc = alpha * acc + jnp.dot(p.astype(bf16), v, preferred_element_type=f32)
            new.append((m_new, l, acc))
        return tuple(new)

    init = (jnp.full((ROWS, 1), NEG, f32), jnp.zeros((ROWS, 1), f32), jnp.zeros((ROWS, LANES), f32))
    carry = lax.fori_loop(0, n_chunks - 1, lambda c, cr: chunk(c, cr, False), (init, init))
    carry = chunk(n_chunks - 1, carry, True)

    out_h = []
    for h in range(N_KV):
        _, l, acc = carry[h]
        o_slc = acc / jnp.maximum(l, 1e-30)

        pw = _softmax_rows(_dot_nt(qh[h], kw_ref[0, pl.ds(wbase, WIN_SPAN), :]), wmask)
        o_win = jnp.dot(pw.astype(bf16), vw_ref[0, pl.ds(wbase, WIN_SPAN), :], preferred_element_type=f32)

        outs = []
        for g in range(GROUP):
            rs = slice(g * Q_BLOCK, (g + 1) * Q_BLOCK)
            c0 = (h * GROUP + g) * 3
            outs.append(sig[:, c0:c0 + 1] * o_cmp[h][rs] + sig[:, c0 + 1:c0 + 2] * o_slc[rs]
                        + sig[:, c0 + 2:c0 + 3] * o_win[rs])
        out_h.append(outs)
    for g in range(GROUP):
        o_ref[0, :, g * LANES:(g + 1) * LANES] = jnp.where(lane < HEAD_DIM, out_h[0][g], out_h[1][g]).astype(o_ref.dtype)


def _block_score_matrix(ncp):
    ratio = SLC_BLOCK // CMP_STRIDE
    ovl = CMP_LEN // CMP_STRIDE - 1
    j = np.arange(LANES)[:, None]
    c = np.arange(ncp)[None, :]
    m = (c >= ratio * j - ovl) & (c <= ratio * j + ratio - 1) & (j < Q_BLOCK) & (c < ncp - 1)
    return jnp.asarray(m, dtype=jnp.bfloat16)


def _nsa_prompt_attention(q_pad, ks, vs, kw, vw, cmp_k, cmp_v, ng):
    b, s, _ = q_pad.shape
    assert s % SLC_CHUNK == 0 and s // SLC_BLOCK <= Q_BLOCK and s >= WIN_SPAN and Q_BLOCK == SLC_BLOCK
    bf16 = jnp.bfloat16
    ncp = cmp_k.shape[1]
    onehot = jax.nn.one_hot(jnp.arange(s) // SLC_BLOCK, Q_BLOCK, dtype=bf16)
    onehot = jnp.broadcast_to(onehot[None], (b, s, Q_BLOCK))
    ksb = ks.astype(bf16)
    ksa = jnp.stack([jnp.concatenate([ksb[:, :, :HEAD_DIM], onehot], axis=-1),
                     jnp.concatenate([onehot, ksb[:, :, HEAD_DIM:]], axis=-1)], axis=1)
    per_b = lambda bi, i: (bi, 0, 0)
    return pl.pallas_call(
        _nsa_prompt_kernel,
        grid=(b, s // Q_BLOCK),
        in_specs=[
            pl.BlockSpec((1, Q_BLOCK, D_QP), lambda bi, i: (bi, i, 0)),
            pl.BlockSpec((1, N_KV, s, LANES), lambda bi, i: (bi, 0, 0, 0)),
            pl.BlockSpec((1, s, LANES), per_b),
            pl.BlockSpec((1, s, LANES), per_b),
            pl.BlockSpec((1, s, LANES), per_b),
            pl.BlockSpec((1, ncp, LANES), per_b),
            pl.BlockSpec((1, ncp, LANES), per_b),
            pl.BlockSpec((1, Q_BLOCK, LANES), lambda bi, i: (bi, i, 0)),
            pl.BlockSpec((LANES, ncp), lambda bi, i: (0, 0)),
        ],
        out_specs=pl.BlockSpec((1, Q_BLOCK, D_Q), lambda bi, i: (bi, i, 0)),
        out_shape=jax.ShapeDtypeStruct((b, s, D_Q), bf16),
        scratch_shapes=[pltpu.VMEM((Q_BLOCK, LANES), jnp.float32)],
        compiler_params=pltpu.CompilerParams(dimension_semantics=("arbitrary", "arbitrary"),
                                             vmem_limit_bytes=VMEM_LIMIT),
        name="nsa_prompt_attention",
    )(q_pad, ksa, vs.astype(bf16), kw.astype(bf16), vw.astype(bf16), cmp_k, cmp_v, ng, _block_score_matrix(ncp))


CHUNK_W = CMP_STRIDE * D_KV
CMP_PARTS = CMP_LEN // CMP_STRIDE
HID2 = N_KV * CMP_HID


def _compress_weights(pos_emb, w1, w2):
    bf16 = jnp.bfloat16
    eye = jnp.eye(N_KV, dtype=w1.dtype)
    w1b = w1.reshape(CMP_PARTS, CMP_STRIDE, HEAD_DIM, CMP_HID)
    w1p = jnp.einsum('pldf,hk->lhdpkf', w1b, eye).reshape(CHUNK_W, CMP_PARTS * HID2).astype(bf16)
    w2p = jnp.einsum('fd,hk->hfkd', w2, eye).reshape(HID2, D_KV).astype(bf16)
    pos8 = jnp.zeros((8, CMP_LEN * HEAD_DIM), bf16).at[0].set(pos_emb.reshape(-1).astype(bf16))
    w1f = w1.reshape(CMP_LEN * HEAD_DIM, CMP_HID).astype(bf16)
    return w1p, w2p, pos8, w1f


def _compress_rows(x, w1p, w2p, pos8, w1f):
    f32 = jnp.float32
    r = x.shape[0]
    part = jnp.dot(x, w1p, preferred_element_type=f32)
    bias = jnp.dot(pos8, w1f, preferred_element_type=f32)[0:1]
    bias = jnp.concatenate([bias] * N_KV, axis=-1)
    hid = bias + part[:, :HID2] + pltpu.roll(part[:, HID2:], r - 1, 0)
    out = jnp.dot(jax.nn.gelu(hid).astype(jnp.bfloat16), w2p, preferred_element_type=f32)
    rows = lax.broadcasted_iota(jnp.int32, (r, 1), 0)
    return jnp.where(rows < r - 1, out, 0.0)


def _compress_prompt_kernel(xk_ref, xv_ref, w1k_ref, w2k_ref, pk_ref, fk_ref, w1v_ref, w2v_ref, pv_ref, fv_ref,
                            ok_ref, ov_ref):
    bf16 = jnp.bfloat16
    ok_ref[0] = _compress_rows(xk_ref[0].astype(bf16), w1k_ref[...], w2k_ref[...], pk_ref[...],
                               fk_ref[...]).astype(ok_ref.dtype)
    ov_ref[0] = _compress_rows(xv_ref[0].astype(bf16), w1v_ref[...], w2v_ref[...], pv_ref[...],
                               fv_ref[...]).astype(ov_ref.dtype)


def _compress_prompt(kc, vc, wk, wv):
    b, s, _ = kc.shape
    r = s // CMP_STRIDE
    xspec = pl.BlockSpec((1, r, CHUNK_W), lambda i: (i, 0, 0))
    wspecs = [pl.BlockSpec(w.shape, lambda i: (0, 0)) for w in wk]
    ospec = pl.BlockSpec((1, r, D_KV), lambda i: (i, 0, 0))
    oshape = jax.ShapeDtypeStruct((b, r, D_KV), jnp.bfloat16)
    return pl.pallas_call(
        _compress_prompt_kernel,
        grid=(b,),
        in_specs=[xspec, xspec] + wspecs + wspecs,
        out_specs=(ospec, ospec),
        out_shape=(oshape, oshape),
        compiler_params=pltpu.CompilerParams(dimension_semantics=("arbitrary",), vmem_limit_bytes=VMEM_LIMIT),
        name="compress_prompt",
    )(kc.reshape(b, r, CHUNK_W), vc.reshape(b, r, CHUNK_W), *wk, *wv)


N_PAGES = PAST_LEN // PAGE_SIZE
PAST_CHUNKS = PAST_LEN // CMP_STRIDE
PAST_BLOCKS = PAST_LEN // SLC_BLOCK
BLOCKS_PER_PAGE = PAGE_SIZE // SLC_BLOCK
CHUNKS_PER_PAGE = PAGE_SIZE // CMP_STRIDE
SEL_ROWS = 256
QROWS = SUBLANES
REGROUP_UNROLL = 4


def _sample_cmp_kernel(pt_ref, q8_ref, kpool, vpool, w1k_ref, w2k_ref, pk_ref, fk_ref, w1v_ref, w2v_ref, pv_ref,
                       fv_ref, ocmp_ref, psum_ref, kbuf, vbuf, x_scr, t_scr, sem):
    f32, bf16 = jnp.float32, jnp.bfloat16
    n = pl.program_id(0)

    def fetch(s, slot):
        def body(j, _):
            pg = pt_ref[s, j]
            pltpu.make_async_copy(kpool.at[pg], kbuf.at[slot, j], sem.at[0, slot]).start()
            pltpu.make_async_copy(vpool.at[pg], vbuf.at[slot, j], sem.at[1, slot]).start()
            return 0
        lax.fori_loop(0, N_PAGES, body, 0)

    @pl.when(n == 0)
    def _():
        fetch(0, 0)

    @pl.when(n + 1 < pl.num_programs(0))
    def _():
        fetch(n + 1, (n + 1) % 2)

    slot = n % 2
    pltpu.make_async_copy(kbuf.at[slot], kbuf.at[slot], sem.at[0, slot]).wait()
    pltpu.make_async_copy(vbuf.at[slot], vbuf.at[slot], sem.at[1, slot]).wait()

    def regroup_body(jj, _):
        for u in range(REGROUP_UNROLL):
            j = jj * (REGROUP_UNROLL // 2) + u // 2
            t_scr[u] = (kbuf, vbuf)[u % 2][slot, j].T
            r0 = pl.multiple_of(j * CHUNKS_PER_PAGE, CHUNKS_PER_PAGE)
            for l in range(CMP_STRIDE):
                x_scr[u % 2, pl.ds(r0, CHUNKS_PER_PAGE), l * D_KV:(l + 1) * D_KV] = (
                    t_scr[u, pl.ds(l, CHUNKS_PER_PAGE, stride=CMP_STRIDE), :])
        return 0

    lax.fori_loop(0, N_PAGES // (REGROUP_UNROLL // 2), regroup_body, 0)
    ck = _compress_rows(x_scr[0].astype(bf16), w1k_ref[...], w2k_ref[...], pk_ref[...], fk_ref[...]).astype(bf16)
    cv = _compress_rows(x_scr[1].astype(bf16), w1v_ref[...], w2v_ref[...], pv_ref[...], fv_ref[...]).astype(bf16)
    c_end = lax.broadcasted_iota(jnp.int32, (1, PAST_CHUNKS), 1) * CMP_STRIDE + (CMP_LEN - 1)
    cmask = c_end <= PAST_LEN
    rows = lax.broadcasted_iota(jnp.int32, (QROWS, 1), 0)
    ps = jnp.zeros((QROWS, PAST_CHUNKS), f32)
    for h in range(N_KV):
        p = _softmax_rows(_dot_nt(q8_ref[0, h], ck), cmask)
        ocmp_ref[0, h] = jnp.dot(p.astype(bf16), cv, preferred_element_type=f32)
        ps = jnp.where(rows == h, p[0:1] + p[1:2] + p[2:3] + p[3:4], ps)
    psum_ref[0] = ps


def _sample_cmp(page_table, q8, k_pool, v_pool, wk, wv):
    n = q8.shape[0]
    f32 = jnp.float32
    wspecs = [pl.BlockSpec(w.shape, lambda i, pt: (0, 0)) for w in wk]
    grid_spec = pltpu.PrefetchScalarGridSpec(
        num_scalar_prefetch=1,
        grid=(n,),
        in_specs=[pl.BlockSpec((1, N_KV, QROWS, LANES), lambda i, pt: (i, 0, 0, 0)),
                  pl.BlockSpec(memory_space=pl.ANY), pl.BlockSpec(memory_space=pl.ANY)] + wspecs + wspecs,
        out_specs=(pl.BlockSpec((1, N_KV, QROWS, LANES), lambda i, pt: (i, 0, 0, 0)),
                   pl.BlockSpec((1, QROWS, PAST_CHUNKS), lambda i, pt: (i, 0, 0))),
        scratch_shapes=[pltpu.VMEM((2, N_PAGES, D_KV, PAGE_SIZE), f32),
                        pltpu.VMEM((2, N_PAGES, D_KV, PAGE_SIZE), f32),
                        pltpu.VMEM((2, PAST_CHUNKS, CHUNK_W), f32),
                        pltpu.VMEM((REGROUP_UNROLL, PAGE_SIZE, D_KV), f32),
                        pltpu.SemaphoreType.DMA((2, 2))],
    )
    return pl.pallas_call(
        _sample_cmp_kernel,
        grid_spec=grid_spec,
        out_shape=(jax.ShapeDtypeStruct((n, N_KV, QROWS, LANES), f32),
                   jax.ShapeDtypeStruct((n, QROWS, PAST_CHUNKS), f32)),
        compiler_params=pltpu.CompilerParams(dimension_semantics=("arbitrary",), vmem_limit_bytes=VMEM_LIMIT),
        name="sample_compress_attention",
    )(page_table, q8, k_pool, v_pool, *wk, *wv)


def _sample_select_kernel(p_ref, mmt_ref, tri_ref, idx_ref, sv_scr):
    f32 = jnp.float32
    cols = p_ref.shape[0]
    cur = PAST_LEN // SLC_BLOCK
    mmt = mmt_ref[...]
    sc = sum(_dot_nt(mmt, part) for part in _split3_bf16(p_ref[...]))
    jio = lax.broadcasted_iota(jnp.int32, (SEL_ROWS, cols), 0)
    forced = (jio == 0) | (jio == cur) | (jio == cur - 1)
    sv = jnp.where(forced, BIG, jnp.where(jio <= cur, sc, NEG))
    sv_scr[...] = sv

    def rank_body(i, cnt):
        r = sv_scr[pl.ds(i, 1), :]
        beats = (r > sv) | ((r == sv) & (jio > i))
        return cnt + jnp.where(beats, 1.0, 0.0)

    cnt = lax.fori_loop(0, cur + 1, rank_body, jnp.zeros((SEL_ROWS, cols), f32))
    sel = (cnt < N_SEL) & (sv > NEG / 2)
    self = jnp.where(sel, 1.0, 0.0)
    before = jnp.dot(tri_ref[...], self.astype(jnp.bfloat16), preferred_element_type=f32)
    rows = []
    for k in range(N_SEL):
        rows.append(jnp.sum(jnp.where(sel & (before == k), jio, 0), axis=0, keepdims=True))
    idx_ref[...] = jnp.concatenate(rows, axis=0)


def _sample_select(p_sum):
    n = p_sum.shape[0]
    cols = n * N_KV
    p2 = p_sum[:, :N_KV, :].reshape(cols, PAST_CHUNKS)
    ratio = SLC_BLOCK // CMP_STRIDE
    ovl = CMP_LEN // CMP_STRIDE - 1
    j = np.arange(SEL_ROWS)[:, None]
    c = np.arange(PAST_CHUNKS)[None, :]
    mmt = (c >= ratio * j - ovl) & (c <= ratio * j + ratio - 1) & (j <= PAST_BLOCKS) & (c < PAST_CHUNKS - 1)
    tri = np.tril(np.ones((SEL_ROWS, SEL_ROWS), np.float32), -1)
    idx = pl.pallas_call(
        _sample_select_kernel,
        out_shape=jax.ShapeDtypeStruct((N_SEL, cols), jnp.int32),
        scratch_shapes=[pltpu.VMEM((SEL_ROWS, cols), jnp.float32)],
        compiler_params=pltpu.CompilerParams(vmem_limit_bytes=VMEM_LIMIT),
        name="sample_select",
    )(p2, jnp.asarray(mmt, dtype=jnp.bfloat16), jnp.asarray(tri, dtype=jnp.bfloat16))
    return idx.T.reshape(n, N_KV * N_SEL)


SLC_COLS = N_SEL * PAGE_SIZE


def _sample_attn_kernel(pt_ref, idx_ref, q_ref, kpool, vpool, knew_ref, vnew_ref, kwin_ref, vwin_ref, kwn_ref,
                        vwn_ref, ocmp_ref, ng_ref, o_ref, kwo_ref, vwo_ref, kg, vg, sem):
    f32, bf16 = jnp.float32, jnp.bfloat16
    n = pl.program_id(0)
    nblk = N_KV * N_SEL

    def fetch(s, slot):
        def body(i, _):
            jp = jnp.minimum(idx_ref[s, i], PAST_BLOCKS - 1)
            pg = pt_ref[s, jp // BLOCKS_PER_PAGE]
            h = i // N_SEL
            pltpu.make_async_copy(kpool.at[pg, h], kg.at[slot, i], sem.at[0, slot]).start()
            pltpu.make_async_copy(vpool.at[pg, h], vg.at[slot, i], sem.at[1, slot]).start()
            return 0
        lax.fori_loop(0, nblk, body, 0)

    @pl.when(n == 0)
    def _():
        fetch(0, 0)

    @pl.when(n + 1 < pl.num_programs(0))
    def _():
        fetch(n + 1, (n + 1) % 2)

    slot = n % 2
    pltpu.make_async_copy(kg.at[slot], kg.at[slot], sem.at[0, slot]).wait()
    pltpu.make_async_copy(vg.at[slot], vg.at[slot], sem.at[1, slot]).wait()

    pos = PAST_LEN
    col = lax.broadcasted_iota(jnp.int32, (1, SLC_COLS + LANES), 1)
    wcol = lax.broadcasted_iota(jnp.int32, (1, WINDOW + LANES), 1)
    wpos = jnp.where(wcol < WINDOW, pos - WINDOW + wcol, pos + (wcol - WINDOW))
    wmask = (pos - wpos >= 0) & (pos - wpos < WINDOW)
    wlane = lax.broadcasted_iota(jnp.int32, (1, WINDOW), 1)
    for h in range(N_KV):
        q = q_ref[0, h]
        new_sel = jnp.int32(0)
        lo = jnp.full((1, SLC_COLS + LANES), SLC_COLS + LANES, jnp.int32)
        for p in range(N_SEL):
            j = idx_ref[n, h * N_SEL + p]
            start = p * PAGE_SIZE + (j % BLOCKS_PER_PAGE) * SLC_BLOCK
            start = jnp.where(j < PAST_BLOCKS, start, SLC_COLS + LANES)
            lo = jnp.where((col >> 7) == p, start, lo)
            new_sel = jnp.maximum(new_sel, (j >= PAST_BLOCKS).astype(jnp.int32))
        valid = jnp.where((col >= lo) & (col < lo + SLC_BLOCK), 1, 0)
        kpos_new = PAST_LEN + (col - SLC_COLS)
        valid = jnp.where(col >= SLC_COLS, jnp.where(kpos_new <= pos, new_sel, 0), valid) > 0
        kt = jnp.concatenate([kg[slot, h * N_SEL + p] for p in range(N_SEL)] + [knew_ref[0, h]], axis=-1)
        vt = jnp.concatenate([vg[slot, h * N_SEL + p] for p in range(N_SEL)] + [vnew_ref[0, h]], axis=-1)
        ps = _softmax_rows(jnp.dot(q, kt.astype(bf16), preferred_element_type=f32), valid)
        o_slc = _dot_nt(ps.astype(bf16), vt.astype(bf16))
        kwin, vwin = kwin_ref[0, h], vwin_ref[0, h]
        kw_all = jnp.concatenate([kwin, kwn_ref[0, h]], axis=-1).astype(bf16)
        vw_all = jnp.concatenate([vwin, vwn_ref[0, h]], axis=-1).astype(bf16)
        pw = _softmax_rows(jnp.dot(q, kw_all, preferred_element_type=f32), wmask)
        o_win = _dot_nt(pw.astype(bf16), vw_all)
        sig = jax.nn.sigmoid(ng_ref[0, h])
        o_cmp = ocmp_ref[0, h][:, h * HEAD_DIM:(h + 1) * HEAD_DIM]
        o_ref[0, h] = sig[0] * o_cmp + sig[1] * o_slc + sig[2] * o_win
        kwo_ref[0, h] = jnp.where(wlane == WINDOW - 1, kwn_ref[0, h][:, 0:1], pltpu.roll(kwin, WINDOW - 1, 1))
        vwo_ref[0, h] = jnp.where(wlane == WINDOW - 1, vwn_ref[0, h][:, 0:1], pltpu.roll(vwin, WINDOW - 1, 1))


def _sample_attn(page_table, idx, q8u, k_pool_t, v_pool_t, knew_t, vnew_t, k_win_t, v_win_t, kwn_t, vwn_t, o_cmp, ngb):
    n = q8u.shape[0]
    f32 = jnp.float32
    assert k_win_t.shape[-1] == WINDOW
    per_n4 = lambda i, pt, ix: (i, 0, 0, 0)
    ext_spec = pl.BlockSpec((1, N_KV, HEAD_DIM, LANES), per_n4)
    win_spec = pl.BlockSpec((1, N_KV, HEAD_DIM, WINDOW), per_n4)
    q_spec = pl.BlockSpec((1, N_KV, QROWS, HEAD_DIM), per_n4)
    grid_spec = pltpu.PrefetchScalarGridSpec(
        num_scalar_prefetch=2,
        grid=(n,),
        in_specs=[q_spec, pl.BlockSpec(memory_space=pl.ANY), pl.BlockSpec(memory_space=pl.ANY),
                  ext_spec, ext_spec, win_spec, win_spec, ext_spec, ext_spec,
                  pl.BlockSpec((1, N_KV, QROWS, LANES), per_n4),
                  pl.BlockSpec((1, N_KV, 3, QROWS, HEAD_DIM), lambda i, pt, ix: (i, 0, 0, 0, 0))],
        out_specs=(q_spec, win_spec, win_spec),
        scratch_shapes=[pltpu.VMEM((2, N_KV * N_SEL, HEAD_DIM, PAGE_SIZE), f32),
                        pltpu.VMEM((2, N_KV * N_SEL, HEAD_DIM, PAGE_SIZE), f32),
                        pltpu.SemaphoreType.DMA((2, 2))],
    )
    return pl.pallas_call(
        _sample_attn_kernel,
        grid_spec=grid_spec,
        out_shape=(jax.ShapeDtypeStruct((n, N_KV, QROWS, HEAD_DIM), f32),
                   jax.ShapeDtypeStruct((n, N_KV, HEAD_DIM, WINDOW), f32),
                   jax.ShapeDtypeStruct((n, N_KV, HEAD_DIM, WINDOW), f32)),
        compiler_params=pltpu.CompilerParams(dimension_semantics=("arbitrary",), vmem_limit_bytes=VMEM_LIMIT),
        name="sample_attention",
    )(page_table, idx, q8u, k_pool_t, v_pool_t, knew_t, vnew_t, k_win_t, v_win_t, kwn_t, vwn_t, o_cmp, ngb)


def _nsa_sample_paged(q_pad, ks, vs, kw, vw, ng, page_table, ck_cmp, cv_cmp, ck_slc, cv_slc, ck_win, cv_win,
                      cmp_wk, cmp_wv):
    n = q_pad.shape[0]
    n_phys = ck_cmp.shape[0]
    pool_t = lambda p: p.transpose(0, 2, 3, 1)
    q4 = q_pad.reshape(n, N_KV, GROUP, N_KV, HEAD_DIM)
    q8 = jnp.pad(q_pad.reshape(n, N_KV, GROUP, LANES), ((0, 0), (0, 0), (0, QROWS - GROUP), (0, 0)))
    q8u = jnp.pad(jnp.stack([q4[:, h, :, h, :] for h in range(N_KV)], axis=1),
                  ((0, 0), (0, 0), (0, QROWS - GROUP), (0, 0)))
    o_cmp, p_sum = _sample_cmp(page_table, q8, pool_t(ck_cmp).reshape(n_phys, D_KV, PAGE_SIZE),
                               pool_t(cv_cmp).reshape(n_phys, D_KV, PAGE_SIZE), cmp_wk, cmp_wv)
    idx = _sample_select(p_sum)
    ngb = ng[:, :N_NG].reshape(n, N_KV, GROUP, 3).transpose(0, 1, 3, 2)
    ngb = jnp.pad(ngb, ((0, 0), (0, 0), (0, 0), (0, QROWS - GROUP)))
    ngb = jnp.broadcast_to(ngb[..., None], (n, N_KV, 3, QROWS, HEAD_DIM))
    col0 = lambda t: jnp.pad(t.reshape(n, N_KV, HEAD_DIM, 1), ((0, 0), (0, 0), (0, 0), (0, LANES - 1)))
    o8, kwo, vwo = _sample_attn(page_table, idx, q8u, pool_t(ck_slc), pool_t(cv_slc), col0(ks), col0(vs),
                                pool_t(ck_win), pool_t(cv_win), col0(kw), col0(vw), o_cmp, ngb)
    o = o8[:, :, :GROUP, :].transpose(0, 2, 1, 3).reshape(n, D_Q).astype(jnp.bfloat16)
    return o, kwo.transpose(0, 3, 1, 2), vwo.transpose(0, 3, 1, 2)


LRU_PAIRS = D_RNN // LANES


def _lru_gates(xc, wa_ref, ba, wx_ref, bx, lam):
    f32 = jnp.float32
    xb = xc.astype(jnp.bfloat16)
    rg = jnp.concatenate([jnp.dot(xb[:, j * LANES:(j + 1) * LANES], wa_ref[j], preferred_element_type=f32)
                          for j in range(LRU_PAIRS)], axis=-1)
    ig = jnp.concatenate([jnp.dot(xb[:, j * LANES:(j + 1) * LANES], wx_ref[j], preferred_element_type=f32)
                          for j in range(LRU_PAIRS)], axis=-1)
    r = jax.nn.sigmoid(rg + ba)
    i = jax.nn.sigmoid(ig + bx)
    log_a = -LRU_C * r * jax.nn.softplus(-lam)
    a = jnp.exp(log_a)
    u = jnp.sqrt(-jnp.tanh(log_a) * (a * a + 1.0)) * (i * xc)
    return a, u


def _lru_prompt_kernel(arag_ref, cbuf_ref, h0_ref, cw_ref, cb_ref, wa_ref, ba_ref, wx_ref, bx_ref, lam_ref,
                       y_ref, hl_ref, prev_scr, h_scr, hbuf):
    f32 = jnp.float32
    t = arag_ref.shape[0]

    @pl.when(pl.program_id(1) == 0)
    def _():
        prev_scr[...] = cbuf_ref[0]
        h_scr[...] = h0_ref[0]

    ar = arag_ref[:, :D_RNN]
    row8 = lax.broadcasted_iota(jnp.int32, (SUBLANES, 1), 0)
    prev = prev_scr[...]
    xc = cb_ref[...] + cw_ref[CONV_W - 1:CONV_W, :] * ar
    for k in range(1, CONV_W):
        rolled = pltpu.roll(ar, k, 0)
        head = jnp.where(row8 < k, pltpu.roll(prev, k, 0), rolled[0:SUBLANES])
        shifted = jnp.concatenate([head, rolled[SUBLANES:]], axis=0)
        xc = xc + cw_ref[CONV_W - 1 - k:CONV_W - k, :] * shifted
    prev_scr[...] = ar[t - SUBLANES:]

    a, u = _lru_gates(xc, wa_ref, ba_ref[...], wx_ref, bx_ref[...], lam_ref[...])

    rowm = lax.broadcasted_iota(jnp.int32, (t, 1), 0) & (SUBLANES - 1)
    for s in (1, 2, 4):
        ok = rowm >= s
        u = jnp.where(ok, a * pltpu.roll(u, s, 0) + u, u)
        a = jnp.where(ok, a * pltpu.roll(a, s, 0), a)
    h = h_scr[...]
    for g in range(t // SUBLANES):
        rs = slice(g * SUBLANES, (g + 1) * SUBLANES)
        hg = a[rs] * h + u[rs]
        hbuf[rs, :] = hg
        h = jnp.broadcast_to(hg[SUBLANES - 1:SUBLANES, :], (SUBLANES, D_RNN))
    h_scr[...] = h
    hl_ref[0] = h
    y_ref[...] = (hbuf[...] * jax.nn.gelu(arag_ref[:, D_RNN:])).astype(y_ref.dtype)


def _lru_pair_weights(w):
    w4 = w.reshape(LRU_PAIRS, 2, LRU_BW, LRU_BW)
    eye = jnp.eye(2, dtype=w.dtype)
    return jnp.einsum('jaio,ab->jaibo', w4, eye).reshape(LRU_PAIRS, LANES, LANES).astype(jnp.bfloat16)


def _lru_weights(conv_w, conv_b, wa, ba, wx, bx, lam):
    row = lambda v: v.reshape(1, D_RNN).astype(jnp.float32)
    return (conv_w.astype(jnp.float32), row(conv_b), _lru_pair_weights(wa), row(ba), _lru_pair_weights(wx), row(bx),
            row(lam))


def _lru_prompt(arag, conv_buf, h0, lw, b, s, t):
    assert s % t == 0 and t % SUBLANES == 0
    nt = s // t
    cbuf8 = jnp.pad(conv_buf.astype(jnp.float32), ((0, 0), (SUBLANES - (CONV_W - 1), 0), (0, 0)))
    h08 = jnp.broadcast_to(h0.astype(jnp.float32)[:, None, :], (b, SUBLANES, D_RNN))
    const2 = lambda bi, i: (0, 0)
    const3 = lambda bi, i: (0, 0, 0)
    per_b = lambda bi, i: (bi, 0, 0)
    y, hl = pl.pallas_call(
        _lru_prompt_kernel,
        grid=(b, nt),
        in_specs=[
            pl.BlockSpec((t, 2 * D_RNN), lambda bi, i: (bi * nt + i, 0)),
            pl.BlockSpec((1, SUBLANES, D_RNN), per_b),
            pl.BlockSpec((1, SUBLANES, D_RNN), per_b),
            pl.BlockSpec((CONV_W, D_RNN), const2),
            pl.BlockSpec((1, D_RNN), const2),
            pl.BlockSpec((LRU_PAIRS, LANES, LANES), const3),
            pl.BlockSpec((1, D_RNN), const2),
            pl.BlockSpec((LRU_PAIRS, LANES, LANES), const3),
            pl.BlockSpec((1, D_RNN), const2),
            pl.BlockSpec((1, D_RNN), const2),
        ],
        out_specs=(pl.BlockSpec((t, D_RNN), lambda bi, i: (bi * nt + i, 0)),
                   pl.BlockSpec((1, SUBLANES, D_RNN), per_b)),
        out_shape=(jax.ShapeDtypeStruct((b * s, D_RNN), jnp.bfloat16),
                   jax.ShapeDtypeStruct((b, SUBLANES, D_RNN), jnp.float32)),
        scratch_shapes=[pltpu.VMEM((SUBLANES, D_RNN), jnp.float32), pltpu.VMEM((SUBLANES, D_RNN), jnp.float32),
                        pltpu.VMEM((t, D_RNN), jnp.float32)],
        compiler_params=pltpu.CompilerParams(dimension_semantics=("arbitrary", "arbitrary"),
                                             vmem_limit_bytes=VMEM_LIMIT),
        name="lru_prompt",
    )(arag, cbuf8, h08, *lw)
    return y, hl[:, 0, :]


def _lru_step_kernel(arag_ref, c0_ref, c1_ref, c2_ref, h0_ref, cw_ref, cb_ref, wa_ref, ba_ref, wx_ref, bx_ref,
                     lam_ref, y_ref, h_ref):
    ar = arag_ref[:, :D_RNN]
    xc = (cb_ref[...] + cw_ref[0:1, :] * c0_ref[...] + cw_ref[1:2, :] * c1_ref[...] + cw_ref[2:3, :] * c2_ref[...]
          + cw_ref[3:4, :] * ar)
    a, u = _lru_gates(xc, wa_ref, ba_ref[...], wx_ref, bx_ref[...], lam_ref[...])
    h = a * h0_ref[...] + u
    h_ref[...] = h
    y_ref[...] = (h * jax.nn.gelu(arag_ref[:, D_RNN:])).astype(y_ref.dtype)


def _lru_step(arag, conv_buf, h0, lw):
    n = arag.shape[0]
    f32 = jnp.float32
    cb = conv_buf.astype(f32)
    return pl.pallas_call(
        _lru_step_kernel,
        out_shape=(jax.ShapeDtypeStruct((n, D_RNN), jnp.bfloat16), jax.ShapeDtypeStruct((n, D_RNN), f32)),
        compiler_params=pltpu.CompilerParams(vmem_limit_bytes=VMEM_LIMIT),
        name="lru_step",
    )(arag, cb[:, 0], cb[:, 1], cb[:, 2], h0.astype(f32), *lw)


def _merge_router_kernel(x_ref, ylru_ref, onsa_ref, gab_ref, wa_ref, wb_ref, wo_ref, gffn_ref, rw_ref, rb_ref,
                         tri_ref, xn_ref, hn_ref, te_ref, tr_ref, tg_ref, cnt_ref, run_scr):
    f32, bf16 = jnp.float32, jnp.bfloat16
    tm = x_ref.shape[0]

    @pl.when(pl.program_id(0) == 0)
    def _():
        run_scr[...] = jnp.zeros_like(run_scr)

    ya = jnp.dot(ylru_ref[...], wa_ref[...], preferred_element_type=f32)
    yb = jnp.dot(onsa_ref[...], wb_ref[...], preferred_element_type=f32)
    mix = jax.nn.sigmoid(gab_ref[:, :D_MODEL]) * ya + jax.nn.sigmoid(gab_ref[:, D_MODEL:]) * yb
    xn = x_ref[...] + jnp.dot(mix.astype(bf16), wo_ref[...], preferred_element_type=f32)
    xn_ref[...] = xn
    hn = xn * lax.rsqrt(jnp.mean(xn * xn, axis=-1, keepdims=True) + EPS) * gffn_ref[...]
    hn_ref[...] = hn

    hn_hi = hn.astype(bf16)
    hn_lo = (hn - hn_hi.astype(f32)).astype(bf16)
    logits = (jnp.dot(hn_hi, rw_ref[0], preferred_element_type=f32)
              + jnp.dot(hn_lo, rw_ref[0], preferred_element_type=f32)
              + jnp.dot(hn_hi, rw_ref[1], preferred_element_type=f32)) + rb_ref[...]
    lane = lax.broadcasted_iota(jnp.int32, (tm, LANES), 1)
    ninf = -jnp.inf
    l = jnp.where(lane < N_EXPERTS, logits, ninf)
    onehots, m0 = [], None
    for k in range(TOP_K):
        m = jnp.max(l, axis=-1, keepdims=True)
        idx = jnp.min(jnp.where(l == m, lane, LANES), axis=-1, keepdims=True)
        oh = lane == idx
        onehots.append((oh, idx))
        m0 = m if k == 0 else m0
        l = jnp.where(oh, ninf, l)
    sel = onehots[0][0] | onehots[1][0] | onehots[2][0] | onehots[3][0]
    e = jnp.where(sel, jnp.exp(logits - m0), 0.0)
    gates = e / jnp.sum(e, axis=-1, keepdims=True)

    self = jnp.where(sel, 1.0, 0.0)
    incl = jnp.dot(tri_ref[...], self.astype(bf16), preferred_element_type=f32)
    run = run_scr[0:1, :]
    rank = run + incl - self
    total = run + incl[tm - 1:tm, :]
    run_scr[...] = jnp.broadcast_to(total, run_scr.shape)
    cnt_ref[...] = jnp.broadcast_to(total, cnt_ref.shape)

    te = jnp.zeros((tm, LANES), jnp.int32)
    tr = jnp.zeros((tm, LANES), f32)
    tg = jnp.zeros((tm, LANES), f32)
    for k, (oh, idx) in enumerate(onehots):
        te = jnp.where(lane == k, idx, te)
        tr = jnp.where(lane == k, jnp.sum(jnp.where(oh, rank, 0.0), axis=-1, keepdims=True), tr)
        tg = jnp.where(lane == k, jnp.sum(jnp.where(oh, gates, 0.0), axis=-1, keepdims=True), tg)
    te_ref[...] = te
    tr_ref[...] = tr.astype(jnp.int32)
    tg_ref[...] = tg


def _merge_router(x2d, ylru, onsa, gab, wa, wb, wo, g_ffn, rw_pad, rb_pad, tm):
    m = x2d.shape[0]
    assert m % tm == 0
    f32 = jnp.float32
    row = lambda i: (i, 0)
    const = lambda i: (0, 0)
    tri = jnp.asarray(np.tril(np.ones((tm, tm), np.float32)), dtype=jnp.bfloat16)
    lane_out = lambda dt: (jax.ShapeDtypeStruct((m, LANES), dt), pl.BlockSpec((tm, LANES), row))
    outs = [
        (jax.ShapeDtypeStruct((m, D_MODEL), f32), pl.BlockSpec((tm, D_MODEL), row)),
        (jax.ShapeDtypeStruct((m, D_MODEL), f32), pl.BlockSpec((tm, D_MODEL), row)),
        lane_out(jnp.int32), lane_out(jnp.int32), lane_out(f32),
        (jax.ShapeDtypeStruct((8, LANES), f32), pl.BlockSpec((8, LANES), const)),
    ]
    return pl.pallas_call(
        _merge_router_kernel,
        grid=(m // tm,),
        in_specs=[
            pl.BlockSpec((tm, D_MODEL), row),
            pl.BlockSpec((tm, D_RNN), row),
            pl.BlockSpec((tm, D_Q), row),
            pl.BlockSpec((tm, 2 * D_MODEL), row),
            pl.BlockSpec((D_RNN, D_MODEL), const),
            pl.BlockSpec((D_Q, D_MODEL), const),
            pl.BlockSpec((D_MODEL, D_MODEL), const),
            pl.BlockSpec((1, D_MODEL), const),
            pl.BlockSpec((2, D_MODEL, LANES), lambda i: (0, 0, 0)),
            pl.BlockSpec((1, LANES), const),
            pl.BlockSpec((tm, tm), const),
        ],
        out_specs=tuple(o[1] for o in outs),
        out_shape=tuple(o[0] for o in outs),
        scratch_shapes=[pltpu.VMEM((8, LANES), f32)],
        compiler_params=pltpu.CompilerParams(dimension_semantics=("arbitrary",), vmem_limit_bytes=VMEM_LIMIT),
        name="merge_router",
    )(x2d, ylru, onsa, gab, wa, wb, wo, g_ffn.reshape(1, D_MODEL), rw_pad, rb_pad, tri)


MOE_BLK = 512


def _moe_dispatch_kernel(last_ref, dest_ref, hn_ref, xs_ref, zero_scr, sem, zsem):
    tm = hn_ref.shape[0]

    @pl.when(pl.program_id(0) == 0)
    def _():
        zero_scr[...] = jnp.zeros_like(zero_scr)
        for e in range(N_EXPERTS):
            @pl.when(last_ref[e] >= 0)
            def _():
                start = pl.multiple_of(last_ref[e], MOE_BLK)
                cp = pltpu.make_async_copy(zero_scr, xs_ref.at[pl.ds(start, MOE_BLK)], zsem)
                cp.start()
                cp.wait()

    def issue(r, _):
        for k in range(TOP_K):
            d = dest_ref[0, 0, r * TOP_K + k]
            pltpu.make_async_copy(hn_ref.at[pl.ds(r, 1)], xs_ref.at[pl.ds(d, 1)], sem).start()
        return 0

    lax.fori_loop(0, tm, issue, 0)
    pltpu.make_async_copy(xs_ref.at[pl.ds(0, tm * TOP_K)], xs_ref.at[pl.ds(0, tm * TOP_K)], sem).wait()


def _moe_dispatch(hn, dest, last_tile, n_slots, tm):
    m = hn.shape[0]
    nt = m // tm
    grid_spec = pltpu.PrefetchScalarGridSpec(
        num_scalar_prefetch=1,
        grid=(nt,),
        in_specs=[
            pl.BlockSpec((1, 1, tm * TOP_K), lambda i, lt: (i, 0, 0), memory_space=pltpu.SMEM),
            pl.BlockSpec((tm, D_MODEL), lambda i, lt: (i, 0)),
        ],
        out_specs=pl.BlockSpec(memory_space=pl.ANY),
        scratch_shapes=[pltpu.VMEM((MOE_BLK, D_MODEL), hn.dtype), pltpu.SemaphoreType.DMA(()),
                        pltpu.SemaphoreType.DMA(())],
    )
    return pl.pallas_call(
        _moe_dispatch_kernel,
        grid_spec=grid_spec,
        out_shape=jax.ShapeDtypeStruct((n_slots, D_MODEL), hn.dtype),
        compiler_params=pltpu.CompilerParams(dimension_semantics=("arbitrary",), vmem_limit_bytes=VMEM_LIMIT),
        name="moe_dispatch",
    )(last_tile, dest.reshape(nt, 1, tm * TOP_K), hn)


def _moe_experts_kernel(te_ref, nused_ref, xs_ref, wup_ref, bup_ref, wdn_ref, bdn_ref, ys_ref, wup_scr, wdn_scr):
    f32, bf16 = jnp.float32, jnp.bfloat16
    j = pl.program_id(0)
    changed = jnp.logical_or(j == 0, te_ref[j] != te_ref[jnp.maximum(j - 1, 0)])

    @pl.when(jnp.logical_and(j < nused_ref[0], changed))
    def _():
        wup_scr[...] = wup_ref[0].astype(bf16)
        wdn_scr[...] = wdn_ref[0].astype(bf16)

    @pl.when(j < nused_ref[0])
    def _():
        x = xs_ref[...].astype(bf16)
        gu = jnp.dot(x, wup_scr[...], preferred_element_type=f32) + bup_ref[0]
        a = jnp.minimum(gu[:, :D_FF], SWIGLU_LIMIT)
        u = jnp.clip(gu[:, D_FF:], -SWIGLU_LIMIT, SWIGLU_LIMIT)
        act = (u + 1.0) * a * jax.nn.sigmoid(SWIGLU_ALPHA * a)
        ys_ref[...] = jnp.dot(act.astype(bf16), wdn_scr[...], preferred_element_type=f32) + bdn_ref[0]


def _moe_experts(xs, tile_expert, n_used, w_up, b_up, w_down, b_down):
    n_slots = xs.shape[0]
    nt = n_slots // MOE_BLK
    tile = lambda j, te, nu: (jnp.minimum(j, nu[0] - 1), 0)
    wsel = lambda j, te, nu: (te[jnp.minimum(j, nu[0] - 1)], 0, 0)
    grid_spec = pltpu.PrefetchScalarGridSpec(
        num_scalar_prefetch=2,
        grid=(nt,),
        in_specs=[
            pl.BlockSpec((MOE_BLK, D_MODEL), tile),
            pl.BlockSpec((1, D_MODEL, 2 * D_FF), wsel),
            pl.BlockSpec((1, 1, 2 * D_FF), wsel),
            pl.BlockSpec((1, D_FF, D_MODEL), wsel),
            pl.BlockSpec((1, 1, D_MODEL), wsel),
        ],
        out_specs=pl.BlockSpec((MOE_BLK, D_MODEL), tile),
        scratch_shapes=[pltpu.VMEM((D_MODEL, 2 * D_FF), jnp.bfloat16), pltpu.VMEM((D_FF, D_MODEL), jnp.bfloat16)],
    )
    return pl.pallas_call(
        _moe_experts_kernel,
        grid_spec=grid_spec,
        out_shape=jax.ShapeDtypeStruct((n_slots, D_MODEL), jnp.float32),
        compiler_params=pltpu.CompilerParams(dimension_semantics=("arbitrary",), vmem_limit_bytes=VMEM_LIMIT),
        name="moe_experts",
    )(tile_expert, n_used, xs, w_up, b_up.reshape(N_EXPERTS, 1, 2 * D_FF), w_down,
      b_down.reshape(N_EXPERTS, 1, D_MODEL))


def _moe_combine_kernel(dest_ref, xn_ref, tg_ref, ys_ref, y_ref, buf, sem):
    tm = xn_ref.shape[0]

    def issue(r, _):
        for k in range(TOP_K):
            d = dest_ref[0, 0, r * TOP_K + k]
            pltpu.make_async_copy(ys_ref.at[pl.ds(d, 1)], buf.at[k, pl.ds(r, 1)], sem).start()
        return 0

    lax.fori_loop(0, tm, issue, 0)
    pltpu.make_async_copy(ys_ref.at[pl.ds(0, tm * TOP_K)], ys_ref.at[pl.ds(0, tm * TOP_K)], sem).wait()
    tg = tg_ref[...]
    acc = xn_ref[...]
    for k in range(TOP_K):
        acc = acc + tg[:, k:k + 1] * buf[k]
    y_ref[...] = acc


def _moe_combine(xn, tg, ys, dest, tm):
    m = xn.shape[0]
    nt = m // tm
    return pl.pallas_call(
        _moe_combine_kernel,
        grid=(nt,),
        in_specs=[
            pl.BlockSpec((1, 1, tm * TOP_K), lambda i: (i, 0, 0), memory_space=pltpu.SMEM),
            pl.BlockSpec((tm, D_MODEL), lambda i: (i, 0)),
            pl.BlockSpec((tm, LANES), lambda i: (i, 0)),
            pl.BlockSpec(memory_space=pl.ANY),
        ],
        out_specs=pl.BlockSpec((tm, D_MODEL), lambda i: (i, 0)),
        out_shape=jax.ShapeDtypeStruct((m, D_MODEL), jnp.float32),
        scratch_shapes=[pltpu.VMEM((TOP_K, tm, D_MODEL), jnp.float32), pltpu.SemaphoreType.DMA(())],
        compiler_params=pltpu.CompilerParams(dimension_semantics=("arbitrary",), vmem_limit_bytes=VMEM_LIMIT),
        name="moe_combine",
    )(dest.reshape(nt, 1, tm * TOP_K), xn, tg, ys)


def _moe_layer(xn, hn, te, tr, tg, cnt, w_up, b_up, w_down, b_down, tm):
    m = xn.shape[0]
    n_tiles = -(-m * TOP_K // MOE_BLK) + N_EXPERTS
    counts = cnt[0, :N_EXPERTS].astype(jnp.int32)
    padded = (counts + MOE_BLK - 1) // MOE_BLK * MOE_BLK
    pad_end = jnp.cumsum(padded)
    pad_start = pad_end - padded
    dest = (pad_start[te[:, :TOP_K]] + tr[:, :TOP_K]).astype(jnp.int32)
    tile_start = jnp.arange(n_tiles, dtype=jnp.int32) * MOE_BLK
    tile_expert = jnp.minimum(jnp.sum(pad_end[None, :] <= tile_start[:, None], axis=1), N_EXPERTS - 1).astype(jnp.int32)
    n_used = (pad_end[-1:] // MOE_BLK).astype(jnp.int32)
    last_tile = jnp.where(padded > 0, pad_end - MOE_BLK, -1).astype(jnp.int32)
    xs = _moe_dispatch(hn, dest, last_tile, n_tiles * MOE_BLK, tm)
    ys = _moe_experts(xs, tile_expert, n_used, w_up, b_up, w_down, b_down)
    return _moe_combine(xn, tg, ys, dest, tm)


def _pad_w_in(w):
    c0 = 2 * D_RNN
    wq = w[:, c0:c0 + D_Q].reshape(D_MODEL, N_KV, GROUP, 1, HEAD_DIM)
    sel = jnp.eye(N_KV, dtype=w.dtype).reshape(N_KV, 1, N_KV, 1)
    wq = (wq * sel).reshape(D_MODEL, D_QP)
    c1 = c0 + D_Q
    c2 = c1 + 6 * D_KV + N_NG
    pad = jnp.zeros((D_MODEL, LANES - N_NG), w.dtype)
    return jnp.concatenate([w[:, :c0], wq, w[:, c1:c2], pad, w[:, c2:]], axis=1).astype(jnp.bfloat16)


def kernel(x_prompt, x_sample, page_table, cache_k_cmp, cache_v_cmp, cache_k_slc, cache_v_slc,
           cache_k_win, cache_v_win, state_h, state_conv, g_mix, w_in, conv_w, conv_b, lru_wa, lru_ba,
           lru_wx, lru_bx, lru_lambda, g_q, g_k_cmp, g_k_slc, g_k_win, cmp_pos_k, cmp_w1_k, cmp_w2_k,
           cmp_pos_v, cmp_w1_v, cmp_w2_v, w_proj_a, w_proj_b, w_out, g_ffn, router_w, router_b,
           exp_w_up, exp_b_up, exp_w_down, exp_b_down):
    xp, xs = x_prompt, x_sample
    bp, sp = xp.shape[:2]
    bs, ss = xs.shape[:2]
    pos_p = jnp.arange(sp, dtype=jnp.int32)
    pos_s = PAST_LEN + jnp.arange(ss, dtype=jnp.int32)
    depth = w_in.shape[0]
    layer_states = []
    tile2 = lambda g: jnp.concatenate([g, g], axis=-1)
    for l in range(depth):
        lru_w = (conv_w[l], conv_b[l], lru_wa[l], lru_ba[l], lru_wx[l], lru_bx[l], lru_lambda[l])
        w_pad = _pad_w_in(w_in[l])
        gq2 = tile2(g_q[l]).reshape(1, LANES)
        gk3 = jnp.stack([tile2(g_k_cmp[l]), tile2(g_k_slc[l]), tile2(g_k_win[l])])
        cmp_wk = _compress_weights(cmp_pos_k[l], cmp_w1_k[l], cmp_w2_k[l])
        cmp_wv = _compress_weights(cmp_pos_v[l], cmp_w1_v[l], cmp_w2_v[l])
        wb_perm = w_proj_b[l].reshape(N_KV, GROUP, HEAD_DIM, D_MODEL).transpose(1, 0, 2, 3).reshape(D_Q, D_MODEL)
        bf16 = jnp.bfloat16
        lru_k = _lru_weights(*lru_w)
        rw_pad = jnp.pad(router_w[l].astype(jnp.float32), ((0, 0), (0, LANES - N_EXPERTS)))
        rw_hi = rw_pad.astype(bf16)
        rw_pad = jnp.stack([rw_hi, (rw_pad - rw_hi.astype(jnp.float32)).astype(bf16)])
        rb_pad = jnp.pad(router_b[l].astype(jnp.float32), (0, LANES - N_EXPERTS)).reshape(1, LANES)
        merge_w = (w_proj_a[l].astype(bf16), wb_perm.astype(bf16), w_out[l].astype(bf16), g_ffn[l], rw_pad, rb_pad)
        moe_w = (exp_w_up[l], exp_b_up[l], exp_w_down[l], exp_b_down[l])
        arag, q_pad, kc, vc, ks, vs, kw, vw, ng, gab, *kv_t = _input_projection(
            xp.reshape(bp * sp, D_MODEL), pos_p, 256, g_mix[l], w_pad, gq2, gk3, kv_transposed=True)
        rows = lambda t: t.reshape(bp, sp, t.shape[-1])
        heads_t = lambda t: t.reshape(bp, N_KV, HEAD_DIM, t.shape[-1]).transpose(0, 3, 1, 2)
        conv0 = jnp.zeros((bp, CONV_W - 1, D_RNN), jnp.float32)
        y_lru, p_h = _lru_prompt(arag, conv0, jnp.zeros((bp, D_RNN), jnp.float32), lru_k, bp, sp, 256)
        p_conv = jnp.concatenate([conv0, rows(arag)[:, :, :D_RNN]], axis=1)[:, sp:]
        cmp_k, cmp_v = _compress_prompt(rows(kc), rows(vc), cmp_wk, cmp_wv)
        o_nsa = _nsa_prompt_attention(rows(q_pad), rows(ks), rows(vs), rows(kw), rows(vw), cmp_k, cmp_v, rows(ng))
        xn, hn, te, tr, tg, cnt = _merge_router(xp.reshape(bp * sp, D_MODEL), y_lru, o_nsa.reshape(bp * sp, D_Q),
                                                gab, *merge_w, 256)
        xp = _moe_layer(xn, hn, te, tr, tg, cnt, *moe_w, 256).reshape(bp, sp, D_MODEL)
        wbp = min(WINDOW, sp)
        p_states = tuple(heads_t(t) for t in kv_t[:4]) + tuple(heads_t(t[:, :, sp - wbp:]) for t in kv_t[4:]) + (
            p_h, p_conv)
        assert ss == 1
        pos_rows_s = jnp.broadcast_to(pos_s[None, :], (bs, ss)).reshape(-1)
        arag, q_pad, kc, vc, ks, vs, kw, vw, ng, gab = _input_projection(
            xs.reshape(bs, D_MODEL), pos_rows_s, bs, g_mix[l], w_pad, gq2, gk3)
        heads = lambda t: t.reshape(bs, ss, N_KV, HEAD_DIM)
        y_lru, s_h_l = _lru_step(arag, state_conv[l], state_h[l], lru_k)
        s_conv_l = jnp.concatenate([state_conv[l][:, 1:], arag[:, None, :D_RNN]], axis=1)
        o_nsa, s_kw_l, s_vw_l = _nsa_sample_paged(
            q_pad, ks, vs, kw, vw, ng, page_table, cache_k_cmp[l], cache_v_cmp[l], cache_k_slc[l], cache_v_slc[l],
            cache_k_win[l], cache_v_win[l], cmp_wk, cmp_wv)
        xn, hn, te, tr, tg, cnt = _merge_router(xs.reshape(bs, D_MODEL), y_lru, o_nsa, gab, *merge_w, bs)
        xs = _moe_layer(xn, hn, te, tr, tg, cnt, *moe_w, bs).reshape(bs, ss, D_MODEL)
        s_states = (heads(kc), heads(vc), heads(ks), heads(vs), s_kw_l, s_vw_l, s_h_l, s_conv_l)
        layer_states.append(p_states + s_states)
    (p_k_cmp, p_v_cmp, p_k_slc, p_v_slc, p_k_win, p_v_win, p_h, p_conv,
     s_k_cmp, s_v_cmp, s_k_slc, s_v_slc, s_k_win, s_v_win, s_h, s_conv) = [jnp.stack(z) for z in zip(*layer_states)]
    return (xp, xs, p_k_cmp, p_v_cmp, p_k_slc, p_v_slc, p_k_win, p_v_win, p_h, p_conv,
            s_k_cmp, s_v_cmp, s_k_slc, s_v_slc, s_k_win, s_v_win, s_h, s_conv)
```

```python
import jax
import jax.numpy as jnp
import numpy as np
from jax import lax
from jax.experimental import pallas as pl
from jax.experimental.pallas import tpu as pltpu

D_MODEL = 1024
PAST_LEN = 8192
PAGE_SIZE = 128
D_RNN = D_MODEL
LRU_BLOCKS = 16
LRU_BW = D_RNN // LRU_BLOCKS
CONV_W = 4
LRU_C = 8.0
N_HEADS = 8
HEAD_DIM = 64
N_KV = 2
GROUP = N_HEADS // N_KV
D_Q = N_HEADS * HEAD_DIM
D_KV = N_KV * HEAD_DIM
ROPE_DIM = HEAD_DIM // 4
ROPE_THETA = 500000.0
CMP_LEN = 32
CMP_STRIDE = 16
CMP_HID = 256
SLC_BLOCK = 64
N_SEL = 16
WINDOW = 512
Q_BLOCK = 64
N_EXPERTS = 32
TOP_K = 4
D_FF = D_MODEL
SWIGLU_LIMIT = 7.0
SWIGLU_ALPHA = 1.702
EPS = 1e-6
NEG = -1e30
BIG = 1e30

LANES = 128
SUBLANES = 8
VMEM_LIMIT = 56 * 1024 * 1024

N_NG = 3 * N_HEADS
D_QP = N_HEADS * LANES
C_AR = 0
C_Q = 2 * D_RNN
C_KV = C_Q + D_QP
C_NG = C_KV + 6 * D_KV
C_GAB = C_NG + LANES
N_IN_PAD = C_GAB + 2 * D_MODEL


def _seg_sum64(x2, ones_blk):
    hi = x2.astype(jnp.bfloat16)
    lo = (x2 - hi.astype(jnp.float32)).astype(jnp.bfloat16)
    return (jnp.dot(hi, ones_blk, preferred_element_type=jnp.float32)
            + jnp.dot(lo, ones_blk, preferred_element_type=jnp.float32))


def _head_norm_rope(y, gain, ones_blk, rc, rs1, rs2):
    ss = _seg_sum64(y * y, ones_blk)
    yn = y * lax.rsqrt(ss * (1.0 / HEAD_DIM) + EPS) * gain
    half = ROPE_DIM // 2
    return yn * rc + pltpu.roll(yn, LANES - half, 1) * rs1 + pltpu.roll(yn, half, 1) * rs2


def _proj_kernel(x_ref, gmix_ref, w_ref, ones_ref, gq_ref, gk_ref, rc_ref, rs1_ref, rs2_ref,
                 arag_ref, q_ref, kc_ref, vc_ref, ks_ref, vs_ref, kw_ref, vw_ref, ng_ref, gab_ref, *kv_t_refs):
    x = x_ref[...]
    xn = x * lax.rsqrt(jnp.mean(x * x, axis=-1, keepdims=True) + EPS) * gmix_ref[...]
    xb = xn.astype(jnp.bfloat16)

    def proj(c0, width):
        return jnp.dot(xb, w_ref[:, c0:c0 + width], preferred_element_type=jnp.float32)

    arag_ref[...] = proj(C_AR, 2 * D_RNN)
    gab_ref[...] = proj(C_GAB, 2 * D_MODEL)
    ng_ref[...] = proj(C_NG, LANES)
    ones_blk = ones_ref[...]
    rc, rs1, rs2 = rc_ref[...], rs1_ref[...], rs2_ref[...]
    scale = HEAD_DIM ** -0.5
    for j in range(N_HEADS):
        y = proj(C_Q + j * LANES, LANES)
        yr = _head_norm_rope(y, gq_ref[...], ones_blk, rc, rs1, rs2)
        q_ref[:, j * LANES:(j + 1) * LANES] = (yr * scale).astype(q_ref.dtype)
    kv_outs = (kc_ref, vc_ref, ks_ref, vs_ref, kw_ref, vw_ref)
    for j in range(3):
        yk = _head_norm_rope(proj(C_KV + 2 * j * LANES, LANES), gk_ref[j:j + 1, :], ones_blk, rc, rs1, rs2)
        yv = proj(C_KV + (2 * j + 1) * LANES, LANES)
        for i, y in ((2 * j, yk), (2 * j + 1, yv)):
            kv_outs[i][...] = y.astype(kv_outs[i].dtype)
            if kv_t_refs:
                kv_t_refs[i][0] = y.T


def _rope_tables(pos):
    half = ROPE_DIM // 2
    inv = ROPE_THETA ** (-jnp.arange(half, dtype=jnp.float32) / half)
    ang = pos.astype(jnp.float32)[:, None] * inv[None, :]
    cos, sin = jnp.cos(ang), jnp.sin(ang)
    p = pos.shape[0]
    ones = jnp.ones((p, HEAD_DIM - ROPE_DIM), jnp.float32)
    zeros = jnp.zeros((p, HEAD_DIM - ROPE_DIM), jnp.float32)
    zh = jnp.zeros((p, half), jnp.float32)
    rc = jnp.concatenate([cos, cos, ones], axis=-1)
    rs1 = jnp.concatenate([-sin, zh, zeros], axis=-1)
    rs2 = jnp.concatenate([zh, sin, zeros], axis=-1)
    tile2 = lambda t: jnp.concatenate([t, t], axis=-1)
    return tile2(rc), tile2(rs1), tile2(rs2)


def _ones_block64():
    i = np.arange(LANES)
    return jnp.asarray((i[:, None] // HEAD_DIM) == (i[None, :] // HEAD_DIM), dtype=jnp.bfloat16)


def _input_projection(x2d, pos_rows, tm, g_mix, w_pad, g_q, g_k3, kv_transposed=False):
    m = x2d.shape[0]
    p = pos_rows.shape[0]
    assert m % tm == 0 and p % tm == 0
    rc, rs1, rs2 = _rope_tables(pos_rows)
    n_tab = p // tm
    row = lambda i: (i, 0)
    const = lambda i: (0, 0)
    tab = lambda i: (i % n_tab, 0)
    f32 = jnp.float32
    kv_dt = jnp.bfloat16 if kv_transposed else f32
    out_shape = (
        jax.ShapeDtypeStruct((m, 2 * D_RNN), f32),
        jax.ShapeDtypeStruct((m, D_QP), jnp.bfloat16),
    ) + tuple(jax.ShapeDtypeStruct((m, D_KV), kv_dt) for _ in range(6)) + (
        jax.ShapeDtypeStruct((m, LANES), f32),
        jax.ShapeDtypeStruct((m, 2 * D_MODEL), f32),
    )
    out_specs = (
        pl.BlockSpec((tm, 2 * D_RNN), row),
        pl.BlockSpec((tm, D_QP), row),
    ) + tuple(pl.BlockSpec((tm, D_KV), row) for _ in range(6)) + (
        pl.BlockSpec((tm, LANES), row),
        pl.BlockSpec((tm, 2 * D_MODEL), row),
    )
    if kv_transposed:
        out_shape += tuple(jax.ShapeDtypeStruct((m // p, D_KV, p), f32) for _ in range(6))
        out_specs += tuple(pl.BlockSpec((1, D_KV, tm), lambda i: (i // n_tab, 0, i % n_tab)) for _ in range(6))
    return pl.pallas_call(
        _proj_kernel,
        grid=(m // tm,),
        in_specs=[
            pl.BlockSpec((tm, D_MODEL), row),
            pl.BlockSpec((1, D_MODEL), const),
            pl.BlockSpec((D_MODEL, N_IN_PAD), const),
            pl.BlockSpec((LANES, LANES), const),
            pl.BlockSpec((1, LANES), const),
            pl.BlockSpec((3, LANES), const),
            pl.BlockSpec((tm, LANES), tab),
            pl.BlockSpec((tm, LANES), tab),
            pl.BlockSpec((tm, LANES), tab),
        ],
        out_specs=out_specs,
        out_shape=out_shape,
        compiler_params=pltpu.CompilerParams(dimension_semantics=("arbitrary",), vmem_limit_bytes=VMEM_LIMIT),
        name="input_projection",
    )(x2d, g_mix.reshape(1, D_MODEL), w_pad, _ones_block64(), g_q, g_k3, rc, rs1, rs2)


SLC_CHUNK = 1024
WIN_SPAN = WINDOW + LANES
ROWS = GROUP * Q_BLOCK


def _dot_nt(a, b):
    return lax.dot_general(a, b, (((1,), (1,)), ((), ())), preferred_element_type=jnp.float32)


def _softmax_rows(s, mask):
    s = jnp.where(mask, s, NEG)
    m = jnp.max(s, axis=-1, keepdims=True)
    p = jnp.where(mask, jnp.exp(s - m), 0.0)
    return p / jnp.maximum(jnp.sum(p, axis=-1, keepdims=True), 1e-30)


def _split3_bf16(x):
    hi = x.astype(jnp.bfloat16)
    r1 = x - hi.astype(jnp.float32)
    mid = r1.astype(jnp.bfloat16)
    lo = (r1 - mid.astype(jnp.float32)).astype(jnp.bfloat16)
    return hi, mid, lo


def _nsa_prompt_kernel(q_ref, ksa_ref, vs_ref, kw_ref, vw_ref, ck_ref, cv_ref, ng_ref, mmt_ref,
                       o_ref, sv_scr):
    f32, bf16 = jnp.float32, jnp.bfloat16
    qb = pl.program_id(1)
    row = lax.broadcasted_iota(jnp.int32, (ROWS, 1), 0)
    pos_t = qb * Q_BLOCK + (row & (Q_BLOCK - 1))
    ncp = ck_ref.shape[1]

    qh = [jnp.concatenate([q_ref[0, :, (h * GROUP + g) * LANES:(h * GROUP + g + 1) * LANES]
                           for g in range(GROUP)], axis=0) for h in range(N_KV)]

    ck, cv = ck_ref[0], cv_ref[0]
    c_end = lax.broadcasted_iota(jnp.int32, (1, ncp), 1) * CMP_STRIDE + (CMP_LEN - 1)
    cmask = c_end <= pos_t
    o_cmp, psum = [], []
    for h in range(N_KV):
        p = _softmax_rows(_dot_nt(qh[h], ck), cmask)
        o_cmp.append(jnp.dot(p.astype(bf16), cv, preferred_element_type=f32))
        psum.append(p[0:Q_BLOCK] + p[Q_BLOCK:2 * Q_BLOCK] + p[2 * Q_BLOCK:3 * Q_BLOCK] + p[3 * Q_BLOCK:])
    ps = jnp.concatenate(psum, axis=0)

    mmt = mmt_ref[...]
    sc = sum(_dot_nt(mmt, part) for part in _split3_bf16(ps))[0:Q_BLOCK]
    jio = lax.broadcasted_iota(jnp.int32, (Q_BLOCK, LANES), 0)
    forced = (jio == 0) | (jio == qb) | (jio == qb - 1)
    sv = jnp.where(forced, BIG, jnp.where(jio <= qb, sc, NEG))
    sv_scr[...] = sv

    def rank_body(i, cnt):
        r = sv_scr[pl.ds(i, 1), :]
        beats = (r > sv) | ((r == sv) & (jio > i))
        return cnt + jnp.where(beats, 1.0, 0.0)

    cnt = lax.fori_loop(0, qb + 1, rank_body, jnp.zeros((Q_BLOCK, LANES), f32))
    bias_lo = jnp.where((cnt < N_SEL) & (sv > NEG / 2), 0.0, NEG)
    lane = lax.broadcasted_iota(jnp.int32, (Q_BLOCK, LANES), 1)
    bias_t = jnp.concatenate([jnp.where(lane >= Q_BLOCK, bias_lo, 0.0),
                              jnp.where(lane < Q_BLOCK, bias_lo, 0.0)], axis=0)
    bias = bias_t.T.astype(bf16)

    kio = lax.broadcasted_iota(jnp.int32, (1, SLC_CHUNK), 1)
    wio = lax.broadcasted_iota(jnp.int32, (1, WIN_SPAN), 1)
    wbase = pl.multiple_of((jnp.maximum(qb - WINDOW // Q_BLOCK, 0) // 2) * LANES, LANES)
    dpos = pos_t - (wbase + wio)
    wmask = (dpos >= 0) & (dpos < WINDOW)
    n_chunks = qb // (SLC_CHUNK // SLC_BLOCK) + 1
    sig = jax.nn.sigmoid(ng_ref[0])
    qa = [qh[h] + jnp.concatenate([bias[h * Q_BLOCK:(h + 1) * Q_BLOCK]] * GROUP, axis=0) for h in range(N_KV)]

    def chunk(c, carry, causal):
        k0 = pl.multiple_of(c * SLC_CHUNK, SLC_CHUNK)
        v = vs_ref[0, pl.ds(k0, SLC_CHUNK), :]
        new = []
        for h in range(N_KV):
            m, l, acc = carry[h]
            s = _dot_nt(qa[h], ksa_ref[0, h, pl.ds(k0, SLC_CHUNK), :])
            if causal:
                s = jnp.where(k0 + kio <= pos_t, s, NEG)
            m_new = jnp.maximum(m, jnp.max(s, axis=-1, keepdims=True))
            alpha = jnp.exp(m - m_new)
            p = jnp.exp(s - m_new)
            l = alpha * l + jnp.sum(p, axis=-1, keepdims=True)
            acc = alpha * acc + jnp.dot(p.astype(bf16), v, preferred_element_type=f32)
            new.append((m_new, l, acc))
        return tuple(new)

    init = (jnp.full((ROWS, 1), NEG, f32), jnp.zeros((ROWS, 1), f32), jnp.zeros((ROWS, LANES), f32))
    carry = lax.fori_loop(0, n_chunks - 1, lambda c, cr: chunk(c, cr, False), (init, init))
    carry = chunk(n_chunks - 1, carry, True)

    out_h = []
    for h in range(N_KV):
        _, l, acc = carry[h]
        o_slc = acc / jnp.maximum(l, 1e-30)

        pw = _softmax_rows(_dot_nt(qh[h], kw_ref[0, pl.ds(wbase, WIN_SPAN), :]), wmask)
        o_win = jnp.dot(pw.astype(bf16), vw_ref[0, pl.ds(wbase, WIN_SPAN), :], preferred_element_type=f32)

        outs = []
        for g in range(GROUP):
            rs = slice(g * Q_BLOCK, (g + 1) * Q_BLOCK)
            c0 = (h * GROUP + g) * 3
            outs.append(sig[:, c0:c0 + 1] * o_cmp[h][rs] + sig[:, c0 + 1:c0 + 2] * o_slc[rs]
                        + sig[:, c0 + 2:c0 + 3] * o_win[rs])
        out_h.append(outs)
    for g in range(GROUP):
        o_ref[0, :, g * LANES:(g + 1) * LANES] = jnp.where(lane < HEAD_DIM, out_h[0][g], out_h[1][g]).astype(o_ref.dtype)


def _block_score_matrix(ncp):
    ratio = SLC_BLOCK // CMP_STRIDE
    ovl = CMP_LEN // CMP_STRIDE - 1
    j = np.arange(LANES)[:, None]
    c = np.arange(ncp)[None, :]
    m = (c >= ratio * j - ovl) & (c <= ratio * j + ratio - 1) & (j < Q_BLOCK) & (c < ncp - 1)
    return jnp.asarray(m, dtype=jnp.bfloat16)


def _nsa_prompt_attention(q_pad, ks, vs, kw, vw, cmp_k, cmp_v, ng):
    b, s, _ = q_pad.shape
    assert s % SLC_CHUNK == 0 and s // SLC_BLOCK <= Q_BLOCK and s >= WIN_SPAN and Q_BLOCK == SLC_BLOCK
    bf16 = jnp.bfloat16
    ncp = cmp_k.shape[1]
    onehot = jax.nn.one_hot(jnp.arange(s) // SLC_BLOCK, Q_BLOCK, dtype=bf16)
    onehot = jnp.broadcast_to(onehot[None], (b, s, Q_BLOCK))
    ksb = ks.astype(bf16)
    ksa = jnp.stack([jnp.concatenate([ksb[:, :, :HEAD_DIM], onehot], axis=-1),
                     jnp.concatenate([onehot, ksb[:, :, HEAD_DIM:]], axis=-1)], axis=1)
    per_b = lambda bi, i: (bi, 0, 0)
    return pl.pallas_call(
        _nsa_prompt_kernel,
        grid=(b, s // Q_BLOCK),
        in_specs=[
            pl.BlockSpec((1, Q_BLOCK, D_QP), lambda bi, i: (bi, i, 0)),
            pl.BlockSpec((1, N_KV, s, LANES), lambda bi, i: (bi, 0, 0, 0)),
            pl.BlockSpec((1, s, LANES), per_b),
            pl.BlockSpec((1, s, LANES), per_b),
            pl.BlockSpec((1, s, LANES), per_b),
            pl.BlockSpec((1, ncp, LANES), per_b),
            pl.BlockSpec((1, ncp, LANES), per_b),
            pl.BlockSpec((1, Q_BLOCK, LANES), lambda bi, i: (bi, i, 0)),
            pl.BlockSpec((LANES, ncp), lambda bi, i: (0, 0)),
        ],
        out_specs=pl.BlockSpec((1, Q_BLOCK, D_Q), lambda bi, i: (bi, i, 0)),
        out_shape=jax.ShapeDtypeStruct((b, s, D_Q), bf16),
        scratch_shapes=[pltpu.VMEM((Q_BLOCK, LANES), jnp.float32)],
        compiler_params=pltpu.CompilerParams(dimension_semantics=("arbitrary", "arbitrary"),
                                             vmem_limit_bytes=VMEM_LIMIT),
        name="nsa_prompt_attention",
    )(q_pad, ksa, vs.astype(bf16), kw.astype(bf16), vw.astype(bf16), cmp_k, cmp_v, ng, _block_score_matrix(ncp))


CHUNK_W = CMP_STRIDE * D_KV
CMP_PARTS = CMP_LEN // CMP_STRIDE
HID2 = N_KV * CMP_HID


def _compress_weights(pos_emb, w1, w2):
    bf16 = jnp.bfloat16
    eye = jnp.eye(N_KV, dtype=w1.dtype)
    w1b = w1.reshape(CMP_PARTS, CMP_STRIDE, HEAD_DIM, CMP_HID)
    w1p = jnp.einsum('pldf,hk->lhdpkf', w1b, eye).reshape(CHUNK_W, CMP_PARTS * HID2).astype(bf16)
    w2p = jnp.einsum('fd,hk->hfkd', w2, eye).reshape(HID2, D_KV).astype(bf16)
    pos8 = jnp.zeros((8, CMP_LEN * HEAD_DIM), bf16).at[0].set(pos_emb.reshape(-1).astype(bf16))
    w1f = w1.reshape(CMP_LEN * HEAD_DIM, CMP_HID).astype(bf16)
    return w1p, w2p, pos8, w1f


def _compress_rows(x, w1p, w2p, pos8, w1f):
    f32 = jnp.float32
    r = x.shape[0]
    part = jnp.dot(x, w1p, preferred_element_type=f32)
    bias = jnp.dot(pos8, w1f, preferred_element_type=f32)[0:1]
    bias = jnp.concatenate([bias] * N_KV, axis=-1)
    hid = bias + part[:, :HID2] + pltpu.roll(part[:, HID2:], r - 1, 0)
    out = jnp.dot(jax.nn.gelu(hid).astype(jnp.bfloat16), w2p, preferred_element_type=f32)
    rows = lax.broadcasted_iota(jnp.int32, (r, 1), 0)
    return jnp.where(rows < r - 1, out, 0.0)


def _compress_prompt_kernel(xk_ref, xv_ref, w1k_ref, w2k_ref, pk_ref, fk_ref, w1v_ref, w2v_ref, pv_ref, fv_ref,
                            ok_ref, ov_ref):
    bf16 = jnp.bfloat16
    ok_ref[0] = _compress_rows(xk_ref[0].astype(bf16), w1k_ref[...], w2k_ref[...], pk_ref[...],
                               fk_ref[...]).astype(ok_ref.dtype)
    ov_ref[0] = _compress_rows(xv_ref[0].astype(bf16), w1v_ref[...], w2v_ref[...], pv_ref[...],
                               fv_ref[...]).astype(ov_ref.dtype)


def _compress_prompt(kc, vc, wk, wv):
    b, s, _ = kc.shape
    r = s // CMP_STRIDE
    xspec = pl.BlockSpec((1, r, CHUNK_W), lambda i: (i, 0, 0))
    wspecs = [pl.BlockSpec(w.shape, lambda i: (0, 0)) for w in wk]
    ospec = pl.BlockSpec((1, r, D_KV), lambda i: (i, 0, 0))
    oshape = jax.ShapeDtypeStruct((b, r, D_KV), jnp.bfloat16)
    return pl.pallas_call(
        _compress_prompt_kernel,
        grid=(b,),
        in_specs=[xspec, xspec] + wspecs + wspecs,
        out_specs=(ospec, ospec),
        out_shape=(oshape, oshape),
        compiler_params=pltpu.CompilerParams(dimension_semantics=("arbitrary",), vmem_limit_bytes=VMEM_LIMIT),
        name="compress_prompt",
    )(kc.reshape(b, r, CHUNK_W), vc.reshape(b, r, CHUNK_W), *wk, *wv)


N_PAGES = PAST_LEN // PAGE_SIZE
PAST_CHUNKS = PAST_LEN // CMP_STRIDE
PAST_BLOCKS = PAST_LEN // SLC_BLOCK
BLOCKS_PER_PAGE = PAGE_SIZE // SLC_BLOCK
CHUNKS_PER_PAGE = PAGE_SIZE // CMP_STRIDE
SEL_ROWS = 256
QROWS = SUBLANES
REGROUP_UNROLL = 16


def _sample_cmp_kernel(pt_ref, q8_ref, kpool, vpool, w1k_ref, w2k_ref, pk_ref, fk_ref, w1v_ref, w2v_ref, pv_ref,
                       fv_ref, ocmp_ref, psum_ref, kbuf, vbuf, x_scr, t_scr, sem):
    f32, bf16 = jnp.float32, jnp.bfloat16
    n = pl.program_id(0)

    def fetch(s, slot):
        def body(j, _):
            pg = pt_ref[s, j]
            pltpu.make_async_copy(kpool.at[pg], kbuf.at[slot, j], sem.at[0, slot]).start()
            pltpu.make_async_copy(vpool.at[pg], vbuf.at[slot, j], sem.at[1, slot]).start()
            return 0
        lax.fori_loop(0, N_PAGES, body, 0)

    @pl.when(n == 0)
    def _():
        fetch(0, 0)

    @pl.when(n + 1 < pl.num_programs(0))
    def _():
        fetch(n + 1, (n + 1) % 2)

    slot = n % 2
    pltpu.make_async_copy(kbuf.at[slot], kbuf.at[slot], sem.at[0, slot]).wait()
    pltpu.make_async_copy(vbuf.at[slot], vbuf.at[slot], sem.at[1, slot]).wait()

    def regroup_body(jj, _):
        for u in range(REGROUP_UNROLL):
            j = jj * (REGROUP_UNROLL // 2) + u // 2
            t_scr[u] = (kbuf, vbuf)[u % 2][slot, j].T
            r0 = pl.multiple_of(j * CHUNKS_PER_PAGE, CHUNKS_PER_PAGE)
            for l in range(CMP_STRIDE):
                x_scr[u % 2, pl.ds(r0, CHUNKS_PER_PAGE), l * D_KV:(l + 1) * D_KV] = (
                    t_scr[u, pl.ds(l, CHUNKS_PER_PAGE, stride=CMP_STRIDE), :])
        return 0

    lax.fori_loop(0, N_PAGES // (REGROUP_UNROLL // 2), regroup_body, 0)
    ck = _compress_rows(x_scr[0].astype(bf16), w1k_ref[...], w2k_ref[...], pk_ref[...], fk_ref[...]).astype(bf16)
    cv = _compress_rows(x_scr[1].astype(bf16), w1v_ref[...], w2v_ref[...], pv_ref[...], fv_ref[...]).astype(bf16)
    c_end = lax.broadcasted_iota(jnp.int32, (1, PAST_CHUNKS), 1) * CMP_STRIDE + (CMP_LEN - 1)
    cmask = c_end <= PAST_LEN
    rows = lax.broadcasted_iota(jnp.int32, (QROWS, 1), 0)
    ps = jnp.zeros((QROWS, PAST_CHUNKS), f32)
    for h in range(N_KV):
        p = _softmax_rows(_dot_nt(q8_ref[0, h], ck), cmask)
        ocmp_ref[0, h] = jnp.dot(p.astype(bf16), cv, preferred_element_type=f32)
        ps = jnp.where(rows == h, p[0:1] + p[1:2] + p[2:3] + p[3:4], ps)
    psum_ref[0] = ps


def _sample_cmp(page_table, q8, k_pool, v_pool, wk, wv):
    n = q8.shape[0]
    f32 = jnp.float32
    wspecs = [pl.BlockSpec(w.shape, lambda i, pt: (0, 0)) for w in wk]
    grid_spec = pltpu.PrefetchScalarGridSpec(
        num_scalar_prefetch=1,
        grid=(n,),
        in_specs=[pl.BlockSpec((1, N_KV, QROWS, LANES), lambda i, pt: (i, 0, 0, 0)),
                  pl.BlockSpec(memory_space=pl.ANY), pl.BlockSpec(memory_space=pl.ANY)] + wspecs + wspecs,
        out_specs=(pl.BlockSpec((1, N_KV, QROWS, LANES), lambda i, pt: (i, 0, 0, 0)),
                   pl.BlockSpec((1, QROWS, PAST_CHUNKS), lambda i, pt: (i, 0, 0))),
        scratch_shapes=[pltpu.VMEM((2, N_PAGES, D_KV, PAGE_SIZE), f32),
                        pltpu.VMEM((2, N_PAGES, D_KV, PAGE_SIZE), f32),
                        pltpu.VMEM((2, PAST_CHUNKS, CHUNK_W), f32),
                        pltpu.VMEM((REGROUP_UNROLL, PAGE_SIZE, D_KV), f32),
                        pltpu.SemaphoreType.DMA((2, 2))],
    )
    return pl.pallas_call(
        _sample_cmp_kernel,
        grid_spec=grid_spec,
        out_shape=(jax.ShapeDtypeStruct((n, N_KV, QROWS, LANES), f32),
                   jax.ShapeDtypeStruct((n, QROWS, PAST_CHUNKS), f32)),
        compiler_params=pltpu.CompilerParams(dimension_semantics=("arbitrary",), vmem_limit_bytes=VMEM_LIMIT),
        name="sample_compress_attention",
    )(page_table, q8, k_pool, v_pool, *wk, *wv)


def _sample_select_kernel(p_ref, mmt_ref, tri_ref, idx_ref, sv_scr):
    f32 = jnp.float32
    cols = p_ref.shape[0]
    cur = PAST_LEN // SLC_BLOCK
    mmt = mmt_ref[...]
    sc = sum(_dot_nt(mmt, part) for part in _split3_bf16(p_ref[...]))
    jio = lax.broadcasted_iota(jnp.int32, (SEL_ROWS, cols), 0)
    forced = (jio == 0) | (jio == cur) | (jio == cur - 1)
    sv = jnp.where(forced, BIG, jnp.where(jio <= cur, sc, NEG))
    sv_scr[...] = sv

    def rank_body(i, cnt):
        r = sv_scr[pl.ds(i, 1), :]
        beats = (r > sv) | ((r == sv) & (jio > i))
        return cnt + jnp.where(beats, 1.0, 0.0)

    cnt = lax.fori_loop(0, cur + 1, rank_body, jnp.zeros((SEL_ROWS, cols), f32))
    sel = (cnt < N_SEL) & (sv > NEG / 2)
    self = jnp.where(sel, 1.0, 0.0)
    before = jnp.dot(tri_ref[...], self.astype(jnp.bfloat16), preferred_element_type=f32)
    rows = []
    for k in range(N_SEL):
        rows.append(jnp.sum(jnp.where(sel & (before == k), jio, 0), axis=0, keepdims=True))
    idx_ref[...] = jnp.concatenate(rows, axis=0)


def _sample_select(p_sum):
    n = p_sum.shape[0]
    cols = n * N_KV
    p2 = p_sum[:, :N_KV, :].reshape(cols, PAST_CHUNKS)
    ratio = SLC_BLOCK // CMP_STRIDE
    ovl = CMP_LEN // CMP_STRIDE - 1
    j = np.arange(SEL_ROWS)[:, None]
    c = np.arange(PAST_CHUNKS)[None, :]
    mmt = (c >= ratio * j - ovl) & (c <= ratio * j + ratio - 1) & (j <= PAST_BLOCKS) & (c < PAST_CHUNKS - 1)
    tri = np.tril(np.ones((SEL_ROWS, SEL_ROWS), np.float32), -1)
    idx = pl.pallas_call(
        _sample_select_kernel,
        out_shape=jax.ShapeDtypeStruct((N_SEL, cols), jnp.int32),
        scratch_shapes=[pltpu.VMEM((SEL_ROWS, cols), jnp.float32)],
        compiler_params=pltpu.CompilerParams(vmem_limit_bytes=VMEM_LIMIT),
        name="sample_select",
    )(p2, jnp.asarray(mmt, dtype=jnp.bfloat16), jnp.asarray(tri, dtype=jnp.bfloat16))
    return idx.T.reshape(n, N_KV * N_SEL)


SLC_COLS = N_SEL * PAGE_SIZE


def _sample_attn_kernel(pt_ref, idx_ref, q_ref, kpool, vpool, knew_ref, vnew_ref, kwin_ref, vwin_ref, kwn_ref,
                        vwn_ref, ocmp_ref, ng_ref, o_ref, kwo_ref, vwo_ref, kg, vg, sem):
    f32, bf16 = jnp.float32, jnp.bfloat16
    n = pl.program_id(0)
    nblk = N_KV * N_SEL

    def fetch(s, slot):
        def body(i, _):
            jp = jnp.minimum(idx_ref[s, i], PAST_BLOCKS - 1)
            pg = pt_ref[s, jp // BLOCKS_PER_PAGE]
            h = i // N_SEL
            pltpu.make_async_copy(kpool.at[pg, h], kg.at[slot, i], sem.at[0, slot]).start()
            pltpu.make_async_copy(vpool.at[pg, h], vg.at[slot, i], sem.at[1, slot]).start()
            return 0
        lax.fori_loop(0, nblk, body, 0)

    @pl.when(n == 0)
    def _():
        fetch(0, 0)

    @pl.when(n + 1 < pl.num_programs(0))
    def _():
        fetch(n + 1, (n + 1) % 2)

    slot = n % 2
    pltpu.make_async_copy(kg.at[slot], kg.at[slot], sem.at[0, slot]).wait()
    pltpu.make_async_copy(vg.at[slot], vg.at[slot], sem.at[1, slot]).wait()

    pos = PAST_LEN
    col = lax.broadcasted_iota(jnp.int32, (1, SLC_COLS + LANES), 1)
    wcol = lax.broadcasted_iota(jnp.int32, (1, WINDOW + LANES), 1)
    wpos = jnp.where(wcol < WINDOW, pos - WINDOW + wcol, pos + (wcol - WINDOW))
    wmask = (pos - wpos >= 0) & (pos - wpos < WINDOW)
    wlane = lax.broadcasted_iota(jnp.int32, (1, WINDOW), 1)
    for h in range(N_KV):
        q = q_ref[0, h]
        new_sel = jnp.int32(0)
        lo = jnp.full((1, SLC_COLS + LANES), SLC_COLS + LANES, jnp.int32)
        for p in range(N_SEL):
            j = idx_ref[n, h * N_SEL + p]
            start = p * PAGE_SIZE + (j % BLOCKS_PER_PAGE) * SLC_BLOCK
            start = jnp.where(j < PAST_BLOCKS, start, SLC_COLS + LANES)
            lo = jnp.where((col >> 7) == p, start, lo)
            new_sel = jnp.maximum(new_sel, (j >= PAST_BLOCKS).astype(jnp.int32))
        valid = jnp.where((col >= lo) & (col < lo + SLC_BLOCK), 1, 0)
        kpos_new = PAST_LEN + (col - SLC_COLS)
        valid = jnp.where(col >= SLC_COLS, jnp.where(kpos_new <= pos, new_sel, 0), valid) > 0
        kt = jnp.concatenate([kg[slot, h * N_SEL + p] for p in range(N_SEL)] + [knew_ref[0, h]], axis=-1)
        vt = jnp.concatenate([vg[slot, h * N_SEL + p] for p in range(N_SEL)] + [vnew_ref[0, h]], axis=-1)
        ps = _softmax_rows(jnp.dot(q, kt.astype(bf16), preferred_element_type=f32), valid)
        o_slc = _dot_nt(ps.astype(bf16), vt.astype(bf16))
        kwin, vwin = kwin_ref[0, h], vwin_ref[0, h]
        kw_all = jnp.concatenate([kwin, kwn_ref[0, h]], axis=-1).astype(bf16)
        vw_all = jnp.concatenate([vwin, vwn_ref[0, h]], axis=-1).astype(bf16)
        pw = _softmax_rows(jnp.dot(q, kw_all, preferred_element_type=f32), wmask)
        o_win = _dot_nt(pw.astype(bf16), vw_all)
        sig = jax.nn.sigmoid(ng_ref[0, h])
        o_cmp = ocmp_ref[0, h][:, h * HEAD_DIM:(h + 1) * HEAD_DIM]
        o_ref[0, h] = sig[0] * o_cmp + sig[1] * o_slc + sig[2] * o_win
        kwo_ref[0, h] = jnp.where(wlane == WINDOW - 1, kwn_ref[0, h][:, 0:1], pltpu.roll(kwin, WINDOW - 1, 1))
        vwo_ref[0, h] = jnp.where(wlane == WINDOW - 1, vwn_ref[0, h][:, 0:1], pltpu.roll(vwin, WINDOW - 1, 1))


def _sample_attn(page_table, idx, q8u, k_pool_t, v_pool_t, knew_t, vnew_t, k_win_t, v_win_t, kwn_t, vwn_t, o_cmp, ngb):
    n = q8u.shape[0]
    f32 = jnp.float32
    assert k_win_t.shape[-1] == WINDOW
    per_n4 = lambda i, pt, ix: (i, 0, 0, 0)
    ext_spec = pl.BlockSpec((1, N_KV, HEAD_DIM, LANES), per_n4)
    win_spec = pl.BlockSpec((1, N_KV, HEAD_DIM, WINDOW), per_n4)
    q_spec = pl.BlockSpec((1, N_KV, QROWS, HEAD_DIM), per_n4)
    grid_spec = pltpu.PrefetchScalarGridSpec(
        num_scalar_prefetch=2,
        grid=(n,),
        in_specs=[q_spec, pl.BlockSpec(memory_space=pl.ANY), pl.BlockSpec(memory_space=pl.ANY),
                  ext_spec, ext_spec, win_spec, win_spec, ext_spec, ext_spec,
                  pl.BlockSpec((1, N_KV, QROWS, LANES), per_n4),
                  pl.BlockSpec((1, N_KV, 3, QROWS, HEAD_DIM), lambda i, pt, ix: (i, 0, 0, 0, 0))],
        out_specs=(q_spec, win_spec, win_spec),
        scratch_shapes=[pltpu.VMEM((2, N_KV * N_SEL, HEAD_DIM, PAGE_SIZE), f32),
                        pltpu.VMEM((2, N_KV * N_SEL, HEAD_DIM, PAGE_SIZE), f32),
                        pltpu.SemaphoreType.DMA((2, 2))],
    )
    return pl.pallas_call(
        _sample_attn_kernel,
        grid_spec=grid_spec,
        out_shape=(jax.ShapeDtypeStruct((n, N_KV, QROWS, HEAD_DIM), f32),
                   jax.ShapeDtypeStruct((n, N_KV, HEAD_DIM, WINDOW), f32),
                   jax.ShapeDtypeStruct((n, N_KV, HEAD_DIM, WINDOW), f32)),
        compiler_params=pltpu.CompilerParams(dimension_semantics=("arbitrary",), vmem_limit_bytes=VMEM_LIMIT),
        name="sample_attention",
    )(page_table, idx, q8u, k_pool_t, v_pool_t, knew_t, vnew_t, k_win_t, v_win_t, kwn_t, vwn_t, o_cmp, ngb)


def _nsa_sample_paged(q_pad, ks, vs, kw, vw, ng, page_table, ck_cmp, cv_cmp, ck_slc, cv_slc, ck_win, cv_win,
                      cmp_wk, cmp_wv):
    n = q_pad.shape[0]
    n_phys = ck_cmp.shape[0]
    pool_t = lambda p: p.transpose(0, 2, 3, 1)
    q4 = q_pad.reshape(n, N_KV, GROUP, N_KV, HEAD_DIM)
    q8 = jnp.pad(q_pad.reshape(n, N_KV, GROUP, LANES), ((0, 0), (0, 0), (0, QROWS - GROUP), (0, 0)))
    q8u = jnp.pad(jnp.stack([q4[:, h, :, h, :] for h in range(N_KV)], axis=1),
                  ((0, 0), (0, 0), (0, QROWS - GROUP), (0, 0)))
    o_cmp, p_sum = _sample_cmp(page_table, q8, pool_t(ck_cmp).reshape(n_phys, D_KV, PAGE_SIZE),
                               pool_t(cv_cmp).reshape(n_phys, D_KV, PAGE_SIZE), cmp_wk, cmp_wv)
    idx = _sample_select(p_sum)
    ngb = ng[:, :N_NG].reshape(n, N_KV, GROUP, 3).transpose(0, 1, 3, 2)
    ngb = jnp.pad(ngb, ((0, 0), (0, 0), (0, 0), (0, QROWS - GROUP)))
    ngb = jnp.broadcast_to(ngb[..., None], (n, N_KV, 3, QROWS, HEAD_DIM))
    col0 = lambda t: jnp.pad(t.reshape(n, N_KV, HEAD_DIM, 1), ((0, 0), (0, 0), (0, 0), (0, LANES - 1)))
    o8, kwo, vwo = _sample_attn(page_table, idx, q8u, pool_t(ck_slc), pool_t(cv_slc), col0(ks), col0(vs),
                                pool_t(ck_win), pool_t(cv_win), col0(kw), col0(vw), o_cmp, ngb)
    o = o8[:, :, :GROUP, :].transpose(0, 2, 1, 3).reshape(n, D_Q).astype(jnp.bfloat16)
    return o, kwo.transpose(0, 3, 1, 2), vwo.transpose(0, 3, 1, 2)


LRU_PAIRS = D_RNN // LANES


def _lru_gates(xc, wa_ref, ba, wx_ref, bx, lam):
    f32 = jnp.float32
    xb = xc.astype(jnp.bfloat16)
    rg = jnp.concatenate([jnp.dot(xb[:, j * LANES:(j + 1) * LANES], wa_ref[j], preferred_element_type=f32)
                          for j in range(LRU_PAIRS)], axis=-1)
    ig = jnp.concatenate([jnp.dot(xb[:, j * LANES:(j + 1) * LANES], wx_ref[j], preferred_element_type=f32)
                          for j in range(LRU_PAIRS)], axis=-1)
    r = jax.nn.sigmoid(rg + ba)
    i = jax.nn.sigmoid(ig + bx)
    log_a = -LRU_C * r * jax.nn.softplus(-lam)
    a = jnp.exp(log_a)
    u = jnp.sqrt(-jnp.tanh(log_a) * (a * a + 1.0)) * (i * xc)
    return a, u


def _lru_prompt_kernel(arag_ref, cbuf_ref, h0_ref, cw_ref, cb_ref, wa_ref, ba_ref, wx_ref, bx_ref, lam_ref,
                       y_ref, hl_ref, prev_scr, h_scr, hbuf):
    f32 = jnp.float32
    t = arag_ref.shape[0]

    @pl.when(pl.program_id(1) == 0)
    def _():
        prev_scr[...] = cbuf_ref[0]
        h_scr[...] = h0_ref[0]

    ar = arag_ref[:, :D_RNN]
    row8 = lax.broadcasted_iota(jnp.int32, (SUBLANES, 1), 0)
    prev = prev_scr[...]
    xc = cb_ref[...] + cw_ref[CONV_W - 1:CONV_W, :] * ar
    for k in range(1, CONV_W):
        rolled = pltpu.roll(ar, k, 0)
        head = jnp.where(row8 < k, pltpu.roll(prev, k, 0), rolled[0:SUBLANES])
        shifted = jnp.concatenate([head, rolled[SUBLANES:]], axis=0)
        xc = xc + cw_ref[CONV_W - 1 - k:CONV_W - k, :] * shifted
    prev_scr[...] = ar[t - SUBLANES:]

    a, u = _lru_gates(xc, wa_ref, ba_ref[...], wx_ref, bx_ref[...], lam_ref[...])

    rowm = lax.broadcasted_iota(jnp.int32, (t, 1), 0) & (SUBLANES - 1)
    for s in (1, 2, 4):
        ok = rowm >= s
        u = jnp.where(ok, a * pltpu.roll(u, s, 0) + u, u)
        a = jnp.where(ok, a * pltpu.roll(a, s, 0), a)
    h = h_scr[...]
    for g in range(t // SUBLANES):
        rs = slice(g * SUBLANES, (g + 1) * SUBLANES)
        hg = a[rs] * h + u[rs]
        hbuf[rs, :] = hg
        h = jnp.broadcast_to(hg[SUBLANES - 1:SUBLANES, :], (SUBLANES, D_RNN))
    h_scr[...] = h
    hl_ref[0] = h
    y_ref[...] = (hbuf[...] * jax.nn.gelu(arag_ref[:, D_RNN:])).astype(y_ref.dtype)


def _lru_pair_weights(w):
    w4 = w.reshape(LRU_PAIRS, 2, LRU_BW, LRU_BW)
    eye = jnp.eye(2, dtype=w.dtype)
    return jnp.einsum('jaio,ab->jaibo', w4, eye).reshape(LRU_PAIRS, LANES, LANES).astype(jnp.bfloat16)


def _lru_weights(conv_w, conv_b, wa, ba, wx, bx, lam):
    row = lambda v: v.reshape(1, D_RNN).astype(jnp.float32)
    return (conv_w.astype(jnp.float32), row(conv_b), _lru_pair_weights(wa), row(ba), _lru_pair_weights(wx), row(bx),
            row(lam))


def _lru_prompt(arag, conv_buf, h0, lw, b, s, t):
    assert s % t == 0 and t % SUBLANES == 0
    nt = s // t
    cbuf8 = jnp.pad(conv_buf.astype(jnp.float32), ((0, 0), (SUBLANES - (CONV_W - 1), 0), (0, 0)))
    h08 = jnp.broadcast_to(h0.astype(jnp.float32)[:, None, :], (b, SUBLANES, D_RNN))
    const2 = lambda bi, i: (0, 0)
    const3 = lambda bi, i: (0, 0, 0)
    per_b = lambda bi, i: (bi, 0, 0)
    y, hl = pl.pallas_call(
        _lru_prompt_kernel,
        grid=(b, nt),
        in_specs=[
            pl.BlockSpec((t, 2 * D_RNN), lambda bi, i: (bi * nt + i, 0)),
            pl.BlockSpec((1, SUBLANES, D_RNN), per_b),
            pl.BlockSpec((1, SUBLANES, D_RNN), per_b),
            pl.BlockSpec((CONV_W, D_RNN), const2),
            pl.BlockSpec((1, D_RNN), const2),
            pl.BlockSpec((LRU_PAIRS, LANES, LANES), const3),
            pl.BlockSpec((1, D_RNN), const2),
            pl.BlockSpec((LRU_PAIRS, LANES, LANES), const3),
            pl.BlockSpec((1, D_RNN), const2),
            pl.BlockSpec((1, D_RNN), const2),
        ],
        out_specs=(pl.BlockSpec((t, D_RNN), lambda bi, i: (bi * nt + i, 0)),
                   pl.BlockSpec((1, SUBLANES, D_RNN), per_b)),
        out_shape=(jax.ShapeDtypeStruct((b * s, D_RNN), jnp.bfloat16),
                   jax.ShapeDtypeStruct((b, SUBLANES, D_RNN), jnp.float32)),
        scratch_shapes=[pltpu.VMEM((SUBLANES, D_RNN), jnp.float32), pltpu.VMEM((SUBLANES, D_RNN), jnp.float32),
                        pltpu.VMEM((t, D_RNN), jnp.float32)],
        compiler_params=pltpu.CompilerParams(dimension_semantics=("arbitrary", "arbitrary"),
                                             vmem_limit_bytes=VMEM_LIMIT),
        name="lru_prompt",
    )(arag, cbuf8, h08, *lw)
    return y, hl[:, 0, :]


def _lru_step_kernel(arag_ref, c0_ref, c1_ref, c2_ref, h0_ref, cw_ref, cb_ref, wa_ref, ba_ref, wx_ref, bx_ref,
                     lam_ref, y_ref, h_ref):
    ar = arag_ref[:, :D_RNN]
    xc = (cb_ref[...] + cw_ref[0:1, :] * c0_ref[...] + cw_ref[1:2, :] * c1_ref[...] + cw_ref[2:3, :] * c2_ref[...]
          + cw_ref[3:4, :] * ar)
    a, u = _lru_gates(xc, wa_ref, ba_ref[...], wx_ref, bx_ref[...], lam_ref[...])
    h = a * h0_ref[...] + u
    h_ref[...] = h
    y_ref[...] = (h * jax.nn.gelu(arag_ref[:, D_RNN:])).astype(y_ref.dtype)


def _lru_step(arag, conv_buf, h0, lw):
    n = arag.shape[0]
    f32 = jnp.float32
    cb = conv_buf.astype(f32)
    return pl.pallas_call(
        _lru_step_kernel,
        out_shape=(jax.ShapeDtypeStruct((n, D_RNN), jnp.bfloat16), jax.ShapeDtypeStruct((n, D_RNN), f32)),
        compiler_params=pltpu.CompilerParams(vmem_limit_bytes=VMEM_LIMIT),
        name="lru_step",
    )(arag, cb[:, 0], cb[:, 1], cb[:, 2], h0.astype(f32), *lw)


def _merge_router_kernel(x_ref, ylru_ref, onsa_ref, gab_ref, wa_ref, wb_ref, wo_ref, gffn_ref, rw_ref, rb_ref,
                         tri_ref, xn_ref, hn_ref, te_ref, tr_ref, tg_ref, cnt_ref, run_scr):
    f32, bf16 = jnp.float32, jnp.bfloat16
    tm = x_ref.shape[0]

    @pl.when(pl.program_id(0) == 0)
    def _():
        run_scr[...] = jnp.zeros_like(run_scr)

    ya = jnp.dot(ylru_ref[...], wa_ref[...], preferred_element_type=f32)
    yb = jnp.dot(onsa_ref[...], wb_ref[...], preferred_element_type=f32)
    mix = jax.nn.sigmoid(gab_ref[:, :D_MODEL]) * ya + jax.nn.sigmoid(gab_ref[:, D_MODEL:]) * yb
    xn = x_ref[...] + jnp.dot(mix.astype(bf16), wo_ref[...], preferred_element_type=f32)
    xn_ref[...] = xn
    hn = xn * lax.rsqrt(jnp.mean(xn * xn, axis=-1, keepdims=True) + EPS) * gffn_ref[...]
    hn_ref[...] = hn

    hn_hi = hn.astype(bf16)
    hn_lo = (hn - hn_hi.astype(f32)).astype(bf16)
    logits = (jnp.dot(hn_hi, rw_ref[0], preferred_element_type=f32)
              + jnp.dot(hn_lo, rw_ref[0], preferred_element_type=f32)
              + jnp.dot(hn_hi, rw_ref[1], preferred_element_type=f32)) + rb_ref[...]
    lane = lax.broadcasted_iota(jnp.int32, (tm, LANES), 1)
    ninf = -jnp.inf
    l = jnp.where(lane < N_EXPERTS, logits, ninf)
    onehots, m0 = [], None
    for k in range(TOP_K):
        m = jnp.max(l, axis=-1, keepdims=True)
        idx = jnp.min(jnp.where(l == m, lane, LANES), axis=-1, keepdims=True)
        oh = lane == idx
        onehots.append((oh, idx))
        m0 = m if k == 0 else m0
        l = jnp.where(oh, ninf, l)
    sel = onehots[0][0] | onehots[1][0] | onehots[2][0] | onehots[3][0]
    e = jnp.where(sel, jnp.exp(logits - m0), 0.0)
    gates = e / jnp.sum(e, axis=-1, keepdims=True)

    self = jnp.where(sel, 1.0, 0.0)
    incl = jnp.dot(tri_ref[...], self.astype(bf16), preferred_element_type=f32)
    run = run_scr[0:1, :]
    rank = run + incl - self
    total = run + incl[tm - 1:tm, :]
    run_scr[...] = jnp.broadcast_to(total, run_scr.shape)
    cnt_ref[...] = jnp.broadcast_to(total, cnt_ref.shape)

    te = jnp.zeros((tm, LANES), jnp.int32)
    tr = jnp.zeros((tm, LANES), f32)
    tg = jnp.zeros((tm, LANES), f32)
    for k, (oh, idx) in enumerate(onehots):
        te = jnp.where(lane == k, idx, te)
        tr = jnp.where(lane == k, jnp.sum(jnp.where(oh, rank, 0.0), axis=-1, keepdims=True), tr)
        tg = jnp.where(lane == k, jnp.sum(jnp.where(oh, gates, 0.0), axis=-1, keepdims=True), tg)
    te_ref[...] = te
    tr_ref[...] = tr.astype(jnp.int32)
    tg_ref[...] = tg


def _merge_router(x2d, ylru, onsa, gab, wa, wb, wo, g_ffn, rw_pad, rb_pad, tm):
    m = x2d.shape[0]
    assert m % tm == 0
    f32 = jnp.float32
    row = lambda i: (i, 0)
    const = lambda i: (0, 0)
    tri = jnp.asarray(np.tril(np.ones((tm, tm), np.float32)), dtype=jnp.bfloat16)
    lane_out = lambda dt: (jax.ShapeDtypeStruct((m, LANES), dt), pl.BlockSpec((tm, LANES), row))
    outs = [
        (jax.ShapeDtypeStruct((m, D_MODEL), f32), pl.BlockSpec((tm, D_MODEL), row)),
        (jax.ShapeDtypeStruct((m, D_MODEL), f32), pl.BlockSpec((tm, D_MODEL), row)),
        lane_out(jnp.int32), lane_out(jnp.int32), lane_out(f32),
        (jax.ShapeDtypeStruct((8, LANES), f32), pl.BlockSpec((8, LANES), const)),
    ]
    return pl.pallas_call(
        _merge_router_kernel,
        grid=(m // tm,),
        in_specs=[
            pl.BlockSpec((tm, D_MODEL), row),
            pl.BlockSpec((tm, D_RNN), row),
            pl.BlockSpec((tm, D_Q), row),
            pl.BlockSpec((tm, 2 * D_MODEL), row),
            pl.BlockSpec((D_RNN, D_MODEL), const),
            pl.BlockSpec((D_Q, D_MODEL), const),
            pl.BlockSpec((D_MODEL, D_MODEL), const),
            pl.BlockSpec((1, D_MODEL), const),
            pl.BlockSpec((2, D_MODEL, LANES), lambda i: (0, 0, 0)),
            pl.BlockSpec((1, LANES), const),
            pl.BlockSpec((tm, tm), const),
        ],
        out_specs=tuple(o[1] for o in outs),
        out_shape=tuple(o[0] for o in outs),
        scratch_shapes=[pltpu.VMEM((8, LANES), f32)],
        compiler_params=pltpu.CompilerParams(dimension_semantics=("arbitrary",), vmem_limit_bytes=VMEM_LIMIT),
        name="merge_router",
    )(x2d, ylru, onsa, gab, wa, wb, wo, g_ffn.reshape(1, D_MODEL), rw_pad, rb_pad, tri)


MOE_BLK = 512


def _moe_dispatch_kernel(last_ref, dest_ref, hn_ref, xs_ref, zero_scr, sem, zsem):
    tm = hn_ref.shape[0]

    @pl.when(pl.program_id(0) == 0)
    def _():
        zero_scr[...] = jnp.zeros_like(zero_scr)
        for e in range(N_EXPERTS):
            @pl.when(last_ref[e] >= 0)
            def _():
                start = pl.multiple_of(last_ref[e], MOE_BLK)
                cp = pltpu.make_async_copy(zero_scr, xs_ref.at[pl.ds(start, MOE_BLK)], zsem)
                cp.start()
                cp.wait()

    def issue(r, _):
        for k in range(TOP_K):
            d = dest_ref[0, 0, r * TOP_K + k]
            pltpu.make_async_copy(hn_ref.at[pl.ds(r, 1)], xs_ref.at[pl.ds(d, 1)], sem).start()
        return 0

    lax.fori_loop(0, tm, issue, 0)
    pltpu.make_async_copy(xs_ref.at[pl.ds(0, tm * TOP_K)], xs_ref.at[pl.ds(0, tm * TOP_K)], sem).wait()


def _moe_dispatch(hn, dest, last_tile, n_slots, tm):
    m = hn.shape[0]
    nt = m // tm
    grid_spec = pltpu.PrefetchScalarGridSpec(
        num_scalar_prefetch=1,
        grid=(nt,),
        in_specs=[
            pl.BlockSpec((1, 1, tm * TOP_K), lambda i, lt: (i, 0, 0), memory_space=pltpu.SMEM),
            pl.BlockSpec((tm, D_MODEL), lambda i, lt: (i, 0)),
        ],
        out_specs=pl.BlockSpec(memory_space=pl.ANY),
        scratch_shapes=[pltpu.VMEM((MOE_BLK, D_MODEL), hn.dtype), pltpu.SemaphoreType.DMA(()),
                        pltpu.SemaphoreType.DMA(())],
    )
    return pl.pallas_call(
        _moe_dispatch_kernel,
        grid_spec=grid_spec,
        out_shape=jax.ShapeDtypeStruct((n_slots, D_MODEL), hn.dtype),
        compiler_params=pltpu.CompilerParams(dimension_semantics=("arbitrary",), vmem_limit_bytes=VMEM_LIMIT),
        name="moe_dispatch",
    )(last_tile, dest.reshape(nt, 1, tm * TOP_K), hn)


def _moe_experts_kernel(te_ref, nused_ref, xs_ref, wup_ref, bup_ref, wdn_ref, bdn_ref, ys_ref, wup_scr, wdn_scr):
    f32, bf16 = jnp.float32, jnp.bfloat16
    j = pl.program_id(0)
    changed = jnp.logical_or(j == 0, te_ref[j] != te_ref[jnp.maximum(j - 1, 0)])

    @pl.when(jnp.logical_and(j < nused_ref[0], changed))
    def _():
        wup_scr[...] = wup_ref[0].astype(bf16)
        wdn_scr[...] = wdn_ref[0].astype(bf16)

    @pl.when(j < nused_ref[0])
    def _():
        x = xs_ref[...].astype(bf16)
        gu = jnp.dot(x, wup_scr[...], preferred_element_type=f32) + bup_ref[0]
        a = jnp.minimum(gu[:, :D_FF], SWIGLU_LIMIT)
        u = jnp.clip(gu[:, D_FF:], -SWIGLU_LIMIT, SWIGLU_LIMIT)
        act = (u + 1.0) * a * jax.nn.sigmoid(SWIGLU_ALPHA * a)
        ys_ref[...] = jnp.dot(act.astype(bf16), wdn_scr[...], preferred_element_type=f32) + bdn_ref[0]


def _moe_experts(xs, tile_expert, n_used, w_up, b_up, w_down, b_down):
    n_slots = xs.shape[0]
    nt = n_slots // MOE_BLK
    tile = lambda j, te, nu: (jnp.minimum(j, nu[0] - 1), 0)
    wsel = lambda j, te, nu: (te[jnp.minimum(j, nu[0] - 1)], 0, 0)
    grid_spec = pltpu.PrefetchScalarGridSpec(
        num_scalar_prefetch=2,
        grid=(nt,),
        in_specs=[
            pl.BlockSpec((MOE_BLK, D_MODEL), tile),
            pl.BlockSpec((1, D_MODEL, 2 * D_FF), wsel),
            pl.BlockSpec((1, 1, 2 * D_FF), wsel),
            pl.BlockSpec((1, D_FF, D_MODEL), wsel),
            pl.BlockSpec((1, 1, D_MODEL), wsel),
        ],
        out_specs=pl.BlockSpec((MOE_BLK, D_MODEL), tile),
        scratch_shapes=[pltpu.VMEM((D_MODEL, 2 * D_FF), jnp.bfloat16), pltpu.VMEM((D_FF, D_MODEL), jnp.bfloat16)],
    )
    return pl.pallas_call(
        _moe_experts_kernel,
        grid_spec=grid_spec,
        out_shape=jax.ShapeDtypeStruct((n_slots, D_MODEL), jnp.float32),
        compiler_params=pltpu.CompilerParams(dimension_semantics=("arbitrary",), vmem_limit_bytes=VMEM_LIMIT),
        name="moe_experts",
    )(tile_expert, n_used, xs, w_up, b_up.reshape(N_EXPERTS, 1, 2 * D_FF), w_down,
      b_down.reshape(N_EXPERTS, 1, D_MODEL))


def _moe_combine_kernel(dest_ref, xn_ref, tg_ref, ys_ref, y_ref, buf, sem):
    tm = xn_ref.shape[0]

    def issue(r, _):
        for k in range(TOP_K):
            d = dest_ref[0, 0, r * TOP_K + k]
            pltpu.make_async_copy(ys_ref.at[pl.ds(d, 1)], buf.at[k, pl.ds(r, 1)], sem).start()
        return 0

    lax.fori_loop(0, tm, issue, 0)
    pltpu.make_async_copy(ys_ref.at[pl.ds(0, tm * TOP_K)], ys_ref.at[pl.ds(0, tm * TOP_K)], sem).wait()
    tg = tg_ref[...]
    acc = xn_ref[...]
    for k in range(TOP_K):
        acc = acc + tg[:, k:k + 1] * buf[k]
    y_ref[...] = acc


def _moe_combine(xn, tg, ys, dest, tm):
    m = xn.shape[0]
    nt = m // tm
    return pl.pallas_call(
        _moe_combine_kernel,
        grid=(nt,),
        in_specs=[
            pl.BlockSpec((1, 1, tm * TOP_K), lambda i: (i, 0, 0), memory_space=pltpu.SMEM),
            pl.BlockSpec((tm, D_MODEL), lambda i: (i, 0)),
            pl.BlockSpec((tm, LANES), lambda i: (i, 0)),
            pl.BlockSpec(memory_space=pl.ANY),
        ],
        out_specs=pl.BlockSpec((tm, D_MODEL), lambda i: (i, 0)),
        out_shape=jax.ShapeDtypeStruct((m, D_MODEL), jnp.float32),
        scratch_shapes=[pltpu.VMEM((TOP_K, tm, D_MODEL), jnp.float32), pltpu.SemaphoreType.DMA(())],
        compiler_params=pltpu.CompilerParams(dimension_semantics=("arbitrary",), vmem_limit_bytes=VMEM_LIMIT),
        name="moe_combine",
    )(dest.reshape(nt, 1, tm * TOP_K), xn, tg, ys)


def _moe_layer(xn, hn, te, tr, tg, cnt, w_up, b_up, w_down, b_down, tm):
    m = xn.shape[0]
    n_tiles = -(-m * TOP_K // MOE_BLK) + N_EXPERTS
    counts = cnt[0, :N_EXPERTS].astype(jnp.int32)
    padded = (counts + MOE_BLK - 1) // MOE_BLK * MOE_BLK
    pad_end = jnp.cumsum(padded)
    pad_start = pad_end - padded
    dest = (pad_start[te[:, :TOP_K]] + tr[:, :TOP_K]).astype(jnp.int32)
    tile_start = jnp.arange(n_tiles, dtype=jnp.int32) * MOE_BLK
    tile_expert = jnp.minimum(jnp.sum(pad_end[None, :] <= tile_start[:, None], axis=1), N_EXPERTS - 1).astype(jnp.int32)
    n_used = (pad_end[-1:] // MOE_BLK).astype(jnp.int32)
    last_tile = jnp.where(padded > 0, pad_end - MOE_BLK, -1).astype(jnp.int32)
    xs = _moe_dispatch(hn, dest, last_tile, n_tiles * MOE_BLK, tm)
    ys = _moe_experts(xs, tile_expert, n_used, w_up, b_up, w_down, b_down)
    return _moe_combine(xn, tg, ys, dest, tm)


def _pad_w_in(w):
    c0 = 2 * D_RNN
    wq = w[:, c0:c0 + D_Q].reshape(D_MODEL, N_KV, GROUP, 1, HEAD_DIM)
    sel = jnp.eye(N_KV, dtype=w.dtype).reshape(N_KV, 1, N_KV, 1)
    wq = (wq * sel).reshape(D_MODEL, D_QP)
    c1 = c0 + D_Q
    c2 = c1 + 6 * D_KV + N_NG
    pad = jnp.zeros((D_MODEL, LANES - N_NG), w.dtype)
    return jnp.concatenate([w[:, :c0], wq, w[:, c1:c2], pad, w[:, c2:]], axis=1).astype(jnp.bfloat16)


def kernel(x_prompt, x_sample, page_table, cache_k_cmp, cache_v_cmp, cache_k_slc, cache_v_slc,
           cache_k_win, cache_v_win, state_h, state_conv, g_mix, w_in, conv_w, conv_b, lru_wa, lru_ba,
           lru_wx, lru_bx, lru_lambda, g_q, g_k_cmp, g_k_slc, g_k_win, cmp_pos_k, cmp_w1_k, cmp_w2_k,
           cmp_pos_v, cmp_w1_v, cmp_w2_v, w_proj_a, w_proj_b, w_out, g_ffn, router_w, router_b,
           exp_w_up, exp_b_up, exp_w_down, exp_b_down):
    xp, xs = x_prompt, x_sample
    bp, sp = xp.shape[:2]
    bs, ss = xs.shape[:2]
    pos_p = jnp.arange(sp, dtype=jnp.int32)
    pos_s = PAST_LEN + jnp.arange(ss, dtype=jnp.int32)
    depth = w_in.shape[0]
    layer_states = []
    tile2 = lambda g: jnp.concatenate([g, g], axis=-1)
    for l in range(depth):
        lru_w = (conv_w[l], conv_b[l], lru_wa[l], lru_ba[l], lru_wx[l], lru_bx[l], lru_lambda[l])
        w_pad = _pad_w_in(w_in[l])
        gq2 = tile2(g_q[l]).reshape(1, LANES)
        gk3 = jnp.stack([tile2(g_k_cmp[l]), tile2(g_k_slc[l]), tile2(g_k_win[l])])
        cmp_wk = _compress_weights(cmp_pos_k[l], cmp_w1_k[l], cmp_w2_k[l])
        cmp_wv = _compress_weights(cmp_pos_v[l], cmp_w1_v[l], cmp_w2_v[l])
        wb_perm = w_proj_b[l].reshape(N_KV, GROUP, HEAD_DIM, D_MODEL).transpose(1, 0, 2, 3).reshape(D_Q, D_MODEL)
        bf16 = jnp.bfloat16
        lru_k = _lru_weights(*lru_w)
        rw_pad = jnp.pad(router_w[l].astype(jnp.float32), ((0, 0), (0, LANES - N_EXPERTS)))
        rw_hi = rw_pad.astype(bf16)
        rw_pad = jnp.stack([rw_hi, (rw_pad - rw_hi.astype(jnp.float32)).astype(bf16)])
        rb_pad = jnp.pad(router_b[l].astype(jnp.float32), (0, LANES - N_EXPERTS)).reshape(1, LANES)
        merge_w = (w_proj_a[l].astype(bf16), wb_perm.astype(bf16), w_out[l].astype(bf16), g_ffn[l], rw_pad, rb_pad)
        moe_w = (exp_w_up[l], exp_b_up[l], exp_w_down[l], exp_b_down[l])
        arag, q_pad, kc, vc, ks, vs, kw, vw, ng, gab, *kv_t = _input_projection(
            xp.reshape(bp * sp, D_MODEL), pos_p, 256, g_mix[l], w_pad, gq2, gk3, kv_transposed=True)
        rows = lambda t: t.reshape(bp, sp, t.shape[-1])
        heads_t = lambda t: t.reshape(bp, N_KV, HEAD_DIM, t.shape[-1]).transpose(0, 3, 1, 2)
        conv0 = jnp.zeros((bp, CONV_W - 1, D_RNN), jnp.float32)
        y_lru, p_h = _lru_prompt(arag, conv0, jnp.zeros((bp, D_RNN), jnp.float32), lru_k, bp, sp, 256)
        p_conv = jnp.concatenate([conv0, rows(arag)[:, :, :D_RNN]], axis=1)[:, sp:]
        cmp_k, cmp_v = _compress_prompt(rows(kc), rows(vc), cmp_wk, cmp_wv)
        o_nsa = _nsa_prompt_attention(rows(q_pad), rows(ks), rows(vs), rows(kw), rows(vw), cmp_k, cmp_v, rows(ng))
        xn, hn, te, tr, tg, cnt = _merge_router(xp.reshape(bp * sp, D_MODEL), y_lru, o_nsa.reshape(bp * sp, D_Q),
                                                gab, *merge_w, 256)
        xp = _moe_layer(xn, hn, te, tr, tg, cnt, *moe_w, 256).reshape(bp, sp, D_MODEL)
        wbp = min(WINDOW, sp)
        p_states = tuple(heads_t(t) for t in kv_t[:4]) + tuple(heads_t(t[:, :, sp - wbp:]) for t in kv_t[4:]) + (
            p_h, p_conv)
        assert ss == 1
        pos_rows_s = jnp.broadcast_to(pos_s[None, :], (bs, ss)).reshape(-1)
        arag, q_pad, kc, vc, ks, vs, kw, vw, ng, gab = _input_projection(
            xs.reshape(bs, D_MODEL), pos_rows_s, bs, g_mix[l], w_pad, gq2, gk3)
        heads = lambda t: t.reshape(bs, ss, N_KV, HEAD_DIM)
        y_lru, s_h_l = _lru_step(arag, state_conv[l], state_h[l], lru_k)
        s_conv_l = jnp.concatenate([state_conv[l][:, 1:], arag[:, None, :D_RNN]], axis=1)
        o_nsa, s_kw_l, s_vw_l = _nsa_sample_paged(
            q_pad, ks, vs, kw, vw, ng, page_table, cache_k_cmp[l], cache_v_cmp[l], cache_k_slc[l], cache_v_slc[l],
            cache_k_win[l], cache_v_win[l], cmp_wk, cmp_wv)
        xn, hn, te, tr, tg, cnt = _merge_router(xs.reshape(bs, D_MODEL), y_lru, o_nsa, gab, *merge_w, bs)
        xs = _moe_layer(xn, hn, te, tr, tg, cnt, *moe_w, bs).reshape(bs, ss, D_MODEL)
        s_states = (heads(kc), heads(vc), heads(ks), heads(vs), s_kw_l, s_vw_l, s_h_l, s_conv_l)
        layer_states.append(p_states + s_states)
    (p_k_cmp, p_v_cmp, p_k_slc, p_v_slc, p_k_win, p_v_win, p_h, p_conv,
     s_k_cmp, s_v_cmp, s_k_slc, s_v_slc, s_k_win, s_v_win, s_h, s_conv) = [jnp.stack(z) for z in zip(*layer_states)]
    return (xp, xs, p_k_cmp, p_v_cmp, p_k_slc, p_v_slc, p_k_win, p_v_win, p_h, p_conv,
            s_k_cmp, s_v_cmp, s_k_slc, s_v_slc, s_k_win, s_v_win, s_h, s_conv)
```

```python
import jax
import jax.numpy as jnp
import numpy as np
from jax import lax
from jax.experimental import pallas as pl
from jax.experimental.pallas import tpu as pltpu

D_MODEL = 1024
PAST_LEN = 8192
PAGE_SIZE = 128
D_RNN = D_MODEL
LRU_BLOCKS = 16
LRU_BW = D_RNN // LRU_BLOCKS
CONV_W = 4
LRU_C = 8.0
N_HEADS = 8
HEAD_DIM = 64
N_KV = 2
GROUP = N_HEADS // N_KV
D_Q = N_HEADS * HEAD_DIM
D_KV = N_KV * HEAD_DIM
ROPE_DIM = HEAD_DIM // 4
ROPE_THETA = 500000.0
CMP_LEN = 32
CMP_STRIDE = 16
CMP_HID = 256
SLC_BLOCK = 64
N_SEL = 16
WINDOW = 512
Q_BLOCK = 64
N_EXPERTS = 32
TOP_K = 4
D_FF = D_MODEL
SWIGLU_LIMIT = 7.0
SWIGLU_ALPHA = 1.702
EPS = 1e-6
NEG = -1e30
BIG = 1e30

LANES = 128
SUBLANES = 8
VMEM_LIMIT = 56 * 1024 * 1024

N_NG = 3 * N_HEADS
D_QP = N_HEADS * LANES
C_AR = 0
C_Q = 2 * D_RNN
C_KV = C_Q + D_QP
C_NG = C_KV + 6 * D_KV
C_GAB = C_NG + LANES
N_IN_PAD = C_GAB + 2 * D_MODEL


def _seg_sum64(x2, ones_blk):
    hi = x2.astype(jnp.bfloat16)
    lo = (x2 - hi.astype(jnp.float32)).astype(jnp.bfloat16)
    return (jnp.dot(hi, ones_blk, preferred_element_type=jnp.float32)
            + jnp.dot(lo, ones_blk, preferred_element_type=jnp.float32))


def _head_norm_rope(y, gain, ones_blk, rc, rs1, rs2):
    ss = _seg_sum64(y * y, ones_blk)
    yn = y * lax.rsqrt(ss * (1.0 / HEAD_DIM) + EPS) * gain
    half = ROPE_DIM // 2
    return yn * rc + pltpu.roll(yn, LANES - half, 1) * rs1 + pltpu.roll(yn, half, 1) * rs2


def _proj_kernel(x_ref, gmix_ref, w_ref, ones_ref, gq_ref, gk_ref, rc_ref, rs1_ref, rs2_ref,
                 arag_ref, q_ref, kc_ref, vc_ref, ks_ref, vs_ref, kw_ref, vw_ref, ng_ref, gab_ref, *kv_t_refs):
    x = x_ref[...]
    xn = x * lax.rsqrt(jnp.mean(x * x, axis=-1, keepdims=True) + EPS) * gmix_ref[...]
    xb = xn.astype(jnp.bfloat16)

    def proj(c0, width):
        return jnp.dot(xb, w_ref[:, c0:c0 + width], preferred_element_type=jnp.float32)

    arag_ref[...] = proj(C_AR, 2 * D_RNN)
    gab_ref[...] = proj(C_GAB, 2 * D_MODEL)
    ng_ref[...] = proj(C_NG, LANES)
    ones_blk = ones_ref[...]
    rc, rs1, rs2 = rc_ref[...], rs1_ref[...], rs2_ref[...]
    scale = HEAD_DIM ** -0.5
    for j in range(N_HEADS):
        y = proj(C_Q + j * LANES, LANES)
        yr = _head_norm_rope(y, gq_ref[...], ones_blk, rc, rs1, rs2)
        q_ref[:, j * LANES:(j + 1) * LANES] = (yr * scale).astype(q_ref.dtype)
    kv_outs = (kc_ref, vc_ref, ks_ref, vs_ref, kw_ref, vw_ref)
    for j in range(3):
        yk = _head_norm_rope(proj(C_KV + 2 * j * LANES, LANES), gk_ref[j:j + 1, :], ones_blk, rc, rs1, rs2)
        yv = proj(C_KV + (2 * j + 1) * LANES, LANES)
        for i, y in ((2 * j, yk), (2 * j + 1, yv)):
            kv_outs[i][...] = y.astype(kv_outs[i].dtype)
            if kv_t_refs:
                kv_t_refs[i][0] = y.T


def _rope_tables(pos):
    half = ROPE_DIM // 2
    inv = ROPE_THETA ** (-jnp.arange(half, dtype=jnp.float32) / half)
    ang = pos.astype(jnp.float32)[:, None] * inv[None, :]
    cos, sin = jnp.cos(ang), jnp.sin(ang)
    p = pos.shape[0]
    ones = jnp.ones((p, HEAD_DIM - ROPE_DIM), jnp.float32)
    zeros = jnp.zeros((p, HEAD_DIM - ROPE_DIM), jnp.float32)
    zh = jnp.zeros((p, half), jnp.float32)
    rc = jnp.concatenate([cos, cos, ones], axis=-1)
    rs1 = jnp.concatenate([-sin, zh, zeros], axis=-1)
    rs2 = jnp.concatenate([zh, sin, zeros], axis=-1)
    tile2 = lambda t: jnp.concatenate([t, t], axis=-1)
    return tile2(rc), tile2(rs1), tile2(rs2)


def _ones_block64():
    i = np.arange(LANES)
    return jnp.asarray((i[:, None] // HEAD_DIM) == (i[None, :] // HEAD_DIM), dtype=jnp.bfloat16)


def _input_projection(x2d, pos_rows, tm, g_mix, w_pad, g_q, g_k3, kv_transposed=False):
    m = x2d.shape[0]
    p = pos_rows.shape[0]
    assert m % tm == 0 and p % tm == 0
    rc, rs1, rs2 = _rope_tables(pos_rows)
    n_tab = p // tm
    row = lambda i: (i, 0)
    const = lambda i: (0, 0)
    tab = lambda i: (i % n_tab, 0)
    f32 = jnp.float32
    kv_dt = jnp.bfloat16 if kv_transposed else f32
    out_shape = (
        jax.ShapeDtypeStruct((m, 2 * D_RNN), f32),
        jax.ShapeDtypeStruct((m, D_QP), jnp.bfloat16),
    ) + tuple(jax.ShapeDtypeStruct((m, D_KV), kv_dt) for _ in range(6)) + (
        jax.ShapeDtypeStruct((m, LANES), f32),
        jax.ShapeDtypeStruct((m, 2 * D_MODEL), f32),
    )
    out_specs = (
        pl.BlockSpec((tm, 2 * D_RNN), row),
        pl.BlockSpec((tm, D_QP), row),
    ) + tuple(pl.BlockSpec((tm, D_KV), row) for _ in range(6)) + (
        pl.BlockSpec((tm, LANES), row),
        pl.BlockSpec((tm, 2 * D_MODEL), row),
    )
    if kv_transposed:
        out_shape += tuple(jax.ShapeDtypeStruct((m // p, D_KV, p), f32) for _ in range(6))
        out_specs += tuple(pl.BlockSpec((1, D_KV, tm), lambda i: (i // n_tab, 0, i % n_tab)) for _ in range(6))
    return pl.pallas_call(
        _proj_kernel,
        grid=(m // tm,),
        in_specs=[
            pl.BlockSpec((tm, D_MODEL), row),
            pl.BlockSpec((1, D_MODEL), const),
            pl.BlockSpec((D_MODEL, N_IN_PAD), const),
            pl.BlockSpec((LANES, LANES), const),
            pl.BlockSpec((1, LANES), const),
            pl.BlockSpec((3, LANES), const),
            pl.BlockSpec((tm, LANES), tab),
            pl.BlockSpec((tm, LANES), tab),
            pl.BlockSpec((tm, LANES), tab),
        ],
        out_specs=out_specs,
        out_shape=out_shape,
        compiler_params=pltpu.CompilerParams(dimension_semantics=("arbitrary",), vmem_limit_bytes=VMEM_LIMIT),
        name="input_projection",
    )(x2d, g_mix.reshape(1, D_MODEL), w_pad, _ones_block64(), g_q, g_k3, rc, rs1, rs2)


SLC_CHUNK = 1024
WIN_SPAN = WINDOW + LANES
ROWS = GROUP * Q_BLOCK


def _dot_nt(a, b):
    return lax.dot_general(a, b, (((1,), (1,)), ((), ())), preferred_element_type=jnp.float32)


def _softmax_rows(s, mask):
    s = jnp.where(mask, s, NEG)
    m = jnp.max(s, axis=-1, keepdims=True)
    p = jnp.where(mask, jnp.exp(s - m), 0.0)
    return p / jnp.maximum(jnp.sum(p, axis=-1, keepdims=True), 1e-30)


def _split3_bf16(x):
    hi = x.astype(jnp.bfloat16)
    r1 = x - hi.astype(jnp.float32)
    mid = r1.astype(jnp.bfloat16)
    lo = (r1 - mid.astype(jnp.float32)).astype(jnp.bfloat16)
    return hi, mid, lo


def _nsa_prompt_kernel(q_ref, ksa_ref, vs_ref, kw_ref, vw_ref, ck_ref, cv_ref, ng_ref, mmt_ref,
                       o_ref, sv_scr):
    f32, bf16 = jnp.float32, jnp.bfloat16
    qb = pl.program_id(1)
    row = lax.broadcasted_iota(jnp.int32, (ROWS, 1), 0)
    pos_t = qb * Q_BLOCK + (row & (Q_BLOCK - 1))
    ncp = ck_ref.shape[1]

    qh = [jnp.concatenate([q_ref[0, :, (h * GROUP + g) * LANES:(h * GROUP + g + 1) * LANES]
                           for g in range(GROUP)], axis=0) for h in range(N_KV)]

    ck, cv = ck_ref[0], cv_ref[0]
    c_end = lax.broadcasted_iota(jnp.int32, (1, ncp), 1) * CMP_STRIDE + (CMP_LEN - 1)
    cmask = c_end <= pos_t
    o_cmp, psum = [], []
    for h in range(N_KV):
        p = _softmax_rows(_dot_nt(qh[h], ck), cmask)
        o_cmp.append(jnp.dot(p.astype(bf16), cv, preferred_element_type=f32))
        psum.append(p[0:Q_BLOCK] + p[Q_BLOCK:2 * Q_BLOCK] + p[2 * Q_BLOCK:3 * Q_BLOCK] + p[3 * Q_BLOCK:])
    ps = jnp.concatenate(psum, axis=0)

    mmt = mmt_ref[...]
    sc = sum(_dot_nt(mmt, part) for part in _split3_bf16(ps))[0:Q_BLOCK]
    jio = lax.broadcasted_iota(jnp.int32, (Q_BLOCK, LANES), 0)
    forced = (jio == 0) | (jio == qb) | (jio == qb - 1)
    sv = jnp.where(forced, BIG, jnp.where(jio <= qb, sc, NEG))
    sv_scr[...] = sv

    def rank_body(i, cnt):
        r = sv_scr[pl.ds(i, 1), :]
        beats = (r > sv) | ((r == sv) & (jio > i))
        return cnt + jnp.where(beats, 1.0, 0.0)

    cnt = lax.fori_loop(0, qb + 1, rank_body, jnp.zeros((Q_BLOCK, LANES), f32))
    bias_lo = jnp.where((cnt < N_SEL) & (sv > NEG / 2), 0.0, NEG)
    lane = lax.broadcasted_iota(jnp.int32, (Q_BLOCK, LANES), 1)
    bias_t = jnp.concatenate([jnp.where(lane >= Q_BLOCK, bias_lo, 0.0),
                              jnp.where(lane < Q_BLOCK, bias_lo, 0.0)], axis=0)
    bias = bias_t.T.astype(bf16)

    kio = lax.broadcasted_iota(jnp.int32, (1, SLC_CHUNK), 1)
    wio = lax.broadcasted_iota(jnp.int32, (1, WIN_SPAN), 1)
    wbase = pl.multiple_of((jnp.maximum(qb - WINDOW // Q_BLOCK, 0) // 2) * LANES, LANES)
    dpos = pos_t - (wbase + wio)
    wmask = (dpos >= 0) & (dpos < WINDOW)
    n_chunks = qb // (SLC_CHUNK // SLC_BLOCK) + 1
    sig = jax.nn.sigmoid(ng_ref[0])
    qa = [qh[h] + jnp.concatenate([bias[h * Q_BLOCK:(h + 1) * Q_BLOCK]] * GROUP, axis=0) for h in range(N_KV)]

    def chunk(c, carry, causal):
        k0 = pl.multiple_of(c * SLC_CHUNK, SLC_CHUNK)
        v = vs_ref[0, pl.ds(k0, SLC_CHUNK), :]
        new = []
        for h in range(N_KV):
            m, l, acc = carry[h]
            s = _dot_nt(qa[h], ksa_ref[0, h, pl.ds(k0, SLC_CHUNK), :])
            if causal:
                s = jnp.where(k0 + kio <= pos_t, s, NEG)
            m_new = jnp.maximum(m, jnp.max(s, axis=-1, keepdims=True))
            alpha = jnp.exp(m - m_new)
            p = jnp.exp(s - m_new)
            l = alpha * l + jnp.sum(p, axis=-1, keepdims=True)
            acc = alpha * acc + jnp.dot(p.astype(bf16), v, preferred_element_type=f32)
            new.append((m_new, l, acc))
        return tuple(new)

    init = (jnp.full((ROWS, 1), NEG, f32), jnp.zeros((ROWS, 1), f32), jnp.zeros((ROWS, LANES), f32))
    carry = lax.fori_loop(0, n_chunks - 1, lambda c, cr: chunk(c, cr, False), (init, init))
    carry = chunk(n_chunks - 1, carry, True)

    out_h = []
    for h in range(N_KV):
        _, l, acc = carry[h]
        o_slc = acc / jnp.maximum(l, 1e-30)

        pw = _softmax_rows(_dot_nt(qh[h], kw_ref[0, pl.ds(wbase, WIN_SPAN), :]), wmask)
        o_win = jnp.dot(pw.astype(bf16), vw_ref[0, pl.ds(wbase, WIN_SPAN), :], preferred_element_type=f32)

        outs = []
        for g in range(GROUP):
            rs = slice(g * Q_BLOCK, (g + 1) * Q_BLOCK)
            c0 = (h * GROUP + g) * 3
            outs.append(sig[:, c0:c0 + 1] * o_cmp[h][rs] + sig[:, c0 + 1:c0 + 2] * o_slc[rs]
                        + sig[:, c0 + 2:c0 + 3] * o_win[rs])
        out_h.append(outs)
    for g in range(GROUP):
        o_ref[0, :, g * LANES:(g + 1) * LANES] = jnp.where(lane < HEAD_DIM, out_h[0][g], out_h[1][g]).astype(o_ref.dtype)


def _block_score_matrix(ncp):
    ratio = SLC_BLOCK // CMP_STRIDE
    ovl = CMP_LEN // CMP_STRIDE - 1
    j = np.arange(LANES)[:, None]
    c = np.arange(ncp)[None, :]
    m = (c >= ratio * j - ovl) & (c <= ratio * j + ratio - 1) & (j < Q_BLOCK) & (c < ncp - 1)
    return jnp.asarray(m, dtype=jnp.bfloat16)


def _nsa_prompt_attention(q_pad, ks, vs, kw, vw, cmp_k, cmp_v, ng):
    b, s, _ = q_pad.shape
    assert s % SLC_CHUNK == 0 and s // SLC_BLOCK <= Q_BLOCK and s >= WIN_SPAN and Q_BLOCK == SLC_BLOCK
    bf16 = jnp.bfloat16
    ncp = cmp_k.shape[1]
    onehot = jax.nn.one_hot(jnp.arange(s) // SLC_BLOCK, Q_BLOCK, dtype=bf16)
    onehot = jnp.broadcast_to(onehot[None], (b, s, Q_BLOCK))
    ksb = ks.astype(bf16)
    ksa = jnp.stack([jnp.concatenate([ksb[:, :, :HEAD_DIM], onehot], axis=-1),
                     jnp.concatenate([onehot, ksb[:, :, HEAD_DIM:]], axis=-1)], axis=1)
    per_b = lambda bi, i: (bi, 0, 0)
    return pl.pallas_call(
        _nsa_prompt_kernel,
        grid=(b, s // Q_BLOCK),
        in_specs=[
            pl.BlockSpec((1, Q_BLOCK, D_QP), lambda bi, i: (bi, i, 0)),
            pl.BlockSpec((1, N_KV, s, LANES), lambda bi, i: (bi, 0, 0, 0)),
            pl.BlockSpec((1, s, LANES), per_b),
            pl.BlockSpec((1, s, LANES), per_b),
            pl.BlockSpec((1, s, LANES), per_b),
            pl.BlockSpec((1, ncp, LANES), per_b),
            pl.BlockSpec((1, ncp, LANES), per_b),
            pl.BlockSpec((1, Q_BLOCK, LANES), lambda bi, i: (bi, i, 0)),
            pl.BlockSpec((LANES, ncp), lambda bi, i: (0, 0)),
        ],
        out_specs=pl.BlockSpec((1, Q_BLOCK, D_Q), lambda bi, i: (bi, i, 0)),
        out_shape=jax.ShapeDtypeStruct((b, s, D_Q), bf16),
        scratch_shapes=[pltpu.VMEM((Q_BLOCK, LANES), jnp.float32)],
        compiler_params=pltpu.CompilerParams(dimension_semantics=("arbitrary", "arbitrary"),
                                             vmem_limit_bytes=VMEM_LIMIT),
        name="nsa_prompt_attention",
    )(q_pad, ksa, vs.astype(bf16), kw.astype(bf16), vw.astype(bf16), cmp_k, cmp_v, ng, _block_score_matrix(ncp))


CHUNK_W = CMP_STRIDE * D_KV
CMP_PARTS = CMP_LEN // CMP_STRIDE
HID2 = N_KV * CMP_HID


def _compress_weights(pos_emb, w1, w2):
    bf16 = jnp.bfloat16
    eye = jnp.eye(N_KV, dtype=w1.dtype)
    w1b = w1.reshape(CMP_PARTS, CMP_STRIDE, HEAD_DIM, CMP_HID)
    w1p = jnp.einsum('pldf,hk->lhdpkf', w1b, eye).reshape(CHUNK_W, CMP_PARTS * HID2).astype(bf16)
    w2p = jnp.einsum('fd,hk->hfkd', w2, eye).reshape(HID2, D_KV).astype(bf16)
    pos8 = jnp.zeros((8, CMP_LEN * HEAD_DIM), bf16).at[0].set(pos_emb.reshape(-1).astype(bf16))
    w1f = w1.reshape(CMP_LEN * HEAD_DIM, CMP_HID).astype(bf16)
    return w1p, w2p, pos8, w1f


def _compress_rows(x, w1p, w2p, pos8, w1f):
    f32 = jnp.float32
    r = x.shape[0]
    part = jnp.dot(x, w1p, preferred_element_type=f32)
    bias = jnp.dot(pos8, w1f, preferred_element_type=f32)[0:1]
    bias = jnp.concatenate([bias] * N_KV, axis=-1)
    hid = bias + part[:, :HID2] + pltpu.roll(part[:, HID2:], r - 1, 0)
    out = jnp.dot(jax.nn.gelu(hid).astype(jnp.bfloat16), w2p, preferred_element_type=f32)
    rows = lax.broadcasted_iota(jnp.int32, (r, 1), 0)
    return jnp.where(rows < r - 1, out, 0.0)


def _compress_prompt_kernel(xk_ref, xv_ref, w1k_ref, w2k_ref, pk_ref, fk_ref, w1v_ref, w2v_ref, pv_ref, fv_ref,
                            ok_ref, ov_ref):
    bf16 = jnp.bfloat16
    ok_ref[0] = _compress_rows(xk_ref[0].astype(bf16), w1k_ref[...], w2k_ref[...], pk_ref[...],
                               fk_ref[...]).astype(ok_ref.dtype)
    ov_ref[0] = _compress_rows(xv_ref[0].astype(bf16), w1v_ref[...], w2v_ref[...], pv_ref[...],
                               fv_ref[...]).astype(ov_ref.dtype)


def _compress_prompt(kc, vc, wk, wv):
    b, s, _ = kc.shape
    r = s // CMP_STRIDE
    xspec = pl.BlockSpec((1, r, CHUNK_W), lambda i: (i, 0, 0))
    wspecs = [pl.BlockSpec(w.shape, lambda i: (0, 0)) for w in wk]
    ospec = pl.BlockSpec((1, r, D_KV), lambda i: (i, 0, 0))
    oshape = jax.ShapeDtypeStruct((b, r, D_KV), jnp.bfloat16)
    return pl.pallas_call(
        _compress_prompt_kernel,
        grid=(b,),
        in_specs=[xspec, xspec] + wspecs + wspecs,
        out_specs=(ospec, ospec),
        out_shape=(oshape, oshape),
        compiler_params=pltpu.CompilerParams(dimension_semantics=("arbitrary",), vmem_limit_bytes=VMEM_LIMIT),
        name="compress_prompt",
    )(kc.reshape(b, r, CHUNK_W), vc.reshape(b, r, CHUNK_W), *wk, *wv)


N_PAGES = PAST_LEN // PAGE_SIZE
PAST_CHUNKS = PAST_LEN // CMP_STRIDE
PAST_BLOCKS = PAST_LEN // SLC_BLOCK
BLOCKS_PER_PAGE = PAGE_SIZE // SLC_BLOCK
CHUNKS_PER_PAGE = PAGE_SIZE // CMP_STRIDE
SEL_ROWS = 256
QROWS = SUBLANES
REGROUP_UNROLL = 16


def _sample_cmp_kernel(pt_ref, q8_ref, kpool, vpool, w1k_ref, w2k_ref, pk_ref, fk_ref, w1v_ref, w2v_ref, pv_ref,
                       fv_ref, ocmp_ref, psum_ref, kbuf, vbuf, x_scr, t_scr, sem):
    f32, bf16 = jnp.float32, jnp.bfloat16
    n = pl.program_id(0)

    def fetch(s, slot):
        def body(j, _):
            pg = pt_ref[s, j]
            pltpu.make_async_copy(kpool.at[pg], kbuf.at[slot, j], sem.at[0, slot]).start()
            pltpu.make_async_copy(vpool.at[pg], vbuf.at[slot, j], sem.at[1, slot]).start()
            return 0
        lax.fori_loop(0, N_PAGES, body, 0)

    @pl.when(n == 0)
    def _():
        fetch(0, 0)

    @pl.when(n + 1 < pl.num_programs(0))
    def _():
        fetch(n + 1, (n + 1) % 2)

    slot = n % 2
    pltpu.make_async_copy(kbuf.at[slot], kbuf.at[slot], sem.at[0, slot]).wait()
    pltpu.make_async_copy(vbuf.at[slot], vbuf.at[slot], sem.at[1, slot]).wait()

    def regroup_body(jj, _):
        for u in range(REGROUP_UNROLL):
            j = jj * (REGROUP_UNROLL // 2) + u // 2
            t_scr[u] = (kbuf, vbuf)[u % 2][slot, j].T
            r0 = pl.multiple_of(j * CHUNKS_PER_PAGE, CHUNKS_PER_PAGE)
            for l in range(CMP_STRIDE):
                x_scr[u % 2, pl.ds(r0, CHUNKS_PER_PAGE), l * D_KV:(l + 1) * D_KV] = (
                    t_scr[u, pl.ds(l, CHUNKS_PER_PAGE, stride=CMP_STRIDE), :])
        return 0

    lax.fori_loop(0, N_PAGES // (REGROUP_UNROLL // 2), regroup_body, 0)
    ck = _compress_rows(x_scr[0].astype(bf16), w1k_ref[...], w2k_ref[...], pk_ref[...], fk_ref[...]).astype(bf16)
    cv = _compress_rows(x_scr[1].astype(bf16), w1v_ref[...], w2v_ref[...], pv_ref[...], fv_ref[...]).astype(bf16)
    c_end = lax.broadcasted_iota(jnp.int32, (1, PAST_CHUNKS), 1) * CMP_STRIDE + (CMP_LEN - 1)
    cmask = c_end <= PAST_LEN
    rows = lax.broadcasted_iota(jnp.int32, (QROWS, 1), 0)
    ps = jnp.zeros((QROWS, PAST_CHUNKS), f32)
    for h in range(N_KV):
        p = _softmax_rows(_dot_nt(q8_ref[0, h], ck), cmask)
        ocmp_ref[0, h] = jnp.dot(p.astype(bf16), cv, preferred_element_type=f32)
        ps = jnp.where(rows == h, p[0:1] + p[1:2] + p[2:3] + p[3:4], ps)
    psum_ref[0] = ps


def _sample_cmp(page_table, q8, k_pool, v_pool, wk, wv):
    n = q8.shape[0]
    f32 = jnp.float32
    wspecs = [pl.BlockSpec(w.shape, lambda i, pt: (0, 0)) for w in wk]
    grid_spec = pltpu.PrefetchScalarGridSpec(
        num_scalar_prefetch=1,
        grid=(n,),
        in_specs=[pl.BlockSpec((1, N_KV, QROWS, LANES), lambda i, pt: (i, 0, 0, 0)),
                  pl.BlockSpec(memory_space=pl.ANY), pl.BlockSpec(memory_space=pl.ANY)] + wspecs + wspecs,
        out_specs=(pl.BlockSpec((1, N_KV, QROWS, LANES), lambda i, pt: (i, 0, 0, 0)),
                   pl.BlockSpec((1, QROWS, PAST_CHUNKS), lambda i, pt: (i, 0, 0))),
        scratch_shapes=[pltpu.VMEM((2, N_PAGES, D_KV, PAGE_SIZE), f32),
                        pltpu.VMEM((2, N_PAGES, D_KV, PAGE_SIZE), f32),
                        pltpu.VMEM((2, PAST_CHUNKS, CHUNK_W), f32),
                        pltpu.VMEM((REGROUP_UNROLL, PAGE_SIZE, D_KV), f32),
                        pltpu.SemaphoreType.DMA((2, 2))],
    )
    return pl.pallas_call(
        _sample_cmp_kernel,
        grid_spec=grid_spec,
        out_shape=(jax.ShapeDtypeStruct((n, N_KV, QROWS, LANES), f32),
                   jax.ShapeDtypeStruct((n, QROWS, PAST_CHUNKS), f32)),
        compiler_params=pltpu.CompilerParams(dimension_semantics=("arbitrary",), vmem_limit_bytes=VMEM_LIMIT),
        name="sample_compress_attention",
    )(page_table, q8, k_pool, v_pool, *wk, *wv)


def _sample_select_kernel(p_ref, mmt_ref, tri_ref, idx_ref, sv_scr):
    f32 = jnp.float32
    cols = p_ref.shape[0]
    cur = PAST_LEN // SLC_BLOCK
    mmt = mmt_ref[...]
    sc = sum(_dot_nt(mmt, part) for part in _split3_bf16(p_ref[...]))
    jio = lax.broadcasted_iota(jnp.int32, (SEL_ROWS, cols), 0)
    forced = (jio == 0) | (jio == cur) | (jio == cur - 1)
    sv = jnp.where(forced, BIG, jnp.where(jio <= cur, sc, NEG))
    sv_scr[...] = sv

    def rank_body(i, cnt):
        r = sv_scr[pl.ds(i, 1), :]
        beats = (r > sv) | ((r == sv) & (jio > i))
        return cnt + jnp.where(beats, 1.0, 0.0)

    cnt = lax.fori_loop(0, cur + 1, rank_body, jnp.zeros((SEL_ROWS, cols), f32))
    sel = (cnt < N_SEL) & (sv > NEG / 2)
    self = jnp.where(sel, 1.0, 0.0)
    before = jnp.dot(tri_ref[...], self.astype(jnp.bfloat16), preferred_element_type=f32)
    rows = []
    for k in range(N_SEL):
        rows.append(jnp.sum(jnp.where(sel & (before == k), jio, 0), axis=0, keepdims=True))
    idx_ref[...] = jnp.concatenate(rows, axis=0)


def _sample_select(p_sum):
    n = p_sum.shape[0]
    cols = n * N_KV
    p2 = p_sum[:, :N_KV, :].reshape(cols, PAST_CHUNKS)
    ratio = SLC_BLOCK // CMP_STRIDE
    ovl = CMP_LEN // CMP_STRIDE - 1
    j = np.arange(SEL_ROWS)[:, None]
    c = np.arange(PAST_CHUNKS)[None, :]
    mmt = (c >= ratio * j - ovl) & (c <= ratio * j + ratio - 1) & (j <= PAST_BLOCKS) & (c < PAST_CHUNKS - 1)
    tri = np.tril(np.ones((SEL_ROWS, SEL_ROWS), np.float32), -1)
    idx = pl.pallas_call(
        _sample_select_kernel,
        out_shape=jax.ShapeDtypeStruct((N_SEL, cols), jnp.int32),
        scratch_shapes=[pltpu.VMEM((SEL_ROWS, cols), jnp.float32)],
        compiler_params=pltpu.CompilerParams(vmem_limit_bytes=VMEM_LIMIT),
        name="sample_select",
    )(p2, jnp.asarray(mmt, dtype=jnp.bfloat16), jnp.asarray(tri, dtype=jnp.bfloat16))
    return idx.T.reshape(n, N_KV * N_SEL)


SLC_COLS = N_SEL * PAGE_SIZE


def _sample_attn_kernel(pt_ref, idx_ref, q_ref, kpool, vpool, knew_ref, vnew_ref, kwin_ref, vwin_ref, kwn_ref,
                        vwn_ref, ocmp_ref, ng_ref, o_ref, kwo_ref, vwo_ref, kg, vg, sem):
    f32, bf16 = jnp.float32, jnp.bfloat16
    n = pl.program_id(0)
    nblk = N_KV * N_SEL

    def fetch(s, slot):
        def body(i, _):
            jp = jnp.minimum(idx_ref[s, i], PAST_BLOCKS - 1)
            pg = pt_ref[s, jp // BLOCKS_PER_PAGE]
            h = i // N_SEL
            pltpu.make_async_copy(kpool.at[pg, h], kg.at[slot, i], sem.at[0, slot]).start()
            pltpu.make_async_copy(vpool.at[pg, h], vg.at[slot, i], sem.at[1, slot]).start()
            return 0
        lax.fori_loop(0, nblk, body, 0)

    @pl.when(n == 0)
    def _():
        fetch(0, 0)

    @pl.when(n + 1 < pl.num_programs(0))
    def _():
        fetch(n + 1, (n + 1) % 2)

    slot = n % 2
    pltpu.make_async_copy(kg.at[slot], kg.at[slot], sem.at[0, slot]).wait()
    pltpu.make_async_copy(vg.at[slot], vg.at[slot], sem.at[1, slot]).wait()

    pos = PAST_LEN
    col = lax.broadcasted_iota(jnp.int32, (1, SLC_COLS + LANES), 1)
    wcol = lax.broadcasted_iota(jnp.int32, (1, WINDOW + LANES), 1)
    wpos = jnp.where(wcol < WINDOW, pos - WINDOW + wcol, pos + (wcol - WINDOW))
    wmask = (pos - wpos >= 0) & (pos - wpos < WINDOW)
    wlane = lax.broadcasted_iota(jnp.int32, (1, WINDOW), 1)
    for h in range(N_KV):
        q = q_ref[0, h]
        new_sel = jnp.int32(0)
        lo = jnp.full((1, SLC_COLS + LANES), SLC_COLS + LANES, jnp.int32)
        for p in range(N_SEL):
            j = idx_ref[n, h * N_SEL + p]
            start = p * PAGE_SIZE + (j % BLOCKS_PER_PAGE) * SLC_BLOCK
            start = jnp.where(j < PAST_BLOCKS, start, SLC_COLS + LANES)
            lo = jnp.where((col >> 7) == p, start, lo)
            new_sel = jnp.maximum(new_sel, (j >= PAST_BLOCKS).astype(jnp.int32))
        valid = jnp.where((col >= lo) & (col < lo + SLC_BLOCK), 1, 0)
        kpos_new = PAST_LEN + (col - SLC_COLS)
        valid = jnp.where(col >= SLC_COLS, jnp.where(kpos_new <= pos, new_sel, 0), valid) > 0
        kt = jnp.concatenate([kg[slot, h * N_SEL + p] for p in range(N_SEL)] + [knew_ref[0, h]], axis=-1)
        vt = jnp.concatenate([vg[slot, h * N_SEL + p] for p in range(N_SEL)] + [vnew_ref[0, h]], axis=-1)
        ps = _softmax_rows(jnp.dot(q, kt.astype(bf16), preferred_element_type=f32), valid)
        o_slc = _dot_nt(ps.astype(bf16), vt.astype(bf16))
        kwin, vwin = kwin_ref[0, h], vwin_ref[0, h]
        kw_all = jnp.concatenate([kwin, kwn_ref[0, h]], axis=-1).astype(bf16)
        vw_all = jnp.concatenate([vwin, vwn_ref[0, h]], axis=-1).astype(bf16)
        pw = _softmax_rows(jnp.dot(q, kw_all, preferred_element_type=f32), wmask)
        o_win = _dot_nt(pw.astype(bf16), vw_all)
        sig = jax.nn.sigmoid(ng_ref[0, h])
        o_cmp = ocmp_ref[0, h][:, h * HEAD_DIM:(h + 1) * HEAD_DIM]
        o_ref[0, h] = sig[0] * o_cmp + sig[1] * o_slc + sig[2] * o_win
        kwo_ref[0, h] = jnp.where(wlane == WINDOW - 1, kwn_ref[0, h][:, 0:1], pltpu.roll(kwin, WINDOW - 1, 1))
        vwo_ref[0, h] = jnp.where(wlane == WINDOW - 1, vwn_ref[0, h][:, 0:1], pltpu.roll(vwin, WINDOW - 1, 1))


def _sample_attn(page_table, idx, q8u, k_pool_t, v_pool_t, knew_t, vnew_t, k_win_t, v_win_t, kwn_t, vwn_t, o_cmp, ngb):
    n = q8u.shape[0]
    f32 = jnp.float32
    assert k_win_t.shape[-1] == WINDOW
    per_n4 = lambda i, pt, ix: (i, 0, 0, 0)
    ext_spec = pl.BlockSpec((1, N_KV, HEAD_DIM, LANES), per_n4)
    win_spec = pl.BlockSpec((1, N_KV, HEAD_DIM, WINDOW), per_n4)
    q_spec = pl.BlockSpec((1, N_KV, QROWS, HEAD_DIM), per_n4)
    grid_spec = pltpu.PrefetchScalarGridSpec(
        num_scalar_prefetch=2,
        grid=(n,),
        in_specs=[q_spec, pl.BlockSpec(memory_space=pl.ANY), pl.BlockSpec(memory_space=pl.ANY),
                  ext_spec, ext_spec, win_spec, win_spec, ext_spec, ext_spec,
                  pl.BlockSpec((1, N_KV, QROWS, LANES), per_n4),
                  pl.BlockSpec((1, N_KV, 3, QROWS, HEAD_DIM), lambda i, pt, ix: (i, 0, 0, 0, 0))],
        out_specs=(q_spec, win_spec, win_spec),
        scratch_shapes=[pltpu.VMEM((2, N_KV * N_SEL, HEAD_DIM, PAGE_SIZE), f32),
                        pltpu.VMEM((2, N_KV * N_SEL, HEAD_DIM, PAGE_SIZE), f32),
                        pltpu.SemaphoreType.DMA((2, 2))],
    )
    return pl.pallas_call(
        _sample_attn_kernel,
        grid_spec=grid_spec,
        out_shape=(jax.ShapeDtypeStruct((n, N_KV, QROWS, HEAD_DIM), f32),
                   jax.ShapeDtypeStruct((n, N_KV, HEAD_DIM, WINDOW), f32),
                   jax.ShapeDtypeStruct((n, N_KV, HEAD_DIM, WINDOW), f32)),
        compiler_params=pltpu.CompilerParams(dimension_semantics=("arbitrary",), vmem_limit_bytes=VMEM_LIMIT),
        name="sample_attention",
    )(page_table, idx, q8u, k_pool_t, v_pool_t, knew_t, vnew_t, k_win_t, v_win_t, kwn_t, vwn_t, o_cmp, ngb)


def _nsa_sample_paged(q_pad, ks, vs, kw, vw, ng, page_table, ck_cmp, cv_cmp, ck_slc, cv_slc, ck_win, cv_win,
                      cmp_wk, cmp_wv):
    n = q_pad.shape[0]
    n_phys = ck_cmp.shape[0]
    pool_t = lambda p: p.transpose(0, 2, 3, 1)
    q4 = q_pad.reshape(n, N_KV, GROUP, N_KV, HEAD_DIM)
    q8 = jnp.pad(q_pad.reshape(n, N_KV, GROUP, LANES), ((0, 0), (0, 0), (0, QROWS - GROUP), (0, 0)))
    q8u = jnp.pad(jnp.stack([q4[:, h, :, h, :] for h in range(N_KV)], axis=1),
                  ((0, 0), (0, 0), (0, QROWS - GROUP), (0, 0)))
    o_cmp, p_sum = _sample_cmp(page_table, q8, pool_t(ck_cmp).reshape(n_phys, D_KV, PAGE_SIZE),
                               pool_t(cv_cmp).reshape(n_phys, D_KV, PAGE_SIZE), cmp_wk, cmp_wv)
    idx = _sample_select(p_sum)
    ngb = ng[:, :N_NG].reshape(n, N_KV, GROUP, 3).transpose(0, 1, 3, 2)
    ngb = jnp.pad(ngb, ((0, 0), (0, 0), (0, 0), (0, QROWS - GROUP)))
    ngb = jnp.broadcast_to(ngb[..., None], (n, N_KV, 3, QROWS, HEAD_DIM))
    col0 = lambda t: jnp.pad(t.reshape(n, N_KV, HEAD_DIM, 1), ((0, 0), (0, 0), (0, 0), (0, LANES - 1)))
    o8, kwo, vwo = _sample_attn(page_table, idx, q8u, pool_t(ck_slc), pool_t(cv_slc), col0(ks), col0(vs),
                                pool_t(ck_win), pool_t(cv_win), col0(kw), col0(vw), o_cmp, ngb)
    o = o8[:, :, :GROUP, :].transpose(0, 2, 1, 3).reshape(n, D_Q).astype(jnp.bfloat16)
    return o, kwo.transpose(0, 3, 1, 2), vwo.transpose(0, 3, 1, 2)


LRU_PAIRS = D_RNN // LANES


def _lru_gates(xc, wa_ref, ba, wx_ref, bx, lam):
    f32 = jnp.float32
    xb = xc.astype(jnp.bfloat16)
    rg = jnp.concatenate([jnp.dot(xb[:, j * LANES:(j + 1) * LANES], wa_ref[j], preferred_element_type=f32)
                          for j in range(LRU_PAIRS)], axis=-1)
    ig = jnp.concatenate([jnp.dot(xb[:, j * LANES:(j + 1) * LANES], wx_ref[j], preferred_element_type=f32)
                          for j in range(LRU_PAIRS)], axis=-1)
    r = jax.nn.sigmoid(rg + ba)
    i = jax.nn.sigmoid(ig + bx)
    log_a = -LRU_C * r * jax.nn.softplus(-lam)
    a = jnp.exp(log_a)
    u = jnp.sqrt(-jnp.tanh(log_a) * (a * a + 1.0)) * (i * xc)
    return a, u


def _lru_prompt_kernel(arag_ref, cbuf_ref, h0_ref, cw_ref, cb_ref, wa_ref, ba_ref, wx_ref, bx_ref, lam_ref,
                       y_ref, hl_ref, prev_scr, h_scr, hbuf):
    f32 = jnp.float32
    t = arag_ref.shape[0]

    @pl.when(pl.program_id(1) == 0)
    def _():
        prev_scr[...] = cbuf_ref[0]
        h_scr[...] = h0_ref[0]

    ar = arag_ref[:, :D_RNN]
    row8 = lax.broadcasted_iota(jnp.int32, (SUBLANES, 1), 0)
    prev = prev_scr[...]
    xc = cb_ref[...] + cw_ref[CONV_W - 1:CONV_W, :] * ar
    for k in range(1, CONV_W):
        rolled = pltpu.roll(ar, k, 0)
        head = jnp.where(row8 < k, pltpu.roll(prev, k, 0), rolled[0:SUBLANES])
        shifted = jnp.concatenate([head, rolled[SUBLANES:]], axis=0)
        xc = xc + cw_ref[CONV_W - 1 - k:CONV_W - k, :] * shifted
    prev_scr[...] = ar[t - SUBLANES:]

    a, u = _lru_gates(xc, wa_ref, ba_ref[...], wx_ref, bx_ref[...], lam_ref[...])

    rowm = lax.broadcasted_iota(jnp.int32, (t, 1), 0) & (SUBLANES - 1)
    for s in (1, 2, 4):
        ok = rowm >= s
        u = jnp.where(ok, a * pltpu.roll(u, s, 0) + u, u)
        a = jnp.where(ok, a * pltpu.roll(a, s, 0), a)
    h = h_scr[...]
    for g in range(t // SUBLANES):
        rs = slice(g * SUBLANES, (g + 1) * SUBLANES)
        hg = a[rs] * h + u[rs]
        hbuf[rs, :] = hg
        h = jnp.broadcast_to(hg[SUBLANES - 1:SUBLANES, :], (SUBLANES, D_RNN))
    h_scr[...] = h
    hl_ref[0] = h
    y_ref[...] = (hbuf[...] * jax.nn.gelu(arag_ref[:, D_RNN:])).astype(y_ref.dtype)


def _lru_pair_weights(w):
    w4 = w.reshape(LRU_PAIRS, 2, LRU_BW, LRU_BW)
    eye = jnp.eye(2, dtype=w.dtype)
    return jnp.einsum('jaio,ab->jaibo', w4, eye).reshape(LRU_PAIRS, LANES, LANES).astype(jnp.bfloat16)


def _lru_weights(conv_w, conv_b, wa, ba, wx, bx, lam):
    row = lambda v: v.reshape(1, D_RNN).astype(jnp.float32)
    return (conv_w.astype(jnp.float32), row(conv_b), _lru_pair_weights(wa), row(ba), _lru_pair_weights(wx), row(bx),
            row(lam))


def _lru_prompt(arag, conv_buf, h0, lw, b, s, t):
    assert s % t == 0 and t % SUBLANES == 0
    nt = s // t
    cbuf8 = jnp.pad(conv_buf.astype(jnp.float32), ((0, 0), (SUBLANES - (CONV_W - 1), 0), (0, 0)))
    h08 = jnp.broadcast_to(h0.astype(jnp.float32)[:, None, :], (b, SUBLANES, D_RNN))
    const2 = lambda bi, i: (0, 0)
    const3 = lambda bi, i: (0, 0, 0)
    per_b = lambda bi, i: (bi, 0, 0)
    y, hl = pl.pallas_call(
        _lru_prompt_kernel,
        grid=(b, nt),
        in_specs=[
            pl.BlockSpec((t, 2 * D_RNN), lambda bi, i: (bi * nt + i, 0)),
            pl.BlockSpec((1, SUBLANES, D_RNN), per_b),
            pl.BlockSpec((1, SUBLANES, D_RNN), per_b),
            pl.BlockSpec((CONV_W, D_RNN), const2),
            pl.BlockSpec((1, D_RNN), const2),
            pl.BlockSpec((LRU_PAIRS, LANES, LANES), const3),
            pl.BlockSpec((1, D_RNN), const2),
            pl.BlockSpec((LRU_PAIRS, LANES, LANES), const3),
            pl.BlockSpec((1, D_RNN), const2),
            pl.BlockSpec((1, D_RNN), const2),
        ],
        out_specs=(pl.BlockSpec((t, D_RNN), lambda bi, i: (bi * nt + i, 0)),
                   pl.BlockSpec((1, SUBLANES, D_RNN), per_b)),
        out_shape=(jax.ShapeDtypeStruct((b * s, D_RNN), jnp.bfloat16),
                   jax.ShapeDtypeStruct((b, SUBLANES, D_RNN), jnp.float32)),
        scratch_shapes=[pltpu.VMEM((SUBLANES, D_RNN), jnp.float32), pltpu.VMEM((SUBLANES, D_RNN), jnp.float32),
                        pltpu.VMEM((t, D_RNN), jnp.float32)],
        compiler_params=pltpu.CompilerParams(dimension_semantics=("arbitrary", "arbitrary"),
                                             vmem_limit_bytes=VMEM_LIMIT),
        name="lru_prompt",
    )(arag, cbuf8, h08, *lw)
    return y, hl[:, 0, :]


def _lru_step_kernel(arag_ref, c0_ref, c1_ref, c2_ref, h0_ref, cw_ref, cb_ref, wa_ref, ba_ref, wx_ref, bx_ref,
                     lam_ref, y_ref, h_ref):
    ar = arag_ref[:, :D_RNN]
    xc = (cb_ref[...] + cw_ref[0:1, :] * c0_ref[...] + cw_ref[1:2, :] * c1_ref[...] + cw_ref[2:3, :] * c2_ref[...]
          + cw_ref[3:4, :] * ar)
    a, u = _lru_gates(xc, wa_ref, ba_ref[...], wx_ref, bx_ref[...], lam_ref[...])
    h = a * h0_ref[...] + u
    h_ref[...] = h
    y_ref[...] = (h * jax.nn.gelu(arag_ref[:, D_RNN:])).astype(y_ref.dtype)


def _lru_step(arag, conv_buf, h0, lw):
    n = arag.shape[0]
    f32 = jnp.float32
    cb = conv_buf.astype(f32)
    return pl.pallas_call(
        _lru_step_kernel,
        out_shape=(jax.ShapeDtypeStruct((n, D_RNN), jnp.bfloat16), jax.ShapeDtypeStruct((n, D_RNN), f32)),
        compiler_params=pltpu.CompilerParams(vmem_limit_bytes=VMEM_LIMIT),
        name="lru_step",
    )(arag, cb[:, 0], cb[:, 1], cb[:, 2], h0.astype(f32), *lw)


def _merge_router_kernel(x_ref, ylru_ref, onsa_ref, gab_ref, wa_ref, wb_ref, wo_ref, gffn_ref, rw_ref, rb_ref,
                         tri_ref, xn_ref, hn_ref, te_ref, tr_ref, tg_ref, cnt_ref, run_scr):
    f32, bf16 = jnp.float32, jnp.bfloat16
    tm = x_ref.shape[0]

    @pl.when(pl.program_id(0) == 0)
    def _():
        run_scr[...] = jnp.zeros_like(run_scr)

    ya = jnp.dot(ylru_ref[...], wa_ref[...], preferred_element_type=f32)
    yb = jnp.dot(onsa_ref[...], wb_ref[...], preferred_element_type=f32)
    mix = jax.nn.sigmoid(gab_ref[:, :D_MODEL]) * ya + jax.nn.sigmoid(gab_ref[:, D_MODEL:]) * yb
    xn = x_ref[...] + jnp.dot(mix.astype(bf16), wo_ref[...], preferred_element_type=f32)
    xn_ref[...] = xn
    hn = xn * lax.rsqrt(jnp.mean(xn * xn, axis=-1, keepdims=True) + EPS) * gffn_ref[...]
    hn_ref[...] = hn

    hn_hi = hn.astype(bf16)
    hn_lo = (hn - hn_hi.astype(f32)).astype(bf16)
    logits = (jnp.dot(hn_hi, rw_ref[0], preferred_element_type=f32)
              + jnp.dot(hn_lo, rw_ref[0], preferred_element_type=f32)
              + jnp.dot(hn_hi, rw_ref[1], preferred_element_type=f32)) + rb_ref[...]
    lane = lax.broadcasted_iota(jnp.int32, (tm, LANES), 1)
    ninf = -jnp.inf
    l = jnp.where(lane < N_EXPERTS, logits, ninf)
    onehots, m0 = [], None
    for k in range(TOP_K):
        m = jnp.max(l, axis=-1, keepdims=True)
        idx = jnp.min(jnp.where(l == m, lane, LANES), axis=-1, keepdims=True)
        oh = lane == idx
        onehots.append((oh, idx))
        m0 = m if k == 0 else m0
        l = jnp.where(oh, ninf, l)
    sel = onehots[0][0] | onehots[1][0] | onehots[2][0] | onehots[3][0]
    e = jnp.where(sel, jnp.exp(logits - m0), 0.0)
    gates = e / jnp.sum(e, axis=-1, keepdims=True)

    self = jnp.where(sel, 1.0, 0.0)
    incl = jnp.dot(tri_ref[...], self.astype(bf16), preferred_element_type=f32)
    run = run_scr[0:1, :]
    rank = run + incl - self
    total = run + incl[tm - 1:tm, :]
    run_scr[...] = jnp.broadcast_to(total, run_scr.shape)
    cnt_ref[...] = jnp.broadcast_to(total, cnt_ref.shape)

    te = jnp.zeros((tm, LANES), jnp.int32)
    tr = jnp.zeros((tm, LANES), f32)
    tg = jnp.zeros((tm, LANES), f32)
    for k, (oh, idx) in enumerate(onehots):
        te = jnp.where(lane == k, idx, te)
        tr = jnp.where(lane == k, jnp.sum(jnp.where(oh, rank, 0.0), axis=-1, keepdims=True), tr)
        tg = jnp.where(lane == k, jnp.sum(jnp.where(oh, gates, 0.0), axis=-1, keepdims=True), tg)
    te_ref[...] = te
    tr_ref[...] = tr.astype(jnp.int32)
    tg_ref[...] = tg


def _merge_router(x2d, ylru, onsa, gab, wa, wb, wo, g_ffn, rw_pad, rb_pad, tm):
    m = x2d.shape[0]
    assert m % tm == 0
    f32 = jnp.float32
    row = lambda i: (i, 0)
    const = lambda i: (0, 0)
    tri = jnp.asarray(np.tril(np.ones((tm, tm), np.float32)), dtype=jnp.bfloat16)
    lane_out = lambda dt: (jax.ShapeDtypeStruct((m, LANES), dt), pl.BlockSpec((tm, LANES), row))
    outs = [
        (jax.ShapeDtypeStruct((m, D_MODEL), f32), pl.BlockSpec((tm, D_MODEL), row)),
        (jax.ShapeDtypeStruct((m, D_MODEL), f32), pl.BlockSpec((tm, D_MODEL), row)),
        lane_out(jnp.int32), lane_out(jnp.int32), lane_out(f32),
        (jax.ShapeDtypeStruct((8, LANES), f32), pl.BlockSpec((8, LANES), const)),
    ]
    return pl.pallas_call(
        _merge_router_kernel,
        grid=(m // tm,),
        in_specs=[
            pl.BlockSpec((tm, D_MODEL), row),
            pl.BlockSpec((tm, D_RNN), row),
            pl.BlockSpec((tm, D_Q), row),
            pl.BlockSpec((tm, 2 * D_MODEL), row),
            pl.BlockSpec((D_RNN, D_MODEL), const),
            pl.BlockSpec((D_Q, D_MODEL), const),
            pl.BlockSpec((D_MODEL, D_MODEL), const),
            pl.BlockSpec((1, D_MODEL), const),
            pl.BlockSpec((2, D_MODEL, LANES), lambda i: (0, 0, 0)),
            pl.BlockSpec((1, LANES), const),
            pl.BlockSpec((tm, tm), const),
        ],
        out_specs=tuple(o[1] for o in outs),
        out_shape=tuple(o[0] for o in outs),
        scratch_shapes=[pltpu.VMEM((8, LANES), f32)],
        compiler_params=pltpu.CompilerParams(dimension_semantics=("arbitrary",), vmem_limit_bytes=VMEM_LIMIT),
        name="merge_router",
    )(x2d, ylru, onsa, gab, wa, wb, wo, g_ffn.reshape(1, D_MODEL), rw_pad, rb_pad, tri)


MOE_BLK = 512


def _moe_dispatch_kernel(last_ref, dest_ref, hn_ref, xs_ref, zero_scr, sem, zsem):
    tm = hn_ref.shape[0]

    @pl.when(pl.program_id(0) == 0)
    def _():
        zero_scr[...] = jnp.zeros_like(zero_scr)
        for e in range(N_EXPERTS):
            @pl.when(last_ref[e] >= 0)
            def _():
                start = pl.multiple_of(last_ref[e], MOE_BLK)
                cp = pltpu.make_async_copy(zero_scr, xs_ref.at[pl.ds(start, MOE_BLK)], zsem)
                cp.start()
                cp.wait()

    def issue(r, _):
        for k in range(TOP_K):
            d = dest_ref[0, 0, r * TOP_K + k]
            pltpu.make_async_copy(hn_ref.at[pl.ds(r, 1)], xs_ref.at[pl.ds(d, 1)], sem).start(priority=k % 2)
        return 0

    lax.fori_loop(0, tm, issue, 0)
    pltpu.make_async_copy(xs_ref.at[pl.ds(0, tm * TOP_K)], xs_ref.at[pl.ds(0, tm * TOP_K)], sem).wait()


def _moe_dispatch(hn, dest, last_tile, n_slots, tm):
    m = hn.shape[0]
    nt = m // tm
    grid_spec = pltpu.PrefetchScalarGridSpec(
        num_scalar_prefetch=1,
        grid=(nt,),
        in_specs=[
            pl.BlockSpec((1, 1, tm * TOP_K), lambda i, lt: (i, 0, 0), memory_space=pltpu.SMEM),
            pl.BlockSpec((tm, D_MODEL), lambda i, lt: (i, 0)),
        ],
        out_specs=pl.BlockSpec(memory_space=pl.ANY),
        scratch_shapes=[pltpu.VMEM((MOE_BLK, D_MODEL), hn.dtype), pltpu.SemaphoreType.DMA(()),
                        pltpu.SemaphoreType.DMA(())],
    )
    return pl.pallas_call(
        _moe_dispatch_kernel,
        grid_spec=grid_spec,
        out_shape=jax.ShapeDtypeStruct((n_slots, D_MODEL), hn.dtype),
        compiler_params=pltpu.CompilerParams(dimension_semantics=("arbitrary",), vmem_limit_bytes=VMEM_LIMIT),
        name="moe_dispatch",
    )(last_tile, dest.reshape(nt, 1, tm * TOP_K), hn)


def _moe_experts_kernel(te_ref, nused_ref, xs_ref, wup_ref, bup_ref, wdn_ref, bdn_ref, ys_ref, wup_scr, wdn_scr):
    f32, bf16 = jnp.float32, jnp.bfloat16
    j = pl.program_id(0)
    changed = jnp.logical_or(j == 0, te_ref[j] != te_ref[jnp.maximum(j - 1, 0)])

    @pl.when(jnp.logical_and(j < nused_ref[0], changed))
    def _():
        wup_scr[...] = wup_ref[0].astype(bf16)
        wdn_scr[...] = wdn_ref[0].astype(bf16)

    @pl.when(j < nused_ref[0])
    def _():
        x = xs_ref[...].astype(bf16)
        gu = jnp.dot(x, wup_scr[...], preferred_element_type=f32) + bup_ref[0]
        a = jnp.minimum(gu[:, :D_FF], SWIGLU_LIMIT)
        u = jnp.clip(gu[:, D_FF:], -SWIGLU_LIMIT, SWIGLU_LIMIT)
        act = (u + 1.0) * a * jax.nn.sigmoid(SWIGLU_ALPHA * a)
        ys_ref[...] = jnp.dot(act.astype(bf16), wdn_scr[...], preferred_element_type=f32) + bdn_ref[0]


def _moe_experts(xs, tile_expert, n_used, w_up, b_up, w_down, b_down):
    n_slots = xs.shape[0]
    nt = n_slots // MOE_BLK
    tile = lambda j, te, nu: (jnp.minimum(j, nu[0] - 1), 0)
    wsel = lambda j, te, nu: (te[jnp.minimum(j, nu[0] - 1)], 0, 0)
    grid_spec = pltpu.PrefetchScalarGridSpec(
        num_scalar_prefetch=2,
        grid=(nt,),
        in_specs=[
            pl.BlockSpec((MOE_BLK, D_MODEL), tile),
            pl.BlockSpec((1, D_MODEL, 2 * D_FF), wsel),
            pl.BlockSpec((1, 1, 2 * D_FF), wsel),
            pl.BlockSpec((1, D_FF, D_MODEL), wsel),
            pl.BlockSpec((1, 1, D_MODEL), wsel),
        ],
        out_specs=pl.BlockSpec((MOE_BLK, D_MODEL), tile),
        scratch_shapes=[pltpu.VMEM((D_MODEL, 2 * D_FF), jnp.bfloat16), pltpu.VMEM((D_FF, D_MODEL), jnp.bfloat16)],
    )
    return pl.pallas_call(
        _moe_experts_kernel,
        grid_spec=grid_spec,
        out_shape=jax.ShapeDtypeStruct((n_slots, D_MODEL), jnp.float32),
        compiler_params=pltpu.CompilerParams(dimension_semantics=("arbitrary",), vmem_limit_bytes=VMEM_LIMIT),
        name="moe_experts",
    )(tile_expert, n_used, xs, w_up, b_up.reshape(N_EXPERTS, 1, 2 * D_FF), w_down,
      b_down.reshape(N_EXPERTS, 1, D_MODEL))


def _moe_combine_kernel(dest_ref, xn_ref, tg_ref, ys_ref, y_ref, buf, sem):
    tm = xn_ref.shape[0]

    def issue(r, _):
        for k in range(TOP_K):
            d = dest_ref[0, 0, r * TOP_K + k]
            pltpu.make_async_copy(ys_ref.at[pl.ds(d, 1)], buf.at[k, pl.ds(r, 1)], sem).start(priority=k % 2)
        return 0

    lax.fori_loop(0, tm, issue, 0)
    pltpu.make_async_copy(ys_ref.at[pl.ds(0, tm * TOP_K)], ys_ref.at[pl.ds(0, tm * TOP_K)], sem).wait()
    tg = tg_ref[...]
    acc = xn_ref[...]
    for k in range(TOP_K):
        acc = acc + tg[:, k:k + 1] * buf[k]
    y_ref[...] = acc


def _moe_combine(xn, tg, ys, dest, tm):
    m = xn.shape[0]
    nt = m // tm
    return pl.pallas_call(
        _moe_combine_kernel,
        grid=(nt,),
        in_specs=[
            pl.BlockSpec((1, 1, tm * TOP_K), lambda i: (i, 0, 0), memory_space=pltpu.SMEM),
            pl.BlockSpec((tm, D_MODEL), lambda i: (i, 0)),
            pl.BlockSpec((tm, LANES), lambda i: (i, 0)),
            pl.BlockSpec(memory_space=pl.ANY),
        ],
        out_specs=pl.BlockSpec((tm, D_MODEL), lambda i: (i, 0)),
        out_shape=jax.ShapeDtypeStruct((m, D_MODEL), jnp.float32),
        scratch_shapes=[pltpu.VMEM((TOP_K, tm, D_MODEL), jnp.float32), pltpu.SemaphoreType.DMA(())],
        compiler_params=pltpu.CompilerParams(dimension_semantics=("arbitrary",), vmem_limit_bytes=VMEM_LIMIT),
        name="moe_combine",
    )(dest.reshape(nt, 1, tm * TOP_K), xn, tg, ys)


def _moe_layer(xn, hn, te, tr, tg, cnt, w_up, b_up, w_down, b_down, tm):
    m = xn.shape[0]
    n_tiles = -(-m * TOP_K // MOE_BLK) + N_EXPERTS
    counts = cnt[0, :N_EXPERTS].astype(jnp.int32)
    padded = (counts + MOE_BLK - 1) // MOE_BLK * MOE_BLK
    pad_end = jnp.cumsum(padded)
    pad_start = pad_end - padded
    dest = (pad_start[te[:, :TOP_K]] + tr[:, :TOP_K]).astype(jnp.int32)
    tile_start = jnp.arange(n_tiles, dtype=jnp.int32) * MOE_BLK
    tile_expert = jnp.minimum(jnp.sum(pad_end[None, :] <= tile_start[:, None], axis=1), N_EXPERTS - 1).astype(jnp.int32)
    n_used = (pad_end[-1:] // MOE_BLK).astype(jnp.int32)
    last_tile = jnp.where(padded > 0, pad_end - MOE_BLK, -1).astype(jnp.int32)
    xs = _moe_dispatch(hn, dest, last_tile, n_tiles * MOE_BLK, tm)
    ys = _moe_experts(xs, tile_expert, n_used, w_up, b_up, w_down, b_down)
    return _moe_combine(xn, tg, ys, dest, tm)


def _pad_w_in(w):
    c0 = 2 * D_RNN
    wq = w[:, c0:c0 + D_Q].reshape(D_MODEL, N_KV, GROUP, 1, HEAD_DIM)
    sel = jnp.eye(N_KV, dtype=w.dtype).reshape(N_KV, 1, N_KV, 1)
    wq = (wq * sel).reshape(D_MODEL, D_QP)
    c1 = c0 + D_Q
    c2 = c1 + 6 * D_KV + N_NG
    pad = jnp.zeros((D_MODEL, LANES - N_NG), w.dtype)
    return jnp.concatenate([w[:, :c0], wq, w[:, c1:c2], pad, w[:, c2:]], axis=1).astype(jnp.bfloat16)


def kernel(x_prompt, x_sample, page_table, cache_k_cmp, cache_v_cmp, cache_k_slc, cache_v_slc,
           cache_k_win, cache_v_win, state_h, state_conv, g_mix, w_in, conv_w, conv_b, lru_wa, lru_ba,
           lru_wx, lru_bx, lru_lambda, g_q, g_k_cmp, g_k_slc, g_k_win, cmp_pos_k, cmp_w1_k, cmp_w2_k,
           cmp_pos_v, cmp_w1_v, cmp_w2_v, w_proj_a, w_proj_b, w_out, g_ffn, router_w, router_b,
           exp_w_up, exp_b_up, exp_w_down, exp_b_down):
    xp, xs = x_prompt, x_sample
    bp, sp = xp.shape[:2]
    bs, ss = xs.shape[:2]
    pos_p = jnp.arange(sp, dtype=jnp.int32)
    pos_s = PAST_LEN + jnp.arange(ss, dtype=jnp.int32)
    depth = w_in.shape[0]
    layer_states = []
    tile2 = lambda g: jnp.concatenate([g, g], axis=-1)
    for l in range(depth):
        lru_w = (conv_w[l], conv_b[l], lru_wa[l], lru_ba[l], lru_wx[l], lru_bx[l], lru_lambda[l])
        w_pad = _pad_w_in(w_in[l])
        gq2 = tile2(g_q[l]).reshape(1, LANES)
        gk3 = jnp.stack([tile2(g_k_cmp[l]), tile2(g_k_slc[l]), tile2(g_k_win[l])])
        cmp_wk = _compress_weights(cmp_pos_k[l], cmp_w1_k[l], cmp_w2_k[l])
        cmp_wv = _compress_weights(cmp_pos_v[l], cmp_w1_v[l], cmp_w2_v[l])
        wb_perm = w_proj_b[l].reshape(N_KV, GROUP, HEAD_DIM, D_MODEL).transpose(1, 0, 2, 3).reshape(D_Q, D_MODEL)
        bf16 = jnp.bfloat16
        lru_k = _lru_weights(*lru_w)
        rw_pad = jnp.pad(router_w[l].astype(jnp.float32), ((0, 0), (0, LANES - N_EXPERTS)))
        rw_hi = rw_pad.astype(bf16)
        rw_pad = jnp.stack([rw_hi, (rw_pad - rw_hi.astype(jnp.float32)).astype(bf16)])
        rb_pad = jnp.pad(router_b[l].astype(jnp.float32), (0, LANES - N_EXPERTS)).reshape(1, LANES)
        merge_w = (w_proj_a[l].astype(bf16), wb_perm.astype(bf16), w_out[l].astype(bf16), g_ffn[l], rw_pad, rb_pad)
        moe_w = (exp_w_up[l], exp_b_up[l], exp_w_down[l], exp_b_down[l])
        arag, q_pad, kc, vc, ks, vs, kw, vw, ng, gab, *kv_t = _input_projection(
            xp.reshape(bp * sp, D_MODEL), pos_p, 256, g_mix[l], w_pad, gq2, gk3, kv_transposed=True)
        rows = lambda t: t.reshape(bp, sp, t.shape[-1])
        heads_t = lambda t: t.reshape(bp, N_KV, HEAD_DIM, t.shape[-1]).transpose(0, 3, 1, 2)
        conv0 = jnp.zeros((bp, CONV_W - 1, D_RNN), jnp.float32)
        y_lru, p_h = _lru_prompt(arag, conv0, jnp.zeros((bp, D_RNN), jnp.float32), lru_k, bp, sp, 256)
        p_conv = jnp.concatenate([conv0, rows(arag)[:, :, :D_RNN]], axis=1)[:, sp:]
        cmp_k, cmp_v = _compress_prompt(rows(kc), rows(vc), cmp_wk, cmp_wv)
        o_nsa = _nsa_prompt_attention(rows(q_pad), rows(ks), rows(vs), rows(kw), rows(vw), cmp_k, cmp_v, rows(ng))
        xn, hn, te, tr, tg, cnt = _merge_router(xp.reshape(bp * sp, D_MODEL), y_lru, o_nsa.reshape(bp * sp, D_Q),
                                                gab, *merge_w, 256)
        xp = _moe_layer(xn, hn, te, tr, tg, cnt, *moe_w, 256).reshape(bp, sp, D_MODEL)
        wbp = min(WINDOW, sp)
        p_states = tuple(heads_t(t) for t in kv_t[:4]) + tuple(heads_t(t[:, :, sp - wbp:]) for t in kv_t[4:]) + (
            p_h, p_conv)
        assert ss == 1
        pos_rows_s = jnp.broadcast_to(pos_s[None, :], (bs, ss)).reshape(-1)
        arag, q_pad, kc, vc, ks, vs, kw, vw, ng, gab = _input_projection(
            xs.reshape(bs, D_MODEL), pos_rows_s, bs, g_mix[l], w_pad, gq2, gk3)
        heads = lambda t: t.reshape(bs, ss, N_KV, HEAD_DIM)
        y_lru, s_h_l = _lru_step(arag, state_conv[l], state_h[l], lru_k)
        s_conv_l = jnp.concatenate([state_conv[l][:, 1:], arag[:, None, :D_RNN]], axis=1)
        o_nsa, s_kw_l, s_vw_l = _nsa_sample_paged(
            q_pad, ks, vs, kw, vw, ng, page_table, cache_k_cmp[l], cache_v_cmp[l], cache_k_slc[l], cache_v_slc[l],
            cache_k_win[l], cache_v_win[l], cmp_wk, cmp_wv)
        xn, hn, te, tr, tg, cnt = _merge_router(xs.reshape(bs, D_MODEL), y_lru, o_nsa, gab, *merge_w, bs)
        xs = _moe_layer(xn, hn, te, tr, tg, cnt, *moe_w, bs).reshape(bs, ss, D_MODEL)
        s_states = (heads(kc), heads(vc), heads(ks), heads(vs), s_kw_l, s_vw_l, s_h_l, s_conv_l)
        layer_states.append(p_states + s_states)
    (p_k_cmp, p_v_cmp, p_k_slc, p_v_slc, p_k_win, p_v_win, p_h, p_conv,
     s_k_cmp, s_v_cmp, s_k_slc, s_v_slc, s_k_win, s_v_win, s_h, s_conv) = [jnp.stack(z) for z in zip(*layer_states)]
    return (xp, xs, p_k_cmp, p_v_cmp, p_k_slc, p_v_slc, p_k_win, p_v_win, p_h, p_conv,
            s_k_cmp, s_v_cmp, s_k_slc, s_v_slc, s_k_win, s_v_win, s_h, s_conv)
```
